```python
import math
import jax
import jax.numpy as jnp
from jax import lax
import numpy as np

D_MODEL = 2048
BATCH = 4
SEQ = 2048
DEPTH = 2

GRID_W = 64
EPS = 1e-6
BRANCH_W = D_MODEL // 2
MOD_SCALE = 0.5
ATT_HEAD_DIM = 128
ATT_HEADS = BRANCH_W // ATT_HEAD_DIM
ATT_KV_HEADS = ATT_HEADS // 4
Q_BLOCK = 128
ROPE_THETA = 10000.0
SSD_HEAD_DIM = 64
SSD_HEADS = BRANCH_W // SSD_HEAD_DIM
SSD_GROUPS = 2
SSD_STATE = 128
SSD_CONV = 4
SSD_CHUNK = 128
GDN_HEAD_DIM = 128
GDN_V_HEADS = BRANCH_W // GDN_HEAD_DIM
GDN_QK_HEADS = GDN_V_HEADS // 2
GDN_CONV = 4
GDN_CHUNK = 64
LRU_WIDTH = BRANCH_W
LRU_BLOCKS = 8
LRU_CONV = 4
LRU_C = 8.0

AB_SPLITS = (ATT_HEADS * ATT_HEAD_DIM, ATT_KV_HEADS * ATT_HEAD_DIM, ATT_KV_HEADS * ATT_HEAD_DIM, BRANCH_W,
             BRANCH_W, SSD_GROUPS * SSD_STATE, SSD_GROUPS * SSD_STATE, SSD_HEADS, SSD_HEADS, BRANCH_W)
CD_SPLITS = (GDN_QK_HEADS * GDN_HEAD_DIM, GDN_QK_HEADS * GDN_HEAD_DIM, BRANCH_W, GDN_V_HEADS, GDN_V_HEADS,
             GDN_V_HEADS, GDN_V_HEADS, BRANCH_W, LRU_WIDTH, LRU_WIDTH)
AB_IN = sum(AB_SPLITS)
CD_IN = sum(CD_SPLITS)

kernel_name = "hybrid_bidir_attn_ssd_deltanet_rglru"


def _rmsnorm(x, w):
    xf = x.astype(jnp.float32)
    y = xf * lax.rsqrt(jnp.mean(xf * xf, axis=-1, keepdims=True) + EPS)
    return (y * w.astype(jnp.float32)).astype(x.dtype)


def _l2norm(x):
    return x * lax.rsqrt(jnp.sum(x * x, axis=-1, keepdims=True) + EPS)


def _split(h, sizes):
    outs, start = [], 0
    for s in sizes:
        outs.append(h[..., start:start + s])
        start += s
    return outs


def _flip(t):
    return jnp.flip(t, axis=1)


def _centred_dwconv(x, w, b):
    K, C = w.shape
    y = lax.conv_general_dilated(x, w[:, None, :].astype(x.dtype), window_strides=(1,),
                                 padding=[(K // 2, K - 1 - K // 2)],
                                 dimension_numbers=('NWC', 'WIO', 'NWC'), feature_group_count=C)
    return y + b


def _axial_rope(seq_len):
    rows = seq_len // GRID_W
    t = jnp.arange(rows * GRID_W, dtype=jnp.int32)
    row = (t // GRID_W).astype(jnp.float32)
    col = (t % GRID_W).astype(jnp.float32)
    n_pairs = ATT_HEAD_DIM // 4
    freqs = ROPE_THETA ** (-jnp.arange(n_pairs, dtype=jnp.float32) / n_pairs)
    ang = jnp.concatenate([row[:, None] * freqs, col[:, None] * freqs], axis=-1)
    return jnp.cos(ang), jnp.sin(ang)


def _apply_rope(x, cos, sin):
    xf = x.astype(jnp.float32).reshape(*x.shape[:-1], x.shape[-1] // 2, 2)
    x0, x1 = xf[..., 0], xf[..., 1]
    c, s = cos[None, :, None, :], sin[None, :, None, :]
    out = jnp.stack([x0 * c - x1 * s, x0 * s + x1 * c], axis=-1)
    return out.reshape(x.shape).astype(x.dtype)


def _blocked_gqa(q, k, v):
    Bsz, S, H, Dh = q.shape
    Hkv = k.shape[2]
    G = H // Hkv
    nblk = S // Q_BLOCK
    qb = q.reshape(Bsz, nblk, Q_BLOCK, Hkv, G, Dh).transpose(1, 0, 2, 3, 4, 5)
    scale = Dh ** -0.5

    def block(qi):
        s = jnp.einsum('bqkgd,bskd->bkgqs', qi, k, preferred_element_type=jnp.float32) * scale
        p = jax.nn.softmax(s, axis=-1).astype(v.dtype)
        return jnp.einsum('bkgqs,bskd->bqkgd', p, v)

    o = lax.map(block, qb)
    return o.transpose(1, 0, 2, 3, 4, 5).reshape(Bsz, S, H * Dh)


def _ssd_scan(x, dt, A, Bm, Cm):
    Bsz, S, H, P = x.shape
    N = Bm.shape[-1]
    L = SSD_CHUNK
    nc = S // L
    xd = (x * dt[..., None]).reshape(Bsz, nc, L, H, P)
    a = (dt * A).reshape(Bsz, nc, L, H).transpose(0, 3, 1, 2)
    Bc = Bm.reshape(Bsz, nc, L, H, N)
    Cc = Cm.reshape(Bsz, nc, L, H, N)
    a_cs = jnp.cumsum(a, axis=-1)
    incl = jnp.tril(jnp.ones((L, L), dtype=bool))
    Lmat = jnp.exp(jnp.where(incl, a_cs[..., :, None] - a_cs[..., None, :], -jnp.inf))
    y_diag = jnp.einsum('bclhn,bcshn,bhcls,bcshp->bclhp', Cc, Bc, Lmat, xd)
    decay_states = jnp.exp(a_cs[..., -1:] - a_cs)
    states = jnp.einsum('bclhn,bhcl,bclhp->bchpn', Bc, decay_states, xd)
    chunk_decay = jnp.exp(a_cs[..., -1])

    def step(prev, inp):
        st, dec = inp
        return prev * dec[..., None, None] + st, prev

    init = jnp.zeros((Bsz, H, P, N), x.dtype)
    _, prev_states = lax.scan(step, init, (jnp.moveaxis(states, 1, 0), jnp.moveaxis(chunk_decay, 2, 0)))
    prev_states = jnp.moveaxis(prev_states, 0, 1)
    y_off = jnp.einsum('bclhn,bchpn,bhcl->bclhp', Cc, prev_states, jnp.exp(a_cs))
    return (y_diag + y_off).reshape(Bsz, S, H, P)


def _bidir_ssd(xs, bs, cs, dtf, dtb, z, dt_bias_f, dt_bias_b, a_log_f, a_log_b, d_skip, norm_w):
    f32 = jnp.float32
    Bsz, S, _ = xs.shape
    rep = SSD_HEADS // SSD_GROUPS
    xh = xs.astype(f32).reshape(Bsz, S, SSD_HEADS, SSD_HEAD_DIM)
    bh = jnp.repeat(bs.astype(f32).reshape(Bsz, S, SSD_GROUPS, SSD_STATE), rep, axis=2)
    ch = jnp.repeat(cs.astype(f32).reshape(Bsz, S, SSD_GROUPS, SSD_STATE), rep, axis=2)
    dt_f = jax.nn.softplus(dtf.astype(f32) + dt_bias_f.astype(f32))
    dt_b = jax.nn.softplus(dtb.astype(f32) + dt_bias_b.astype(f32))
    y_f = _ssd_scan(xh, dt_f, -jnp.exp(a_log_f.astype(f32)), bh, ch)
    y_b = _flip(_ssd_scan(_flip(xh), _flip(dt_b), -jnp.exp(a_log_b.astype(f32)), _flip(bh), _flip(ch)))
    y = (y_f + y_b + d_skip.astype(f32)[:, None] * xh).reshape(Bsz, S, BRANCH_W)
    return _rmsnorm(y * jax.nn.silu(z.astype(f32)), norm_w)


def _gated_delta_chunked(q, k, v, g, beta):
    Bsz, S, H, Dk = q.shape
    Dv = v.shape[-1]
    L = GDN_CHUNK
    nc = S // L

    def chunks(t):
        return t.reshape(Bsz, nc, L, H, t.shape[-1]).transpose(0, 3, 1, 2, 4)

    qc, kc, vc = chunks(q), chunks(k), chunks(v)
    gc = g.reshape(Bsz, nc, L, H).transpose(0, 3, 1, 2)
    bc = beta.reshape(Bsz, nc, L, H).transpose(0, 3, 1, 2)
    G = jnp.cumsum(gc, axis=-1)
    incl = jnp.tril(jnp.ones((L, L), dtype=bool))
    decay = jnp.exp(jnp.where(incl, G[..., :, None] - G[..., None, :], -jnp.inf))
    k_beta = kc * bc[..., None]
    strict = jnp.tril(jnp.ones((L, L), dtype=q.dtype), -1)
    lower = jnp.einsum('bhcid,bhcjd->bhcij', k_beta, kc) * decay * strict
    eye = jnp.eye(L, dtype=q.dtype)
    T = lax.linalg.triangular_solve(lower + eye, jnp.broadcast_to(eye, lower.shape),
                                    left_side=True, lower=True, unit_diagonal=True)
    u = jnp.einsum('bhcij,bhcjv->bhciv', T, vc * bc[..., None])
    w = jnp.einsum('bhcij,bhcjd->bhcid', T, k_beta * jnp.exp(G)[..., None])
    attn = jnp.einsum('bhcid,bhcjd->bhcij', qc, kc) * decay
    q_dec = qc * jnp.exp(G)[..., None]
    k_dec = kc * jnp.exp(G[..., -1:] - G)[..., None]
    chunk_dec = jnp.exp(G[..., -1])

    def step(state, inp):
        u_c, w_c, a_c, qd_c, kd_c, d_c = inp
        v_new = u_c - jnp.einsum('bhld,bhdv->bhlv', w_c, state)
        o_c = jnp.einsum('bhld,bhdv->bhlv', qd_c, state) + jnp.einsum('bhls,bhsv->bhlv', a_c, v_new)
        state = state * d_c[..., None, None] + jnp.einsum('bhld,bhlv->bhdv', kd_c, v_new)
        return state, o_c

    init = jnp.zeros((Bsz, H, Dk, Dv), q.dtype)
    xs = (jnp.moveaxis(u, 2, 0), jnp.moveaxis(w, 2, 0), jnp.moveaxis(attn, 2, 0),
          jnp.moveaxis(q_dec, 2, 0), jnp.moveaxis(k_dec, 2, 0), jnp.moveaxis(chunk_dec, 2, 0))
    _, o = lax.scan(step, init, xs)
    return o.transpose(1, 0, 3, 2, 4).reshape(Bsz, S, H, Dv)


def _bidir_gated_delta(q, k, v, bf, bb, af, ab, z, a_log_f, a_log_b, dt_bias_f, dt_bias_b, norm_w):
    f32 = jnp.float32
    Bsz, S, _ = q.shape
    rep = GDN_V_HEADS // GDN_QK_HEADS
    q = _l2norm(q.astype(f32).reshape(Bsz, S, GDN_QK_HEADS, GDN_HEAD_DIM)) * GDN_HEAD_DIM ** -0.5
    k = _l2norm(k.astype(f32).reshape(Bsz, S, GDN_QK_HEADS, GDN_HEAD_DIM))
    q = jnp.repeat(q, rep, axis=2)
    k = jnp.repeat(k, rep, axis=2)
    v = v.astype(f32).reshape(Bsz, S, GDN_V_HEADS, GDN_HEAD_DIM)
    g_f = -jnp.exp(a_log_f.astype(f32)) * jax.nn.softplus(af.astype(f32) + dt_bias_f.astype(f32))
    g_b = -jnp.exp(a_log_b.astype(f32)) * jax.nn.softplus(ab.astype(f32) + dt_bias_b.astype(f32))
    beta_f = jax.nn.sigmoid(bf.astype(f32))
    beta_b = jax.nn.sigmoid(bb.astype(f32))
    o = _gated_delta_chunked(q, k, v, g_f, beta_f) + _flip(
        _gated_delta_chunked(_flip(q), _flip(k), _flip(v), _flip(g_b), _flip(beta_b)))
    zz = z.astype(f32).reshape(Bsz, S, GDN_V_HEADS, GDN_HEAD_DIM)
    o = _rmsnorm(o, norm_w) * jax.nn.silu(zz)
    return o.reshape(Bsz, S, BRANCH_W)


def _linear_combine(e1, e2):
    a1, b1 = e1
    a2, b2 = e2
    return a1 * a2, a2 * b1 + b2


def _rglru(xc, w_a, b_a, w_x, b_x, lam, reverse):
    f32 = jnp.float32
    Bsz, S, W = xc.shape
    xb = xc.reshape(Bsz, S, LRU_BLOCKS, W // LRU_BLOCKS)
    r = jax.nn.sigmoid(jnp.einsum('bsni,nij->bsnj', xb, w_a.astype(f32)).reshape(Bsz, S, W) + b_a.astype(f32))
    i = jax.nn.sigmoid(jnp.einsum('bsni,nij->bsnj', xb, w_x.astype(f32)).reshape(Bsz, S, W) + b_x.astype(f32))
    log_a = -LRU_C * r * jax.nn.softplus(-lam.astype(f32))
    a = jnp.exp(log_a)
    u = jnp.sqrt(-jnp.expm1(2.0 * log_a)) * (i * xc)
    _, h = lax.associative_scan(_linear_combine, (a, u), reverse=reverse, axis=1)
    return h


def _ab_mixer(h, w_in, q_norm, k_norm, conv_w, conv_b, dt_bias_f, dt_bias_b, a_log_f, a_log_b,
              d_skip, ssd_norm, w_out):
    Bsz, S, _ = h.shape
    proj = jnp.einsum('bsd,de->bse', h, w_in)
    q, k, v, ga, xs, bs, cs, dtf, dtb, z = _split(proj, AB_SPLITS)
    q = _rmsnorm(q.reshape(Bsz, S, ATT_HEADS, ATT_HEAD_DIM), q_norm)
    k = _rmsnorm(k.reshape(Bsz, S, ATT_KV_HEADS, ATT_HEAD_DIM), k_norm)
    v = v.reshape(Bsz, S, ATT_KV_HEADS, ATT_HEAD_DIM)
    cos, sin = _axial_rope(S)
    att = _blocked_gqa(_apply_rope(q, cos, sin), _apply_rope(k, cos, sin), v) * jax.nn.silu(ga)
    xbc = jax.nn.silu(_centred_dwconv(jnp.concatenate([xs, bs, cs], axis=-1), conv_w, conv_b))
    xs, bs, cs = _split(xbc, (BRANCH_W, SSD_GROUPS * SSD_STATE, SSD_GROUPS * SSD_STATE))
    ssd = _bidir_ssd(xs, bs, cs, dtf, dtb, z, dt_bias_f, dt_bias_b, a_log_f, a_log_b, d_skip, ssd_norm)
    y = jnp.concatenate([att.astype(h.dtype), ssd.astype(h.dtype)], axis=-1)
    return jnp.einsum('bse,ed->bsd', y, w_out)


def _cd_mixer(h, w_in, conv_w, conv_b, a_log_f, a_log_b, dt_bias_f, dt_bias_b, gdn_norm,
              lru_conv_w, lru_conv_b, wa_f, ba_f, wx_f, bx_f, lam_f, wa_b, ba_b, wx_b, bx_b, lam_b, w_out):
    proj = jnp.einsum('bsd,de->bse', h, w_in)
    q, k, v, bf, bb, af, ab, z, xl, gl = _split(proj, CD_SPLITS)
    qkv = jax.nn.silu(_centred_dwconv(jnp.concatenate([q, k, v], axis=-1), conv_w, conv_b))
    q, k, v = _split(qkv, (GDN_QK_HEADS * GDN_HEAD_DIM, GDN_QK_HEADS * GDN_HEAD_DIM, BRANCH_W))
    gdn = _bidir_gated_delta(q, k, v, bf, bb, af, ab, z, a_log_f, a_log_b, dt_bias_f, dt_bias_b, gdn_norm)
    xc = _centred_dwconv(xl, lru_conv_w, lru_conv_b).astype(jnp.float32)
    hs = _rglru(xc, wa_f, ba_f, wx_f, bx_f, lam_f, False) + _rglru(xc, wa_b, ba_b, wx_b, bx_b, lam_b, True)
    lru = hs * jax.nn.silu(gl.astype(jnp.float32))
    y = jnp.concatenate([gdn.astype(h.dtype), lru.astype(h.dtype)], axis=-1)
    return jnp.einsum('bse,ed->bsd', y, w_out)


def setup_inputs(seed: int = 0) -> dict:
    key = jax.random.key(seed)
    ks = iter(jax.random.split(key, 48))
    ne, no = (DEPTH + 1) // 2, DEPTH // 2
    f32 = jnp.float32
    d = D_MODEL

    def nrm(shape, scale):
        return jax.random.normal(next(ks), shape, f32) * scale

    def gain(shape):
        return 1.0 + nrm(shape, 0.02)

    def dt_bias(shape):
        dt = jnp.exp(jax.random.uniform(next(ks), shape, f32, math.log(1e-3), math.log(1e-1)))
        return dt + jnp.log(-jnp.expm1(-dt))

    def a_log(shape):
        return jnp.log(jax.random.uniform(next(ks), shape, f32, 1.0, 16.0))

    def lru_lambda(shape):
        a_c = jax.random.uniform(next(ks), shape, f32, 0.9, 0.999)
        s = a_c ** (1.0 / LRU_C)
        return jnp.log(s) - jnp.log1p(-s)

    ssd_ch = BRANCH_W + 2 * SSD_GROUPS * SSD_STATE
    gdn_ch = 2 * GDN_QK_HEADS * GDN_HEAD_DIM + BRANCH_W
    bw = LRU_WIDTH // LRU_BLOCKS
    return {
        "x": nrm((BATCH, SEQ, d), 1.0),
        "c": nrm((BATCH, d), 1.0),
        "w_mod": nrm((DEPTH, d, 3 * d), MOD_SCALE * d ** -0.5),
        "b_mod": nrm((DEPTH, 3 * d), 0.01),
        "norm_w": gain((DEPTH, d)),
        "ab_w_in": nrm((ne, d, AB_IN), d ** -0.5),
        "ab_q_norm": gain((ne, ATT_HEAD_DIM)),
        "ab_k_norm": gain((ne, ATT_HEAD_DIM)),
        "ab_conv_w": nrm((ne, SSD_CONV, ssd_ch), SSD_CONV ** -0.5),
        "ab_conv_b": nrm((ne, ssd_ch), 0.01),
        "ab_dt_bias_f": dt_bias((ne, SSD_HEADS)),
        "ab_dt_bias_b": dt_bias((ne, SSD_HEADS)),
        "ab_a_log_f": a_log((ne, SSD_HEADS)),
        "ab_a_log_b": a_log((ne, SSD_HEADS)),
        "ab_d_skip": gain((ne, SSD_HEADS)),
        "ab_ssd_norm": gain((ne, BRANCH_W)),
        "ab_w_out": nrm((ne, 2 * BRANCH_W, d), (2 * BRANCH_W) ** -0.5),
        "cd_w_in": nrm((no, d, CD_IN), d ** -0.5),
        "cd_conv_w": nrm((no, GDN_CONV, gdn_ch), GDN_CONV ** -0.5),
        "cd_conv_b": nrm((no, gdn_ch), 0.01),
        "cd_a_log_f": a_log((no, GDN_V_HEADS)),
        "cd_a_log_b": a_log((no, GDN_V_HEADS)),
        "cd_dt_bias_f": dt_bias((no, GDN_V_HEADS)),
        "cd_dt_bias_b": dt_bias((no, GDN_V_HEADS)),
        "cd_gdn_norm": gain((no, GDN_HEAD_DIM)),
        "cd_lru_conv_w": nrm((no, LRU_CONV, LRU_WIDTH), LRU_CONV ** -0.5),
        "cd_lru_conv_b": nrm((no, LRU_WIDTH), 0.01),
        "cd_lru_wa_f": nrm((no, LRU_BLOCKS, bw, bw), bw ** -0.5),
        "cd_lru_ba_f": nrm((no, LRU_WIDTH), 0.01),
        "cd_lru_wx_f": nrm((no, LRU_BLOCKS, bw, bw), bw ** -0.5),
        "cd_lru_bx_f": nrm((no, LRU_WIDTH), 0.01),
        "cd_lru_lam_f": lru_lambda((no, LRU_WIDTH)),
        "cd_lru_wa_b": nrm((no, LRU_BLOCKS, bw, bw), bw ** -0.5),
        "cd_lru_ba_b": nrm((no, LRU_WIDTH), 0.01),
        "cd_lru_wx_b": nrm((no, LRU_BLOCKS, bw, bw), bw ** -0.5),
        "cd_lru_bx_b": nrm((no, LRU_WIDTH), 0.01),
        "cd_lru_lam_b": lru_lambda((no, LRU_WIDTH)),
        "cd_w_out": nrm((no, 2 * BRANCH_W, d), (2 * BRANCH_W) ** -0.5),
        "final_norm_w": gain((d,)),
    }


def reference(x, c, w_mod, b_mod, norm_w,
              ab_w_in, ab_q_norm, ab_k_norm, ab_conv_w, ab_conv_b, ab_dt_bias_f, ab_dt_bias_b,
              ab_a_log_f, ab_a_log_b, ab_d_skip, ab_ssd_norm, ab_w_out,
              cd_w_in, cd_conv_w, cd_conv_b, cd_a_log_f, cd_a_log_b, cd_dt_bias_f, cd_dt_bias_b,
              cd_gdn_norm, cd_lru_conv_w, cd_lru_conv_b, cd_lru_wa_f, cd_lru_ba_f, cd_lru_wx_f,
              cd_lru_bx_f, cd_lru_lam_f, cd_lru_wa_b, cd_lru_ba_b, cd_lru_wx_b, cd_lru_bx_b,
              cd_lru_lam_b, cd_w_out, final_norm_w):
    d = x.shape[-1]
    cond = jax.nn.silu(c)
    for layer in range(DEPTH):
        mod = jnp.einsum('bd,de->be', cond, w_mod[layer]) + b_mod[layer]
        shift, scale, gate = mod[:, :d], mod[:, d:2 * d], mod[:, 2 * d:]
        h = _rmsnorm(x, norm_w[layer]) * (1.0 + scale[:, None, :]) + shift[:, None, :]
        j = layer // 2
        if layer % 2 == 0:
            y = _ab_mixer(h, ab_w_in[j], ab_q_norm[j], ab_k_norm[j], ab_conv_w[j], ab_conv_b[j],
                          ab_dt_bias_f[j], ab_dt_bias_b[j], ab_a_log_f[j], ab_a_log_b[j],
                          ab_d_skip[j], ab_ssd_norm[j], ab_w_out[j])
        else:
            y = _cd_mixer(h, cd_w_in[j], cd_conv_w[j], cd_conv_b[j], cd_a_log_f[j], cd_a_log_b[j],
                          cd_dt_bias_f[j], cd_dt_bias_b[j], cd_gdn_norm[j], cd_lru_conv_w[j],
                          cd_lru_conv_b[j], cd_lru_wa_f[j], cd_lru_ba_f[j], cd_lru_wx_f[j],
                          cd_lru_bx_f[j], cd_lru_lam_f[j], cd_lru_wa_b[j], cd_lru_ba_b[j],
                          cd_lru_wx_b[j], cd_lru_bx_b[j], cd_lru_lam_b[j], cd_w_out[j])
        x = x + gate[:, None, :] * y.astype(x.dtype)
    return _rmsnorm(x, final_norm_w)
```

```python
import functools
import math

import jax
import jax.numpy as jnp
from jax import lax
from jax.experimental import pallas as pl
from jax.experimental.pallas import tpu as pltpu

F32 = jnp.float32
BF16 = jnp.bfloat16

D_MODEL = 2048
GRID_W = 64
EPS = 1e-6
BRANCH_W = D_MODEL // 2
ATT_HEAD_DIM = 128
ATT_HEADS = BRANCH_W // ATT_HEAD_DIM
ATT_KV_HEADS = ATT_HEADS // 4
ATT_GROUP = ATT_HEADS // ATT_KV_HEADS
ROPE_THETA = 10000.0
SSD_HEAD_DIM = 64
SSD_HEADS = BRANCH_W // SSD_HEAD_DIM
SSD_GROUPS = 2
SSD_STATE = 128
SSD_CONV = 4
SSD_CHUNK = 128
GDN_HEAD_DIM = 128
GDN_V_HEADS = BRANCH_W // GDN_HEAD_DIM
GDN_QK_HEADS = GDN_V_HEADS // 2
GDN_CONV = 4
GDN_CHUNK = 64
LRU_WIDTH = BRANCH_W
LRU_BLOCKS = 8
LRU_CONV = 4
LRU_C = 8.0

LANES = 128
SUBLANES = 8
VMEM_LIMIT_BYTES = 56 * 1024 * 1024

HALO = SUBLANES
NEG_BIG = -1e30


def _cparams(sem):
    return pltpu.CompilerParams(dimension_semantics=sem, vmem_limit_bytes=VMEM_LIMIT_BYTES)


def _silu(x):
    return x / (1.0 + jnp.exp(-x))


def _sigmoid(x):
    return 1.0 / (1.0 + jnp.exp(-x))


def _softplus(x):
    return jnp.maximum(x, 0.0) + jnp.log(1.0 + jnp.exp(-jnp.abs(x)))


def _split_bf16(a):
    hi = a.astype(BF16)
    lo = (a - hi.astype(F32)).astype(BF16)
    return hi, lo


def _dot(a, b):
    return jnp.dot(a, b, preferred_element_type=F32)


def _dot_nt(a, b):
    return lax.dot_general(a, b, (((1,), (1,)), ((), ())), preferred_element_type=F32)


def _dot_exact_rhs(a, b_bf16):
    hi, lo = _split_bf16(a)
    return _dot(hi, b_bf16) + _dot(lo, b_bf16)


def _dot_exact_lhs(a_bf16, b):
    hi, lo = _split_bf16(b)
    return _dot(a_bf16, hi) + _dot(a_bf16, lo)


def _mod_kernel(c_ref, w_ref, b_ref, o_ref):
    cond = _silu(c_ref[...])
    c_hi, c_lo = _split_bf16(cond)
    w = w_ref[0]
    w_hi, w_lo = _split_bf16(w)
    o_ref[0] = _dot(c_hi, w_hi) + _dot(c_lo, w_hi) + _dot(c_hi, w_lo) + b_ref[0]


def _modulation(c, w_mod, b_mod):
    depth, d, n = w_mod.shape
    bsz = c.shape[0]
    rows = -(-bsz // SUBLANES) * SUBLANES
    c_pad = jnp.zeros((rows, d), F32).at[:bsz].set(c)
    tn = 1024
    out = pl.pallas_call(
        _mod_kernel,
        out_shape=jax.ShapeDtypeStruct((depth, rows, n), F32),
        grid=(depth, n // tn),
        in_specs=[
            pl.BlockSpec((rows, d), lambda l, j: (0, 0)),
            pl.BlockSpec((1, d, tn), lambda l, j: (l, 0, j)),
            pl.BlockSpec((1, 1, tn), lambda l, j: (l, 0, j)),
        ],
        out_specs=pl.BlockSpec((1, rows, tn), lambda l, j: (l, 0, j)),
        compiler_params=_cparams(("arbitrary", "arbitrary")),
        name="adaln_mod",
    )(c_pad, w_mod, b_mod.reshape(depth, 1, n))
    return out[:, :bsz].reshape(depth, bsz, 3, d)


def _inproj_kernel(x_ref, mod_ref, nw_ref, w_ref, *rest, has_small, time_major):
    if has_small:
        ws_ref, wst_ref, o_ref, os_ref, ost_ref, h_scr = rest
    else:
        o_ref, h_scr = rest
    j = pl.program_id(2)

    @pl.when(j == 0)
    def _():
        x = x_ref[0]
        ms = jnp.mean(x * x, axis=-1, keepdims=True)
        y = x * lax.rsqrt(ms + EPS) * nw_ref[...]
        shift = mod_ref[0, 0:1, :]
        scale = mod_ref[0, 1:2, :]
        h = (y * (1.0 + scale) + shift).astype(BF16)
        h_scr[...] = h
        if has_small:
            os_ref[0] = _dot(h, ws_ref[...])
            ost_ref[0] = _dot_nt(wst_ref[...], h)

    res = _dot(h_scr[...], w_ref[...])
    if time_major:
        o_ref[...] = res
    else:
        o_ref[0] = res


def _inproj(x, mod_l, norm_w, w_main, w_small=None, *, time_major=False, tm=1024, tn=1024):
    bsz, s, d = x.shape
    n = w_main.shape[1]
    tm = min(tm, s)
    tn = min(tn, n)
    has_small = w_small is not None
    in_specs = [
        pl.BlockSpec((1, tm, d), lambda b, i, j: (b, i, 0)),
        pl.BlockSpec((1, 3, d), lambda b, i, j: (b, 0, 0)),
        pl.BlockSpec((1, d), lambda b, i, j: (0, 0)),
        pl.BlockSpec((d, tn), lambda b, i, j: (0, j)),
    ]
    args = [x, mod_l, norm_w.reshape(1, d), w_main]
    if time_major:
        out_shape = [jax.ShapeDtypeStruct((s, bsz * n), F32)]
        out_specs = [pl.BlockSpec((tm, tn), lambda b, i, j: (i, b * (n // tn) + j))]
    else:
        out_shape = [jax.ShapeDtypeStruct((bsz, s, n), F32)]
        out_specs = [pl.BlockSpec((1, tm, tn), lambda b, i, j: (b, i, j))]
    if has_small:
        ns = w_small.shape[1]
        in_specs += [pl.BlockSpec((d, ns), lambda b, i, j: (0, 0)),
                     pl.BlockSpec((ns, d), lambda b, i, j: (0, 0))]
        args += [w_small, w_small.T]
        out_shape += [jax.ShapeDtypeStruct((bsz, s, ns), F32), jax.ShapeDtypeStruct((bsz, ns, s), F32)]
        out_specs += [pl.BlockSpec((1, tm, ns), lambda b, i, j: (b, i, 0)),
                      pl.BlockSpec((1, ns, tm), lambda b, i, j: (b, 0, i))]
    outs = pl.pallas_call(
        functools.partial(_inproj_kernel, has_small=has_small, time_major=time_major),
        out_shape=out_shape,
        grid=(bsz, s // tm, n // tn),
        in_specs=in_specs,
        out_specs=out_specs,
        scratch_shapes=[pltpu.VMEM((tm, d), BF16)],
        compiler_params=_cparams(("arbitrary", "arbitrary", "arbitrary")),
        name="norm_mod_inproj",
    )(*args)
    return outs if has_small else outs[0]


def _outproj_kernel(ya_ref, yb_ref, w_ref, x_ref, mod_ref, *rest, final_norm, half):
    if final_norm:
        fnw_ref, o_ref = rest
    else:
        (o_ref,) = rest
    yb = yb_ref[...] if len(yb_ref.shape) == 2 else yb_ref[0]
    acc = _dot(ya_ref[0], w_ref[0:half, :]) + _dot(yb, w_ref[half:, :])
    gate = mod_ref[0, 2:3, :]
    xn = x_ref[0] + gate * acc
    if final_norm:
        ms = jnp.mean(xn * xn, axis=-1, keepdims=True)
        xn = xn * lax.rsqrt(ms + EPS) * fnw_ref[...]
    o_ref[0] = xn


def _outproj(ya, yb, w_out, x, mod_l, final_w=None, *, yb_time_major=False, tm=512):
    bsz, s, d = x.shape
    half = ya.shape[-1]
    tm = min(tm, s)
    final_norm = final_w is not None
    if yb_time_major:
        yb_spec = pl.BlockSpec((tm, half), lambda b, i: (i, b))
    else:
        yb_spec = pl.BlockSpec((1, tm, half), lambda b, i: (b, i, 0))
    in_specs = [
        pl.BlockSpec((1, tm, half), lambda b, i: (b, i, 0)),
        yb_spec,
        pl.BlockSpec((2 * half, d), lambda b, i: (0, 0)),
        pl.BlockSpec((1, tm, d), lambda b, i: (b, i, 0)),
        pl.BlockSpec((1, 3, d), lambda b, i: (b, 0, 0)),
    ]
    args = [ya, yb, w_out, x, mod_l]
    if final_norm:
        in_specs.append(pl.BlockSpec((1, d), lambda b, i: (0, 0)))
        args.append(final_w.reshape(1, d))
    return pl.pallas_call(
        functools.partial(_outproj_kernel, final_norm=final_norm, half=half),
        out_shape=jax.ShapeDtypeStruct((bsz, s, d), F32),
        grid=(bsz, s // tm),
        in_specs=in_specs,
        out_specs=pl.BlockSpec((1, tm, d), lambda b, i: (b, i, 0)),
        compiler_params=_cparams(("arbitrary", "arbitrary")),
        name="outproj_residual",
    )(*args)


def _rms_rope(x, nw, cos, sin_signed):
    ms = jnp.mean(x * x, axis=-1, keepdims=True)
    y = x * lax.rsqrt(ms + EPS) * nw
    return y * cos + pltpu.roll(y, ATT_HEAD_DIM // 2, 1) * sin_signed


def _attn_kernel(q_ref, k_ref, v_ref, ga_ref, cq_ref, sq_ref, ck_ref, sk_ref, qn_ref, kn_ref,
                 o_ref, k_scr, v_scr, *, tq):
    qi = pl.program_id(2)

    @pl.when(qi == 0)
    def _():
        k_scr[...] = _rms_rope(k_ref[0], kn_ref[...], ck_ref[...], sk_ref[...]).astype(BF16)
        v_scr[...] = v_ref[0].astype(BF16)

    scale = ATT_HEAD_DIM ** -0.5
    cos = cq_ref[...]
    sin = sq_ref[...]
    qs = []
    for g in range(ATT_GROUP):
        qg = q_ref[0, :, g * ATT_HEAD_DIM:(g + 1) * ATT_HEAD_DIM]
        qs.append((_rms_rope(qg, qn_ref[...], cos, sin) * scale).astype(BF16))
    q_all = jnp.concatenate(qs, axis=0)
    s = _dot_nt(q_all, k_scr[...])
    m = jnp.max(s, axis=-1, keepdims=True)
    p = jnp.exp(s - m)
    l = jnp.sum(p, axis=-1, keepdims=True)
    o = _dot(p.astype(BF16), v_scr[...]) / l
    for g in range(ATT_GROUP):
        og = o[g * tq:(g + 1) * tq] * _silu(ga_ref[0, :, g * ATT_HEAD_DIM:(g + 1) * ATT_HEAD_DIM])
        o_ref[0, :, g * ATT_HEAD_DIM:(g + 1) * ATT_HEAD_DIM] = og.astype(BF16)


def _attention(proj, cos_t, sin_t, q_norm, k_norm, *, col_q, col_k, col_v, col_ga, tq=256):
    bsz, s, _ = proj.shape
    tq = min(tq, s)
    gw = ATT_GROUP * ATT_HEAD_DIM
    dh = ATT_HEAD_DIM
    qb, kb, vb, gb = col_q // gw, col_k // dh, col_v // dh, col_ga // gw
    return pl.pallas_call(
        functools.partial(_attn_kernel, tq=tq),
        out_shape=jax.ShapeDtypeStruct((bsz, s, BRANCH_W), BF16),
        grid=(bsz, ATT_KV_HEADS, s // tq),
        in_specs=[
            pl.BlockSpec((1, tq, gw), lambda b, h, i: (b, i, qb + h)),
            pl.BlockSpec((1, s, dh), lambda b, h, i: (b, 0, kb + h)),
            pl.BlockSpec((1, s, dh), lambda b, h, i: (b, 0, vb + h)),
            pl.BlockSpec((1, tq, gw), lambda b, h, i: (b, i, gb + h)),
            pl.BlockSpec((tq, dh), lambda b, h, i: (i, 0)),
            pl.BlockSpec((tq, dh), lambda b, h, i: (i, 0)),
            pl.BlockSpec((s, dh), lambda b, h, i: (0, 0)),
            pl.BlockSpec((s, dh), lambda b, h, i: (0, 0)),
            pl.BlockSpec((1, dh), lambda b, h, i: (0, 0)),
            pl.BlockSpec((1, dh), lambda b, h, i: (0, 0)),
        ],
        out_specs=pl.BlockSpec((1, tq, gw), lambda b, h, i: (b, i, h)),
        scratch_shapes=[pltpu.VMEM((s, dh), BF16), pltpu.VMEM((s, dh), BF16)],
        compiler_params=_cparams(("arbitrary", "arbitrary", "arbitrary")),
        name="gqa_attention",
    )(proj, proj, proj, proj, cos_t, sin_t, cos_t, sin_t, q_norm, k_norm)


def _rope_tables(s):
    t = jnp.arange(s, dtype=jnp.int32)
    row = (t // GRID_W).astype(F32)
    col = (t % GRID_W).astype(F32)
    n_pairs = ATT_HEAD_DIM // 4
    freqs = ROPE_THETA ** (-jnp.arange(n_pairs, dtype=F32) / n_pairs)
    ang = jnp.concatenate([row[:, None] * freqs, col[:, None] * freqs], axis=-1)
    cos, sin = jnp.cos(ang), jnp.sin(ang)
    return jnp.concatenate([cos, cos], axis=-1), jnp.concatenate([-sin, sin], axis=-1)


def _deinterleave_cols(w, heads):
    d = w.shape[0]
    return w.reshape(d, heads, ATT_HEAD_DIM // 2, 2).transpose(0, 1, 3, 2).reshape(d, heads * ATT_HEAD_DIM)


def _deinterleave_vec(v):
    return v.reshape(ATT_HEAD_DIM // 2, 2).T.reshape(1, ATT_HEAD_DIM)


def _chunk_of_step(step, nc):
    return jnp.where(step < nc, step, 2 * nc - 1 - step)


def _fill_conv_window(ext_scr, cur, prev, nxt, c, nc, rows):
    ext_scr[0:HALO, :] = jnp.where(c > 0, prev, 0.0)
    ext_scr[HALO:HALO + rows, :] = cur
    ext_scr[HALO + rows:HALO + rows + HALO, :] = jnp.where(c < nc - 1, nxt, 0.0)


def _centred_conv4(ext_scr, cw_ref, cb_ref, rows, shift_rows=1):
    acc = cb_ref[...]
    for k in range(4):
        acc = acc + ext_scr[pl.ds(HALO + (k - 2) * shift_rows, rows), :] * cw_ref[k:k + 1, :]
    return acc


def _scan_masks(fwd, n):
    row = lax.broadcasted_iota(jnp.int32, (n, n), 0)
    col = lax.broadcasted_iota(jnp.int32, (n, n), 1)
    sgn = jnp.where(fwd, 1, -1)
    d = (row - col) * sgn
    return d >= 0, d <= 0, d > 0


def _head_expander(heads, width):
    r = lax.broadcasted_iota(jnp.int32, (heads, heads * width), 0)
    c = lax.broadcasted_iota(jnp.int32, (heads, heads * width), 1)
    return jnp.where((c >= r * width) & (c < (r + 1) * width), 1.0, 0.0).astype(BF16)


def _ssd_kernel(xbc_ref, prev_ref, next_ref, z_ref, dt_ref, dtt_ref, cw_ref, cb_ref,
                bias_r_ref, bias_c_ref, alog_r_ref, alog_c_ref, dskip_ref, nw_ref,
                o_ref, ext_scr, state_scr, yf_scr, *, nc):
    L, H, P, N = SSD_CHUNK, SSD_HEADS, SSD_HEAD_DIM, SSD_STATE
    HG = H // SSD_GROUPS
    GW = HG * P
    step = pl.program_id(1)
    fwd = step < nc
    c = _chunk_of_step(step, nc)

    @pl.when((step == 0) | (step == nc))
    def _():
        state_scr[...] = jnp.zeros_like(state_scr)

    _fill_conv_window(ext_scr, xbc_ref[0], prev_ref[0], next_ref[0], c, nc, L)
    xbc = _silu(_centred_conv4(ext_scr, cw_ref, cb_ref, L))
    xs = xbc[:, :BRANCH_W]
    bs = xbc[:, BRANCH_W:BRANCH_W + SSD_GROUPS * N]
    cs = xbc[:, BRANCH_W + SSD_GROUPS * N:]

    dt = _softplus(dt_ref[0][:, 0:H] + bias_r_ref[0][:, 0:H])
    a = dt * (-jnp.exp(alog_r_ref[0][:, 0:H]))
    dt_t = _softplus(dtt_ref[0][0:H, :] + bias_c_ref[0][0:H, :])
    a_t = dt_t * (-jnp.exp(alog_c_ref[0][0:H, :]))
    mask, mask_t, _ = _scan_masks(fwd, L)
    cum = _dot_exact_lhs(jnp.where(mask, 1.0, 0.0).astype(BF16), a)
    cum_t = _dot_exact_rhs(a_t, jnp.where(mask_t, 1.0, 0.0).astype(BF16))
    total = jnp.sum(a, axis=0, keepdims=True)

    expand = _head_expander(H, P)
    dt_e = _dot_exact_rhs(dt, expand)
    p_e = _dot_exact_rhs(jnp.exp(cum), expand)
    q_e = _dot_exact_rhs(jnp.exp(total - cum), expand)
    tot_e = _dot_exact_rhs(jnp.broadcast_to(jnp.exp(total), (SUBLANES, H)), expand)[0:1, :]
    xd = xs * dt_e
    xdq = (xd * q_e).astype(BF16)
    xd_b = xd.astype(BF16)
    lane = lax.broadcasted_iota(jnp.int32, (L, 2 * P), 1)

    y_parts = []
    for g in range(SSD_GROUPS):
        cg = cs[:, g * N:(g + 1) * N].astype(BF16)
        bg = bs[:, g * N:(g + 1) * N]
        gmat = _dot_nt(cg, bg.astype(BF16))
        h_prev = state_scr[:, g * GW:(g + 1) * GW]
        y_off = _dot(cg, h_prev.astype(BF16)) * p_e[:, g * GW:(g + 1) * GW]
        pairs = []
        for hp in range(HG // 2):
            h0 = g * HG + 2 * hp
            xpair = xd_b[:, h0 * P:(h0 + 2) * P]
            ys = []
            for h in (h0, h0 + 1):
                dec = jnp.exp(jnp.where(mask, cum[:, h:h + 1] - cum_t[h:h + 1, :], NEG_BIG))
                ys.append(_dot((gmat * dec).astype(BF16), xpair))
            pairs.append(jnp.where(lane < P, ys[0], ys[1]))
        y_parts.append(jnp.concatenate(pairs, axis=1) + y_off)
        state_scr[:, g * GW:(g + 1) * GW] = (
            h_prev * tot_e[:, g * GW:(g + 1) * GW] + _dot(bg.T.astype(BF16), xdq[:, g * GW:(g + 1) * GW]))
    y_dir = jnp.concatenate(y_parts, axis=1)

    row0 = pl.multiple_of(c * L, L)

    @pl.when(fwd)
    def _():
        yf_scr[pl.ds(row0, L), :] = y_dir

    @pl.when(jnp.logical_not(fwd))
    def _():
        y = yf_scr[pl.ds(row0, L), :] + y_dir + dskip_ref[...] * xs
        y = y * _silu(z_ref[0])
        ms = jnp.mean(y * y, axis=-1, keepdims=True)
        o_ref[0] = (y * lax.rsqrt(ms + EPS) * nw_ref[...]).astype(BF16)


def _ssd(proj, small, small_t, conv_w, conv_b, dt_bias, a_log, d_skip, norm_w, *, col_xbc, col_z):
    bsz, s, _ = proj.shape
    L = SSD_CHUNK
    nc = s // L
    cw = BRANCH_W + 2 * SSD_GROUPS * SSD_STATE
    hb = L // HALO
    xb, zb = col_xbc // cw, col_z // BRANCH_W
    assert col_xbc % cw == 0 and col_z % BRANCH_W == 0

    def chunk(t):
        return _chunk_of_step(t, nc)

    def direction(t):
        return jnp.where(t < nc, 0, 1)

    pad = lambda v: jnp.zeros((2, 1, LANES), F32).at[:, 0, :SSD_HEADS].set(v)
    pad_c = lambda v: jnp.zeros((2, LANES, 1), F32).at[:, :SSD_HEADS, 0].set(v)
    return pl.pallas_call(
        functools.partial(_ssd_kernel, nc=nc),
        out_shape=jax.ShapeDtypeStruct((bsz, s, BRANCH_W), BF16),
        grid=(bsz, 2 * nc),
        in_specs=[
            pl.BlockSpec((1, L, cw), lambda b, t: (b, chunk(t), xb)),
            pl.BlockSpec((1, HALO, cw), lambda b, t: (b, jnp.maximum(chunk(t) * hb - 1, 0), xb)),
            pl.BlockSpec((1, HALO, cw), lambda b, t: (b, jnp.minimum((chunk(t) + 1) * hb, s // HALO - 1), xb)),
            pl.BlockSpec((1, L, BRANCH_W), lambda b, t: (b, chunk(t), zb)),
            pl.BlockSpec((1, L, LANES), lambda b, t: (b, chunk(t), direction(t))),
            pl.BlockSpec((1, LANES, L), lambda b, t: (b, direction(t), chunk(t))),
            pl.BlockSpec((SSD_CONV, cw), lambda b, t: (0, 0)),
            pl.BlockSpec((1, cw), lambda b, t: (0, 0)),
            pl.BlockSpec((1, 1, LANES), lambda b, t: (direction(t), 0, 0)),
            pl.BlockSpec((1, LANES, 1), lambda b, t: (direction(t), 0, 0)),
            pl.BlockSpec((1, 1, LANES), lambda b, t: (direction(t), 0, 0)),
            pl.BlockSpec((1, LANES, 1), lambda b, t: (direction(t), 0, 0)),
            pl.BlockSpec((1, BRANCH_W), lambda b, t: (0, 0)),
            pl.BlockSpec((1, BRANCH_W), lambda b, t: (0, 0)),
        ],
        out_specs=pl.BlockSpec((1, L, BRANCH_W), lambda b, t: (b, jnp.where(t < nc, nc - 1, 2 * nc - 1 - t), 0)),
        scratch_shapes=[
            pltpu.VMEM((L + 2 * HALO, cw), F32),
            pltpu.VMEM((SSD_STATE, BRANCH_W), F32),
            pltpu.VMEM((s, BRANCH_W), F32),
        ],
        compiler_params=_cparams(("arbitrary", "arbitrary")),
        name="bidir_ssd",
    )(proj, proj, proj, proj, small, small_t, conv_w, conv_b.reshape(1, cw),
      pad(dt_bias), pad_c(dt_bias), pad(a_log), pad_c(a_log),
      jnp.repeat(d_skip, SSD_HEAD_DIM).reshape(1, BRANCH_W), norm_w.reshape(1, BRANCH_W))


AB_COL_XBC, AB_COL_K, AB_COL_V, AB_COL_Z, AB_COL_Q, AB_COL_GA, AB_N = 0, 1536, 1792, 2048, 3072, 4096, 5120


def _ab_weights(w_in):
    hq, hk = ATT_HEADS * ATT_HEAD_DIM, ATT_KV_HEADS * ATT_HEAD_DIM
    gn = SSD_GROUPS * SSD_STATE
    o = 0
    q = w_in[:, o:o + hq]; o += hq
    k = w_in[:, o:o + hk]; o += hk
    v = w_in[:, o:o + hk]; o += hk
    ga = w_in[:, o:o + BRANCH_W]; o += BRANCH_W
    xs = w_in[:, o:o + BRANCH_W]; o += BRANCH_W
    bs = w_in[:, o:o + gn]; o += gn
    cs = w_in[:, o:o + gn]; o += gn
    dtf = w_in[:, o:o + SSD_HEADS]; o += SSD_HEADS
    dtb = w_in[:, o:o + SSD_HEADS]; o += SSD_HEADS
    z = w_in[:, o:o + BRANCH_W]
    main = jnp.concatenate([xs, bs, cs, _deinterleave_cols(k, ATT_KV_HEADS), v, z,
                            _deinterleave_cols(q, ATT_HEADS), ga], axis=1).astype(BF16)
    zpad = jnp.zeros((w_in.shape[0], LANES - SSD_HEADS), w_in.dtype)
    small = jnp.concatenate([dtf, zpad, dtb, zpad], axis=1).astype(BF16)
    return main, small


def _layer0(x, mod_l, norm_w, w_in, q_norm, k_norm, conv_w, conv_b, dt_bias_f, dt_bias_b,
            a_log_f, a_log_b, d_skip, ssd_norm, w_out, final_w=None):
    s = x.shape[1]
    w_main, w_small = _ab_weights(w_in)
    proj, small, small_t = _inproj(x, mod_l, norm_w, w_main, w_small)
    cos_t, sin_t = _rope_tables(s)
    att = _attention(proj, cos_t, sin_t, _deinterleave_vec(q_norm), _deinterleave_vec(k_norm),
                     col_q=AB_COL_Q, col_k=AB_COL_K, col_v=AB_COL_V, col_ga=AB_COL_GA)
    ssd = _ssd(proj, small, small_t, conv_w, conv_b, jnp.stack([dt_bias_f, dt_bias_b]),
               jnp.stack([a_log_f, a_log_b]), d_skip, ssd_norm, col_xbc=AB_COL_XBC, col_z=AB_COL_Z)
    return _outproj(att, ssd, w_out.astype(BF16), x, mod_l, final_w), (proj, small, att, ssd)


def _unit_tri_inverse(nmat, n):
    row = lax.broadcasted_iota(jnp.int32, (n, n), 0)
    col = lax.broadcasted_iota(jnp.int32, (n, n), 1)

    def same_block(size):
        return (row // size) == (col // size)

    def mm(a, b):
        return _dot(a.astype(BF16), b.astype(BF16))

    eye = jnp.where(row == col, 1.0, 0.0)
    base = SUBLANES
    blk = same_block(base)
    nd = jnp.where(blk, nmat, 0.0)
    p1 = mm(nd, nd)
    p2 = mm(p1, p1)
    t = eye - nd
    t = t + mm(t, p1)
    t = t + mm(t, p2)
    size = base
    while size < n:
        nxt = same_block(2 * size)
        e = jnp.where(nxt & jnp.logical_not(blk), nmat, 0.0)
        t = t - mm(t, mm(e, t))
        blk = nxt
        size *= 2
    return t


def _l2norm(x):
    return x * lax.rsqrt(jnp.sum(x * x, axis=-1, keepdims=True) + EPS)


def _gdn_kernel(qkv_ref, prev_ref, next_ref, z_ref, sm_ref, smt_ref, cw_ref, cb_ref,
                bias_r_ref, bias_c_ref, alog_r_ref, alog_c_ref, nw_ref,
                o_ref, ext_scr, state_scr, of_scr, *, nc):
    L, HV, HQ, DK = GDN_CHUNK, GDN_V_HEADS, GDN_QK_HEADS, GDN_HEAD_DIM
    rep = HV // HQ
    step = pl.program_id(1)
    fwd = step < nc
    c = _chunk_of_step(step, nc)

    @pl.when((step == 0) | (step == nc))
    def _():
        state_scr[...] = jnp.zeros_like(state_scr)

    _fill_conv_window(ext_scr, qkv_ref[0], prev_ref[0], next_ref[0], c, nc, L)
    qkv = _silu(_centred_conv4(ext_scr, cw_ref, cb_ref, L))
    q = qkv[:, :HQ * DK]
    k = qkv[:, HQ * DK:2 * HQ * DK]
    v = qkv[:, 2 * HQ * DK:]

    sm = sm_ref[0]
    beta = _sigmoid(sm[:, 0:HV])
    g = -jnp.exp(alog_r_ref[0][:, 0:HV]) * _softplus(sm[:, HV:2 * HV] + bias_r_ref[0][:, 0:HV])
    smt = jnp.where(c % 2 == 0, smt_ref[0][HV:2 * HV, 0:L], smt_ref[0][HV:2 * HV, L:2 * L])
    g_t = -jnp.exp(alog_c_ref[0][0:HV, :]) * _softplus(smt + bias_c_ref[0][0:HV, :])
    mask, mask_t, strict = _scan_masks(fwd, L)
    cum = _dot_exact_lhs(jnp.where(mask, 1.0, 0.0).astype(BF16), g)
    cum_t = _dot_exact_rhs(g_t, jnp.where(mask_t, 1.0, 0.0).astype(BF16))
    total = jnp.sum(g, axis=0, keepdims=True)

    outs = []
    for hq in range(HQ):
        qh = _l2norm(q[:, hq * DK:(hq + 1) * DK]) * (DK ** -0.5)
        kh = _l2norm(k[:, hq * DK:(hq + 1) * DK])
        kb = kh.astype(BF16)
        kk = _dot_nt(kb, kb)
        qk = _dot_nt(qh.astype(BF16), kb)
        for hv in range(hq * rep, (hq + 1) * rep):
            colv = cum[:, hv:hv + 1]
            dec = jnp.exp(jnp.where(mask, colv - cum_t[hv:hv + 1, :], NEG_BIG))
            bcol = beta[:, hv:hv + 1]
            ecol = jnp.exp(colv)
            t_inv = _unit_tri_inverse(jnp.where(strict, kk * bcol * dec, 0.0), L).astype(BF16)
            u = _dot(t_inv, (v[:, hv * DK:(hv + 1) * DK] * bcol).astype(BF16))
            w = _dot(t_inv, (kh * (bcol * ecol)).astype(BF16))
            s_prev = state_scr[:, hv * DK:(hv + 1) * DK]
            s_b = s_prev.astype(BF16)
            v_new = (u - _dot(w.astype(BF16), s_b)).astype(BF16)
            o = _dot((qh * ecol).astype(BF16), s_b) + _dot((qk * dec).astype(BF16), v_new)
            tot = total[:, hv:hv + 1]
            k_dec = kh * jnp.exp(tot - colv)
            state_scr[:, hv * DK:(hv + 1) * DK] = s_prev * jnp.exp(tot) + _dot(k_dec.T.astype(BF16), v_new)
            outs.append(o)
    o_dir = jnp.concatenate(outs, axis=1)

    row0 = pl.multiple_of(c * L, L)

    @pl.when(fwd)
    def _():
        of_scr[pl.ds(row0, L), :] = o_dir

    @pl.when(jnp.logical_not(fwd))
    def _():
        o_sum = of_scr[pl.ds(row0, L), :] + o_dir
        zz = z_ref[0]
        for hv in range(HV):
            oh = o_sum[:, hv * DK:(hv + 1) * DK]
            ms = jnp.mean(oh * oh, axis=-1, keepdims=True)
            res = oh * lax.rsqrt(ms + EPS) * nw_ref[...] * _silu(zz[:, hv * DK:(hv + 1) * DK])
            o_ref[0, :, hv * DK:(hv + 1) * DK] = res.astype(BF16)


def _pad_dir_rows(v, n):
    return jnp.zeros((2, 1, LANES), F32).at[:, 0, :n].set(v)


def _pad_dir_cols(v, n):
    return jnp.zeros((2, LANES, 1), F32).at[:, :n, 0].set(v)


def _gdn(proj, small, small_t, conv_w, conv_b, dt_bias, a_log, norm_w, *, col_qkv, col_z):
    bsz, s, _ = proj.shape
    L = GDN_CHUNK
    nc = s // L
    cw = 2 * GDN_QK_HEADS * GDN_HEAD_DIM + BRANCH_W
    hb = L // HALO
    qb, zb = col_qkv // cw, col_z // BRANCH_W
    assert col_qkv % cw == 0 and col_z % BRANCH_W == 0

    def chunk(t):
        return _chunk_of_step(t, nc)

    def direction(t):
        return jnp.where(t < nc, 0, 1)

    hv = GDN_V_HEADS
    return pl.pallas_call(
        functools.partial(_gdn_kernel, nc=nc),
        out_shape=jax.ShapeDtypeStruct((bsz, s, BRANCH_W), BF16),
        grid=(bsz, 2 * nc),
        in_specs=[
            pl.BlockSpec((1, L, cw), lambda b, t: (b, chunk(t), qb)),
            pl.BlockSpec((1, HALO, cw), lambda b, t: (b, jnp.maximum(chunk(t) * hb - 1, 0), qb)),
            pl.BlockSpec((1, HALO, cw), lambda b, t: (b, jnp.minimum((chunk(t) + 1) * hb, s // HALO - 1), qb)),
            pl.BlockSpec((1, L, BRANCH_W), lambda b, t: (b, chunk(t), zb)),
            pl.BlockSpec((1, L, LANES), lambda b, t: (b, chunk(t), direction(t))),
            pl.BlockSpec((1, LANES, 2 * L), lambda b, t: (b, direction(t), chunk(t) // 2)),
            pl.BlockSpec((GDN_CONV, cw), lambda b, t: (0, 0)),
            pl.BlockSpec((1, cw), lambda b, t: (0, 0)),
            pl.BlockSpec((1, 1, LANES), lambda b, t: (direction(t), 0, 0)),
            pl.BlockSpec((1, LANES, 1), lambda b, t: (direction(t), 0, 0)),
            pl.BlockSpec((1, 1, LANES), lambda b, t: (direction(t), 0, 0)),
            pl.BlockSpec((1, LANES, 1), lambda b, t: (direction(t), 0, 0)),
            pl.BlockSpec((1, GDN_HEAD_DIM), lambda b, t: (0, 0)),
        ],
        out_specs=pl.BlockSpec((1, L, BRANCH_W), lambda b, t: (b, jnp.where(t < nc, nc - 1, 2 * nc - 1 - t), 0)),
        scratch_shapes=[
            pltpu.VMEM((L + 2 * HALO, cw), F32),
            pltpu.VMEM((GDN_HEAD_DIM, BRANCH_W), F32),
            pltpu.VMEM((s, BRANCH_W), F32),
        ],
        compiler_params=_cparams(("arbitrary", "arbitrary")),
        name="bidir_gated_deltanet",
    )(proj, proj, proj, proj, small, small_t, conv_w, conv_b.reshape(1, cw),
      _pad_dir_rows(dt_bias, hv), _pad_dir_cols(dt_bias, hv), _pad_dir_rows(a_log, hv), _pad_dir_cols(a_log, hv),
      norm_w.reshape(1, GDN_HEAD_DIM))


def _neg_expm1(t):
    return -jnp.tanh(0.5 * t) * (jnp.exp(t) + 1.0)


def _lru_kernel(xl_ref, prev_ref, next_ref, gl_ref, cw_ref, cb_ref, wa_ref, ba_ref, wx_ref, bx_ref, lam_ref,
                o_ref, ext_scr, a_scr, u_scr, hf_scr, carry_scr, *, nb, bsz, unroll):
    rows, width = a_scr.shape
    steps = rows // bsz
    bw = LRU_WIDTH // LRU_BLOCKS
    step = pl.program_id(1)
    fwd = step < nb
    blk = _chunk_of_step(step, nb)

    @pl.when((step == 0) | (step == nb))
    def _():
        carry_scr[...] = jnp.zeros_like(carry_scr)

    _fill_conv_window(ext_scr, xl_ref[...], prev_ref[...], next_ref[...], blk, nb, rows)
    xc = _centred_conv4(ext_scr, cw_ref, cb_ref, rows, shift_rows=bsz)
    r_parts, i_parts = [], []
    for n in range(width // bw):
        xb = xc[:, n * bw:(n + 1) * bw].astype(BF16)
        r_parts.append(_dot(xb, wa_ref[0, n]))
        i_parts.append(_dot(xb, wx_ref[0, n]))
    r = _sigmoid(jnp.concatenate(r_parts, axis=1) + ba_ref[0])
    i = _sigmoid(jnp.concatenate(i_parts, axis=1) + bx_ref[0])
    log_a = -LRU_C * r * _softplus(-lam_ref[0])
    a_scr[...] = jnp.exp(log_a)
    u_scr[...] = jnp.sqrt(_neg_expm1(2.0 * log_a)) * (i * xc)

    group = unroll * bsz
    n_iter = rows // group
    base_out = pl.multiple_of(blk * rows, rows)

    def scan(ascending):
        def body(it, h):
            pos = it if ascending else n_iter - 1 - it
            r0 = pl.multiple_of(pos * group, group)
            a_blk = a_scr[pl.ds(r0, group), :]
            u_blk = u_scr[pl.ds(r0, group), :]
            hs = [None] * unroll
            for kk in (range(unroll) if ascending else range(unroll - 1, -1, -1)):
                h = a_blk[kk * bsz:(kk + 1) * bsz] * h + u_blk[kk * bsz:(kk + 1) * bsz]
                hs[kk] = h
            a_scr[pl.ds(r0, group), :] = jnp.concatenate(hs, axis=0)
            return h

        carry_scr[0:bsz, :] = lax.fori_loop(0, n_iter, body, carry_scr[0:bsz, :])

    @pl.when(fwd)
    def _():
        scan(True)
        hf_scr[pl.ds(base_out, rows), :] = a_scr[...]

    @pl.when(jnp.logical_not(fwd))
    def _():
        scan(False)
        o_ref[...] = ((hf_scr[pl.ds(base_out, rows), :] + a_scr[...]) * _silu(gl_ref[...])).astype(BF16)


def _lru(lru_in, bsz, conv_w, conv_b, wa, ba, wx, bx, lam, *, tt=256, width=512, unroll=8):
    n_rows = lru_in.shape[0]
    s = n_rows // bsz
    tt = min(tt, s)
    rows = tt * bsz
    nb = n_rows // rows
    w_total = LRU_WIDTH
    bw = w_total // LRU_BLOCKS
    nbw = width // bw
    gcol = w_total // width
    hb = rows // HALO

    def blk(t):
        return _chunk_of_step(t, nb)

    def direction(t):
        return jnp.where(t < nb, 0, 1)

    return pl.pallas_call(
        functools.partial(_lru_kernel, nb=nb, bsz=bsz, unroll=unroll),
        out_shape=jax.ShapeDtypeStruct((n_rows, w_total), BF16),
        grid=(w_total // width, 2 * nb),
        in_specs=[
            pl.BlockSpec((rows, width), lambda j, t: (blk(t), j)),
            pl.BlockSpec((HALO, width), lambda j, t: (jnp.maximum(blk(t) * hb - 1, 0), j)),
            pl.BlockSpec((HALO, width), lambda j, t: (jnp.minimum((blk(t) + 1) * hb, n_rows // HALO - 1), j)),
            pl.BlockSpec((rows, width), lambda j, t: (blk(t), gcol + j)),
            pl.BlockSpec((LRU_CONV, width), lambda j, t: (0, j)),
            pl.BlockSpec((1, width), lambda j, t: (0, j)),
            pl.BlockSpec((1, nbw, bw, bw), lambda j, t: (direction(t), j, 0, 0)),
            pl.BlockSpec((1, 1, width), lambda j, t: (direction(t), 0, j)),
            pl.BlockSpec((1, nbw, bw, bw), lambda j, t: (direction(t), j, 0, 0)),
            pl.BlockSpec((1, 1, width), lambda j, t: (direction(t), 0, j)),
            pl.BlockSpec((1, 1, width), lambda j, t: (direction(t), 0, j)),
        ],
        out_specs=pl.BlockSpec((rows, width), lambda j, t: (jnp.where(t < nb, nb - 1, 2 * nb - 1 - t), j)),
        scratch_shapes=[
            pltpu.VMEM((rows + 2 * HALO, width), F32),
            pltpu.VMEM((rows, width), F32),
            pltpu.VMEM((rows, width), F32),
            pltpu.VMEM((n_rows, width), F32),
            pltpu.VMEM((SUBLANES, width), F32),
        ],
        compiler_params=_cparams(("arbitrary", "arbitrary")),
        name="bidir_rglru",
    )(lru_in, lru_in, lru_in, lru_in, conv_w, conv_b.reshape(1, w_total),
      wa.astype(BF16), ba.reshape(2, 1, w_total), wx.astype(BF16), bx.reshape(2, 1, w_total),
      lam.reshape(2, 1, w_total))


CD_COL_QKV, CD_COL_Z = 0, 2048


def _cd_weights(w_in):
    nqk = GDN_QK_HEADS * GDN_HEAD_DIM
    hv = GDN_V_HEADS
    o = 0
    q = w_in[:, o:o + nqk]; o += nqk
    k = w_in[:, o:o + nqk]; o += nqk
    v = w_in[:, o:o + BRANCH_W]; o += BRANCH_W
    bf = w_in[:, o:o + hv]; o += hv
    bb = w_in[:, o:o + hv]; o += hv
    af = w_in[:, o:o + hv]; o += hv
    ab = w_in[:, o:o + hv]; o += hv
    z = w_in[:, o:o + BRANCH_W]; o += BRANCH_W
    xl = w_in[:, o:o + LRU_WIDTH]; o += LRU_WIDTH
    gl = w_in[:, o:o + LRU_WIDTH]
    main = jnp.concatenate([q, k, v, z], axis=1).astype(BF16)
    zpad = jnp.zeros((w_in.shape[0], LANES - 2 * hv), w_in.dtype)
    small = jnp.concatenate([bf, af, zpad, bb, ab, zpad], axis=1).astype(BF16)
    lru = jnp.concatenate([xl, gl], axis=1).astype(BF16)
    return main, small, lru


def _layer1(x, mod_l, norm_w, w_in, conv_w, conv_b, a_log_f, a_log_b, dt_bias_f, dt_bias_b, gdn_norm,
            lru_conv_w, lru_conv_b, wa_f, ba_f, wx_f, bx_f, lam_f, wa_b, ba_b, wx_b, bx_b, lam_b, w_out,
            final_w=None):
    bsz, s, _ = x.shape
    w_main, w_small, w_lru = _cd_weights(w_in)
    proj, small, small_t = _inproj(x, mod_l, norm_w, w_main, w_small)
    lru_in = _inproj(x, mod_l, norm_w, w_lru, time_major=True)
    gdn = _gdn(proj, small, small_t, conv_w, conv_b, jnp.stack([dt_bias_f, dt_bias_b]),
               jnp.stack([a_log_f, a_log_b]), gdn_norm, col_qkv=CD_COL_QKV, col_z=CD_COL_Z)
    lru = _lru(lru_in.reshape(s * bsz, 2 * LRU_WIDTH), bsz, lru_conv_w, lru_conv_b,
               jnp.stack([wa_f, wa_b]), jnp.stack([ba_f, ba_b]), jnp.stack([wx_f, wx_b]),
               jnp.stack([bx_f, bx_b]), jnp.stack([lam_f, lam_b]))
    out = _outproj(gdn, lru.reshape(s, bsz * LRU_WIDTH), w_out.astype(BF16), x, mod_l, final_w, yb_time_major=True)
    return out, (proj, small, gdn, lru)


def kernel(x, c, w_mod, b_mod, norm_w, ab_w_in, ab_q_norm, ab_k_norm, ab_conv_w, ab_conv_b, ab_dt_bias_f, ab_dt_bias_b, ab_a_log_f, ab_a_log_b, ab_d_skip, ab_ssd_norm, ab_w_out, cd_w_in, cd_conv_w, cd_conv_b, cd_a_log_f, cd_a_log_b, cd_dt_bias_f, cd_dt_bias_b, cd_gdn_norm, cd_lru_conv_w, cd_lru_conv_b, cd_lru_wa_f, cd_lru_ba_f, cd_lru_wx_f, cd_lru_bx_f, cd_lru_lam_f, cd_lru_wa_b, cd_lru_ba_b, cd_lru_wx_b, cd_lru_bx_b, cd_lru_lam_b, cd_w_out, final_norm_w):
    mods = _modulation(c, w_mod, b_mod)
    x1, _ = _layer0(x, mods[0], norm_w[0], ab_w_in[0], ab_q_norm[0], ab_k_norm[0], ab_conv_w[0], ab_conv_b[0],
                    ab_dt_bias_f[0], ab_dt_bias_b[0], ab_a_log_f[0], ab_a_log_b[0], ab_d_skip[0],
                    ab_ssd_norm[0], ab_w_out[0])
    out, _ = _layer1(x1, mods[1], norm_w[1], cd_w_in[0], cd_conv_w[0], cd_conv_b[0], cd_a_log_f[0], cd_a_log_b[0],
                     cd_dt_bias_f[0], cd_dt_bias_b[0], cd_gdn_norm[0], cd_lru_conv_w[0], cd_lru_conv_b[0],
                     cd_lru_wa_f[0], cd_lru_ba_f[0], cd_lru_wx_f[0], cd_lru_bx_f[0], cd_lru_lam_f[0],
                     cd_lru_wa_b[0], cd_lru_ba_b[0], cd_lru_wx_b[0], cd_lru_bx_b[0], cd_lru_lam_b[0],
                     cd_w_out[0], final_norm_w)
    return out
```

```python
import functools
import math

import jax
import jax.numpy as jnp
from jax import lax
from jax.experimental import pallas as pl
from jax.experimental.pallas import tpu as pltpu

F32 = jnp.float32
BF16 = jnp.bfloat16

D_MODEL = 2048
GRID_W = 64
EPS = 1e-6
BRANCH_W = D_MODEL // 2
ATT_HEAD_DIM = 128
ATT_HEADS = BRANCH_W // ATT_HEAD_DIM
ATT_KV_HEADS = ATT_HEADS // 4
ATT_GROUP = ATT_HEADS // ATT_KV_HEADS
ROPE_THETA = 10000.0
SSD_HEAD_DIM = 64
SSD_HEADS = BRANCH_W // SSD_HEAD_DIM
SSD_GROUPS = 2
SSD_STATE = 128
SSD_CONV = 4
SSD_CHUNK = 128
GDN_HEAD_DIM = 128
GDN_V_HEADS = BRANCH_W // GDN_HEAD_DIM
GDN_QK_HEADS = GDN_V_HEADS // 2
GDN_CONV = 4
GDN_CHUNK = 64
LRU_WIDTH = BRANCH_W
LRU_BLOCKS = 8
LRU_CONV = 4
LRU_C = 8.0

LANES = 128
SUBLANES = 8
VMEM_LIMIT_BYTES = 56 * 1024 * 1024

HALO = SUBLANES
NEG_BIG = -1e30


def _cparams(sem):
    return pltpu.CompilerParams(dimension_semantics=sem, vmem_limit_bytes=VMEM_LIMIT_BYTES)


def _silu(x):
    return x / (1.0 + jnp.exp(-x))


def _sigmoid(x):
    return 1.0 / (1.0 + jnp.exp(-x))


def _softplus(x):
    return jnp.maximum(x, 0.0) + jnp.log(1.0 + jnp.exp(-jnp.abs(x)))


def _split_bf16(a):
    hi = a.astype(BF16)
    lo = (a - hi.astype(F32)).astype(BF16)
    return hi, lo


def _dot(a, b):
    return jnp.dot(a, b, preferred_element_type=F32)


def _dot_nt(a, b):
    return lax.dot_general(a, b, (((1,), (1,)), ((), ())), preferred_element_type=F32)


def _dot_exact_rhs(a, b_bf16):
    hi, lo = _split_bf16(a)
    return _dot(hi, b_bf16) + _dot(lo, b_bf16)


def _dot_exact_lhs(a_bf16, b):
    hi, lo = _split_bf16(b)
    return _dot(a_bf16, hi) + _dot(a_bf16, lo)


def _mod_kernel(c_ref, w_ref, b_ref, o_ref):
    cond = _silu(c_ref[...])
    c_hi, c_lo = _split_bf16(cond)
    w = w_ref[0]
    w_hi, w_lo = _split_bf16(w)
    o_ref[0] = _dot(c_hi, w_hi) + _dot(c_lo, w_hi) + _dot(c_hi, w_lo) + b_ref[0]


def _modulation(c, w_mod, b_mod):
    depth, d, n = w_mod.shape
    bsz = c.shape[0]
    rows = -(-bsz // SUBLANES) * SUBLANES
    c_pad = jnp.zeros((rows, d), F32).at[:bsz].set(c)
    tn = 1024
    out = pl.pallas_call(
        _mod_kernel,
        out_shape=jax.ShapeDtypeStruct((depth, rows, n), F32),
        grid=(depth, n // tn),
        in_specs=[
            pl.BlockSpec((rows, d), lambda l, j: (0, 0)),
            pl.BlockSpec((1, d, tn), lambda l, j: (l, 0, j)),
            pl.BlockSpec((1, 1, tn), lambda l, j: (l, 0, j)),
        ],
        out_specs=pl.BlockSpec((1, rows, tn), lambda l, j: (l, 0, j)),
        compiler_params=_cparams(("arbitrary", "arbitrary")),
        name="adaln_mod",
    )(c_pad, w_mod, b_mod.reshape(depth, 1, n))
    return out[:, :bsz].reshape(depth, bsz, 3, d)


def _inproj_kernel(x_ref, mod_ref, nw_ref, w_ref, *rest, has_small, time_major):
    if has_small:
        ws_ref, wst_ref, o_ref, os_ref, ost_ref, h_scr = rest
    else:
        o_ref, h_scr = rest
    j = pl.program_id(2)

    @pl.when(j == 0)
    def _():
        x = x_ref[0]
        ms = jnp.mean(x * x, axis=-1, keepdims=True)
        y = x * lax.rsqrt(ms + EPS) * nw_ref[...]
        shift = mod_ref[0, 0:1, :]
        scale = mod_ref[0, 1:2, :]
        h = (y * (1.0 + scale) + shift).astype(BF16)
        h_scr[...] = h
        if has_small:
            os_ref[0] = _dot(h, ws_ref[...])
            ost_ref[0] = _dot_nt(wst_ref[...], h)

    res = _dot(h_scr[...], w_ref[...])
    if time_major:
        o_ref[...] = res
    else:
        o_ref[0] = res


def _inproj(x, mod_l, norm_w, w_main, w_small=None, *, time_major=False, tm=1024, tn=1024):
    bsz, s, d = x.shape
    n = w_main.shape[1]
    tm = min(tm, s)
    tn = min(tn, n)
    has_small = w_small is not None
    in_specs = [
        pl.BlockSpec((1, tm, d), lambda b, i, j: (b, i, 0)),
        pl.BlockSpec((1, 3, d), lambda b, i, j: (b, 0, 0)),
        pl.BlockSpec((1, d), lambda b, i, j: (0, 0)),
        pl.BlockSpec((d, tn), lambda b, i, j: (0, j)),
    ]
    args = [x, mod_l, norm_w.reshape(1, d), w_main]
    if time_major:
        out_shape = [jax.ShapeDtypeStruct((s, bsz * n), F32)]
        out_specs = [pl.BlockSpec((tm, tn), lambda b, i, j: (i, b * (n // tn) + j))]
    else:
        out_shape = [jax.ShapeDtypeStruct((bsz, s, n), F32)]
        out_specs = [pl.BlockSpec((1, tm, tn), lambda b, i, j: (b, i, j))]
    if has_small:
        ns = w_small.shape[1]
        in_specs += [pl.BlockSpec((d, ns), lambda b, i, j: (0, 0)),
                     pl.BlockSpec((ns, d), lambda b, i, j: (0, 0))]
        args += [w_small, w_small.T]
        out_shape += [jax.ShapeDtypeStruct((bsz, s, ns), F32), jax.ShapeDtypeStruct((bsz, ns, s), F32)]
        out_specs += [pl.BlockSpec((1, tm, ns), lambda b, i, j: (b, i, 0)),
                      pl.BlockSpec((1, ns, tm), lambda b, i, j: (b, 0, i))]
    outs = pl.pallas_call(
        functools.partial(_inproj_kernel, has_small=has_small, time_major=time_major),
        out_shape=out_shape,
        grid=(bsz, s // tm, n // tn),
        in_specs=in_specs,
        out_specs=out_specs,
        scratch_shapes=[pltpu.VMEM((tm, d), BF16)],
        compiler_params=_cparams(("arbitrary", "arbitrary", "arbitrary")),
        name="norm_mod_inproj",
    )(*args)
    return outs if has_small else outs[0]


def _outproj_kernel(ya_ref, yb_ref, w_ref, x_ref, mod_ref, *rest, final_norm, half):
    if final_norm:
        fnw_ref, o_ref = rest
    else:
        (o_ref,) = rest
    yb = yb_ref[...] if len(yb_ref.shape) == 2 else yb_ref[0]
    acc = _dot(ya_ref[0], w_ref[0:half, :]) + _dot(yb, w_ref[half:, :])
    gate = mod_ref[0, 2:3, :]
    xn = x_ref[0] + gate * acc
    if final_norm:
        ms = jnp.mean(xn * xn, axis=-1, keepdims=True)
        xn = xn * lax.rsqrt(ms + EPS) * fnw_ref[...]
    o_ref[0] = xn


def _outproj(ya, yb, w_out, x, mod_l, final_w=None, *, yb_time_major=False, tm=512):
    bsz, s, d = x.shape
    half = ya.shape[-1]
    tm = min(tm, s)
    final_norm = final_w is not None
    if yb_time_major:
        yb_spec = pl.BlockSpec((tm, half), lambda b, i: (i, b))
    else:
        yb_spec = pl.BlockSpec((1, tm, half), lambda b, i: (b, i, 0))
    in_specs = [
        pl.BlockSpec((1, tm, half), lambda b, i: (b, i, 0)),
        yb_spec,
        pl.BlockSpec((2 * half, d), lambda b, i: (0, 0)),
        pl.BlockSpec((1, tm, d), lambda b, i: (b, i, 0)),
        pl.BlockSpec((1, 3, d), lambda b, i: (b, 0, 0)),
    ]
    args = [ya, yb, w_out, x, mod_l]
    if final_norm:
        in_specs.append(pl.BlockSpec((1, d), lambda b, i: (0, 0)))
        args.append(final_w.reshape(1, d))
    return pl.pallas_call(
        functools.partial(_outproj_kernel, final_norm=final_norm, half=half),
        out_shape=jax.ShapeDtypeStruct((bsz, s, d), F32),
        grid=(bsz, s // tm),
        in_specs=in_specs,
        out_specs=pl.BlockSpec((1, tm, d), lambda b, i: (b, i, 0)),
        compiler_params=_cparams(("arbitrary", "arbitrary")),
        name="outproj_residual",
    )(*args)


def _rms_rope(x, nw, cos, sin_signed):
    ms = jnp.mean(x * x, axis=-1, keepdims=True)
    y = x * lax.rsqrt(ms + EPS) * nw
    return y * cos + pltpu.roll(y, ATT_HEAD_DIM // 2, 1) * sin_signed


def _attn_kernel(q_ref, k_ref, v_ref, ga_ref, cq_ref, sq_ref, ck_ref, sk_ref, qn_ref, kn_ref,
                 o_ref, k_scr, v_scr, *, tq):
    qi = pl.program_id(2)

    @pl.when(qi == 0)
    def _():
        k_scr[...] = _rms_rope(k_ref[0], kn_ref[...], ck_ref[...], sk_ref[...]).astype(BF16)
        v_scr[...] = v_ref[0].astype(BF16)

    scale = ATT_HEAD_DIM ** -0.5
    cos = cq_ref[...]
    sin = sq_ref[...]
    qs = []
    for g in range(ATT_GROUP):
        qg = q_ref[0, :, g * ATT_HEAD_DIM:(g + 1) * ATT_HEAD_DIM]
        qs.append((_rms_rope(qg, qn_ref[...], cos, sin) * scale).astype(BF16))
    q_all = jnp.concatenate(qs, axis=0)
    s = _dot_nt(q_all, k_scr[...])
    m = jnp.max(s, axis=-1, keepdims=True)
    p = jnp.exp(s - m)
    l = jnp.sum(p, axis=-1, keepdims=True)
    o = _dot(p.astype(BF16), v_scr[...]) / l
    for g in range(ATT_GROUP):
        og = o[g * tq:(g + 1) * tq] * _silu(ga_ref[0, :, g * ATT_HEAD_DIM:(g + 1) * ATT_HEAD_DIM])
        o_ref[0, :, g * ATT_HEAD_DIM:(g + 1) * ATT_HEAD_DIM] = og.astype(BF16)


def _attention(proj, cos_t, sin_t, q_norm, k_norm, *, col_q, col_k, col_v, col_ga, tq=256):
    bsz, s, _ = proj.shape
    tq = min(tq, s)
    gw = ATT_GROUP * ATT_HEAD_DIM
    dh = ATT_HEAD_DIM
    qb, kb, vb, gb = col_q // gw, col_k // dh, col_v // dh, col_ga // gw
    return pl.pallas_call(
        functools.partial(_attn_kernel, tq=tq),
        out_shape=jax.ShapeDtypeStruct((bsz, s, BRANCH_W), BF16),
        grid=(bsz, ATT_KV_HEADS, s // tq),
        in_specs=[
            pl.BlockSpec((1, tq, gw), lambda b, h, i: (b, i, qb + h)),
            pl.BlockSpec((1, s, dh), lambda b, h, i: (b, 0, kb + h)),
            pl.BlockSpec((1, s, dh), lambda b, h, i: (b, 0, vb + h)),
            pl.BlockSpec((1, tq, gw), lambda b, h, i: (b, i, gb + h)),
            pl.BlockSpec((tq, dh), lambda b, h, i: (i, 0)),
            pl.BlockSpec((tq, dh), lambda b, h, i: (i, 0)),
            pl.BlockSpec((s, dh), lambda b, h, i: (0, 0)),
            pl.BlockSpec((s, dh), lambda b, h, i: (0, 0)),
            pl.BlockSpec((1, dh), lambda b, h, i: (0, 0)),
            pl.BlockSpec((1, dh), lambda b, h, i: (0, 0)),
        ],
        out_specs=pl.BlockSpec((1, tq, gw), lambda b, h, i: (b, i, h)),
        scratch_shapes=[pltpu.VMEM((s, dh), BF16), pltpu.VMEM((s, dh), BF16)],
        compiler_params=_cparams(("arbitrary", "arbitrary", "arbitrary")),
        name="gqa_attention",
    )(proj, proj, proj, proj, cos_t, sin_t, cos_t, sin_t, q_norm, k_norm)


def _rope_tables(s):
    t = jnp.arange(s, dtype=jnp.int32)
    row = (t // GRID_W).astype(F32)
    col = (t % GRID_W).astype(F32)
    n_pairs = ATT_HEAD_DIM // 4
    freqs = ROPE_THETA ** (-jnp.arange(n_pairs, dtype=F32) / n_pairs)
    ang = jnp.concatenate([row[:, None] * freqs, col[:, None] * freqs], axis=-1)
    cos, sin = jnp.cos(ang), jnp.sin(ang)
    return jnp.concatenate([cos, cos], axis=-1), jnp.concatenate([-sin, sin], axis=-1)


def _deinterleave_cols(w, heads):
    d = w.shape[0]
    return w.reshape(d, heads, ATT_HEAD_DIM // 2, 2).transpose(0, 1, 3, 2).reshape(d, heads * ATT_HEAD_DIM)


def _deinterleave_vec(v):
    return v.reshape(ATT_HEAD_DIM // 2, 2).T.reshape(1, ATT_HEAD_DIM)


def _chunk_of_step(step, nc):
    return jnp.where(step < nc, step, 2 * nc - 1 - step)


def _fill_conv_window(ext_scr, cur, prev, nxt, c, nc, rows):
    ext_scr[0:HALO, :] = jnp.where(c > 0, prev, 0.0)
    ext_scr[HALO:HALO + rows, :] = cur
    ext_scr[HALO + rows:HALO + rows + HALO, :] = jnp.where(c < nc - 1, nxt, 0.0)


def _centred_conv4(ext_scr, cw_ref, cb_ref, rows, shift_rows=1):
    acc = cb_ref[...]
    for k in range(4):
        acc = acc + ext_scr[pl.ds(HALO + (k - 2) * shift_rows, rows), :] * cw_ref[k:k + 1, :]
    return acc


def _scan_masks(fwd, n):
    row = lax.broadcasted_iota(jnp.int32, (n, n), 0)
    col = lax.broadcasted_iota(jnp.int32, (n, n), 1)
    sgn = jnp.where(fwd, 1, -1)
    d = (row - col) * sgn
    return d >= 0, d <= 0, d > 0


def _head_expander(heads, width):
    r = lax.broadcasted_iota(jnp.int32, (heads, heads * width), 0)
    c = lax.broadcasted_iota(jnp.int32, (heads, heads * width), 1)
    return jnp.where((c >= r * width) & (c < (r + 1) * width), 1.0, 0.0).astype(BF16)


def _ssd_kernel(xbc_ref, prev_ref, next_ref, z_ref, dt_ref, dtt_ref, cw_ref, cb_ref,
                bias_r_ref, bias_c_ref, alog_r_ref, alog_c_ref, dskip_ref, nw_ref,
                o_ref, ext_scr, state_scr, yf_scr, *, nc):
    L, H, P, N = SSD_CHUNK, SSD_HEADS, SSD_HEAD_DIM, SSD_STATE
    HG = H // SSD_GROUPS
    GW = HG * P
    step = pl.program_id(1)
    fwd = step < nc
    c = _chunk_of_step(step, nc)

    @pl.when((step == 0) | (step == nc))
    def _():
        state_scr[...] = jnp.zeros_like(state_scr)

    _fill_conv_window(ext_scr, xbc_ref[0], prev_ref[0], next_ref[0], c, nc, L)
    xbc = _silu(_centred_conv4(ext_scr, cw_ref, cb_ref, L))
    xs = xbc[:, :BRANCH_W]
    bs = xbc[:, BRANCH_W:BRANCH_W + SSD_GROUPS * N]
    cs = xbc[:, BRANCH_W + SSD_GROUPS * N:]

    dt = _softplus(dt_ref[0][:, 0:H] + bias_r_ref[0][:, 0:H])
    a = dt * (-jnp.exp(alog_r_ref[0][:, 0:H]))
    dt_t = _softplus(dtt_ref[0][0:H, :] + bias_c_ref[0][0:H, :])
    a_t = dt_t * (-jnp.exp(alog_c_ref[0][0:H, :]))
    mask, mask_t, _ = _scan_masks(fwd, L)
    cum = _dot_exact_lhs(jnp.where(mask, 1.0, 0.0).astype(BF16), a)
    cum_t = _dot_exact_rhs(a_t, jnp.where(mask_t, 1.0, 0.0).astype(BF16))
    total = jnp.sum(a, axis=0, keepdims=True)

    expand = _head_expander(H, P)
    dt_e = _dot_exact_rhs(dt, expand)
    p_e = _dot_exact_rhs(jnp.exp(cum), expand)
    q_e = _dot_exact_rhs(jnp.exp(total - cum), expand)
    tot_e = _dot_exact_rhs(jnp.broadcast_to(jnp.exp(total), (SUBLANES, H)), expand)[0:1, :]
    xd = xs * dt_e
    xdq = (xd * q_e).astype(BF16)
    xd_b = xd.astype(BF16)
    lane = lax.broadcasted_iota(jnp.int32, (L, 2 * P), 1)

    y_parts = []
    for g in range(SSD_GROUPS):
        cg = cs[:, g * N:(g + 1) * N].astype(BF16)
        bg = bs[:, g * N:(g + 1) * N]
        gmat = _dot_nt(cg, bg.astype(BF16))
        h_prev = state_scr[:, g * GW:(g + 1) * GW]
        y_off = _dot(cg, h_prev.astype(BF16)) * p_e[:, g * GW:(g + 1) * GW]
        pairs = []
        for hp in range(HG // 2):
            h0 = g * HG + 2 * hp
            xpair = xd_b[:, h0 * P:(h0 + 2) * P]
            ys = []
            for h in (h0, h0 + 1):
                dec = jnp.exp(jnp.where(mask, cum[:, h:h + 1] - cum_t[h:h + 1, :], NEG_BIG))
                ys.append(_dot((gmat * dec).astype(BF16), xpair))
            pairs.append(jnp.where(lane < P, ys[0], ys[1]))
        y_parts.append(jnp.concatenate(pairs, axis=1) + y_off)
        state_scr[:, g * GW:(g + 1) * GW] = (
            h_prev * tot_e[:, g * GW:(g + 1) * GW] + _dot(bg.T.astype(BF16), xdq[:, g * GW:(g + 1) * GW]))
    y_dir = jnp.concatenate(y_parts, axis=1)

    row0 = pl.multiple_of(c * L, L)

    @pl.when(fwd)
    def _():
        yf_scr[pl.ds(row0, L), :] = y_dir

    @pl.when(jnp.logical_not(fwd))
    def _():
        y = yf_scr[pl.ds(row0, L), :] + y_dir + dskip_ref[...] * xs
        y = y * _silu(z_ref[0])
        ms = jnp.mean(y * y, axis=-1, keepdims=True)
        o_ref[0] = (y * lax.rsqrt(ms + EPS) * nw_ref[...]).astype(BF16)


def _ssd(proj, small, small_t, conv_w, conv_b, dt_bias, a_log, d_skip, norm_w, *, col_xbc, col_z):
    bsz, s, _ = proj.shape
    L = SSD_CHUNK
    nc = s // L
    cw = BRANCH_W + 2 * SSD_GROUPS * SSD_STATE
    hb = L // HALO
    xb, zb = col_xbc // cw, col_z // BRANCH_W
    assert col_xbc % cw == 0 and col_z % BRANCH_W == 0

    def chunk(t):
        return _chunk_of_step(t, nc)

    def direction(t):
        return jnp.where(t < nc, 0, 1)

    pad = lambda v: jnp.zeros((2, 1, LANES), F32).at[:, 0, :SSD_HEADS].set(v)
    pad_c = lambda v: jnp.zeros((2, LANES, 1), F32).at[:, :SSD_HEADS, 0].set(v)
    return pl.pallas_call(
        functools.partial(_ssd_kernel, nc=nc),
        out_shape=jax.ShapeDtypeStruct((bsz, s, BRANCH_W), BF16),
        grid=(bsz, 2 * nc),
        in_specs=[
            pl.BlockSpec((1, L, cw), lambda b, t: (b, chunk(t), xb)),
            pl.BlockSpec((1, HALO, cw), lambda b, t: (b, jnp.maximum(chunk(t) * hb - 1, 0), xb)),
            pl.BlockSpec((1, HALO, cw), lambda b, t: (b, jnp.minimum((chunk(t) + 1) * hb, s // HALO - 1), xb)),
            pl.BlockSpec((1, L, BRANCH_W), lambda b, t: (b, chunk(t), zb)),
            pl.BlockSpec((1, L, LANES), lambda b, t: (b, chunk(t), direction(t))),
            pl.BlockSpec((1, LANES, L), lambda b, t: (b, direction(t), chunk(t))),
            pl.BlockSpec((SSD_CONV, cw), lambda b, t: (0, 0)),
            pl.BlockSpec((1, cw), lambda b, t: (0, 0)),
            pl.BlockSpec((1, 1, LANES), lambda b, t: (direction(t), 0, 0)),
            pl.BlockSpec((1, LANES, 1), lambda b, t: (direction(t), 0, 0)),
            pl.BlockSpec((1, 1, LANES), lambda b, t: (direction(t), 0, 0)),
            pl.BlockSpec((1, LANES, 1), lambda b, t: (direction(t), 0, 0)),
            pl.BlockSpec((1, BRANCH_W), lambda b, t: (0, 0)),
            pl.BlockSpec((1, BRANCH_W), lambda b, t: (0, 0)),
        ],
        out_specs=pl.BlockSpec((1, L, BRANCH_W), lambda b, t: (b, jnp.where(t < nc, nc - 1, 2 * nc - 1 - t), 0)),
        scratch_shapes=[
            pltpu.VMEM((L + 2 * HALO, cw), F32),
            pltpu.VMEM((SSD_STATE, BRANCH_W), F32),
            pltpu.VMEM((s, BRANCH_W), F32),
        ],
        compiler_params=_cparams(("arbitrary", "arbitrary")),
        name="bidir_ssd",
    )(proj, proj, proj, proj, small, small_t, conv_w, conv_b.reshape(1, cw),
      pad(dt_bias), pad_c(dt_bias), pad(a_log), pad_c(a_log),
      jnp.repeat(d_skip, SSD_HEAD_DIM).reshape(1, BRANCH_W), norm_w.reshape(1, BRANCH_W))


AB_COL_XBC, AB_COL_K, AB_COL_V, AB_COL_Z, AB_COL_Q, AB_COL_GA, AB_N = 0, 1536, 1792, 2048, 3072, 4096, 5120


def _ab_weights(w_in):
    hq, hk = ATT_HEADS * ATT_HEAD_DIM, ATT_KV_HEADS * ATT_HEAD_DIM
    gn = SSD_GROUPS * SSD_STATE
    o = 0
    q = w_in[:, o:o + hq]; o += hq
    k = w_in[:, o:o + hk]; o += hk
    v = w_in[:, o:o + hk]; o += hk
    ga = w_in[:, o:o + BRANCH_W]; o += BRANCH_W
    xs = w_in[:, o:o + BRANCH_W]; o += BRANCH_W
    bs = w_in[:, o:o + gn]; o += gn
    cs = w_in[:, o:o + gn]; o += gn
    dtf = w_in[:, o:o + SSD_HEADS]; o += SSD_HEADS
    dtb = w_in[:, o:o + SSD_HEADS]; o += SSD_HEADS
    z = w_in[:, o:o + BRANCH_W]
    main = jnp.concatenate([xs, bs, cs, _deinterleave_cols(k, ATT_KV_HEADS), v, z,
                            _deinterleave_cols(q, ATT_HEADS), ga], axis=1).astype(BF16)
    zpad = jnp.zeros((w_in.shape[0], LANES - SSD_HEADS), w_in.dtype)
    small = jnp.concatenate([dtf, zpad, dtb, zpad], axis=1).astype(BF16)
    return main, small


def _layer0(x, mod_l, norm_w, w_in, q_norm, k_norm, conv_w, conv_b, dt_bias_f, dt_bias_b,
            a_log_f, a_log_b, d_skip, ssd_norm, w_out, final_w=None):
    s = x.shape[1]
    w_main, w_small = _ab_weights(w_in)
    proj, small, small_t = _inproj(x, mod_l, norm_w, w_main, w_small)
    cos_t, sin_t = _rope_tables(s)
    att = _attention(proj, cos_t, sin_t, _deinterleave_vec(q_norm), _deinterleave_vec(k_norm),
                     col_q=AB_COL_Q, col_k=AB_COL_K, col_v=AB_COL_V, col_ga=AB_COL_GA)
    ssd = _ssd(proj, small, small_t, conv_w, conv_b, jnp.stack([dt_bias_f, dt_bias_b]),
               jnp.stack([a_log_f, a_log_b]), d_skip, ssd_norm, col_xbc=AB_COL_XBC, col_z=AB_COL_Z)
    return _outproj(att, ssd, w_out.astype(BF16), x, mod_l, final_w), (proj, small, att, ssd)


def _unit_tri_inverse(nmats, n):
    row = lax.broadcasted_iota(jnp.int32, (n, n), 0)
    col = lax.broadcasted_iota(jnp.int32, (n, n), 1)

    def same_block(size):
        return (row // size) == (col // size)

    def mm(a, b):
        return _dot(a.astype(BF16), b.astype(BF16))

    eye = jnp.where(row == col, 1.0, 0.0)
    base = SUBLANES
    blk = same_block(base)
    nd = [jnp.where(blk, m, 0.0) for m in nmats]
    p1 = [mm(x, x) for x in nd]
    p2 = [mm(x, x) for x in p1]
    t = [eye - x for x in nd]
    t = [x + mm(x, p) for x, p in zip(t, p1)]
    t = [x + mm(x, p) for x, p in zip(t, p2)]
    size = base
    while size < n:
        nxt = same_block(2 * size)
        off = nxt & jnp.logical_not(blk)
        et = [mm(jnp.where(off, m, 0.0), x) for m, x in zip(nmats, t)]
        t = [x - mm(x, y) for x, y in zip(t, et)]
        blk = nxt
        size *= 2
    return t


def _l2norm(x):
    return x * lax.rsqrt(jnp.sum(x * x, axis=-1, keepdims=True) + EPS)


def _gdn_kernel(qkv_ref, prev_ref, next_ref, z_ref, sm_ref, smt_ref, cw_ref, cb_ref,
                bias_r_ref, bias_c_ref, alog_r_ref, alog_c_ref, nw_ref,
                o_ref, ext_scr, state_scr, of_scr, *, nc):
    L, HV, HQ, DK = GDN_CHUNK, GDN_V_HEADS, GDN_QK_HEADS, GDN_HEAD_DIM
    rep = HV // HQ
    step = pl.program_id(1)
    fwd = step < nc
    c = _chunk_of_step(step, nc)

    @pl.when((step == 0) | (step == nc))
    def _():
        state_scr[...] = jnp.zeros_like(state_scr)

    _fill_conv_window(ext_scr, qkv_ref[0], prev_ref[0], next_ref[0], c, nc, L)
    qkv = _silu(_centred_conv4(ext_scr, cw_ref, cb_ref, L))
    q = qkv[:, :HQ * DK]
    k = qkv[:, HQ * DK:2 * HQ * DK]
    v = qkv[:, 2 * HQ * DK:]

    sm = sm_ref[0]
    beta = _sigmoid(sm[:, 0:HV])
    g = -jnp.exp(alog_r_ref[0][:, 0:HV]) * _softplus(sm[:, HV:2 * HV] + bias_r_ref[0][:, 0:HV])
    smt = jnp.where(c % 2 == 0, smt_ref[0][HV:2 * HV, 0:L], smt_ref[0][HV:2 * HV, L:2 * L])
    g_t = -jnp.exp(alog_c_ref[0][0:HV, :]) * _softplus(smt + bias_c_ref[0][0:HV, :])
    mask, mask_t, strict = _scan_masks(fwd, L)
    cum = _dot_exact_lhs(jnp.where(mask, 1.0, 0.0).astype(BF16), g)
    cum_t = _dot_exact_rhs(g_t, jnp.where(mask_t, 1.0, 0.0).astype(BF16))
    total = jnp.sum(g, axis=0, keepdims=True)

    heads = range(HV)
    qh = [_l2norm(q[:, h * DK:(h + 1) * DK]) * (DK ** -0.5) for h in range(HQ)]
    kh = [_l2norm(k[:, h * DK:(h + 1) * DK]) for h in range(HQ)]
    kb = [x.astype(BF16) for x in kh]
    kk = [_dot_nt(kb[h], kb[h]) for h in range(HQ)]
    qk = [_dot_nt(qh[h].astype(BF16), kb[h]) for h in range(HQ)]
    colv = [cum[:, h:h + 1] for h in heads]
    dec = [jnp.exp(jnp.where(mask, colv[h] - cum_t[h:h + 1, :], NEG_BIG)) for h in heads]
    bcol = [beta[:, h:h + 1] for h in heads]
    ecol = [jnp.exp(colv[h]) for h in heads]
    t_inv = _unit_tri_inverse([jnp.where(strict, kk[h // rep] * bcol[h] * dec[h], 0.0) for h in heads], L)
    t_inv = [t.astype(BF16) for t in t_inv]
    u = [_dot(t_inv[h], (v[:, h * DK:(h + 1) * DK] * bcol[h]).astype(BF16)) for h in heads]
    w = [_dot(t_inv[h], (kh[h // rep] * (bcol[h] * ecol[h])).astype(BF16)) for h in heads]
    s_prev = [state_scr[:, h * DK:(h + 1) * DK] for h in heads]
    s_b = [x.astype(BF16) for x in s_prev]
    v_new = [(u[h] - _dot(w[h].astype(BF16), s_b[h])).astype(BF16) for h in heads]
    outs = [_dot((qh[h // rep] * ecol[h]).astype(BF16), s_b[h])
            + _dot((qk[h // rep] * dec[h]).astype(BF16), v_new[h]) for h in heads]
    for h in heads:
        tot = total[:, h:h + 1]
        k_dec = kh[h // rep] * jnp.exp(tot - colv[h])
        state_scr[:, h * DK:(h + 1) * DK] = s_prev[h] * jnp.exp(tot) + _dot(k_dec.T.astype(BF16), v_new[h])
    o_dir = jnp.concatenate(outs, axis=1)

    row0 = pl.multiple_of(c * L, L)

    @pl.when(fwd)
    def _():
        of_scr[pl.ds(row0, L), :] = o_dir

    @pl.when(jnp.logical_not(fwd))
    def _():
        o_sum = of_scr[pl.ds(row0, L), :] + o_dir
        zz = z_ref[0]
        for hv in range(HV):
            oh = o_sum[:, hv * DK:(hv + 1) * DK]
            ms = jnp.mean(oh * oh, axis=-1, keepdims=True)
            res = oh * lax.rsqrt(ms + EPS) * nw_ref[...] * _silu(zz[:, hv * DK:(hv + 1) * DK])
            o_ref[0, :, hv * DK:(hv + 1) * DK] = res.astype(BF16)


def _pad_dir_rows(v, n):
    return jnp.zeros((2, 1, LANES), F32).at[:, 0, :n].set(v)


def _pad_dir_cols(v, n):
    return jnp.zeros((2, LANES, 1), F32).at[:, :n, 0].set(v)


def _gdn(proj, small, small_t, conv_w, conv_b, dt_bias, a_log, norm_w, *, col_qkv, col_z):
    bsz, s, _ = proj.shape
    L = GDN_CHUNK
    nc = s // L
    cw = 2 * GDN_QK_HEADS * GDN_HEAD_DIM + BRANCH_W
    hb = L // HALO
    qb, zb = col_qkv // cw, col_z // BRANCH_W
    assert col_qkv % cw == 0 and col_z % BRANCH_W == 0

    def chunk(t):
        return _chunk_of_step(t, nc)

    def direction(t):
        return jnp.where(t < nc, 0, 1)

    hv = GDN_V_HEADS
    return pl.pallas_call(
        functools.partial(_gdn_kernel, nc=nc),
        out_shape=jax.ShapeDtypeStruct((bsz, s, BRANCH_W), BF16),
        grid=(bsz, 2 * nc),
        in_specs=[
            pl.BlockSpec((1, L, cw), lambda b, t: (b, chunk(t), qb)),
            pl.BlockSpec((1, HALO, cw), lambda b, t: (b, jnp.maximum(chunk(t) * hb - 1, 0), qb)),
            pl.BlockSpec((1, HALO, cw), lambda b, t: (b, jnp.minimum((chunk(t) + 1) * hb, s // HALO - 1), qb)),
            pl.BlockSpec((1, L, BRANCH_W), lambda b, t: (b, chunk(t), zb)),
            pl.BlockSpec((1, L, LANES), lambda b, t: (b, chunk(t), direction(t))),
            pl.BlockSpec((1, LANES, 2 * L), lambda b, t: (b, direction(t), chunk(t) // 2)),
            pl.BlockSpec((GDN_CONV, cw), lambda b, t: (0, 0)),
            pl.BlockSpec((1, cw), lambda b, t: (0, 0)),
            pl.BlockSpec((1, 1, LANES), lambda b, t: (direction(t), 0, 0)),
            pl.BlockSpec((1, LANES, 1), lambda b, t: (direction(t), 0, 0)),
            pl.BlockSpec((1, 1, LANES), lambda b, t: (direction(t), 0, 0)),
            pl.BlockSpec((1, LANES, 1), lambda b, t: (direction(t), 0, 0)),
            pl.BlockSpec((1, GDN_HEAD_DIM), lambda b, t: (0, 0)),
        ],
        out_specs=pl.BlockSpec((1, L, BRANCH_W), lambda b, t: (b, jnp.where(t < nc, nc - 1, 2 * nc - 1 - t), 0)),
        scratch_shapes=[
            pltpu.VMEM((L + 2 * HALO, cw), F32),
            pltpu.VMEM((GDN_HEAD_DIM, BRANCH_W), F32),
            pltpu.VMEM((s, BRANCH_W), F32),
        ],
        compiler_params=_cparams(("arbitrary", "arbitrary")),
        name="bidir_gated_deltanet",
    )(proj, proj, proj, proj, small, small_t, conv_w, conv_b.reshape(1, cw),
      _pad_dir_rows(dt_bias, hv), _pad_dir_cols(dt_bias, hv), _pad_dir_rows(a_log, hv), _pad_dir_cols(a_log, hv),
      norm_w.reshape(1, GDN_HEAD_DIM))


def _neg_expm1(t):
    return -jnp.tanh(0.5 * t) * (jnp.exp(t) + 1.0)


def _lru_kernel(xl_ref, prev_ref, next_ref, gl_ref, cw_ref, cb_ref, wa_ref, ba_ref, wx_ref, bx_ref, lam_ref,
                o_ref, ext_scr, a_scr, u_scr, hf_scr, carry_scr, *, nb, bsz, unroll):
    rows, width = a_scr.shape
    steps = rows // bsz
    bw = LRU_WIDTH // LRU_BLOCKS
    step = pl.program_id(1)
    fwd = step < nb
    blk = _chunk_of_step(step, nb)

    @pl.when((step == 0) | (step == nb))
    def _():
        carry_scr[...] = jnp.zeros_like(carry_scr)

    _fill_conv_window(ext_scr, xl_ref[...], prev_ref[...], next_ref[...], blk, nb, rows)
    xc = _centred_conv4(ext_scr, cw_ref, cb_ref, rows, shift_rows=bsz)
    r_parts, i_parts = [], []
    for n in range(width // bw):
        xb = xc[:, n * bw:(n + 1) * bw].astype(BF16)
        r_parts.append(_dot(xb, wa_ref[0, n]))
        i_parts.append(_dot(xb, wx_ref[0, n]))
    r = _sigmoid(jnp.concatenate(r_parts, axis=1) + ba_ref[0])
    i = _sigmoid(jnp.concatenate(i_parts, axis=1) + bx_ref[0])
    log_a = -LRU_C * r * _softplus(-lam_ref[0])
    a_scr[...] = jnp.exp(log_a)
    u_scr[...] = jnp.sqrt(_neg_expm1(2.0 * log_a)) * (i * xc)

    group = unroll * bsz
    n_iter = rows // group
    base_out = pl.multiple_of(blk * rows, rows)

    def scan(ascending):
        def body(it, h):
            pos = it if ascending else n_iter - 1 - it
            r0 = pl.multiple_of(pos * group, group)
            a_blk = a_scr[pl.ds(r0, group), :]
            u_blk = u_scr[pl.ds(r0, group), :]
            hs = [None] * unroll
            for kk in (range(unroll) if ascending else range(unroll - 1, -1, -1)):
                h = a_blk[kk * bsz:(kk + 1) * bsz] * h + u_blk[kk * bsz:(kk + 1) * bsz]
                hs[kk] = h
            a_scr[pl.ds(r0, group), :] = jnp.concatenate(hs, axis=0)
            return h

        carry_scr[0:bsz, :] = lax.fori_loop(0, n_iter, body, carry_scr[0:bsz, :])

    @pl.when(fwd)
    def _():
        scan(True)
        hf_scr[pl.ds(base_out, rows), :] = a_scr[...]

    @pl.when(jnp.logical_not(fwd))
    def _():
        scan(False)
        o_ref[...] = ((hf_scr[pl.ds(base_out, rows), :] + a_scr[...]) * _silu(gl_ref[...])).astype(BF16)


def _lru(lru_in, bsz, conv_w, conv_b, wa, ba, wx, bx, lam, *, tt=256, width=512, unroll=8):
    n_rows = lru_in.shape[0]
    s = n_rows // bsz
    tt = min(tt, s)
    rows = tt * bsz
    nb = n_rows // rows
    w_total = LRU_WIDTH
    bw = w_total // LRU_BLOCKS
    nbw = width // bw
    gcol = w_total // width
    hb = rows // HALO

    def blk(t):
        return _chunk_of_step(t, nb)

    def direction(t):
        return jnp.where(t < nb, 0, 1)

    return pl.pallas_call(
        functools.partial(_lru_kernel, nb=nb, bsz=bsz, unroll=unroll),
        out_shape=jax.ShapeDtypeStruct((n_rows, w_total), BF16),
        grid=(w_total // width, 2 * nb),
        in_specs=[
            pl.BlockSpec((rows, width), lambda j, t: (blk(t), j)),
            pl.BlockSpec((HALO, width), lambda j, t: (jnp.maximum(blk(t) * hb - 1, 0), j)),
            pl.BlockSpec((HALO, width), lambda j, t: (jnp.minimum((blk(t) + 1) * hb, n_rows // HALO - 1), j)),
            pl.BlockSpec((rows, width), lambda j, t: (blk(t), gcol + j)),
            pl.BlockSpec((LRU_CONV, width), lambda j, t: (0, j)),
            pl.BlockSpec((1, width), lambda j, t: (0, j)),
            pl.BlockSpec((1, nbw, bw, bw), lambda j, t: (direction(t), j, 0, 0)),
            pl.BlockSpec((1, 1, width), lambda j, t: (direction(t), 0, j)),
            pl.BlockSpec((1, nbw, bw, bw), lambda j, t: (direction(t), j, 0, 0)),
            pl.BlockSpec((1, 1, width), lambda j, t: (direction(t), 0, j)),
            pl.BlockSpec((1, 1, width), lambda j, t: (direction(t), 0, j)),
        ],
        out_specs=pl.BlockSpec((rows, width), lambda j, t: (jnp.where(t < nb, nb - 1, 2 * nb - 1 - t), j)),
        scratch_shapes=[
            pltpu.VMEM((rows + 2 * HALO, width), F32),
            pltpu.VMEM((rows, width), F32),
            pltpu.VMEM((rows, width), F32),
            pltpu.VMEM((n_rows, width), F32),
            pltpu.VMEM((SUBLANES, width), F32),
        ],
        compiler_params=_cparams(("arbitrary", "arbitrary")),
        name="bidir_rglru",
    )(lru_in, lru_in, lru_in, lru_in, conv_w, conv_b.reshape(1, w_total),
      wa.astype(BF16), ba.reshape(2, 1, w_total), wx.astype(BF16), bx.reshape(2, 1, w_total),
      lam.reshape(2, 1, w_total))


CD_COL_QKV, CD_COL_Z = 0, 2048


def _cd_weights(w_in):
    nqk = GDN_QK_HEADS * GDN_HEAD_DIM
    hv = GDN_V_HEADS
    o = 0
    q = w_in[:, o:o + nqk]; o += nqk
    k = w_in[:, o:o + nqk]; o += nqk
    v = w_in[:, o:o + BRANCH_W]; o += BRANCH_W
    bf = w_in[:, o:o + hv]; o += hv
    bb = w_in[:, o:o + hv]; o += hv
    af = w_in[:, o:o + hv]; o += hv
    ab = w_in[:, o:o + hv]; o += hv
    z = w_in[:, o:o + BRANCH_W]; o += BRANCH_W
    xl = w_in[:, o:o + LRU_WIDTH]; o += LRU_WIDTH
    gl = w_in[:, o:o + LRU_WIDTH]
    main = jnp.concatenate([q, k, v, z], axis=1).astype(BF16)
    zpad = jnp.zeros((w_in.shape[0], LANES - 2 * hv), w_in.dtype)
    small = jnp.concatenate([bf, af, zpad, bb, ab, zpad], axis=1).astype(BF16)
    lru = jnp.concatenate([xl, gl], axis=1).astype(BF16)
    return main, small, lru


def _layer1(x, mod_l, norm_w, w_in, conv_w, conv_b, a_log_f, a_log_b, dt_bias_f, dt_bias_b, gdn_norm,
            lru_conv_w, lru_conv_b, wa_f, ba_f, wx_f, bx_f, lam_f, wa_b, ba_b, wx_b, bx_b, lam_b, w_out,
            final_w=None):
    bsz, s, _ = x.shape
    w_main, w_small, w_lru = _cd_weights(w_in)
    proj, small, small_t = _inproj(x, mod_l, norm_w, w_main, w_small)
    lru_in = _inproj(x, mod_l, norm_w, w_lru, time_major=True)
    gdn = _gdn(proj, small, small_t, conv_w, conv_b, jnp.stack([dt_bias_f, dt_bias_b]),
               jnp.stack([a_log_f, a_log_b]), gdn_norm, col_qkv=CD_COL_QKV, col_z=CD_COL_Z)
    lru = _lru(lru_in.reshape(s * bsz, 2 * LRU_WIDTH), bsz, lru_conv_w, lru_conv_b,
               jnp.stack([wa_f, wa_b]), jnp.stack([ba_f, ba_b]), jnp.stack([wx_f, wx_b]),
               jnp.stack([bx_f, bx_b]), jnp.stack([lam_f, lam_b]))
    out = _outproj(gdn, lru.reshape(s, bsz * LRU_WIDTH), w_out.astype(BF16), x, mod_l, final_w, yb_time_major=True)
    return out, (proj, small, gdn, lru)


def kernel(x, c, w_mod, b_mod, norm_w, ab_w_in, ab_q_norm, ab_k_norm, ab_conv_w, ab_conv_b, ab_dt_bias_f, ab_dt_bias_b, ab_a_log_f, ab_a_log_b, ab_d_skip, ab_ssd_norm, ab_w_out, cd_w_in, cd_conv_w, cd_conv_b, cd_a_log_f, cd_a_log_b, cd_dt_bias_f, cd_dt_bias_b, cd_gdn_norm, cd_lru_conv_w, cd_lru_conv_b, cd_lru_wa_f, cd_lru_ba_f, cd_lru_wx_f, cd_lru_bx_f, cd_lru_lam_f, cd_lru_wa_b, cd_lru_ba_b, cd_lru_wx_b, cd_lru_bx_b, cd_lru_lam_b, cd_w_out, final_norm_w):
    mods = _modulation(c, w_mod, b_mod)
    x1, _ = _layer0(x, mods[0], norm_w[0], ab_w_in[0], ab_q_norm[0], ab_k_norm[0], ab_conv_w[0], ab_conv_b[0],
                    ab_dt_bias_f[0], ab_dt_bias_b[0], ab_a_log_f[0], ab_a_log_b[0], ab_d_skip[0],
                    ab_ssd_norm[0], ab_w_out[0])
    out, _ = _layer1(x1, mods[1], norm_w[1], cd_w_in[0], cd_conv_w[0], cd_conv_b[0], cd_a_log_f[0], cd_a_log_b[0],
                     cd_dt_bias_f[0], cd_dt_bias_b[0], cd_gdn_norm[0], cd_lru_conv_w[0], cd_lru_conv_b[0],
                     cd_lru_wa_f[0], cd_lru_ba_f[0], cd_lru_wx_f[0], cd_lru_bx_f[0], cd_lru_lam_f[0],
                     cd_lru_wa_b[0], cd_lru_ba_b[0], cd_lru_wx_b[0], cd_lru_bx_b[0], cd_lru_lam_b[0],
                     cd_w_out[0], final_norm_w)
    return out
```

```python
import functools
import math

import jax
import jax.numpy as jnp
from jax import lax
from jax.experimental import pallas as pl
from jax.experimental.pallas import tpu as pltpu

F32 = jnp.float32
BF16 = jnp.bfloat16

D_MODEL = 2048
GRID_W = 64
EPS = 1e-6
BRANCH_W = D_MODEL // 2
ATT_HEAD_DIM = 128
ATT_HEADS = BRANCH_W // ATT_HEAD_DIM
ATT_KV_HEADS = ATT_HEADS // 4
ATT_GROUP = ATT_HEADS // ATT_KV_HEADS
ROPE_THETA = 10000.0
SSD_HEAD_DIM = 64
SSD_HEADS = BRANCH_W // SSD_HEAD_DIM
SSD_GROUPS = 2
SSD_STATE = 128
SSD_CONV = 4
SSD_CHUNK = 128
GDN_HEAD_DIM = 128
GDN_V_HEADS = BRANCH_W // GDN_HEAD_DIM
GDN_QK_HEADS = GDN_V_HEADS // 2
GDN_CONV = 4
GDN_CHUNK = 64
LRU_WIDTH = BRANCH_W
LRU_BLOCKS = 8
LRU_CONV = 4
LRU_C = 8.0

LANES = 128
SUBLANES = 8
VMEM_LIMIT_BYTES = 56 * 1024 * 1024

HALO = SUBLANES
NEG_BIG = -1e30


def _cparams(sem):
    return pltpu.CompilerParams(dimension_semantics=sem, vmem_limit_bytes=VMEM_LIMIT_BYTES)


def _silu(x):
    return x / (1.0 + jnp.exp(-x))


def _sigmoid(x):
    return 1.0 / (1.0 + jnp.exp(-x))


def _softplus(x):
    return jnp.maximum(x, 0.0) + jnp.log(1.0 + jnp.exp(-jnp.abs(x)))


def _split_bf16(a):
    hi = a.astype(BF16)
    lo = (a - hi.astype(F32)).astype(BF16)
    return hi, lo


def _dot(a, b):
    return jnp.dot(a, b, preferred_element_type=F32)


def _dot_nt(a, b):
    return lax.dot_general(a, b, (((1,), (1,)), ((), ())), preferred_element_type=F32)


def _dot_exact_rhs(a, b_bf16):
    hi, lo = _split_bf16(a)
    return _dot(hi, b_bf16) + _dot(lo, b_bf16)


def _dot_exact_lhs(a_bf16, b):
    hi, lo = _split_bf16(b)
    return _dot(a_bf16, hi) + _dot(a_bf16, lo)


def _mod_kernel(c_ref, w_ref, b_ref, o_ref):
    cond = _silu(c_ref[...])
    c_hi, c_lo = _split_bf16(cond)
    w = w_ref[0]
    w_hi, w_lo = _split_bf16(w)
    o_ref[0] = _dot(c_hi, w_hi) + _dot(c_lo, w_hi) + _dot(c_hi, w_lo) + b_ref[0]


def _modulation(c, w_mod, b_mod):
    depth, d, n = w_mod.shape
    bsz = c.shape[0]
    rows = -(-bsz // SUBLANES) * SUBLANES
    c_pad = jnp.zeros((rows, d), F32).at[:bsz].set(c)
    tn = 1024
    out = pl.pallas_call(
        _mod_kernel,
        out_shape=jax.ShapeDtypeStruct((depth, rows, n), F32),
        grid=(depth, n // tn),
        in_specs=[
            pl.BlockSpec((rows, d), lambda l, j: (0, 0)),
            pl.BlockSpec((1, d, tn), lambda l, j: (l, 0, j)),
            pl.BlockSpec((1, 1, tn), lambda l, j: (l, 0, j)),
        ],
        out_specs=pl.BlockSpec((1, rows, tn), lambda l, j: (l, 0, j)),
        compiler_params=_cparams(("arbitrary", "arbitrary")),
        name="adaln_mod",
    )(c_pad, w_mod, b_mod.reshape(depth, 1, n))
    return out[:, :bsz].reshape(depth, bsz, 3, d)


def _inproj_kernel(x_ref, mod_ref, nw_ref, w_ref, *rest, has_small, time_major):
    if has_small:
        ws_ref, wst_ref, o_ref, os_ref, ost_ref, h_scr = rest
    else:
        o_ref, h_scr = rest
    j = pl.program_id(2)

    @pl.when(j == 0)
    def _():
        x = x_ref[0]
        ms = jnp.mean(x * x, axis=-1, keepdims=True)
        y = x * lax.rsqrt(ms + EPS) * nw_ref[...]
        shift = mod_ref[0, 0:1, :]
        scale = mod_ref[0, 1:2, :]
        h = (y * (1.0 + scale) + shift).astype(BF16)
        h_scr[...] = h
        if has_small:
            os_ref[0] = _dot(h, ws_ref[...])
            ost_ref[0] = _dot_nt(wst_ref[...], h)

    res = _dot(h_scr[...], w_ref[...])
    if time_major:
        o_ref[...] = res
    else:
        o_ref[0] = res


def _inproj(x, mod_l, norm_w, w_main, w_small=None, *, time_major=False, tm=1024, tn=1024):
    bsz, s, d = x.shape
    n = w_main.shape[1]
    tm = min(tm, s)
    tn = min(tn, n)
    has_small = w_small is not None
    in_specs = [
        pl.BlockSpec((1, tm, d), lambda b, i, j: (b, i, 0)),
        pl.BlockSpec((1, 3, d), lambda b, i, j: (b, 0, 0)),
        pl.BlockSpec((1, d), lambda b, i, j: (0, 0)),
        pl.BlockSpec((d, tn), lambda b, i, j: (0, j)),
    ]
    args = [x, mod_l, norm_w.reshape(1, d), w_main]
    if time_major:
        out_shape = [jax.ShapeDtypeStruct((s, bsz * n), F32)]
        out_specs = [pl.BlockSpec((tm, tn), lambda b, i, j: (i, b * (n // tn) + j))]
    else:
        out_shape = [jax.ShapeDtypeStruct((bsz, s, n), F32)]
        out_specs = [pl.BlockSpec((1, tm, tn), lambda b, i, j: (b, i, j))]
    if has_small:
        ns = w_small.shape[1]
        in_specs += [pl.BlockSpec((d, ns), lambda b, i, j: (0, 0)),
                     pl.BlockSpec((ns, d), lambda b, i, j: (0, 0))]
        args += [w_small, w_small.T]
        out_shape += [jax.ShapeDtypeStruct((bsz, s, ns), F32), jax.ShapeDtypeStruct((bsz, ns, s), F32)]
        out_specs += [pl.BlockSpec((1, tm, ns), lambda b, i, j: (b, i, 0)),
                      pl.BlockSpec((1, ns, tm), lambda b, i, j: (b, 0, i))]
    outs = pl.pallas_call(
        functools.partial(_inproj_kernel, has_small=has_small, time_major=time_major),
        out_shape=out_shape,
        grid=(bsz, s // tm, n // tn),
        in_specs=in_specs,
        out_specs=out_specs,
        scratch_shapes=[pltpu.VMEM((tm, d), BF16)],
        compiler_params=_cparams(("arbitrary", "arbitrary", "arbitrary")),
        name="norm_mod_inproj",
    )(*args)
    return outs if has_small else outs[0]


def _outproj_kernel(ya_ref, yb_ref, w_ref, x_ref, mod_ref, *rest, final_norm, half):
    if final_norm:
        fnw_ref, o_ref = rest
    else:
        (o_ref,) = rest
    yb = yb_ref[...] if len(yb_ref.shape) == 2 else yb_ref[0]
    acc = _dot(ya_ref[0], w_ref[0:half, :]) + _dot(yb, w_ref[half:, :])
    gate = mod_ref[0, 2:3, :]
    xn = x_ref[0] + gate * acc
    if final_norm:
        ms = jnp.mean(xn * xn, axis=-1, keepdims=True)
        xn = xn * lax.rsqrt(ms + EPS) * fnw_ref[...]
    o_ref[0] = xn


def _outproj(ya, yb, w_out, x, mod_l, final_w=None, *, yb_time_major=False, tm=512):
    bsz, s, d = x.shape
    half = ya.shape[-1]
    tm = min(tm, s)
    final_norm = final_w is not None
    if yb_time_major:
        yb_spec = pl.BlockSpec((tm, half), lambda b, i: (i, b))
    else:
        yb_spec = pl.BlockSpec((1, tm, half), lambda b, i: (b, i, 0))
    in_specs = [
        pl.BlockSpec((1, tm, half), lambda b, i: (b, i, 0)),
        yb_spec,
        pl.BlockSpec((2 * half, d), lambda b, i: (0, 0)),
        pl.BlockSpec((1, tm, d), lambda b, i: (b, i, 0)),
        pl.BlockSpec((1, 3, d), lambda b, i: (b, 0, 0)),
    ]
    args = [ya, yb, w_out, x, mod_l]
    if final_norm:
        in_specs.append(pl.BlockSpec((1, d), lambda b, i: (0, 0)))
        args.append(final_w.reshape(1, d))
    return pl.pallas_call(
        functools.partial(_outproj_kernel, final_norm=final_norm, half=half),
        out_shape=jax.ShapeDtypeStruct((bsz, s, d), F32),
        grid=(bsz, s // tm),
        in_specs=in_specs,
        out_specs=pl.BlockSpec((1, tm, d), lambda b, i: (b, i, 0)),
        compiler_params=_cparams(("arbitrary", "arbitrary")),
        name="outproj_residual",
    )(*args)


def _rms_rope(x, nw, cos, sin_signed):
    ms = jnp.mean(x * x, axis=-1, keepdims=True)
    y = x * lax.rsqrt(ms + EPS) * nw
    return y * cos + pltpu.roll(y, ATT_HEAD_DIM // 2, 1) * sin_signed


def _attn_kernel(q_ref, k_ref, v_ref, ga_ref, cq_ref, sq_ref, ck_ref, sk_ref, qn_ref, kn_ref,
                 o_ref, k_scr, v_scr, *, tq):
    qi = pl.program_id(2)

    dh = ATT_HEAD_DIM

    @pl.when(qi == 0)
    def _():
        k_scr[...] = _rms_rope(k_ref[0], kn_ref[...], ck_ref[...], sk_ref[...]).astype(BF16)
        v_scr[:, 0:dh] = v_ref[0].astype(BF16)
        v_scr[:, dh:2 * dh] = jnp.ones((v_scr.shape[0], dh), BF16)

    scale = dh ** -0.5
    cos = cq_ref[...]
    sin = sq_ref[...]

    def scores(g):
        qg = (_rms_rope(q_ref[0, :, g * dh:(g + 1) * dh], qn_ref[...], cos, sin) * scale).astype(BF16)
        return _dot_nt(qg, k_scr[...])

    s_next = scores(0)
    for g in range(ATT_GROUP):
        s = s_next
        if g + 1 < ATT_GROUP:
            s_next = scores(g + 1)
        m = jnp.max(s, axis=-1, keepdims=True)
        p = jnp.exp((s - m).astype(BF16))
        o_ext = _dot(p, v_scr[...])
        og = o_ext[:, 0:dh] / o_ext[:, dh:2 * dh] * _silu(ga_ref[0, :, g * dh:(g + 1) * dh])
        o_ref[0, :, g * dh:(g + 1) * dh] = og.astype(BF16)


def _attention(proj, cos_t, sin_t, q_norm, k_norm, *, col_q, col_k, col_v, col_ga, tq=512):
    bsz, s, _ = proj.shape
    tq = min(tq, s)
    gw = ATT_GROUP * ATT_HEAD_DIM
    dh = ATT_HEAD_DIM
    qb, kb, vb, gb = col_q // gw, col_k // dh, col_v // dh, col_ga // gw
    return pl.pallas_call(
        functools.partial(_attn_kernel, tq=tq),
        out_shape=jax.ShapeDtypeStruct((bsz, s, BRANCH_W), BF16),
        grid=(bsz, ATT_KV_HEADS, s // tq),
        in_specs=[
            pl.BlockSpec((1, tq, gw), lambda b, h, i: (b, i, qb + h)),
            pl.BlockSpec((1, s, dh), lambda b, h, i: (b, 0, kb + h)),
            pl.BlockSpec((1, s, dh), lambda b, h, i: (b, 0, vb + h)),
            pl.BlockSpec((1, tq, gw), lambda b, h, i: (b, i, gb + h)),
            pl.BlockSpec((tq, dh), lambda b, h, i: (i, 0)),
            pl.BlockSpec((tq, dh), lambda b, h, i: (i, 0)),
            pl.BlockSpec((s, dh), lambda b, h, i: (0, 0)),
            pl.BlockSpec((s, dh), lambda b, h, i: (0, 0)),
            pl.BlockSpec((1, dh), lambda b, h, i: (0, 0)),
            pl.BlockSpec((1, dh), lambda b, h, i: (0, 0)),
        ],
        out_specs=pl.BlockSpec((1, tq, gw), lambda b, h, i: (b, i, h)),
        scratch_shapes=[pltpu.VMEM((s, dh), BF16), pltpu.VMEM((s, 2 * dh), BF16)],
        compiler_params=_cparams(("arbitrary", "arbitrary", "arbitrary")),
        name="gqa_attention",
    )(proj, proj, proj, proj, cos_t, sin_t, cos_t, sin_t, q_norm, k_norm)


def _rope_tables(s):
    t = jnp.arange(s, dtype=jnp.int32)
    row = (t // GRID_W).astype(F32)
    col = (t % GRID_W).astype(F32)
    n_pairs = ATT_HEAD_DIM // 4
    freqs = ROPE_THETA ** (-jnp.arange(n_pairs, dtype=F32) / n_pairs)
    ang = jnp.concatenate([row[:, None] * freqs, col[:, None] * freqs], axis=-1)
    cos, sin = jnp.cos(ang), jnp.sin(ang)
    return jnp.concatenate([cos, cos], axis=-1), jnp.concatenate([-sin, sin], axis=-1)


def _deinterleave_cols(w, heads):
    d = w.shape[0]
    return w.reshape(d, heads, ATT_HEAD_DIM // 2, 2).transpose(0, 1, 3, 2).reshape(d, heads * ATT_HEAD_DIM)


def _deinterleave_vec(v):
    return v.reshape(ATT_HEAD_DIM // 2, 2).T.reshape(1, ATT_HEAD_DIM)


def _chunk_of_step(step, nc):
    return jnp.where(step < nc, step, 2 * nc - 1 - step)


def _fill_conv_window(ext_scr, cur, prev, nxt, c, nc, rows):
    ext_scr[0:HALO, :] = jnp.where(c > 0, prev, 0.0)
    ext_scr[HALO:HALO + rows, :] = cur
    ext_scr[HALO + rows:HALO + rows + HALO, :] = jnp.where(c < nc - 1, nxt, 0.0)


def _centred_conv4(ext_scr, cw_ref, cb_ref, rows, shift_rows=1):
    acc = cb_ref[...]
    for k in range(4):
        acc = acc + ext_scr[pl.ds(HALO + (k - 2) * shift_rows, rows), :] * cw_ref[k:k + 1, :]
    return acc


def _scan_masks(fwd, n):
    row = lax.broadcasted_iota(jnp.int32, (n, n), 0)
    col = lax.broadcasted_iota(jnp.int32, (n, n), 1)
    sgn = jnp.where(fwd, 1, -1)
    d = (row - col) * sgn
    return d >= 0, d <= 0, d > 0


def _head_expander(heads, width):
    r = lax.broadcasted_iota(jnp.int32, (heads, heads * width), 0)
    c = lax.broadcasted_iota(jnp.int32, (heads, heads * width), 1)
    return jnp.where((c >= r * width) & (c < (r + 1) * width), 1.0, 0.0).astype(BF16)


def _ssd_kernel(xbc_ref, prev_ref, next_ref, z_ref, dt_ref, dtt_ref, cw_ref, cb_ref,
                bias_r_ref, bias_c_ref, alog_r_ref, alog_c_ref, dskip_ref, nw_ref,
                o_ref, ext_scr, state_scr, yf_scr, xbc_scr, *, nc):
    L, H, P, N = SSD_CHUNK, SSD_HEADS, SSD_HEAD_DIM, SSD_STATE
    HG = H // SSD_GROUPS
    GW = HG * P
    step = pl.program_id(1)
    fwd = step < nc
    c = _chunk_of_step(step, nc)

    @pl.when((step == 0) | (step == nc))
    def _():
        state_scr[...] = jnp.zeros_like(state_scr)

    row0 = pl.multiple_of(c * L, L)

    @pl.when(fwd)
    def _():
        _fill_conv_window(ext_scr, xbc_ref[0], prev_ref[0], next_ref[0], c, nc, L)
        xbc_scr[pl.ds(row0, L), :] = _silu(_centred_conv4(ext_scr, cw_ref, cb_ref, L))

    xbc = xbc_scr[pl.ds(row0, L), :]
    xs = xbc[:, :BRANCH_W]
    bs = xbc[:, BRANCH_W:BRANCH_W + SSD_GROUPS * N]
    cs = xbc[:, BRANCH_W + SSD_GROUPS * N:]

    dt = _softplus(dt_ref[0][:, 0:H] + bias_r_ref[0][:, 0:H])
    a = dt * (-jnp.exp(alog_r_ref[0][:, 0:H]))
    dt_t = _softplus(dtt_ref[0][0:H, :] + bias_c_ref[0][0:H, :])
    a_t = dt_t * (-jnp.exp(alog_c_ref[0][0:H, :]))
    mask, mask_t, _ = _scan_masks(fwd, L)
    cum = _dot_exact_lhs(jnp.where(mask, 1.0, 0.0).astype(BF16), a)
    cum_t = _dot_exact_rhs(a_t, jnp.where(mask_t, 1.0, 0.0).astype(BF16))
    total = jnp.sum(a, axis=0, keepdims=True)

    expand = _head_expander(H, P)
    dt_e = _dot_exact_rhs(dt, expand)
    p_e = _dot_exact_rhs(jnp.exp(cum), expand)
    q_e = _dot_exact_rhs(jnp.exp(total - cum), expand)
    tot_e = _dot_exact_rhs(jnp.broadcast_to(jnp.exp(total), (SUBLANES, H)), expand)[0:1, :]
    xd = xs * dt_e
    xdq = (xd * q_e).astype(BF16)
    xd_b = xd.astype(BF16)
    lane = lax.broadcasted_iota(jnp.int32, (L, 2 * P), 1)

    y_parts = []
    for g in range(SSD_GROUPS):
        cg = cs[:, g * N:(g + 1) * N].astype(BF16)
        bg = bs[:, g * N:(g + 1) * N]
        gmat = _dot_nt(cg, bg.astype(BF16))
        h_prev = state_scr[:, g * GW:(g + 1) * GW]
        y_off = _dot(cg, h_prev.astype(BF16)) * p_e[:, g * GW:(g + 1) * GW]
        pairs = []
        for hp in range(HG // 2):
            h0 = g * HG + 2 * hp
            xpair = xd_b[:, h0 * P:(h0 + 2) * P]
            ys = []
            for h in (h0, h0 + 1):
                dec = jnp.exp(jnp.where(mask, cum[:, h:h + 1] - cum_t[h:h + 1, :], NEG_BIG))
                ys.append(_dot((gmat * dec).astype(BF16), xpair))
            pairs.append(jnp.where(lane < P, ys[0], ys[1]))
        y_parts.append(jnp.concatenate(pairs, axis=1) + y_off)
        state_scr[:, g * GW:(g + 1) * GW] = (
            h_prev * tot_e[:, g * GW:(g + 1) * GW] + _dot(bg.T.astype(BF16), xdq[:, g * GW:(g + 1) * GW]))
    y_dir = jnp.concatenate(y_parts, axis=1)

    @pl.when(fwd)
    def _():
        yf_scr[pl.ds(row0, L), :] = y_dir

    @pl.when(jnp.logical_not(fwd))
    def _():
        y = yf_scr[pl.ds(row0, L), :] + y_dir + dskip_ref[...] * xs
        y = y * _silu(z_ref[0])
        ms = jnp.mean(y * y, axis=-1, keepdims=True)
        o_ref[0] = (y * lax.rsqrt(ms + EPS) * nw_ref[...]).astype(BF16)


def _ssd(proj, small, small_t, conv_w, conv_b, dt_bias, a_log, d_skip, norm_w, *, col_xbc, col_z):
    bsz, s, _ = proj.shape
    L = SSD_CHUNK
    nc = s // L
    cw = BRANCH_W + 2 * SSD_GROUPS * SSD_STATE
    hb = L // HALO
    xb, zb = col_xbc // cw, col_z // BRANCH_W
    assert col_xbc % cw == 0 and col_z % BRANCH_W == 0

    def chunk(t):
        return _chunk_of_step(t, nc)

    def direction(t):
        return jnp.where(t < nc, 0, 1)

    def conv_chunk(t):
        return jnp.minimum(t, nc - 1)

    pad = lambda v: jnp.zeros((2, 1, LANES), F32).at[:, 0, :SSD_HEADS].set(v)
    pad_c = lambda v: jnp.zeros((2, LANES, 1), F32).at[:, :SSD_HEADS, 0].set(v)
    return pl.pallas_call(
        functools.partial(_ssd_kernel, nc=nc),
        out_shape=jax.ShapeDtypeStruct((bsz, s, BRANCH_W), BF16),
        grid=(bsz, 2 * nc),
        in_specs=[
            pl.BlockSpec((1, L, cw), lambda b, t: (b, conv_chunk(t), xb)),
            pl.BlockSpec((1, HALO, cw), lambda b, t: (b, jnp.maximum(conv_chunk(t) * hb - 1, 0), xb)),
            pl.BlockSpec((1, HALO, cw), lambda b, t: (b, jnp.minimum((conv_chunk(t) + 1) * hb, s // HALO - 1), xb)),
            pl.BlockSpec((1, L, BRANCH_W), lambda b, t: (b, chunk(t), zb)),
            pl.BlockSpec((1, L, LANES), lambda b, t: (b, chunk(t), direction(t))),
            pl.BlockSpec((1, LANES, L), lambda b, t: (b, direction(t), chunk(t))),
            pl.BlockSpec((SSD_CONV, cw), lambda b, t: (0, 0)),
            pl.BlockSpec((1, cw), lambda b, t: (0, 0)),
            pl.BlockSpec((1, 1, LANES), lambda b, t: (direction(t), 0, 0)),
            pl.BlockSpec((1, LANES, 1), lambda b, t: (direction(t), 0, 0)),
            pl.BlockSpec((1, 1, LANES), lambda b, t: (direction(t), 0, 0)),
            pl.BlockSpec((1, LANES, 1), lambda b, t: (direction(t), 0, 0)),
            pl.BlockSpec((1, BRANCH_W), lambda b, t: (0, 0)),
            pl.BlockSpec((1, BRANCH_W), lambda b, t: (0, 0)),
        ],
        out_specs=pl.BlockSpec((1, L, BRANCH_W), lambda b, t: (b, jnp.where(t < nc, nc - 1, 2 * nc - 1 - t), 0)),
        scratch_shapes=[
            pltpu.VMEM((L + 2 * HALO, cw), F32),
            pltpu.VMEM((SSD_STATE, BRANCH_W), F32),
            pltpu.VMEM((s, BRANCH_W), F32),
            pltpu.VMEM((s, cw), F32),
        ],
        compiler_params=_cparams(("arbitrary", "arbitrary")),
        name="bidir_ssd",
    )(proj, proj, proj, proj, small, small_t, conv_w, conv_b.reshape(1, cw),
      pad(dt_bias), pad_c(dt_bias), pad(a_log), pad_c(a_log),
      jnp.repeat(d_skip, SSD_HEAD_DIM).reshape(1, BRANCH_W), norm_w.reshape(1, BRANCH_W))


AB_COL_XBC, AB_COL_K, AB_COL_V, AB_COL_Z, AB_COL_Q, AB_COL_GA, AB_N = 0, 1536, 1792, 2048, 3072, 4096, 5120


def _ab_weights(w_in):
    hq, hk = ATT_HEADS * ATT_HEAD_DIM, ATT_KV_HEADS * ATT_HEAD_DIM
    gn = SSD_GROUPS * SSD_STATE
    o = 0
    q = w_in[:, o:o + hq]; o += hq
    k = w_in[:, o:o + hk]; o += hk
    v = w_in[:, o:o + hk]; o += hk
    ga = w_in[:, o:o + BRANCH_W]; o += BRANCH_W
    xs = w_in[:, o:o + BRANCH_W]; o += BRANCH_W
    bs = w_in[:, o:o + gn]; o += gn
    cs = w_in[:, o:o + gn]; o += gn
    dtf = w_in[:, o:o + SSD_HEADS]; o += SSD_HEADS
    dtb = w_in[:, o:o + SSD_HEADS]; o += SSD_HEADS
    z = w_in[:, o:o + BRANCH_W]
    main = jnp.concatenate([xs, bs, cs, _deinterleave_cols(k, ATT_KV_HEADS), v, z,
                            _deinterleave_cols(q, ATT_HEADS), ga], axis=1).astype(BF16)
    zpad = jnp.zeros((w_in.shape[0], LANES - SSD_HEADS), w_in.dtype)
    small = jnp.concatenate([dtf, zpad, dtb, zpad], axis=1).astype(BF16)
    return main, small


def _layer0(x, mod_l, norm_w, w_in, q_norm, k_norm, conv_w, conv_b, dt_bias_f, dt_bias_b,
            a_log_f, a_log_b, d_skip, ssd_norm, w_out, final_w=None):
    s = x.shape[1]
    w_main, w_small = _ab_weights(w_in)
    proj, small, small_t = _inproj(x, mod_l, norm_w, w_main, w_small)
    cos_t, sin_t = _rope_tables(s)
    att = _attention(proj, cos_t, sin_t, _deinterleave_vec(q_norm), _deinterleave_vec(k_norm),
                     col_q=AB_COL_Q, col_k=AB_COL_K, col_v=AB_COL_V, col_ga=AB_COL_GA)
    ssd = _ssd(proj, small, small_t, conv_w, conv_b, jnp.stack([dt_bias_f, dt_bias_b]),
               jnp.stack([a_log_f, a_log_b]), d_skip, ssd_norm, col_xbc=AB_COL_XBC, col_z=AB_COL_Z)
    return _outproj(att, ssd, w_out.astype(BF16), x, mod_l, final_w), (proj, small, att, ssd)


def _unit_tri_inverse(nmats, n):
    row = lax.broadcasted_iota(jnp.int32, (n, n), 0)
    col = lax.broadcasted_iota(jnp.int32, (n, n), 1)

    def same_block(size):
        return (row // size) == (col // size)

    def mm(a, b):
        return _dot(a.astype(BF16), b.astype(BF16))

    eye = jnp.where(row == col, 1.0, 0.0)
    base = SUBLANES
    blk = same_block(base)
    nd = [jnp.where(blk, m, 0.0) for m in nmats]
    p1 = [mm(x, x) for x in nd]
    p2 = [mm(x, x) for x in p1]
    t = [eye - x for x in nd]
    t = [x + mm(x, p) for x, p in zip(t, p1)]
    t = [x + mm(x, p) for x, p in zip(t, p2)]
    size = base
    while size < n:
        nxt = same_block(2 * size)
        off = nxt & jnp.logical_not(blk)
        et = [mm(jnp.where(off, m, 0.0), x) for m, x in zip(nmats, t)]
        t = [x - mm(x, y) for x, y in zip(t, et)]
        blk = nxt
        size *= 2
    return t


def _l2norm(x):
    return x * lax.rsqrt(jnp.sum(x * x, axis=-1, keepdims=True) + EPS)


def _gdn_kernel(qf_ref, pf_ref, nf_ref, qb_ref, pb_ref, nb_ref, z_ref, smf_ref, smtf_ref, smb_ref, smtb_ref,
                cw_ref, cb_ref, bias_r_ref, bias_c_ref, alog_r_ref, alog_c_ref, nw_ref,
                o_ref, ext_scr, state_scr, acc_scr, *, nc, te):
    L, HV, HQ, DK = GDN_CHUNK, GDN_V_HEADS, GDN_QK_HEADS, GDN_HEAD_DIM
    rep = HV // HQ
    t = pl.program_id(1)

    @pl.when(t == 0)
    def _():
        state_scr[...] = jnp.zeros_like(state_scr)

    @pl.when(t < nc)
    def _():
        chunks = (t, nc - 1 - t)
        blocks = ((qf_ref, pf_ref, nf_ref, smf_ref, smtf_ref), (qb_ref, pb_ref, nb_ref, smb_ref, smtb_ref))
        kk, qk, qh, kh, vh, colv, dec, bcol, ecol, tot, nmat = ([] for _ in range(11))
        for d in range(2):
            c = chunks[d]
            q_ref, p_ref, n_ref, sm_ref, smt_ref = blocks[d]
            ext = ext_scr.at[d]
            _fill_conv_window(ext, q_ref[0], p_ref[0], n_ref[0], c, nc, L)
            act = _silu(_centred_conv4(ext, cw_ref, cb_ref, L))
            q_n = [_l2norm(act[:, h * DK:(h + 1) * DK]) * (DK ** -0.5) for h in range(HQ)]
            k_n = [_l2norm(act[:, (HQ + h) * DK:(HQ + h + 1) * DK]) for h in range(HQ)]
            k_b = [x.astype(BF16) for x in k_n]
            kk_d = [_dot_nt(k_b[h], k_b[h]) for h in range(HQ)]
            qk_d = [_dot_nt(q_n[h].astype(BF16), k_b[h]) for h in range(HQ)]

            sm = sm_ref[0]
            beta = _sigmoid(sm[:, 0:HV])
            g = -jnp.exp(alog_r_ref[d][:, 0:HV]) * _softplus(sm[:, HV:2 * HV] + bias_r_ref[d][:, 0:HV])
            smt = jnp.where(c % 2 == 0, smt_ref[0][HV:2 * HV, 0:L], smt_ref[0][HV:2 * HV, L:2 * L])
            g_t = -jnp.exp(alog_c_ref[d][0:HV, :]) * _softplus(smt + bias_c_ref[d][0:HV, :])
            mask, mask_t, strict = _scan_masks(d == 0, L)
            cum = _dot_exact_lhs(jnp.where(mask, 1.0, 0.0).astype(BF16), g)
            cum_t = _dot_exact_rhs(g_t, jnp.where(mask_t, 1.0, 0.0).astype(BF16))
            total = jnp.sum(g, axis=0, keepdims=True)
            for h in range(HV):
                cv = cum[:, h:h + 1]
                dc = jnp.exp(jnp.where(mask, cv - cum_t[h:h + 1, :], NEG_BIG))
                bc = beta[:, h:h + 1]
                kk.append(kk_d[h // rep]); qk.append(qk_d[h // rep])
                qh.append(q_n[h // rep]); kh.append(k_n[h // rep])
                vh.append(act[:, (2 * HQ + h) * DK:(2 * HQ + h + 1) * DK])
                colv.append(cv); dec.append(dc); bcol.append(bc); ecol.append(jnp.exp(cv))
                tot.append(total[:, h:h + 1])
                nmat.append(jnp.where(strict, kk_d[h // rep] * bc * dc, 0.0))

        idx = range(2 * HV)
        t_inv = [x.astype(BF16) for x in _unit_tri_inverse(nmat, L)]
        u = [_dot(t_inv[i], (vh[i] * bcol[i]).astype(BF16)) for i in idx]
        w = [_dot(t_inv[i], (kh[i] * (bcol[i] * ecol[i])).astype(BF16)) for i in idx]
        s_prev = [state_scr[i // HV, :, (i % HV) * DK:(i % HV + 1) * DK] for i in idx]
        s_b = [x.astype(BF16) for x in s_prev]
        v_new = [(u[i] - _dot(w[i].astype(BF16), s_b[i])).astype(BF16) for i in idx]
        outs = [_dot((qh[i] * ecol[i]).astype(BF16), s_b[i]) + _dot((qk[i] * dec[i]).astype(BF16), v_new[i])
                for i in idx]
        for i in idx:
            k_dec = kh[i] * jnp.exp(tot[i] - colv[i])
            state_scr[i // HV, :, (i % HV) * DK:(i % HV + 1) * DK] = (
                s_prev[i] * jnp.exp(tot[i]) + _dot(k_dec.T.astype(BF16), v_new[i]))
        o_dirs = [jnp.concatenate(outs[d * HV:(d + 1) * HV], axis=1) for d in range(2)]
        rows = [pl.multiple_of(c * L, L) for c in chunks]

        @pl.when(t < nc // 2)
        def _():
            for d in range(2):
                acc_scr[pl.ds(rows[d], L), :] = o_dirs[d]

        @pl.when(t >= nc // 2)
        def _():
            for d in range(2):
                acc_scr[pl.ds(rows[d], L), :] = acc_scr[pl.ds(rows[d], L), :] + o_dirs[d]

    @pl.when(t >= nc)
    def _():
        r0 = pl.multiple_of((t - nc) * te, te)
        zz = z_ref[0]
        for hv in range(HV):
            oh = acc_scr[pl.ds(r0, te), hv * DK:(hv + 1) * DK]
            ms = jnp.mean(oh * oh, axis=-1, keepdims=True)
            res = oh * lax.rsqrt(ms + EPS) * nw_ref[...] * _silu(zz[:, hv * DK:(hv + 1) * DK])
            o_ref[0, :, hv * DK:(hv + 1) * DK] = res.astype(BF16)


def _pad_dir_rows(v, n):
    return jnp.zeros((2, 1, LANES), F32).at[:, 0, :n].set(v)


def _pad_dir_cols(v, n):
    return jnp.zeros((2, LANES, 1), F32).at[:, :n, 0].set(v)


def _gdn(proj, small, small_t, conv_w, conv_b, dt_bias, a_log, norm_w, *, col_qkv, col_z, te=256):
    bsz, s, _ = proj.shape
    L = GDN_CHUNK
    nc = s // L
    te = min(te, s)
    ne = s // te
    cw = 2 * GDN_QK_HEADS * GDN_HEAD_DIM + BRANCH_W
    hb = L // HALO
    qb, zb = col_qkv // cw, col_z // BRANCH_W
    assert col_qkv % cw == 0 and col_z % BRANCH_W == 0 and nc % 2 == 0

    def cf(t):
        return jnp.minimum(t, nc - 1)

    def cbk(t):
        return jnp.maximum(nc - 1 - t, 0)

    def ep(t):
        return jnp.maximum(t - nc, 0)

    def qkv_specs(chunk):
        return [
            pl.BlockSpec((1, L, cw), lambda b, t: (b, chunk(t), qb)),
            pl.BlockSpec((1, HALO, cw), lambda b, t: (b, jnp.maximum(chunk(t) * hb - 1, 0), qb)),
            pl.BlockSpec((1, HALO, cw), lambda b, t: (b, jnp.minimum((chunk(t) + 1) * hb, s // HALO - 1), qb)),
        ]

    def small_specs(chunk, d):
        return [
            pl.BlockSpec((1, L, LANES), lambda b, t: (b, chunk(t), d)),
            pl.BlockSpec((1, LANES, 2 * L), lambda b, t: (b, d, chunk(t) // 2)),
        ]

    full = lambda shape: pl.BlockSpec(shape, lambda b, t: (0,) * len(shape))
    hv = GDN_V_HEADS
    return pl.pallas_call(
        functools.partial(_gdn_kernel, nc=nc, te=te),
        out_shape=jax.ShapeDtypeStruct((bsz, s, BRANCH_W), BF16),
        grid=(bsz, nc + ne),
        in_specs=qkv_specs(cf) + qkv_specs(cbk)
        + [pl.BlockSpec((1, te, BRANCH_W), lambda b, t: (b, ep(t), zb))]
        + small_specs(cf, 0) + small_specs(cbk, 1)
        + [full((GDN_CONV, cw)), full((1, cw)), full((2, 1, LANES)), full((2, LANES, 1)),
           full((2, 1, LANES)), full((2, LANES, 1)), full((1, GDN_HEAD_DIM))],
        out_specs=pl.BlockSpec((1, te, BRANCH_W), lambda b, t: (b, ep(t), 0)),
        scratch_shapes=[
            pltpu.VMEM((2, L + 2 * HALO, cw), F32),
            pltpu.VMEM((2, GDN_HEAD_DIM, BRANCH_W), F32),
            pltpu.VMEM((s, BRANCH_W), F32),
        ],
        compiler_params=_cparams(("arbitrary", "arbitrary")),
        name="bidir_gated_deltanet",
    )(proj, proj, proj, proj, proj, proj, proj, small, small_t, small, small_t, conv_w, conv_b.reshape(1, cw),
      _pad_dir_rows(dt_bias, hv), _pad_dir_cols(dt_bias, hv), _pad_dir_rows(a_log, hv), _pad_dir_cols(a_log, hv),
      norm_w.reshape(1, GDN_HEAD_DIM))


def _neg_expm1(t):
    return -jnp.tanh(0.5 * t) * (jnp.exp(t) + 1.0)


def _lru_kernel(xl_ref, prev_ref, next_ref, gl_ref, cw_ref, cb_ref, wa_ref, ba_ref, wx_ref, bx_ref, lam_ref,
                o_ref, ext_scr, a_scr, u_scr, hf_scr, carry_scr, *, nb, bsz, unroll):
    rows, width = a_scr.shape
    steps = rows // bsz
    bw = LRU_WIDTH // LRU_BLOCKS
    step = pl.program_id(1)
    fwd = step < nb
    blk = _chunk_of_step(step, nb)

    @pl.when((step == 0) | (step == nb))
    def _():
        carry_scr[...] = jnp.zeros_like(carry_scr)

    _fill_conv_window(ext_scr, xl_ref[...], prev_ref[...], next_ref[...], blk, nb, rows)
    xc = _centred_conv4(ext_scr, cw_ref, cb_ref, rows, shift_rows=bsz)
    r_parts, i_parts = [], []
    for n in range(width // bw):
        xb = xc[:, n * bw:(n + 1) * bw].astype(BF16)
        r_parts.append(_dot(xb, wa_ref[0, n]))
        i_parts.append(_dot(xb, wx_ref[0, n]))
    r = _sigmoid(jnp.concatenate(r_parts, axis=1) + ba_ref[0])
    i = _sigmoid(jnp.concatenate(i_parts, axis=1) + bx_ref[0])
    log_a = -LRU_C * r * _softplus(-lam_ref[0])
    a_scr[...] = jnp.exp(log_a)
    u_scr[...] = jnp.sqrt(_neg_expm1(2.0 * log_a)) * (i * xc)

    group = unroll * bsz
    n_iter = rows // group
    base_out = pl.multiple_of(blk * rows, rows)

    def scan(ascending):
        def body(it, h):
            pos = it if ascending else n_iter - 1 - it
            r0 = pl.multiple_of(pos * group, group)
            a_blk = a_scr[pl.ds(r0, group), :]
            u_blk = u_scr[pl.ds(r0, group), :]
            hs = [None] * unroll
            for kk in (range(unroll) if ascending else range(unroll - 1, -1, -1)):
                h = a_blk[kk * bsz:(kk + 1) * bsz] * h + u_blk[kk * bsz:(kk + 1) * bsz]
                hs[kk] = h
            a_scr[pl.ds(r0, group), :] = jnp.concatenate(hs, axis=0)
            return h

        carry_scr[0:bsz, :] = lax.fori_loop(0, n_iter, body, carry_scr[0:bsz, :])

    @pl.when(fwd)
    def _():
        scan(True)
        hf_scr[pl.ds(base_out, rows), :] = a_scr[...]

    @pl.when(jnp.logical_not(fwd))
    def _():
        scan(False)
        o_ref[...] = ((hf_scr[pl.ds(base_out, rows), :] + a_scr[...]) * _silu(gl_ref[...])).astype(BF16)


def _lru(lru_in, bsz, conv_w, conv_b, wa, ba, wx, bx, lam, *, tt=256, width=512, unroll=8):
    n_rows = lru_in.shape[0]
    s = n_rows // bsz
    tt = min(tt, s)
    rows = tt * bsz
    nb = n_rows // rows
    w_total = LRU_WIDTH
    bw = w_total // LRU_BLOCKS
    nbw = width // bw
    gcol = w_total // width
    hb = rows // HALO

    def blk(t):
        return _chunk_of_step(t, nb)

    def direction(t):
        return jnp.where(t < nb, 0, 1)

    return pl.pallas_call(
        functools.partial(_lru_kernel, nb=nb, bsz=bsz, unroll=unroll),
        out_shape=jax.ShapeDtypeStruct((n_rows, w_total), BF16),
        grid=(w_total // width, 2 * nb),
        in_specs=[
            pl.BlockSpec((rows, width), lambda j, t: (blk(t), j)),
            pl.BlockSpec((HALO, width), lambda j, t: (jnp.maximum(blk(t) * hb - 1, 0), j)),
            pl.BlockSpec((HALO, width), lambda j, t: (jnp.minimum((blk(t) + 1) * hb, n_rows // HALO - 1), j)),
            pl.BlockSpec((rows, width), lambda j, t: (blk(t), gcol + j)),
            pl.BlockSpec((LRU_CONV, width), lambda j, t: (0, j)),
            pl.BlockSpec((1, width), lambda j, t: (0, j)),
            pl.BlockSpec((1, nbw, bw, bw), lambda j, t: (direction(t), j, 0, 0)),
            pl.BlockSpec((1, 1, width), lambda j, t: (direction(t), 0, j)),
            pl.BlockSpec((1, nbw, bw, bw), lambda j, t: (direction(t), j, 0, 0)),
            pl.BlockSpec((1, 1, width), lambda j, t: (direction(t), 0, j)),
            pl.BlockSpec((1, 1, width), lambda j, t: (direction(t), 0, j)),
        ],
        out_specs=pl.BlockSpec((rows, width), lambda j, t: (jnp.where(t < nb, nb - 1, 2 * nb - 1 - t), j)),
        scratch_shapes=[
            pltpu.VMEM((rows + 2 * HALO, width), F32),
            pltpu.VMEM((rows, width), F32),
            pltpu.VMEM((rows, width), F32),
            pltpu.VMEM((n_rows, width), F32),
            pltpu.VMEM((SUBLANES, width), F32),
        ],
        compiler_params=_cparams(("arbitrary", "arbitrary")),
        name="bidir_rglru",
    )(lru_in, lru_in, lru_in, lru_in, conv_w, conv_b.reshape(1, w_total),
      wa.astype(BF16), ba.reshape(2, 1, w_total), wx.astype(BF16), bx.reshape(2, 1, w_total),
      lam.reshape(2, 1, w_total))


CD_COL_QKV, CD_COL_Z = 0, 2048


def _cd_weights(w_in):
    nqk = GDN_QK_HEADS * GDN_HEAD_DIM
    hv = GDN_V_HEADS
    o = 0
    q = w_in[:, o:o + nqk]; o += nqk
    k = w_in[:, o:o + nqk]; o += nqk
    v = w_in[:, o:o + BRANCH_W]; o += BRANCH_W
    bf = w_in[:, o:o + hv]; o += hv
    bb = w_in[:, o:o + hv]; o += hv
    af = w_in[:, o:o + hv]; o += hv
    ab = w_in[:, o:o + hv]; o += hv
    z = w_in[:, o:o + BRANCH_W]; o += BRANCH_W
    xl = w_in[:, o:o + LRU_WIDTH]; o += LRU_WIDTH
    gl = w_in[:, o:o + LRU_WIDTH]
    main = jnp.concatenate([q, k, v, z], axis=1).astype(BF16)
    zpad = jnp.zeros((w_in.shape[0], LANES - 2 * hv), w_in.dtype)
    small = jnp.concatenate([bf, af, zpad, bb, ab, zpad], axis=1).astype(BF16)
    lru = jnp.concatenate([xl, gl], axis=1).astype(BF16)
    return main, small, lru


def _layer1(x, mod_l, norm_w, w_in, conv_w, conv_b, a_log_f, a_log_b, dt_bias_f, dt_bias_b, gdn_norm,
            lru_conv_w, lru_conv_b, wa_f, ba_f, wx_f, bx_f, lam_f, wa_b, ba_b, wx_b, bx_b, lam_b, w_out,
            final_w=None):
    bsz, s, _ = x.shape
    w_main, w_small, w_lru = _cd_weights(w_in)
    proj, small, small_t = _inproj(x, mod_l, norm_w, w_main, w_small)
    lru_in = _inproj(x, mod_l, norm_w, w_lru, time_major=True)
    gdn = _gdn(proj, small, small_t, conv_w, conv_b, jnp.stack([dt_bias_f, dt_bias_b]),
               jnp.stack([a_log_f, a_log_b]), gdn_norm, col_qkv=CD_COL_QKV, col_z=CD_COL_Z)
    lru = _lru(lru_in.reshape(s * bsz, 2 * LRU_WIDTH), bsz, lru_conv_w, lru_conv_b,
               jnp.stack([wa_f, wa_b]), jnp.stack([ba_f, ba_b]), jnp.stack([wx_f, wx_b]),
               jnp.stack([bx_f, bx_b]), jnp.stack([lam_f, lam_b]))
    out = _outproj(gdn, lru.reshape(s, bsz * LRU_WIDTH), w_out.astype(BF16), x, mod_l, final_w, yb_time_major=True)
    return out, (proj, small, gdn, lru)


def kernel(x, c, w_mod, b_mod, norm_w, ab_w_in, ab_q_norm, ab_k_norm, ab_conv_w, ab_conv_b, ab_dt_bias_f, ab_dt_bias_b, ab_a_log_f, ab_a_log_b, ab_d_skip, ab_ssd_norm, ab_w_out, cd_w_in, cd_conv_w, cd_conv_b, cd_a_log_f, cd_a_log_b, cd_dt_bias_f, cd_dt_bias_b, cd_gdn_norm, cd_lru_conv_w, cd_lru_conv_b, cd_lru_wa_f, cd_lru_ba_f, cd_lru_wx_f, cd_lru_bx_f, cd_lru_lam_f, cd_lru_wa_b, cd_lru_ba_b, cd_lru_wx_b, cd_lru_bx_b, cd_lru_lam_b, cd_w_out, final_norm_w):
    mods = _modulation(c, w_mod, b_mod)
    x1, _ = _layer0(x, mods[0], norm_w[0], ab_w_in[0], ab_q_norm[0], ab_k_norm[0], ab_conv_w[0], ab_conv_b[0],
                    ab_dt_bias_f[0], ab_dt_bias_b[0], ab_a_log_f[0], ab_a_log_b[0], ab_d_skip[0],
                    ab_ssd_norm[0], ab_w_out[0])
    out, _ = _layer1(x1, mods[1], norm_w[1], cd_w_in[0], cd_conv_w[0], cd_conv_b[0], cd_a_log_f[0], cd_a_log_b[0],
                     cd_dt_bias_f[0], cd_dt_bias_b[0], cd_gdn_norm[0], cd_lru_conv_w[0], cd_lru_conv_b[0],
                     cd_lru_wa_f[0], cd_lru_ba_f[0], cd_lru_wx_f[0], cd_lru_bx_f[0], cd_lru_lam_f[0],
                     cd_lru_wa_b[0], cd_lru_ba_b[0], cd_lru_wx_b[0], cd_lru_bx_b[0], cd_lru_lam_b[0],
                     cd_w_out[0], final_norm_w)
    return out
```

```python
import functools
import math

import jax
import jax.numpy as jnp
import numpy as np
from jax import lax
from jax.experimental import pallas as pl
from jax.experimental.pallas import tpu as pltpu

F32 = jnp.float32
BF16 = jnp.bfloat16

D_MODEL = 2048
GRID_W = 64
EPS = 1e-6
BRANCH_W = D_MODEL // 2
ATT_HEAD_DIM = 128
ATT_HEADS = BRANCH_W // ATT_HEAD_DIM
ATT_KV_HEADS = ATT_HEADS // 4
ATT_GROUP = ATT_HEADS // ATT_KV_HEADS
ROPE_THETA = 10000.0
SSD_HEAD_DIM = 64
SSD_HEADS = BRANCH_W // SSD_HEAD_DIM
SSD_GROUPS = 2
SSD_STATE = 128
SSD_CONV = 4
SSD_CHUNK = 128
GDN_HEAD_DIM = 128
GDN_V_HEADS = BRANCH_W // GDN_HEAD_DIM
GDN_QK_HEADS = GDN_V_HEADS // 2
GDN_CONV = 4
GDN_CHUNK = 64
LRU_WIDTH = BRANCH_W
LRU_BLOCKS = 8
LRU_CONV = 4
LRU_C = 8.0

LANES = 128
SUBLANES = 8
VMEM_LIMIT_BYTES = 56 * 1024 * 1024

HALO = SUBLANES
NEG_BIG = -1e30


def _cparams(sem):
    return pltpu.CompilerParams(dimension_semantics=sem, vmem_limit_bytes=VMEM_LIMIT_BYTES)


def _sigmoid(x):
    return 0.5 * jnp.tanh(0.5 * x) + 0.5


def _silu(x):
    return x * _sigmoid(x)


def _softplus(x):
    return jnp.maximum(x, 0.0) + jnp.log(1.0 + jnp.exp(-jnp.abs(x)))


def _split_bf16(a):
    hi = a.astype(BF16)
    lo = (a - hi.astype(F32)).astype(BF16)
    return hi, lo


def _dot(a, b):
    return jnp.dot(a, b, preferred_element_type=F32)


def _dot_nt(a, b):
    return lax.dot_general(a, b, (((1,), (1,)), ((), ())), preferred_element_type=F32)


def _dot_exact_rhs(a, b_bf16):
    hi, lo = _split_bf16(a)
    return _dot(hi, b_bf16) + _dot(lo, b_bf16)


def _dot_exact_lhs(a_bf16, b):
    hi, lo = _split_bf16(b)
    return _dot(a_bf16, hi) + _dot(a_bf16, lo)


def _mod_kernel(c_ref, w_ref, b_ref, o_ref):
    cond = _silu(c_ref[...])
    c_hi, c_lo = _split_bf16(cond)
    w = w_ref[0]
    w_hi, w_lo = _split_bf16(w)
    o_ref[0] = _dot(c_hi, w_hi) + _dot(c_lo, w_hi) + _dot(c_hi, w_lo) + b_ref[0]


def _modulation(c, w_mod, b_mod):
    depth, d, n = w_mod.shape
    bsz = c.shape[0]
    rows = -(-bsz // SUBLANES) * SUBLANES
    c_pad = jnp.zeros((rows, d), F32).at[:bsz].set(c)
    tn = 1024
    out = pl.pallas_call(
        _mod_kernel,
        out_shape=jax.ShapeDtypeStruct((depth, rows, n), F32),
        grid=(depth, n // tn),
        in_specs=[
            pl.BlockSpec((rows, d), lambda l, j: (0, 0)),
            pl.BlockSpec((1, d, tn), lambda l, j: (l, 0, j)),
            pl.BlockSpec((1, 1, tn), lambda l, j: (l, 0, j)),
        ],
        out_specs=pl.BlockSpec((1, rows, tn), lambda l, j: (l, 0, j)),
        compiler_params=_cparams(("arbitrary", "arbitrary")),
        name="adaln_mod",
    )(c_pad, w_mod, b_mod.reshape(depth, 1, n))
    return out[:, :bsz].reshape(depth, bsz, 3, d)


def _inproj_kernel(x_ref, mod_ref, nw_ref, w_ref, wst_ref, o_ref, ost_ref, h_even, h_odd, *, n_groups):
    g = pl.program_id(0)
    j = pl.program_id(1)
    rows = x_ref.shape[1]

    def prep(h_dst):
        x = x_ref[0]
        ms = jnp.mean(x * x, axis=-1, keepdims=True)
        y = x * lax.rsqrt(ms + EPS) * nw_ref[...]
        h = (y * (1.0 + mod_ref[0, 1:2, :]) + mod_ref[0, 0:1, :]).astype(BF16)
        h_dst[pl.ds(pl.multiple_of(j * rows, rows), rows), :] = h
        ost_ref[0] = _dot_nt(wst_ref[...], h)

    @pl.when(g == 0)
    def _():
        prep(h_even)

    @pl.when((g > 0) & (g % 2 == 0))
    def _():
        o_ref[0] = _dot(h_odd[...], w_ref[...])
        prep(h_even)

    @pl.when(g % 2 == 1)
    def _():
        o_ref[0] = _dot(h_even[...], w_ref[...])
        prep(h_odd)


def _inproj(x, mod_l, norm_w, w_main, w_small, *, tm=1024, nj=4):
    bsz, s, d = x.shape
    n = w_main.shape[1]
    ns = w_small.shape[1]
    tm = min(tm, s)
    ni = s // tm
    n_groups = bsz * ni
    tn = n // nj
    rows = tm // nj
    assert n % nj == 0 and tn % LANES == 0 and rows % LANES == 0

    def prep(g, j):
        gc = jnp.minimum(g, n_groups - 1)
        return gc // ni, (gc % ni) * nj + jnp.where(g < n_groups, j, nj - 1)

    def mm(g, j):
        gm = jnp.maximum(g - 1, 0)
        return gm // ni, gm % ni, jnp.where(g > 0, j, 0)

    return pl.pallas_call(
        functools.partial(_inproj_kernel, n_groups=n_groups),
        out_shape=[jax.ShapeDtypeStruct((bsz, s, n), F32), jax.ShapeDtypeStruct((bsz, ns, s), F32)],
        grid=(n_groups + 1, nj),
        in_specs=[
            pl.BlockSpec((1, rows, d), lambda g, j: (*prep(g, j), 0)),
            pl.BlockSpec((1, 3, d), lambda g, j: (prep(g, j)[0], 0, 0)),
            pl.BlockSpec((1, d), lambda g, j: (0, 0)),
            pl.BlockSpec((d, tn), lambda g, j: (0, mm(g, j)[2])),
            pl.BlockSpec((ns, d), lambda g, j: (0, 0)),
        ],
        out_specs=[
            pl.BlockSpec((1, tm, tn), lambda g, j: mm(g, j)),
            pl.BlockSpec((1, ns, rows), lambda g, j: (prep(g, j)[0], 0, prep(g, j)[1])),
        ],
        scratch_shapes=[pltpu.VMEM((tm, d), BF16), pltpu.VMEM((tm, d), BF16)],
        compiler_params=_cparams(("arbitrary", "arbitrary")),
        name="norm_mod_inproj",
    )(x, mod_l, norm_w.reshape(1, d), w_main, w_small.T)


def _outproj_kernel(ya_ref, yb_ref, w_ref, x_ref, mod_ref, *rest, final_norm, half):
    if final_norm:
        fnw_ref, o_ref = rest
    else:
        (o_ref,) = rest
    acc = _dot(ya_ref[0], w_ref[0:half, :]) + _dot(yb_ref[0], w_ref[half:, :])
    gate = mod_ref[0, 2:3, :]
    xn = x_ref[0] + gate * acc
    if final_norm:
        ms = jnp.mean(xn * xn, axis=-1, keepdims=True)
        xn = xn * lax.rsqrt(ms + EPS) * fnw_ref[...]
    o_ref[0] = xn


def _outproj(ya, yb, w_out, x, mod_l, final_w=None, *, tm=512):
    bsz, s, d = x.shape
    half = ya.shape[-1]
    tm = min(tm, s)
    final_norm = final_w is not None
    in_specs = [
        pl.BlockSpec((1, tm, half), lambda b, i: (b, i, 0)),
        pl.BlockSpec((1, tm, half), lambda b, i: (b, i, 0)),
        pl.BlockSpec((2 * half, d), lambda b, i: (0, 0)),
        pl.BlockSpec((1, tm, d), lambda b, i: (b, i, 0)),
        pl.BlockSpec((1, 3, d), lambda b, i: (b, 0, 0)),
    ]
    args = [ya, yb, w_out, x, mod_l]
    if final_norm:
        in_specs.append(pl.BlockSpec((1, d), lambda b, i: (0, 0)))
        args.append(final_w.reshape(1, d))
    return pl.pallas_call(
        functools.partial(_outproj_kernel, final_norm=final_norm, half=half),
        out_shape=jax.ShapeDtypeStruct((bsz, s, d), F32),
        grid=(bsz, s // tm),
        in_specs=in_specs,
        out_specs=pl.BlockSpec((1, tm, d), lambda b, i: (b, i, 0)),
        compiler_params=_cparams(("arbitrary", "arbitrary")),
        name="outproj_residual",
    )(*args)


def _rms_rope(x, nw, cos, sin_signed):
    ms = jnp.mean(x * x, axis=-1, keepdims=True)
    y = x * lax.rsqrt(ms + EPS) * nw
    lane = lax.broadcasted_iota(jnp.int32, y.shape, 1)
    partner = jnp.where(lane % 2 == 0, pltpu.roll(y, ATT_HEAD_DIM - 1, 1), pltpu.roll(y, 1, 1))
    return y * cos + partner * sin_signed


def _attn_kernel(q_ref, k_ref, v_ref, ga_ref, cq_ref, sq_ref, ck_ref, sk_ref, qn_ref, kn_ref,
                 o_ref, k_scr, v_scr, *, tq):
    qi = pl.program_id(2)

    dh = ATT_HEAD_DIM

    @pl.when(qi == 0)
    def _():
        k_scr[...] = _rms_rope(k_ref[0], kn_ref[...], ck_ref[...], sk_ref[...]).astype(BF16)
        v_scr[:, 0:dh] = v_ref[0].astype(BF16)
        v_scr[:, dh:2 * dh] = jnp.ones((v_scr.shape[0], dh), BF16)

    scale = dh ** -0.5
    cos = cq_ref[...]
    sin = sq_ref[...]

    def scores(g):
        qg = (_rms_rope(q_ref[0, :, g * dh:(g + 1) * dh], qn_ref[...], cos, sin) * scale).astype(BF16)
        return _dot_nt(qg, k_scr[...])

    s_next = scores(0)
    for g in range(ATT_GROUP):
        s = s_next
        if g + 1 < ATT_GROUP:
            s_next = scores(g + 1)
        m = jnp.max(s, axis=-1, keepdims=True)
        p = jnp.exp((s - m).astype(BF16))
        o_ext = _dot(p, v_scr[...])
        og = o_ext[:, 0:dh] / o_ext[:, dh:2 * dh] * _silu(ga_ref[0, :, g * dh:(g + 1) * dh])
        o_ref[0, :, g * dh:(g + 1) * dh] = og.astype(BF16)


def _attention(proj, cos_t, sin_t, q_norm, k_norm, *, col_q, col_k, col_v, col_ga, tq=512):
    bsz, s, _ = proj.shape
    tq = min(tq, s)
    gw = ATT_GROUP * ATT_HEAD_DIM
    dh = ATT_HEAD_DIM
    qb, kb, vb, gb = col_q // gw, col_k // dh, col_v // dh, col_ga // gw
    return pl.pallas_call(
        functools.partial(_attn_kernel, tq=tq),
        out_shape=jax.ShapeDtypeStruct((bsz, s, BRANCH_W), BF16),
        grid=(bsz, ATT_KV_HEADS, s // tq),
        in_specs=[
            pl.BlockSpec((1, tq, gw), lambda b, h, i: (b, i, qb + h)),
            pl.BlockSpec((1, s, dh), lambda b, h, i: (b, 0, kb + h)),
            pl.BlockSpec((1, s, dh), lambda b, h, i: (b, 0, vb + h)),
            pl.BlockSpec((1, tq, gw), lambda b, h, i: (b, i, gb + h)),
            pl.BlockSpec((tq, dh), lambda b, h, i: (i, 0)),
            pl.BlockSpec((tq, dh), lambda b, h, i: (i, 0)),
            pl.BlockSpec((s, dh), lambda b, h, i: (0, 0)),
            pl.BlockSpec((s, dh), lambda b, h, i: (0, 0)),
            pl.BlockSpec((1, dh), lambda b, h, i: (0, 0)),
            pl.BlockSpec((1, dh), lambda b, h, i: (0, 0)),
        ],
        out_specs=pl.BlockSpec((1, tq, gw), lambda b, h, i: (b, i, h)),
        scratch_shapes=[pltpu.VMEM((s, dh), BF16), pltpu.VMEM((s, 2 * dh), BF16)],
        compiler_params=_cparams(("arbitrary", "arbitrary", "arbitrary")),
        name="gqa_attention",
    )(proj, proj, proj, proj, cos_t, sin_t, cos_t, sin_t, q_norm, k_norm)


def _rope_tables(s):
    t = np.arange(s)
    row = (t // GRID_W).astype(np.float64)
    col = (t % GRID_W).astype(np.float64)
    n_pairs = ATT_HEAD_DIM // 4
    freqs = ROPE_THETA ** (-np.arange(n_pairs, dtype=np.float64) / n_pairs)
    ang = np.concatenate([row[:, None] * freqs, col[:, None] * freqs], axis=-1)
    cos, sin = np.cos(ang), np.sin(ang)
    cos_t = np.repeat(cos, 2, axis=-1)
    sin_t = np.stack([-sin, sin], axis=-1).reshape(s, ATT_HEAD_DIM)
    return jnp.asarray(cos_t, F32), jnp.asarray(sin_t, F32)


def _chunk_of_step(step, nc):
    return jnp.where(step < nc, step, 2 * nc - 1 - step)


def _fill_conv_window(ext_scr, cur, prev, nxt, c, nc, rows):
    ext_scr[0:HALO, :] = jnp.where(c > 0, prev, 0.0)
    ext_scr[HALO:HALO + rows, :] = cur
    ext_scr[HALO + rows:HALO + rows + HALO, :] = jnp.where(c < nc - 1, nxt, 0.0)


def _centred_conv4(ext_scr, cw_ref, cb_ref, rows, shift_rows=1):
    acc = cb_ref[...]
    for k in range(4):
        acc = acc + ext_scr[pl.ds(HALO + (k - 2) * shift_rows, rows), :] * cw_ref[k:k + 1, :]
    return acc


def _scan_masks(fwd, n):
    row = lax.broadcasted_iota(jnp.int32, (n, n), 0)
    col = lax.broadcasted_iota(jnp.int32, (n, n), 1)
    sgn = jnp.where(fwd, 1, -1)
    d = (row - col) * sgn
    return d >= 0, d <= 0, d > 0


def _head_expander(heads, width):
    r = lax.broadcasted_iota(jnp.int32, (heads, heads * width), 0)
    c = lax.broadcasted_iota(jnp.int32, (heads, heads * width), 1)
    return jnp.where((c >= r * width) & (c < (r + 1) * width), 1.0, 0.0).astype(BF16)


def _ssd_kernel(xbc_ref, prev_ref, next_ref, z_ref, dtt_ref, cw_ref, cb_ref,
                bias_r_ref, bias_c_ref, alog_r_ref, alog_c_ref, dskip_ref, nw_ref,
                o_ref, ext_scr, state_scr, yf_scr, xbc_scr, *, nc):
    L, H, P, N = SSD_CHUNK, SSD_HEADS, SSD_HEAD_DIM, SSD_STATE
    HG = H // SSD_GROUPS
    GW = HG * P
    step = pl.program_id(1)
    fwd = step < nc
    c = _chunk_of_step(step, nc)

    @pl.when((step == 0) | (step == nc))
    def _():
        state_scr[...] = jnp.zeros_like(state_scr)

    row0 = pl.multiple_of(c * L, L)

    @pl.when(fwd)
    def _():
        _fill_conv_window(ext_scr, xbc_ref[0], prev_ref[0], next_ref[0], c, nc, L)
        xbc_scr[pl.ds(row0, L), :] = _silu(_centred_conv4(ext_scr, cw_ref, cb_ref, L))

    xbc = xbc_scr[pl.ds(row0, L), :]
    xs = xbc[:, :BRANCH_W]
    bs = xbc[:, BRANCH_W:BRANCH_W + SSD_GROUPS * N]
    cs = xbc[:, BRANCH_W + SSD_GROUPS * N:]

    raw_t = dtt_ref[0]
    raw = raw_t.T
    dt = _softplus(jnp.where(fwd, raw[:, 0:H], raw[:, H:2 * H]) + bias_r_ref[0][:, 0:H])
    a = dt * (-jnp.exp(alog_r_ref[0][:, 0:H]))
    dt_t = _softplus(jnp.where(fwd, raw_t[0:H, :], raw_t[H:2 * H, :]) + bias_c_ref[0][0:H, :])
    a_t = dt_t * (-jnp.exp(alog_c_ref[0][0:H, :]))
    mask, mask_t, _ = _scan_masks(fwd, L)
    cum = _dot_exact_lhs(jnp.where(mask, 1.0, 0.0).astype(BF16), a)
    cum_t = _dot_exact_rhs(a_t, jnp.where(mask_t, 1.0, 0.0).astype(BF16))
    total = jnp.sum(a, axis=0, keepdims=True)

    expand = _head_expander(H, P)
    dt_e = _dot_exact_rhs(dt, expand)
    p_e = _dot_exact_rhs(jnp.exp(cum), expand)
    q_e = _dot_exact_rhs(jnp.exp(total - cum), expand)
    tot_e = _dot_exact_rhs(jnp.broadcast_to(jnp.exp(total), (SUBLANES, H)), expand)[0:1, :]
    xd = xs * dt_e
    xdq = (xd * q_e).astype(BF16)
    xd_b = xd.astype(BF16)
    lane = lax.broadcasted_iota(jnp.int32, (L, 2 * P), 1)

    y_parts = []
    for g in range(SSD_GROUPS):
        cg = cs[:, g * N:(g + 1) * N].astype(BF16)
        bg = bs[:, g * N:(g + 1) * N]
        gmat = _dot_nt(cg, bg.astype(BF16))
        h_prev = state_scr[:, g * GW:(g + 1) * GW]
        y_off = _dot(cg, h_prev.astype(BF16)) * p_e[:, g * GW:(g + 1) * GW]
        pairs = []
        for hp in range(HG // 2):
            h0 = g * HG + 2 * hp
            xpair = xd_b[:, h0 * P:(h0 + 2) * P]
            ys = []
            for h in (h0, h0 + 1):
                dec = jnp.exp(jnp.where(mask, cum[:, h:h + 1] - cum_t[h:h + 1, :], NEG_BIG))
                ys.append(_dot((gmat * dec).astype(BF16), xpair))
            pairs.append(jnp.where(lane < P, ys[0], ys[1]))
        y_parts.append(jnp.concatenate(pairs, axis=1) + y_off)
        state_scr[:, g * GW:(g + 1) * GW] = (
            h_prev * tot_e[:, g * GW:(g + 1) * GW] + _dot(bg.T.astype(BF16), xdq[:, g * GW:(g + 1) * GW]))
    y_dir = jnp.concatenate(y_parts, axis=1)

    @pl.when(fwd)
    def _():
        yf_scr[pl.ds(row0, L), :] = y_dir

    @pl.when(jnp.logical_not(fwd))
    def _():
        y = yf_scr[pl.ds(row0, L), :] + y_dir + dskip_ref[...] * xs
        y = y * _silu(z_ref[0])
        ms = jnp.mean(y * y, axis=-1, keepdims=True)
        o_ref[0] = (y * lax.rsqrt(ms + EPS) * nw_ref[...]).astype(BF16)


def _ssd(proj, small_t, conv_w, conv_b, dt_bias, a_log, d_skip, norm_w, *, col_xbc, col_z):
    bsz, s, _ = proj.shape
    L = SSD_CHUNK
    nc = s // L
    cw = BRANCH_W + 2 * SSD_GROUPS * SSD_STATE
    hb = L // HALO
    xb, zb = col_xbc // cw, col_z // BRANCH_W
    assert col_xbc % cw == 0 and col_z % BRANCH_W == 0

    def chunk(t):
        return _chunk_of_step(t, nc)

    def direction(t):
        return jnp.where(t < nc, 0, 1)

    def conv_chunk(t):
        return jnp.minimum(t, nc - 1)

    pad = lambda v: jnp.zeros((2, 1, LANES), F32).at[:, 0, :SSD_HEADS].set(v)
    pad_c = lambda v: jnp.zeros((2, LANES, 1), F32).at[:, :SSD_HEADS, 0].set(v)
    return pl.pallas_call(
        functools.partial(_ssd_kernel, nc=nc),
        out_shape=jax.ShapeDtypeStruct((bsz, s, BRANCH_W), BF16),
        grid=(bsz, 2 * nc),
        in_specs=[
            pl.BlockSpec((1, L, cw), lambda b, t: (b, conv_chunk(t), xb)),
            pl.BlockSpec((1, HALO, cw), lambda b, t: (b, jnp.maximum(conv_chunk(t) * hb - 1, 0), xb)),
            pl.BlockSpec((1, HALO, cw), lambda b, t: (b, jnp.minimum((conv_chunk(t) + 1) * hb, s // HALO - 1), xb)),
            pl.BlockSpec((1, L, BRANCH_W), lambda b, t: (b, chunk(t), zb)),
            pl.BlockSpec((1, LANES, L), lambda b, t: (b, 0, chunk(t))),
            pl.BlockSpec((SSD_CONV, cw), lambda b, t: (0, 0)),
            pl.BlockSpec((1, cw), lambda b, t: (0, 0)),
            pl.BlockSpec((1, 1, LANES), lambda b, t: (direction(t), 0, 0)),
            pl.BlockSpec((1, LANES, 1), lambda b, t: (direction(t), 0, 0)),
            pl.BlockSpec((1, 1, LANES), lambda b, t: (direction(t), 0, 0)),
            pl.BlockSpec((1, LANES, 1), lambda b, t: (direction(t), 0, 0)),
            pl.BlockSpec((1, BRANCH_W), lambda b, t: (0, 0)),
            pl.BlockSpec((1, BRANCH_W), lambda b, t: (0, 0)),
        ],
        out_specs=pl.BlockSpec((1, L, BRANCH_W), lambda b, t: (b, jnp.where(t < nc, nc - 1, 2 * nc - 1 - t), 0)),
        scratch_shapes=[
            pltpu.VMEM((L + 2 * HALO, cw), F32),
            pltpu.VMEM((SSD_STATE, BRANCH_W), F32),
            pltpu.VMEM((s, BRANCH_W), F32),
            pltpu.VMEM((s, cw), F32),
        ],
        compiler_params=_cparams(("arbitrary", "arbitrary")),
        name="bidir_ssd",
    )(proj, proj, proj, proj, small_t, conv_w, conv_b.reshape(1, cw),
      pad(dt_bias), pad_c(dt_bias), pad(a_log), pad_c(a_log),
      jnp.repeat(d_skip, SSD_HEAD_DIM).reshape(1, BRANCH_W), norm_w.reshape(1, BRANCH_W))


AB_COL_XBC, AB_COL_K, AB_COL_V, AB_COL_Z, AB_COL_Q, AB_COL_GA, AB_N = 0, 1536, 1792, 2048, 3072, 4096, 5120


def _ab_weights(w_in):
    hq, hk = ATT_HEADS * ATT_HEAD_DIM, ATT_KV_HEADS * ATT_HEAD_DIM
    gn = SSD_GROUPS * SSD_STATE
    o = 0
    q = w_in[:, o:o + hq]; o += hq
    k = w_in[:, o:o + hk]; o += hk
    v = w_in[:, o:o + hk]; o += hk
    ga = w_in[:, o:o + BRANCH_W]; o += BRANCH_W
    xs = w_in[:, o:o + BRANCH_W]; o += BRANCH_W
    bs = w_in[:, o:o + gn]; o += gn
    cs = w_in[:, o:o + gn]; o += gn
    dtf = w_in[:, o:o + SSD_HEADS]; o += SSD_HEADS
    dtb = w_in[:, o:o + SSD_HEADS]; o += SSD_HEADS
    z = w_in[:, o:o + BRANCH_W]
    main = jnp.concatenate([xs, bs, cs, k, v, z, q, ga], axis=1).astype(BF16)
    zpad = jnp.zeros((w_in.shape[0], LANES - 2 * SSD_HEADS), w_in.dtype)
    small = jnp.concatenate([dtf, dtb, zpad], axis=1).astype(BF16)
    return main, small


def _layer0(x, mod_l, norm_w, w_in, q_norm, k_norm, conv_w, conv_b, dt_bias_f, dt_bias_b,
            a_log_f, a_log_b, d_skip, ssd_norm, w_out, final_w=None):
    s = x.shape[1]
    w_main, w_small = _ab_weights(w_in)
    proj, small_t = _inproj(x, mod_l, norm_w, w_main, w_small)
    cos_t, sin_t = _rope_tables(s)
    att = _attention(proj, cos_t, sin_t, q_norm.reshape(1, ATT_HEAD_DIM), k_norm.reshape(1, ATT_HEAD_DIM),
                     col_q=AB_COL_Q, col_k=AB_COL_K, col_v=AB_COL_V, col_ga=AB_COL_GA)
    ssd = _ssd(proj, small_t, conv_w, conv_b, jnp.stack([dt_bias_f, dt_bias_b]),
               jnp.stack([a_log_f, a_log_b]), d_skip, ssd_norm, col_xbc=AB_COL_XBC, col_z=AB_COL_Z)
    return _outproj(att, ssd, w_out.astype(BF16), x, mod_l, final_w), (proj, small_t, att, ssd)


def _unit_tri_inverse(nmats, n):
    row = lax.broadcasted_iota(jnp.int32, (n, n), 0)
    col = lax.broadcasted_iota(jnp.int32, (n, n), 1)

    def same_block(size):
        return (row // size) == (col // size)

    def mm(a, b):
        return _dot(a.astype(BF16), b.astype(BF16))

    eye = jnp.where(row == col, 1.0, 0.0)
    base = SUBLANES
    blk = same_block(base)
    nd = [jnp.where(blk, m, 0.0) for m in nmats]
    p1 = [mm(x, x) for x in nd]
    p2 = [mm(x, x) for x in p1]
    t = [eye - x for x in nd]
    t = [x + mm(x, p) for x, p in zip(t, p1)]
    t = [x + mm(x, p) for x, p in zip(t, p2)]
    size = base
    while size < n:
        nxt = same_block(2 * size)
        off = nxt & jnp.logical_not(blk)
        et = [mm(jnp.where(off, m, 0.0), x) for m, x in zip(nmats, t)]
        t = [x - mm(x, y) for x, y in zip(t, et)]
        blk = nxt
        size *= 2
    return t


def _l2norm(x):
    return x * lax.rsqrt(jnp.sum(x * x, axis=-1, keepdims=True) + EPS)


def _gdn_kernel(qf_ref, pf_ref, nf_ref, qb_ref, pb_ref, nb_ref, z_ref, smtf_ref, smtb_ref,
                cw_ref, cb_ref, bias_r_ref, bias_c_ref, alog_r_ref, alog_c_ref, nw_ref,
                o_ref, ext_scr, state_scr, acc_scr, *, nc, te):
    L, HV, HQ, DK = GDN_CHUNK, GDN_V_HEADS, GDN_QK_HEADS, GDN_HEAD_DIM
    rep = HV // HQ
    t = pl.program_id(1)

    @pl.when(t == 0)
    def _():
        state_scr[...] = jnp.zeros_like(state_scr)

    @pl.when(t < nc)
    def _():
        chunks = (t, nc - 1 - t)
        blocks = ((qf_ref, pf_ref, nf_ref, smtf_ref), (qb_ref, pb_ref, nb_ref, smtb_ref))
        kk, qk, qh, kh, vh, colv, dec, bcol, ecol, tot, nmat = ([] for _ in range(11))
        for d in range(2):
            c = chunks[d]
            q_ref, p_ref, n_ref, smt_ref = blocks[d]
            ext = ext_scr.at[d]
            _fill_conv_window(ext, q_ref[0], p_ref[0], n_ref[0], c, nc, L)
            act = _silu(_centred_conv4(ext, cw_ref, cb_ref, L))
            q_n = [_l2norm(act[:, h * DK:(h + 1) * DK]) * (DK ** -0.5) for h in range(HQ)]
            k_n = [_l2norm(act[:, (HQ + h) * DK:(HQ + h + 1) * DK]) for h in range(HQ)]
            k_b = [x.astype(BF16) for x in k_n]
            kk_d = [_dot_nt(k_b[h], k_b[h]) for h in range(HQ)]
            qk_d = [_dot_nt(q_n[h].astype(BF16), k_b[h]) for h in range(HQ)]

            raw_t = smt_ref[0]
            raw = raw_t.T
            even = c % 2 == 0
            sm = jnp.where(even, raw[0:L, :], raw[L:2 * L, :])[:, 2 * HV * d:2 * HV * (d + 1)]
            a_raw_t = jnp.where(even, raw_t[:, 0:L], raw_t[:, L:2 * L])[2 * HV * d + HV:2 * HV * (d + 1), :]
            beta = _sigmoid(sm[:, 0:HV])
            g = -jnp.exp(alog_r_ref[d][:, 0:HV]) * _softplus(sm[:, HV:2 * HV] + bias_r_ref[d][:, 0:HV])
            g_t = -jnp.exp(alog_c_ref[d][0:HV, :]) * _softplus(a_raw_t + bias_c_ref[d][0:HV, :])
            mask, mask_t, strict = _scan_masks(d == 0, L)
            cum = _dot_exact_lhs(jnp.where(mask, 1.0, 0.0).astype(BF16), g)
            cum_t = _dot_exact_rhs(g_t, jnp.where(mask_t, 1.0, 0.0).astype(BF16))
            total = jnp.sum(g, axis=0, keepdims=True)
            for h in range(HV):
                cv = cum[:, h:h + 1]
                dc = jnp.exp(jnp.where(mask, cv - cum_t[h:h + 1, :], NEG_BIG))
                bc = beta[:, h:h + 1]
                kk.append(kk_d[h // rep]); qk.append(qk_d[h // rep])
                qh.append(q_n[h // rep]); kh.append(k_n[h // rep])
                vh.append(act[:, (2 * HQ + h) * DK:(2 * HQ + h + 1) * DK])
                colv.append(cv); dec.append(dc); bcol.append(bc); ecol.append(jnp.exp(cv))
                tot.append(total[:, h:h + 1])
                nmat.append(jnp.where(strict, kk_d[h // rep] * bc * dc, 0.0))

        idx = range(2 * HV)
        t_inv = [x.astype(BF16) for x in _unit_tri_inverse(nmat, L)]
        u = [_dot(t_inv[i], (vh[i] * bcol[i]).astype(BF16)) for i in idx]
        w = [_dot(t_inv[i], (kh[i] * (bcol[i] * ecol[i])).astype(BF16)) for i in idx]
        s_prev = [state_scr[i // HV, :, (i % HV) * DK:(i % HV + 1) * DK] for i in idx]
        s_b = [x.astype(BF16) for x in s_prev]
        v_new = [(u[i] - _dot(w[i].astype(BF16), s_b[i])).astype(BF16) for i in idx]
        outs = [_dot((qh[i] * ecol[i]).astype(BF16), s_b[i]) + _dot((qk[i] * dec[i]).astype(BF16), v_new[i])
                for i in idx]
        for i in idx:
            k_dec = kh[i] * jnp.exp(tot[i] - colv[i])
            state_scr[i // HV, :, (i % HV) * DK:(i % HV + 1) * DK] = (
                s_prev[i] * jnp.exp(tot[i]) + _dot(k_dec.T.astype(BF16), v_new[i]))
        o_dirs = [jnp.concatenate(outs[d * HV:(d + 1) * HV], axis=1) for d in range(2)]
        rows = [pl.multiple_of(c * L, L) for c in chunks]

        @pl.when(t < nc // 2)
        def _():
            for d in range(2):
                acc_scr[pl.ds(rows[d], L), :] = o_dirs[d]

        @pl.when(t >= nc // 2)
        def _():
            for d in range(2):
                acc_scr[pl.ds(rows[d], L), :] = acc_scr[pl.ds(rows[d], L), :] + o_dirs[d]

    @pl.when(t >= nc)
    def _():
        r0 = pl.multiple_of((t - nc) * te, te)
        zz = z_ref[0]
        for hv in range(HV):
            oh = acc_scr[pl.ds(r0, te), hv * DK:(hv + 1) * DK]
            ms = jnp.mean(oh * oh, axis=-1, keepdims=True)
            res = oh * lax.rsqrt(ms + EPS) * nw_ref[...] * _silu(zz[:, hv * DK:(hv + 1) * DK])
            o_ref[0, :, hv * DK:(hv + 1) * DK] = res.astype(BF16)


def _pad_dir_rows(v, n):
    return jnp.zeros((2, 1, LANES), F32).at[:, 0, :n].set(v)


def _pad_dir_cols(v, n):
    return jnp.zeros((2, LANES, 1), F32).at[:, :n, 0].set(v)


def _gdn(proj, small_t, conv_w, conv_b, dt_bias, a_log, norm_w, *, col_qkv, col_z, te=256):
    bsz, s, _ = proj.shape
    L = GDN_CHUNK
    nc = s // L
    te = min(te, s)
    ne = s // te
    cw = 2 * GDN_QK_HEADS * GDN_HEAD_DIM + BRANCH_W
    hb = L // HALO
    qb, zb = col_qkv // cw, col_z // BRANCH_W
    assert col_qkv % cw == 0 and col_z % BRANCH_W == 0 and nc % 2 == 0

    def cf(t):
        return jnp.minimum(t, nc - 1)

    def cbk(t):
        return jnp.maximum(nc - 1 - t, 0)

    def ep(t):
        return jnp.maximum(t - nc, 0)

    def qkv_specs(chunk):
        return [
            pl.BlockSpec((1, L, cw), lambda b, t: (b, chunk(t), qb)),
            pl.BlockSpec((1, HALO, cw), lambda b, t: (b, jnp.maximum(chunk(t) * hb - 1, 0), qb)),
            pl.BlockSpec((1, HALO, cw), lambda b, t: (b, jnp.minimum((chunk(t) + 1) * hb, s // HALO - 1), qb)),
        ]

    def small_spec(chunk):
        return pl.BlockSpec((1, LANES, 2 * L), lambda b, t: (b, 0, chunk(t) // 2))

    full = lambda shape: pl.BlockSpec(shape, lambda b, t: (0,) * len(shape))
    hv = GDN_V_HEADS
    return pl.pallas_call(
        functools.partial(_gdn_kernel, nc=nc, te=te),
        out_shape=jax.ShapeDtypeStruct((bsz, s, BRANCH_W), BF16),
        grid=(bsz, nc + ne),
        in_specs=qkv_specs(cf) + qkv_specs(cbk)
        + [pl.BlockSpec((1, te, BRANCH_W), lambda b, t: (b, ep(t), zb))]
        + [small_spec(cf), small_spec(cbk)]
        + [full((GDN_CONV, cw)), full((1, cw)), full((2, 1, LANES)), full((2, LANES, 1)),
           full((2, 1, LANES)), full((2, LANES, 1)), full((1, GDN_HEAD_DIM))],
        out_specs=pl.BlockSpec((1, te, BRANCH_W), lambda b, t: (b, ep(t), 0)),
        scratch_shapes=[
            pltpu.VMEM((2, L + 2 * HALO, cw), F32),
            pltpu.VMEM((2, GDN_HEAD_DIM, BRANCH_W), F32),
            pltpu.VMEM((s, BRANCH_W), F32),
        ],
        compiler_params=_cparams(("arbitrary", "arbitrary")),
        name="bidir_gated_deltanet",
    )(proj, proj, proj, proj, proj, proj, proj, small_t, small_t, conv_w, conv_b.reshape(1, cw),
      _pad_dir_rows(dt_bias, hv), _pad_dir_cols(dt_bias, hv), _pad_dir_rows(a_log, hv), _pad_dir_cols(a_log, hv),
      norm_w.reshape(1, GDN_HEAD_DIM))


def _one_minus_sq_exp(log_a, a):
    return -jnp.tanh(log_a) * (a * a + 1.0)


def _local_scan(a, u, ascending):
    rows, width = a.shape
    a = a.reshape(rows // SUBLANES, SUBLANES, width)
    u = u.reshape(rows // SUBLANES, SUBLANES, width)
    sub = lax.broadcasted_iota(jnp.int32, a.shape, 1)
    d = 1
    while d < SUBLANES:
        if ascending:
            keep = sub >= d
            shift = d
        else:
            keep = sub < SUBLANES - d
            shift = SUBLANES - d
        a_sh = jnp.where(keep, pltpu.roll(a, shift, 1), 1.0)
        u_sh = jnp.where(keep, pltpu.roll(u, shift, 1), 0.0)
        u = u + a * u_sh
        a = a * a_sh
        d *= 2
    return a.reshape(rows, width), u.reshape(rows, width)


def _lru_kernel(xl_ref, prev_ref, next_ref, gl_ref, cw_ref, cb_ref, wa_ref, ba_ref, wx_ref, bx_ref, lam_ref,
                o_ref, ext_scr, a_scr, u_scr, hf_scr, carry_scr, *, nb, unroll):
    rows, width = a_scr.shape
    bw = LRU_WIDTH // LRU_BLOCKS
    step = pl.program_id(2)
    fwd = step < nb
    blk = _chunk_of_step(step, nb)

    @pl.when((step == 0) | (step == nb))
    def _():
        carry_scr[...] = jnp.zeros_like(carry_scr)

    _fill_conv_window(ext_scr, xl_ref[0], prev_ref[0], next_ref[0], blk, nb, rows)
    xc = _centred_conv4(ext_scr, cw_ref, cb_ref, rows)
    r_parts, i_parts = [], []
    for n in range(width // bw):
        xb = xc[:, n * bw:(n + 1) * bw].astype(BF16)
        r_parts.append(_dot(xb, wa_ref[0, n]))
        i_parts.append(_dot(xb, wx_ref[0, n]))
    r = _sigmoid(jnp.concatenate(r_parts, axis=1) + ba_ref[0])
    i = _sigmoid(jnp.concatenate(i_parts, axis=1) + bx_ref[0])
    log_a = -LRU_C * r * _softplus(-lam_ref[0])
    a = jnp.exp(log_a)
    u = jnp.sqrt(_one_minus_sq_exp(log_a, a)) * (i * xc)

    n_groups = rows // SUBLANES
    base_out = pl.multiple_of(blk * rows, rows)

    def scan(ascending):
        a_loc, u_loc = _local_scan(a, u, ascending)
        a_scr[...] = a_loc
        u_scr[...] = u_loc
        last = SUBLANES - 1 if ascending else 0

        def body(it, carry):
            pos = it if ascending else n_groups - 1 - it
            r0 = pl.multiple_of(pos * SUBLANES, SUBLANES)
            h = u_scr[pl.ds(r0, SUBLANES), :] + a_scr[pl.ds(r0, SUBLANES), :] * carry
            u_scr[pl.ds(r0, SUBLANES), :] = h
            return jnp.broadcast_to(h[last:last + 1, :], h.shape)

        carry_scr[...] = lax.fori_loop(0, n_groups, body, carry_scr[...], unroll=unroll)

    @pl.when(fwd)
    def _():
        scan(True)
        hf_scr[pl.ds(base_out, rows), :] = u_scr[...]

    @pl.when(jnp.logical_not(fwd))
    def _():
        scan(False)
        o_ref[0] = ((hf_scr[pl.ds(base_out, rows), :] + u_scr[...]) * _silu(gl_ref[0])).astype(BF16)


def _lru(proj, conv_w, conv_b, wa, ba, wx, bx, lam, *, col_xl, col_gl, tt=512, width=512, unroll=8):
    bsz, s, _ = proj.shape
    tt = min(tt, s)
    nb = s // tt
    w_total = LRU_WIDTH
    bw = w_total // LRU_BLOCKS
    nbw = width // bw
    xcol, gcol = col_xl // width, col_gl // width
    assert col_xl % width == 0 and col_gl % width == 0
    hb = tt // HALO

    def blk(t):
        return _chunk_of_step(t, nb)

    def direction(t):
        return jnp.where(t < nb, 0, 1)

    return pl.pallas_call(
        functools.partial(_lru_kernel, nb=nb, unroll=unroll),
        out_shape=jax.ShapeDtypeStruct((bsz, s, w_total), BF16),
        grid=(bsz, w_total // width, 2 * nb),
        in_specs=[
            pl.BlockSpec((1, tt, width), lambda b, j, t: (b, blk(t), xcol + j)),
            pl.BlockSpec((1, HALO, width), lambda b, j, t: (b, jnp.maximum(blk(t) * hb - 1, 0), xcol + j)),
            pl.BlockSpec((1, HALO, width),
                         lambda b, j, t: (b, jnp.minimum((blk(t) + 1) * hb, s // HALO - 1), xcol + j)),
            pl.BlockSpec((1, tt, width), lambda b, j, t: (b, blk(t), gcol + j)),
            pl.BlockSpec((LRU_CONV, width), lambda b, j, t: (0, j)),
            pl.BlockSpec((1, width), lambda b, j, t: (0, j)),
            pl.BlockSpec((1, nbw, bw, bw), lambda b, j, t: (direction(t), j, 0, 0)),
            pl.BlockSpec((1, 1, width), lambda b, j, t: (direction(t), 0, j)),
            pl.BlockSpec((1, nbw, bw, bw), lambda b, j, t: (direction(t), j, 0, 0)),
            pl.BlockSpec((1, 1, width), lambda b, j, t: (direction(t), 0, j)),
            pl.BlockSpec((1, 1, width), lambda b, j, t: (direction(t), 0, j)),
        ],
        out_specs=pl.BlockSpec((1, tt, width),
                               lambda b, j, t: (b, jnp.where(t < nb, nb - 1, 2 * nb - 1 - t), j)),
        scratch_shapes=[
            pltpu.VMEM((tt + 2 * HALO, width), F32),
            pltpu.VMEM((tt, width), F32),
            pltpu.VMEM((tt, width), F32),
            pltpu.VMEM((s, width), F32),
            pltpu.VMEM((SUBLANES, width), F32),
        ],
        compiler_params=_cparams(("arbitrary", "arbitrary", "arbitrary")),
        name="bidir_rglru",
    )(proj, proj, proj, proj, conv_w, conv_b.reshape(1, w_total),
      wa.astype(BF16), ba.reshape(2, 1, w_total), wx.astype(BF16), bx.reshape(2, 1, w_total),
      lam.reshape(2, 1, w_total))


CD_COL_QKV, CD_COL_Z, CD_COL_XL, CD_COL_GL = 0, 2048, 3072, 4096


def _cd_weights(w_in):
    nqk = GDN_QK_HEADS * GDN_HEAD_DIM
    hv = GDN_V_HEADS
    o = 0
    q = w_in[:, o:o + nqk]; o += nqk
    k = w_in[:, o:o + nqk]; o += nqk
    v = w_in[:, o:o + BRANCH_W]; o += BRANCH_W
    bf = w_in[:, o:o + hv]; o += hv
    bb = w_in[:, o:o + hv]; o += hv
    af = w_in[:, o:o + hv]; o += hv
    ab = w_in[:, o:o + hv]; o += hv
    z = w_in[:, o:o + BRANCH_W]; o += BRANCH_W
    xl = w_in[:, o:o + LRU_WIDTH]; o += LRU_WIDTH
    gl = w_in[:, o:o + LRU_WIDTH]
    main = jnp.concatenate([q, k, v, z, xl, gl], axis=1).astype(BF16)
    zpad = jnp.zeros((w_in.shape[0], LANES - 4 * hv), w_in.dtype)
    small = jnp.concatenate([bf, af, bb, ab, zpad], axis=1).astype(BF16)
    return main, small


def _layer1(x, mod_l, norm_w, w_in, conv_w, conv_b, a_log_f, a_log_b, dt_bias_f, dt_bias_b, gdn_norm,
            lru_conv_w, lru_conv_b, wa_f, ba_f, wx_f, bx_f, lam_f, wa_b, ba_b, wx_b, bx_b, lam_b, w_out,
            final_w=None):
    w_main, w_small = _cd_weights(w_in)
    proj, small_t = _inproj(x, mod_l, norm_w, w_main, w_small)
    gdn = _gdn(proj, small_t, conv_w, conv_b, jnp.stack([dt_bias_f, dt_bias_b]),
               jnp.stack([a_log_f, a_log_b]), gdn_norm, col_qkv=CD_COL_QKV, col_z=CD_COL_Z)
    lru = _lru(proj, lru_conv_w, lru_conv_b, jnp.stack([wa_f, wa_b]), jnp.stack([ba_f, ba_b]),
               jnp.stack([wx_f, wx_b]), jnp.stack([bx_f, bx_b]), jnp.stack([lam_f, lam_b]),
               col_xl=CD_COL_XL, col_gl=CD_COL_GL)
    out = _outproj(gdn, lru, w_out.astype(BF16), x, mod_l, final_w)
    return out, (proj, small_t, gdn, lru)


def kernel(x, c, w_mod, b_mod, norm_w, ab_w_in, ab_q_norm, ab_k_norm, ab_conv_w, ab_conv_b, ab_dt_bias_f, ab_dt_bias_b, ab_a_log_f, ab_a_log_b, ab_d_skip, ab_ssd_norm, ab_w_out, cd_w_in, cd_conv_w, cd_conv_b, cd_a_log_f, cd_a_log_b, cd_dt_bias_f, cd_dt_bias_b, cd_gdn_norm, cd_lru_conv_w, cd_lru_conv_b, cd_lru_wa_f, cd_lru_ba_f, cd_lru_wx_f, cd_lru_bx_f, cd_lru_lam_f, cd_lru_wa_b, cd_lru_ba_b, cd_lru_wx_b, cd_lru_bx_b, cd_lru_lam_b, cd_w_out, final_norm_w):
    mods = _modulation(c, w_mod, b_mod)
    x1, _ = _layer0(x, mods[0], norm_w[0], ab_w_in[0], ab_q_norm[0], ab_k_norm[0], ab_conv_w[0], ab_conv_b[0],
                    ab_dt_bias_f[0], ab_dt_bias_b[0], ab_a_log_f[0], ab_a_log_b[0], ab_d_skip[0],
                    ab_ssd_norm[0], ab_w_out[0])
    out, _ = _layer1(x1, mods[1], norm_w[1], cd_w_in[0], cd_conv_w[0], cd_conv_b[0], cd_a_log_f[0], cd_a_log_b[0],
                     cd_dt_bias_f[0], cd_dt_bias_b[0], cd_gdn_norm[0], cd_lru_conv_w[0], cd_lru_conv_b[0],
                     cd_lru_wa_f[0], cd_lru_ba_f[0], cd_lru_wx_f[0], cd_lru_bx_f[0], cd_lru_lam_f[0],
                     cd_lru_wa_b[0], cd_lru_ba_b[0], cd_lru_wx_b[0], cd_lru_bx_b[0], cd_lru_lam_b[0],
                     cd_w_out[0], final_norm_w)
    return out
```

```python
import functools
import math

import jax
import jax.numpy as jnp
import numpy as np
from jax import lax
from jax.experimental import pallas as pl
from jax.experimental.pallas import tpu as pltpu

F32 = jnp.float32
BF16 = jnp.bfloat16

D_MODEL = 2048
GRID_W = 64
EPS = 1e-6
BRANCH_W = D_MODEL // 2
ATT_HEAD_DIM = 128
ATT_HEADS = BRANCH_W // ATT_HEAD_DIM
ATT_KV_HEADS = ATT_HEADS // 4
ATT_GROUP = ATT_HEADS // ATT_KV_HEADS
ROPE_THETA = 10000.0
SSD_HEAD_DIM = 64
SSD_HEADS = BRANCH_W // SSD_HEAD_DIM
SSD_GROUPS = 2
SSD_STATE = 128
SSD_CONV = 4
SSD_CHUNK = 128
GDN_HEAD_DIM = 128
GDN_V_HEADS = BRANCH_W // GDN_HEAD_DIM
GDN_QK_HEADS = GDN_V_HEADS // 2
GDN_CONV = 4
GDN_CHUNK = 64
LRU_WIDTH = BRANCH_W
LRU_BLOCKS = 8
LRU_CONV = 4
LRU_C = 8.0

LANES = 128
SUBLANES = 8
VMEM_LIMIT_BYTES = 56 * 1024 * 1024

HALO = SUBLANES
NEG_BIG = -1e30


def _cparams(sem):
    return pltpu.CompilerParams(dimension_semantics=sem, vmem_limit_bytes=VMEM_LIMIT_BYTES)


def _sigmoid(x):
    return 0.5 * jnp.tanh(0.5 * x) + 0.5


def _silu(x):
    return x * _sigmoid(x)


def _softplus(x):
    return jnp.maximum(x, 0.0) + jnp.log(1.0 + jnp.exp(-jnp.abs(x)))


def _split_bf16(a):
    hi = a.astype(BF16)
    lo = (a - hi.astype(F32)).astype(BF16)
    return hi, lo


def _dot(a, b):
    return jnp.dot(a, b, preferred_element_type=F32)


def _dot_nt(a, b):
    return lax.dot_general(a, b, (((1,), (1,)), ((), ())), preferred_element_type=F32)


def _dot_exact_rhs(a, b_bf16):
    hi, lo = _split_bf16(a)
    return _dot(hi, b_bf16) + _dot(lo, b_bf16)


def _dot_exact_lhs(a_bf16, b):
    hi, lo = _split_bf16(b)
    return _dot(a_bf16, hi) + _dot(a_bf16, lo)


def _mod_kernel(c_ref, w_ref, b_ref, o_ref):
    cond = _silu(c_ref[...])
    c_hi, c_lo = _split_bf16(cond)
    w = w_ref[0]
    w_hi, w_lo = _split_bf16(w)
    o_ref[0] = _dot(c_hi, w_hi) + _dot(c_lo, w_hi) + _dot(c_hi, w_lo) + b_ref[0]


def _modulation(c, w_mod, b_mod):
    depth, d, n = w_mod.shape
    bsz = c.shape[0]
    rows = -(-bsz // SUBLANES) * SUBLANES
    c_pad = jnp.zeros((rows, d), F32).at[:bsz].set(c)
    tn = 1024
    out = pl.pallas_call(
        _mod_kernel,
        out_shape=jax.ShapeDtypeStruct((depth, rows, n), F32),
        grid=(depth, n // tn),
        in_specs=[
            pl.BlockSpec((rows, d), lambda l, j: (0, 0)),
            pl.BlockSpec((1, d, tn), lambda l, j: (l, 0, j)),
            pl.BlockSpec((1, 1, tn), lambda l, j: (l, 0, j)),
        ],
        out_specs=pl.BlockSpec((1, rows, tn), lambda l, j: (l, 0, j)),
        compiler_params=_cparams(("arbitrary", "arbitrary")),
        name="adaln_mod",
    )(c_pad, w_mod, b_mod.reshape(depth, 1, n))
    return out[:, :bsz].reshape(depth, bsz, 3, d)


def _inproj_kernel(x_ref, mod_ref, nw_ref, w_ref, wst_ref, o_ref, ost_ref, h_even, h_odd, *, n_groups):
    g = pl.program_id(0)
    j = pl.program_id(1)
    rows = x_ref.shape[1]

    def prep(h_dst):
        x = x_ref[0]
        ms = jnp.mean(x * x, axis=-1, keepdims=True)
        y = x * lax.rsqrt(ms + EPS) * nw_ref[...]
        h = (y * (1.0 + mod_ref[0, 1:2, :]) + mod_ref[0, 0:1, :]).astype(BF16)
        h_dst[pl.ds(pl.multiple_of(j * rows, rows), rows), :] = h
        ost_ref[0] = _dot_nt(wst_ref[...], h)

    @pl.when(g == 0)
    def _():
        prep(h_even)

    @pl.when((g > 0) & (g % 2 == 0))
    def _():
        o_ref[0] = _dot_nt(h_odd[...], w_ref[...])
        prep(h_even)

    @pl.when(g % 2 == 1)
    def _():
        o_ref[0] = _dot_nt(h_even[...], w_ref[...])
        prep(h_odd)


def _inproj(x, mod_l, norm_w, w_main, w_small, *, tm=1024, nj=4):
    bsz, s, d = x.shape
    n = w_main.shape[0]
    ns = w_small.shape[0]
    tm = min(tm, s)
    ni = s // tm
    n_groups = bsz * ni
    tn = n // nj
    rows = tm // nj
    assert n % nj == 0 and tn % LANES == 0 and rows % LANES == 0

    def prep(g, j):
        gc = jnp.minimum(g, n_groups - 1)
        return gc // ni, (gc % ni) * nj + jnp.where(g < n_groups, j, nj - 1)

    def mm(g, j):
        gm = jnp.maximum(g - 1, 0)
        return gm // ni, gm % ni, jnp.where(g > 0, j, 0)

    return pl.pallas_call(
        functools.partial(_inproj_kernel, n_groups=n_groups),
        out_shape=[jax.ShapeDtypeStruct((bsz, s, n), F32), jax.ShapeDtypeStruct((bsz, ns, s), F32)],
        grid=(n_groups + 1, nj),
        in_specs=[
            pl.BlockSpec((1, rows, d), lambda g, j: (*prep(g, j), 0)),
            pl.BlockSpec((1, 3, d), lambda g, j: (prep(g, j)[0], 0, 0)),
            pl.BlockSpec((1, d), lambda g, j: (0, 0)),
            pl.BlockSpec((tn, d), lambda g, j: (mm(g, j)[2], 0)),
            pl.BlockSpec((ns, d), lambda g, j: (0, 0)),
        ],
        out_specs=[
            pl.BlockSpec((1, tm, tn), lambda g, j: mm(g, j)),
            pl.BlockSpec((1, ns, rows), lambda g, j: (prep(g, j)[0], 0, prep(g, j)[1])),
        ],
        scratch_shapes=[pltpu.VMEM((tm, d), BF16), pltpu.VMEM((tm, d), BF16)],
        compiler_params=_cparams(("arbitrary", "arbitrary")),
        name="norm_mod_inproj",
    )(x, mod_l, norm_w.reshape(1, d), w_main, w_small)


def _outproj_kernel(ya_ref, yb_ref, w_ref, x_ref, mod_ref, *rest, final_norm, half):
    if final_norm:
        fnw_ref, o_ref = rest
    else:
        (o_ref,) = rest
    acc = _dot(ya_ref[0], w_ref[0:half, :]) + _dot(yb_ref[0], w_ref[half:, :])
    gate = mod_ref[0, 2:3, :]
    xn = x_ref[0] + gate * acc
    if final_norm:
        ms = jnp.mean(xn * xn, axis=-1, keepdims=True)
        xn = xn * lax.rsqrt(ms + EPS) * fnw_ref[...]
    o_ref[0] = xn


def _outproj(ya, yb, w_out, x, mod_l, final_w=None, *, tm=512):
    bsz, s, d = x.shape
    half = ya.shape[-1]
    tm = min(tm, s)
    final_norm = final_w is not None
    in_specs = [
        pl.BlockSpec((1, tm, half), lambda b, i: (b, i, 0)),
        pl.BlockSpec((1, tm, half), lambda b, i: (b, i, 0)),
        pl.BlockSpec((2 * half, d), lambda b, i: (0, 0)),
        pl.BlockSpec((1, tm, d), lambda b, i: (b, i, 0)),
        pl.BlockSpec((1, 3, d), lambda b, i: (b, 0, 0)),
    ]
    args = [ya, yb, w_out, x, mod_l]
    if final_norm:
        in_specs.append(pl.BlockSpec((1, d), lambda b, i: (0, 0)))
        args.append(final_w.reshape(1, d))
    return pl.pallas_call(
        functools.partial(_outproj_kernel, final_norm=final_norm, half=half),
        out_shape=jax.ShapeDtypeStruct((bsz, s, d), F32),
        grid=(bsz, s // tm),
        in_specs=in_specs,
        out_specs=pl.BlockSpec((1, tm, d), lambda b, i: (b, i, 0)),
        compiler_params=_cparams(("arbitrary", "arbitrary")),
        name="outproj_residual",
    )(*args)


def _rms_rope(x, nw, cos, sin_signed):
    ms = jnp.mean(x * x, axis=-1, keepdims=True)
    y = x * lax.rsqrt(ms + EPS) * nw
    lane = lax.broadcasted_iota(jnp.int32, y.shape, 1)
    partner = jnp.where(lane % 2 == 0, pltpu.roll(y, ATT_HEAD_DIM - 1, 1), pltpu.roll(y, 1, 1))
    return y * cos + partner * sin_signed


def _attn_kernel(q_ref, k_ref, v_ref, ga_ref, cq_ref, sq_ref, ck_ref, sk_ref, qn_ref, kn_ref,
                 o_ref, k_scr, v_scr, *, tq):
    qi = pl.program_id(2)

    dh = ATT_HEAD_DIM

    @pl.when(qi == 0)
    def _():
        k_scr[...] = _rms_rope(k_ref[0], kn_ref[...], ck_ref[...], sk_ref[...]).astype(BF16)
        v_scr[:, 0:dh] = v_ref[0].astype(BF16)
        v_scr[:, dh:2 * dh] = jnp.ones((v_scr.shape[0], dh), BF16)

    scale = dh ** -0.5
    cos = cq_ref[...]
    sin = sq_ref[...]

    def scores(g):
        qg = (_rms_rope(q_ref[0, :, g * dh:(g + 1) * dh], qn_ref[...], cos, sin) * scale).astype(BF16)
        return _dot_nt(qg, k_scr[...])

    s_next = scores(0)
    for g in range(ATT_GROUP):
        s = s_next
        if g + 1 < ATT_GROUP:
            s_next = scores(g + 1)
        s = s.astype(BF16)
        p = jnp.exp(s - jnp.max(s, axis=-1, keepdims=True))
        o_ext = _dot(p, v_scr[...])
        og = o_ext[:, 0:dh] / o_ext[:, dh:2 * dh] * _silu(ga_ref[0, :, g * dh:(g + 1) * dh])
        o_ref[0, :, g * dh:(g + 1) * dh] = og.astype(BF16)


def _attention(proj, cos_t, sin_t, q_norm, k_norm, *, col_q, col_k, col_v, col_ga, tq=512):
    bsz, s, _ = proj.shape
    tq = min(tq, s)
    gw = ATT_GROUP * ATT_HEAD_DIM
    dh = ATT_HEAD_DIM
    qb, kb, vb, gb = col_q // gw, col_k // dh, col_v // dh, col_ga // gw
    return pl.pallas_call(
        functools.partial(_attn_kernel, tq=tq),
        out_shape=jax.ShapeDtypeStruct((bsz, s, BRANCH_W), BF16),
        grid=(bsz, ATT_KV_HEADS, s // tq),
        in_specs=[
            pl.BlockSpec((1, tq, gw), lambda b, h, i: (b, i, qb + h)),
            pl.BlockSpec((1, s, dh), lambda b, h, i: (b, 0, kb + h)),
            pl.BlockSpec((1, s, dh), lambda b, h, i: (b, 0, vb + h)),
            pl.BlockSpec((1, tq, gw), lambda b, h, i: (b, i, gb + h)),
            pl.BlockSpec((tq, dh), lambda b, h, i: (i, 0)),
            pl.BlockSpec((tq, dh), lambda b, h, i: (i, 0)),
            pl.BlockSpec((s, dh), lambda b, h, i: (0, 0)),
            pl.BlockSpec((s, dh), lambda b, h, i: (0, 0)),
            pl.BlockSpec((1, dh), lambda b, h, i: (0, 0)),
            pl.BlockSpec((1, dh), lambda b, h, i: (0, 0)),
        ],
        out_specs=pl.BlockSpec((1, tq, gw), lambda b, h, i: (b, i, h)),
        scratch_shapes=[pltpu.VMEM((s, dh), BF16), pltpu.VMEM((s, 2 * dh), BF16)],
        compiler_params=_cparams(("arbitrary", "arbitrary", "arbitrary")),
        name="gqa_attention",
    )(proj, proj, proj, proj, cos_t, sin_t, cos_t, sin_t, q_norm, k_norm)


def _rope_tables(s):
    t = np.arange(s)
    row = (t // GRID_W).astype(np.float64)
    col = (t % GRID_W).astype(np.float64)
    n_pairs = ATT_HEAD_DIM // 4
    freqs = ROPE_THETA ** (-np.arange(n_pairs, dtype=np.float64) / n_pairs)
    ang = np.concatenate([row[:, None] * freqs, col[:, None] * freqs], axis=-1)
    cos, sin = np.cos(ang), np.sin(ang)
    cos_t = np.repeat(cos, 2, axis=-1)
    sin_t = np.stack([-sin, sin], axis=-1).reshape(s, ATT_HEAD_DIM)
    return jnp.asarray(cos_t, F32), jnp.asarray(sin_t, F32)


def _chunk_of_step(step, nc):
    return jnp.where(step < nc, step, 2 * nc - 1 - step)


def _fill_conv_window(ext_scr, cur, prev, nxt, c, nc, rows):
    ext_scr[0:HALO, :] = jnp.where(c > 0, prev, 0.0)
    ext_scr[HALO:HALO + rows, :] = cur
    ext_scr[HALO + rows:HALO + rows + HALO, :] = jnp.where(c < nc - 1, nxt, 0.0)


CONV_SUB = 128


def _centred_conv4(ext_scr, cw_ref, cb_ref, rows, on_mxu=True):
    if not on_mxu:
        acc = cb_ref[...]
        for k in range(4):
            acc = acc + ext_scr[pl.ds(HALO + k - 2, rows), :] * cw_ref[k:k + 1, :]
        return acc
    sub = min(rows, CONV_SUB)
    n_in = sub + 2 * HALO
    taps = (0, 1, 3)
    r = lax.broadcasted_iota(jnp.int32, (len(taps) * sub, n_in), 0)
    c = lax.broadcasted_iota(jnp.int32, (len(taps) * sub, n_in), 1)
    src = jnp.zeros_like(r)
    for i, k in enumerate(taps):
        src = jnp.where((r >= i * sub) & (r < (i + 1) * sub), r - i * sub + HALO + (k - 2), src)
    shift_mat = jnp.where(c == src, 1.0, 0.0).astype(BF16)
    outs = []
    for s0 in range(0, rows, sub):
        win = ext_scr[s0:s0 + n_in, :]
        shifted = _dot(shift_mat, win.astype(BF16))
        acc = cb_ref[...] + win[HALO:HALO + sub, :] * cw_ref[2:3, :]
        for i, k in enumerate(taps):
            acc = acc + shifted[i * sub:(i + 1) * sub, :] * cw_ref[k:k + 1, :]
        outs.append(acc)
    return outs[0] if len(outs) == 1 else jnp.concatenate(outs, axis=0)


def _scan_masks(fwd, n):
    row = lax.broadcasted_iota(jnp.int32, (n, n), 0)
    col = lax.broadcasted_iota(jnp.int32, (n, n), 1)
    sgn = jnp.where(fwd, 1, -1)
    d = (row - col) * sgn
    return d >= 0, d <= 0, d > 0


def _head_expander(heads, width):
    r = lax.broadcasted_iota(jnp.int32, (heads, heads * width), 0)
    c = lax.broadcasted_iota(jnp.int32, (heads, heads * width), 1)
    return jnp.where((c >= r * width) & (c < (r + 1) * width), 1.0, 0.0).astype(BF16)


def _ssd_kernel(xbc_ref, prev_ref, next_ref, z_ref, dtt_ref, cw_ref, cb_ref,
                bias_r_ref, bias_c_ref, alog_r_ref, alog_c_ref, dskip_ref, nw_ref,
                o_ref, ext_scr, state_scr, yf_scr, xbc_scr, *, nc):
    L, H, P, N = SSD_CHUNK, SSD_HEADS, SSD_HEAD_DIM, SSD_STATE
    HG = H // SSD_GROUPS
    GW = HG * P
    step = pl.program_id(1)
    fwd = step < nc
    c = _chunk_of_step(step, nc)

    @pl.when((step == 0) | (step == nc))
    def _():
        state_scr[...] = jnp.zeros_like(state_scr)

    row0 = pl.multiple_of(c * L, L)

    @pl.when(fwd)
    def _():
        _fill_conv_window(ext_scr, xbc_ref[0], prev_ref[0], next_ref[0], c, nc, L)
        xbc_scr[pl.ds(row0, L), :] = _silu(_centred_conv4(ext_scr, cw_ref, cb_ref, L))

    xbc = xbc_scr[pl.ds(row0, L), :]
    xs = xbc[:, :BRANCH_W]
    bs = xbc[:, BRANCH_W:BRANCH_W + SSD_GROUPS * N]
    cs = xbc[:, BRANCH_W + SSD_GROUPS * N:]

    raw_t = dtt_ref[0]
    raw = raw_t.T
    dt = _softplus(jnp.where(fwd, raw[:, 0:H], raw[:, H:2 * H]) + bias_r_ref[0][:, 0:H])
    a = dt * (-jnp.exp(alog_r_ref[0][:, 0:H]))
    dt_t = _softplus(jnp.where(fwd, raw_t[0:H, :], raw_t[H:2 * H, :]) + bias_c_ref[0][0:H, :])
    a_t = dt_t * (-jnp.exp(alog_c_ref[0][0:H, :]))
    mask, mask_t, _ = _scan_masks(fwd, L)
    cum = _dot_exact_lhs(jnp.where(mask, 1.0, 0.0).astype(BF16), a)
    cum_t = _dot_exact_rhs(a_t, jnp.where(mask_t, 1.0, 0.0).astype(BF16))
    total = jnp.sum(a, axis=0, keepdims=True)

    expand = _head_expander(H, P)
    dt_e = _dot_exact_rhs(dt, expand)
    p_e = _dot_exact_rhs(jnp.exp(cum), expand)
    q_e = _dot_exact_rhs(jnp.exp(total - cum), expand)
    tot_e = _dot_exact_rhs(jnp.broadcast_to(jnp.exp(total), (SUBLANES, H)), expand)[0:1, :]
    xd = xs * dt_e
    xdq = (xd * q_e).astype(BF16)
    xd_b = xd.astype(BF16)
    lane = lax.broadcasted_iota(jnp.int32, (L, 2 * P), 1)

    y_parts = []
    for g in range(SSD_GROUPS):
        cg = cs[:, g * N:(g + 1) * N].astype(BF16)
        bg = bs[:, g * N:(g + 1) * N]
        gmat = _dot_nt(cg, bg.astype(BF16))
        h_prev = state_scr[:, g * GW:(g + 1) * GW]
        y_off = _dot(cg, h_prev.astype(BF16)) * p_e[:, g * GW:(g + 1) * GW]
        pairs = []
        for hp in range(HG // 2):
            h0 = g * HG + 2 * hp
            xpair = xd_b[:, h0 * P:(h0 + 2) * P]
            ys = []
            for h in (h0, h0 + 1):
                dec = jnp.exp(jnp.where(mask, cum[:, h:h + 1] - cum_t[h:h + 1, :], NEG_BIG))
                ys.append(_dot((gmat * dec).astype(BF16), xpair))
            pairs.append(jnp.where(lane < P, ys[0], ys[1]))
        y_parts.append(jnp.concatenate(pairs, axis=1) + y_off)
        state_scr[:, g * GW:(g + 1) * GW] = (
            h_prev * tot_e[:, g * GW:(g + 1) * GW] + _dot(bg.T.astype(BF16), xdq[:, g * GW:(g + 1) * GW]))
    y_dir = jnp.concatenate(y_parts, axis=1)

    @pl.when(fwd)
    def _():
        yf_scr[pl.ds(row0, L), :] = y_dir

    @pl.when(jnp.logical_not(fwd))
    def _():
        y = yf_scr[pl.ds(row0, L), :] + y_dir + dskip_ref[...] * xs
        y = y * _silu(z_ref[0])
        ms = jnp.mean(y * y, axis=-1, keepdims=True)
        o_ref[0] = (y * lax.rsqrt(ms + EPS) * nw_ref[...]).astype(BF16)


def _ssd(proj, small_t, conv_w, conv_b, dt_bias, a_log, d_skip, norm_w, *, col_xbc, col_z):
    bsz, s, _ = proj.shape
    L = SSD_CHUNK
    nc = s // L
    cw = BRANCH_W + 2 * SSD_GROUPS * SSD_STATE
    hb = L // HALO
    xb, zb = col_xbc // cw, col_z // BRANCH_W
    assert col_xbc % cw == 0 and col_z % BRANCH_W == 0

    def chunk(t):
        return _chunk_of_step(t, nc)

    def direction(t):
        return jnp.where(t < nc, 0, 1)

    def conv_chunk(t):
        return jnp.minimum(t, nc - 1)

    pad = lambda v: jnp.zeros((2, 1, LANES), F32).at[:, 0, :SSD_HEADS].set(v)
    pad_c = lambda v: jnp.zeros((2, LANES, 1), F32).at[:, :SSD_HEADS, 0].set(v)
    return pl.pallas_call(
        functools.partial(_ssd_kernel, nc=nc),
        out_shape=jax.ShapeDtypeStruct((bsz, s, BRANCH_W), BF16),
        grid=(bsz, 2 * nc),
        in_specs=[
            pl.BlockSpec((1, L, cw), lambda b, t: (b, conv_chunk(t), xb)),
            pl.BlockSpec((1, HALO, cw), lambda b, t: (b, jnp.maximum(conv_chunk(t) * hb - 1, 0), xb)),
            pl.BlockSpec((1, HALO, cw), lambda b, t: (b, jnp.minimum((conv_chunk(t) + 1) * hb, s // HALO - 1), xb)),
            pl.BlockSpec((1, L, BRANCH_W), lambda b, t: (b, chunk(t), zb)),
            pl.BlockSpec((1, LANES, L), lambda b, t: (b, 0, chunk(t))),
            pl.BlockSpec((SSD_CONV, cw), lambda b, t: (0, 0)),
            pl.BlockSpec((1, cw), lambda b, t: (0, 0)),
            pl.BlockSpec((1, 1, LANES), lambda b, t: (direction(t), 0, 0)),
            pl.BlockSpec((1, LANES, 1), lambda b, t: (direction(t), 0, 0)),
            pl.BlockSpec((1, 1, LANES), lambda b, t: (direction(t), 0, 0)),
            pl.BlockSpec((1, LANES, 1), lambda b, t: (direction(t), 0, 0)),
            pl.BlockSpec((1, BRANCH_W), lambda b, t: (0, 0)),
            pl.BlockSpec((1, BRANCH_W), lambda b, t: (0, 0)),
        ],
        out_specs=pl.BlockSpec((1, L, BRANCH_W), lambda b, t: (b, jnp.where(t < nc, nc - 1, 2 * nc - 1 - t), 0)),
        scratch_shapes=[
            pltpu.VMEM((L + 2 * HALO, cw), F32),
            pltpu.VMEM((SSD_STATE, BRANCH_W), F32),
            pltpu.VMEM((s, BRANCH_W), F32),
            pltpu.VMEM((s, cw), F32),
        ],
        compiler_params=_cparams(("arbitrary", "arbitrary")),
        name="bidir_ssd",
    )(proj, proj, proj, proj, small_t, conv_w, conv_b.reshape(1, cw),
      pad(dt_bias), pad_c(dt_bias), pad(a_log), pad_c(a_log),
      jnp.repeat(d_skip, SSD_HEAD_DIM).reshape(1, BRANCH_W), norm_w.reshape(1, BRANCH_W))


AB_COL_XBC, AB_COL_K, AB_COL_V, AB_COL_Z, AB_COL_Q, AB_COL_GA, AB_N = 0, 1536, 1792, 2048, 3072, 4096, 5120


def _ab_weights(w_in):
    wt = w_in.T
    hq, hk = ATT_HEADS * ATT_HEAD_DIM, ATT_KV_HEADS * ATT_HEAD_DIM
    gn = SSD_GROUPS * SSD_STATE
    o = 0
    q = wt[o:o + hq]; o += hq
    k = wt[o:o + hk]; o += hk
    v = wt[o:o + hk]; o += hk
    ga = wt[o:o + BRANCH_W]; o += BRANCH_W
    xs = wt[o:o + BRANCH_W]; o += BRANCH_W
    bs = wt[o:o + gn]; o += gn
    cs = wt[o:o + gn]; o += gn
    dtf = wt[o:o + SSD_HEADS]; o += SSD_HEADS
    dtb = wt[o:o + SSD_HEADS]; o += SSD_HEADS
    z = wt[o:o + BRANCH_W]
    main = jnp.concatenate([xs, bs, cs, k, v, z, q, ga], axis=0).astype(BF16)
    zpad = jnp.zeros((LANES - 2 * SSD_HEADS, wt.shape[1]), wt.dtype)
    small = jnp.concatenate([dtf, dtb, zpad], axis=0).astype(BF16)
    return main, small


def _layer0(x, mod_l, norm_w, w_in, q_norm, k_norm, conv_w, conv_b, dt_bias_f, dt_bias_b,
            a_log_f, a_log_b, d_skip, ssd_norm, w_out, final_w=None):
    s = x.shape[1]
    w_main, w_small = _ab_weights(w_in)
    proj, small_t = _inproj(x, mod_l, norm_w, w_main, w_small)
    cos_t, sin_t = _rope_tables(s)
    att = _attention(proj, cos_t, sin_t, q_norm.reshape(1, ATT_HEAD_DIM), k_norm.reshape(1, ATT_HEAD_DIM),
                     col_q=AB_COL_Q, col_k=AB_COL_K, col_v=AB_COL_V, col_ga=AB_COL_GA)
    ssd = _ssd(proj, small_t, conv_w, conv_b, jnp.stack([dt_bias_f, dt_bias_b]),
               jnp.stack([a_log_f, a_log_b]), d_skip, ssd_norm, col_xbc=AB_COL_XBC, col_z=AB_COL_Z)
    return _outproj(att, ssd, w_out.astype(BF16), x, mod_l, final_w), (proj, small_t, att, ssd)


def _unit_tri_inverse(nmats, n):
    row = lax.broadcasted_iota(jnp.int32, (n, n), 0)
    col = lax.broadcasted_iota(jnp.int32, (n, n), 1)

    def same_block(size):
        return (row // size) == (col // size)

    def mm(a, b):
        return _dot(a, b).astype(BF16)

    def as_mask(cond):
        return jnp.where(cond, 1.0, 0.0).astype(BF16)

    nmats = [m.astype(BF16) for m in nmats]
    eye = as_mask(row == col)
    base = SUBLANES
    blk = same_block(base)
    blk_m = as_mask(blk)
    nd = [m * blk_m for m in nmats]
    p1 = [mm(x, x) for x in nd]
    p2 = [mm(x, x) for x in p1]
    t = [eye - x for x in nd]
    t = [x + mm(x, p) for x, p in zip(t, p1)]
    t = [x + mm(x, p) for x, p in zip(t, p2)]
    size = base
    while size < n:
        nxt = same_block(2 * size)
        off_m = as_mask(nxt & jnp.logical_not(blk))
        et = [mm(m * off_m, x) for m, x in zip(nmats, t)]
        t = [x - mm(x, y) for x, y in zip(t, et)]
        blk = nxt
        size *= 2
    return t


def _l2norm(x):
    return x * lax.rsqrt(jnp.sum(x * x, axis=-1, keepdims=True) + EPS)


def _gdn_kernel(qf_ref, pf_ref, nf_ref, qb_ref, pb_ref, nb_ref, z_ref, smtf_ref, smtb_ref,
                cw_ref, cb_ref, bias_r_ref, bias_c_ref, alog_r_ref, alog_c_ref, nw_ref,
                o_ref, ext_scr, state_scr, acc_scr, *, nc, te):
    L, HV, HQ, DK = GDN_CHUNK, GDN_V_HEADS, GDN_QK_HEADS, GDN_HEAD_DIM
    rep = HV // HQ
    nbt = acc_scr.shape[0]
    t = pl.program_id(1)

    @pl.when(t == 0)
    def _():
        state_scr[...] = jnp.zeros_like(state_scr)

    @pl.when(t < nc)
    def _():
        chunks = (t, nc - 1 - t)
        blocks = ((qf_ref, pf_ref, nf_ref, smtf_ref), (qb_ref, pb_ref, nb_ref, smtb_ref))
        kk, qk, qh, kh, vh, colv, dec, bcol, ecol, tot, nmat = ([] for _ in range(11))
        for bi, d in [(bi, d) for bi in range(nbt) for d in range(2)]:
            c = chunks[d]
            q_ref, p_ref, n_ref, smt_ref = blocks[d]
            ext = ext_scr.at[bi * 2 + d]
            _fill_conv_window(ext, q_ref[bi], p_ref[bi], n_ref[bi], c, nc, L)
            act = _silu(_centred_conv4(ext, cw_ref, cb_ref, L, on_mxu=False))
            q_n = [_l2norm(act[:, h * DK:(h + 1) * DK]) * (DK ** -0.5) for h in range(HQ)]
            k_n = [_l2norm(act[:, (HQ + h) * DK:(HQ + h + 1) * DK]) for h in range(HQ)]
            k_b = [x.astype(BF16) for x in k_n]
            kk_d = [_dot_nt(k_b[h], k_b[h]) for h in range(HQ)]
            qk_d = [_dot_nt(q_n[h].astype(BF16), k_b[h]) for h in range(HQ)]

            raw_t = smt_ref[bi]
            raw = raw_t.T
            even = c % 2 == 0
            sm = jnp.where(even, raw[0:L, :], raw[L:2 * L, :])[:, 2 * HV * d:2 * HV * (d + 1)]
            a_raw_t = jnp.where(even, raw_t[:, 0:L], raw_t[:, L:2 * L])[2 * HV * d + HV:2 * HV * (d + 1), :]
            beta = _sigmoid(sm[:, 0:HV])
            g = -jnp.exp(alog_r_ref[d][:, 0:HV]) * _softplus(sm[:, HV:2 * HV] + bias_r_ref[d][:, 0:HV])
            g_t = -jnp.exp(alog_c_ref[d][0:HV, :]) * _softplus(a_raw_t + bias_c_ref[d][0:HV, :])
            mask, mask_t, strict = _scan_masks(d == 0, L)
            cum = _dot_exact_lhs(jnp.where(mask, 1.0, 0.0).astype(BF16), g)
            cum_t = _dot_exact_rhs(g_t, jnp.where(mask_t, 1.0, 0.0).astype(BF16))
            total = jnp.sum(g, axis=0, keepdims=True)
            for h in range(HV):
                cv = cum[:, h:h + 1]
                dc = jnp.exp(jnp.where(mask, cv - cum_t[h:h + 1, :], NEG_BIG))
                bc = beta[:, h:h + 1]
                kk.append(kk_d[h // rep]); qk.append(qk_d[h // rep])
                qh.append(q_n[h // rep]); kh.append(k_n[h // rep])
                vh.append(act[:, (2 * HQ + h) * DK:(2 * HQ + h + 1) * DK])
                colv.append(cv); dec.append(dc); bcol.append(bc); ecol.append(jnp.exp(cv))
                tot.append(total[:, h:h + 1])
                nmat.append(jnp.where(strict, kk_d[h // rep] * bc * dc, 0.0))

        idx = range(nbt * 2 * HV)
        t_inv = [x.astype(BF16) for x in _unit_tri_inverse(nmat, L)]
        u = [_dot(t_inv[i], (vh[i] * bcol[i]).astype(BF16)) for i in idx]
        w = [_dot(t_inv[i], (kh[i] * (bcol[i] * ecol[i])).astype(BF16)) for i in idx]
        s_prev = [state_scr[i // HV, :, (i % HV) * DK:(i % HV + 1) * DK] for i in idx]
        s_b = [x.astype(BF16) for x in s_prev]
        v_new = [(u[i] - _dot(w[i].astype(BF16), s_b[i])).astype(BF16) for i in idx]
        outs = [_dot((qh[i] * ecol[i]).astype(BF16), s_b[i]) + _dot((qk[i] * dec[i]).astype(BF16), v_new[i])
                for i in idx]
        for i in idx:
            k_dec = kh[i] * jnp.exp(tot[i] - colv[i])
            state_scr[i // HV, :, (i % HV) * DK:(i % HV + 1) * DK] = (
                s_prev[i] * jnp.exp(tot[i]) + _dot(k_dec.T.astype(BF16), v_new[i]))
        o_dirs = [jnp.concatenate(outs[j * HV:(j + 1) * HV], axis=1) for j in range(nbt * 2)]
        rows = [pl.multiple_of(c * L, L) for c in chunks]

        @pl.when(t < nc // 2)
        def _():
            for bi in range(nbt):
                for d in range(2):
                    acc_scr[bi, pl.ds(rows[d], L), :] = o_dirs[bi * 2 + d]

        @pl.when(t >= nc // 2)
        def _():
            for bi in range(nbt):
                for d in range(2):
                    acc_scr[bi, pl.ds(rows[d], L), :] = acc_scr[bi, pl.ds(rows[d], L), :] + o_dirs[bi * 2 + d]

    @pl.when(t >= nc)
    def _():
        r0 = pl.multiple_of((t - nc) * te, te)
        for bi in range(nbt):
            zz = z_ref[bi]
            for hv in range(HV):
                oh = acc_scr[bi, pl.ds(r0, te), hv * DK:(hv + 1) * DK]
                ms = jnp.mean(oh * oh, axis=-1, keepdims=True)
                res = oh * lax.rsqrt(ms + EPS) * nw_ref[...] * _silu(zz[:, hv * DK:(hv + 1) * DK])
                o_ref[bi, :, hv * DK:(hv + 1) * DK] = res.astype(BF16)


def _pad_dir_rows(v, n):
    return jnp.zeros((2, 1, LANES), F32).at[:, 0, :n].set(v)


def _pad_dir_cols(v, n):
    return jnp.zeros((2, LANES, 1), F32).at[:, :n, 0].set(v)


def _gdn(proj, small_t, conv_w, conv_b, dt_bias, a_log, norm_w, *, col_qkv, col_z, te=256, nbt=1):
    bsz, s, _ = proj.shape
    L = GDN_CHUNK
    nc = s // L
    te = min(te, s)
    ne = s // te
    cw = 2 * GDN_QK_HEADS * GDN_HEAD_DIM + BRANCH_W
    hb = L // HALO
    qb, zb = col_qkv // cw, col_z // BRANCH_W
    assert col_qkv % cw == 0 and col_z % BRANCH_W == 0 and nc % 2 == 0 and bsz % nbt == 0

    def cf(t):
        return jnp.minimum(t, nc - 1)

    def cbk(t):
        return jnp.maximum(nc - 1 - t, 0)

    def ep(t):
        return jnp.maximum(t - nc, 0)

    def qkv_specs(chunk):
        return [
            pl.BlockSpec((nbt, L, cw), lambda b, t: (b, chunk(t), qb)),
            pl.BlockSpec((nbt, HALO, cw), lambda b, t: (b, jnp.maximum(chunk(t) * hb - 1, 0), qb)),
            pl.BlockSpec((nbt, HALO, cw), lambda b, t: (b, jnp.minimum((chunk(t) + 1) * hb, s // HALO - 1), qb)),
        ]

    def small_spec(chunk):
        return pl.BlockSpec((nbt, LANES, 2 * L), lambda b, t: (b, 0, chunk(t) // 2))

    full = lambda shape: pl.BlockSpec(shape, lambda b, t: (0,) * len(shape))
    hv = GDN_V_HEADS
    return pl.pallas_call(
        functools.partial(_gdn_kernel, nc=nc, te=te),
        out_shape=jax.ShapeDtypeStruct((bsz, s, BRANCH_W), BF16),
        grid=(bsz // nbt, nc + ne),
        in_specs=qkv_specs(cf) + qkv_specs(cbk)
        + [pl.BlockSpec((nbt, te, BRANCH_W), lambda b, t: (b, ep(t), zb))]
        + [small_spec(cf), small_spec(cbk)]
        + [full((GDN_CONV, cw)), full((1, cw)), full((2, 1, LANES)), full((2, LANES, 1)),
           full((2, 1, LANES)), full((2, LANES, 1)), full((1, GDN_HEAD_DIM))],
        out_specs=pl.BlockSpec((nbt, te, BRANCH_W), lambda b, t: (b, ep(t), 0)),
        scratch_shapes=[
            pltpu.VMEM((nbt * 2, L + 2 * HALO, cw), F32),
            pltpu.VMEM((nbt * 2, GDN_HEAD_DIM, BRANCH_W), F32),
            pltpu.VMEM((nbt, s, BRANCH_W), F32),
        ],
        compiler_params=_cparams(("arbitrary", "arbitrary")),
        name="bidir_gated_deltanet",
    )(proj, proj, proj, proj, proj, proj, proj, small_t, small_t, conv_w, conv_b.reshape(1, cw),
      _pad_dir_rows(dt_bias, hv), _pad_dir_cols(dt_bias, hv), _pad_dir_rows(a_log, hv), _pad_dir_cols(a_log, hv),
      norm_w.reshape(1, GDN_HEAD_DIM))


def _one_minus_sq_exp(log_a, a):
    return -jnp.tanh(log_a) * (a * a + 1.0)


def _local_scan(a, u, ascending):
    rows, width = a.shape
    a = a.reshape(rows // SUBLANES, SUBLANES, width)
    u = u.reshape(rows // SUBLANES, SUBLANES, width)
    sub = lax.broadcasted_iota(jnp.int32, a.shape, 1)
    d = 1
    while d < SUBLANES:
        if ascending:
            keep = sub >= d
            shift = d
        else:
            keep = sub < SUBLANES - d
            shift = SUBLANES - d
        a_sh = jnp.where(keep, pltpu.roll(a, shift, 1), 1.0)
        u_sh = jnp.where(keep, pltpu.roll(u, shift, 1), 0.0)
        u = u + a * u_sh
        a = a * a_sh
        d *= 2
    return a.reshape(rows, width), u.reshape(rows, width)


def _lru_kernel(xl_ref, prev_ref, next_ref, gl_ref, cw_ref, cb_ref, wa_ref, ba_ref, wx_ref, bx_ref, lam_ref,
                o_ref, ext_scr, a_scr, u_scr, hf_scr, carry_scr, *, nb, unroll):
    rows, width = a_scr.shape
    bw = LRU_WIDTH // LRU_BLOCKS
    step = pl.program_id(2)
    fwd = step < nb
    blk = _chunk_of_step(step, nb)

    @pl.when((step == 0) | (step == nb))
    def _():
        carry_scr[...] = jnp.zeros_like(carry_scr)

    _fill_conv_window(ext_scr, xl_ref[0], prev_ref[0], next_ref[0], blk, nb, rows)
    xc = _centred_conv4(ext_scr, cw_ref, cb_ref, rows)
    r_parts, i_parts = [], []
    for n in range(width // bw):
        xb = xc[:, n * bw:(n + 1) * bw].astype(BF16)
        r_parts.append(_dot(xb, wa_ref[0, n]))
        i_parts.append(_dot(xb, wx_ref[0, n]))
    r = _sigmoid(jnp.concatenate(r_parts, axis=1) + ba_ref[0])
    i = _sigmoid(jnp.concatenate(i_parts, axis=1) + bx_ref[0])
    log_a = -LRU_C * r * _softplus(-lam_ref[0])
    a = jnp.exp(log_a)
    u = jnp.sqrt(_one_minus_sq_exp(log_a, a)) * (i * xc)

    n_groups = rows // SUBLANES
    base_out = pl.multiple_of(blk * rows, rows)

    def scan(ascending):
        a_loc, u_loc = _local_scan(a, u, ascending)
        a_scr[...] = a_loc
        u_scr[...] = u_loc
        last = SUBLANES - 1 if ascending else 0

        def body(it, carry):
            pos = it if ascending else n_groups - 1 - it
            r0 = pl.multiple_of(pos * SUBLANES, SUBLANES)
            h = u_scr[pl.ds(r0, SUBLANES), :] + a_scr[pl.ds(r0, SUBLANES), :] * carry
            u_scr[pl.ds(r0, SUBLANES), :] = h
            return jnp.broadcast_to(h[last:last + 1, :], h.shape)

        carry_scr[...] = lax.fori_loop(0, n_groups, body, carry_scr[...], unroll=unroll)

    @pl.when(fwd)
    def _():
        scan(True)
        hf_scr[pl.ds(base_out, rows), :] = u_scr[...]

    @pl.when(jnp.logical_not(fwd))
    def _():
        scan(False)
        o_ref[0] = ((hf_scr[pl.ds(base_out, rows), :] + u_scr[...]) * _silu(gl_ref[0])).astype(BF16)


def _lru(proj, conv_w, conv_b, wa, ba, wx, bx, lam, *, col_xl, col_gl, tt=512, width=512, unroll=8):
    bsz, s, _ = proj.shape
    tt = min(tt, s)
    nb = s // tt
    w_total = LRU_WIDTH
    bw = w_total // LRU_BLOCKS
    nbw = width // bw
    xcol, gcol = col_xl // width, col_gl // width
    assert col_xl % width == 0 and col_gl % width == 0
    hb = tt // HALO

    def blk(t):
        return _chunk_of_step(t, nb)

    def direction(t):
        return jnp.where(t < nb, 0, 1)

    return pl.pallas_call(
        functools.partial(_lru_kernel, nb=nb, unroll=unroll),
        out_shape=jax.ShapeDtypeStruct((bsz, s, w_total), BF16),
        grid=(bsz, w_total // width, 2 * nb),
        in_specs=[
            pl.BlockSpec((1, tt, width), lambda b, j, t: (b, blk(t), xcol + j)),
            pl.BlockSpec((1, HALO, width), lambda b, j, t: (b, jnp.maximum(blk(t) * hb - 1, 0), xcol + j)),
            pl.BlockSpec((1, HALO, width),
                         lambda b, j, t: (b, jnp.minimum((blk(t) + 1) * hb, s // HALO - 1), xcol + j)),
            pl.BlockSpec((1, tt, width), lambda b, j, t: (b, blk(t), gcol + j)),
            pl.BlockSpec((LRU_CONV, width), lambda b, j, t: (0, j)),
            pl.BlockSpec((1, width), lambda b, j, t: (0, j)),
            pl.BlockSpec((1, nbw, bw, bw), lambda b, j, t: (direction(t), j, 0, 0)),
            pl.BlockSpec((1, 1, width), lambda b, j, t: (direction(t), 0, j)),
            pl.BlockSpec((1, nbw, bw, bw), lambda b, j, t: (direction(t), j, 0, 0)),
            pl.BlockSpec((1, 1, width), lambda b, j, t: (direction(t), 0, j)),
            pl.BlockSpec((1, 1, width), lambda b, j, t: (direction(t), 0, j)),
        ],
        out_specs=pl.BlockSpec((1, tt, width),
                               lambda b, j, t: (b, jnp.where(t < nb, nb - 1, 2 * nb - 1 - t), j)),
        scratch_shapes=[
            pltpu.VMEM((tt + 2 * HALO, width), F32),
            pltpu.VMEM((tt, width), F32),
            pltpu.VMEM((tt, width), F32),
            pltpu.VMEM((s, width), F32),
            pltpu.VMEM((SUBLANES, width), F32),
        ],
        compiler_params=_cparams(("arbitrary", "arbitrary", "arbitrary")),
        name="bidir_rglru",
    )(proj, proj, proj, proj, conv_w, conv_b.reshape(1, w_total),
      wa.astype(BF16), ba.reshape(2, 1, w_total), wx.astype(BF16), bx.reshape(2, 1, w_total),
      lam.reshape(2, 1, w_total))


CD_COL_QKV, CD_COL_Z, CD_COL_XL, CD_COL_GL = 0, 2048, 3072, 4096


def _cd_weights(w_in):
    wt = w_in.T
    nqk = GDN_QK_HEADS * GDN_HEAD_DIM
    hv = GDN_V_HEADS
    o = 0
    q = wt[o:o + nqk]; o += nqk
    k = wt[o:o + nqk]; o += nqk
    v = wt[o:o + BRANCH_W]; o += BRANCH_W
    bf = wt[o:o + hv]; o += hv
    bb = wt[o:o + hv]; o += hv
    af = wt[o:o + hv]; o += hv
    ab = wt[o:o + hv]; o += hv
    z = wt[o:o + BRANCH_W]; o += BRANCH_W
    xl = wt[o:o + LRU_WIDTH]; o += LRU_WIDTH
    gl = wt[o:o + LRU_WIDTH]
    main = jnp.concatenate([q, k, v, z, xl, gl], axis=0).astype(BF16)
    zpad = jnp.zeros((LANES - 4 * hv, wt.shape[1]), wt.dtype)
    small = jnp.concatenate([bf, af, bb, ab, zpad], axis=0).astype(BF16)
    return main, small


def _layer1(x, mod_l, norm_w, w_in, conv_w, conv_b, a_log_f, a_log_b, dt_bias_f, dt_bias_b, gdn_norm,
            lru_conv_w, lru_conv_b, wa_f, ba_f, wx_f, bx_f, lam_f, wa_b, ba_b, wx_b, bx_b, lam_b, w_out,
            final_w=None):
    w_main, w_small = _cd_weights(w_in)
    proj, small_t = _inproj(x, mod_l, norm_w, w_main, w_small)
    gdn = _gdn(proj, small_t, conv_w, conv_b, jnp.stack([dt_bias_f, dt_bias_b]),
               jnp.stack([a_log_f, a_log_b]), gdn_norm, col_qkv=CD_COL_QKV, col_z=CD_COL_Z)
    lru = _lru(proj, lru_conv_w, lru_conv_b, jnp.stack([wa_f, wa_b]), jnp.stack([ba_f, ba_b]),
               jnp.stack([wx_f, wx_b]), jnp.stack([bx_f, bx_b]), jnp.stack([lam_f, lam_b]),
               col_xl=CD_COL_XL, col_gl=CD_COL_GL)
    out = _outproj(gdn, lru, w_out.astype(BF16), x, mod_l, final_w)
    return out, (proj, small_t, gdn, lru)


def kernel(x, c, w_mod, b_mod, norm_w, ab_w_in, ab_q_norm, ab_k_norm, ab_conv_w, ab_conv_b, ab_dt_bias_f, ab_dt_bias_b, ab_a_log_f, ab_a_log_b, ab_d_skip, ab_ssd_norm, ab_w_out, cd_w_in, cd_conv_w, cd_conv_b, cd_a_log_f, cd_a_log_b, cd_dt_bias_f, cd_dt_bias_b, cd_gdn_norm, cd_lru_conv_w, cd_lru_conv_b, cd_lru_wa_f, cd_lru_ba_f, cd_lru_wx_f, cd_lru_bx_f, cd_lru_lam_f, cd_lru_wa_b, cd_lru_ba_b, cd_lru_wx_b, cd_lru_bx_b, cd_lru_lam_b, cd_w_out, final_norm_w):
    mods = _modulation(c, w_mod, b_mod)
    x1, _ = _layer0(x, mods[0], norm_w[0], ab_w_in[0], ab_q_norm[0], ab_k_norm[0], ab_conv_w[0], ab_conv_b[0],
                    ab_dt_bias_f[0], ab_dt_bias_b[0], ab_a_log_f[0], ab_a_log_b[0], ab_d_skip[0],
                    ab_ssd_norm[0], ab_w_out[0])
    out, _ = _layer1(x1, mods[1], norm_w[1], cd_w_in[0], cd_conv_w[0], cd_conv_b[0], cd_a_log_f[0], cd_a_log_b[0],
                     cd_dt_bias_f[0], cd_dt_bias_b[0], cd_gdn_norm[0], cd_lru_conv_w[0], cd_lru_conv_b[0],
                     cd_lru_wa_f[0], cd_lru_ba_f[0], cd_lru_wx_f[0], cd_lru_bx_f[0], cd_lru_lam_f[0],
                     cd_lru_wa_b[0], cd_lru_ba_b[0], cd_lru_wx_b[0], cd_lru_bx_b[0], cd_lru_lam_b[0],
                     cd_w_out[0], final_norm_w)
    return out
```

```python
import functools
import math

import jax
import jax.numpy as jnp
import numpy as np
from jax import lax
from jax.experimental import pallas as pl
from jax.experimental.pallas import tpu as pltpu

F32 = jnp.float32
BF16 = jnp.bfloat16

D_MODEL = 2048
GRID_W = 64
EPS = 1e-6
BRANCH_W = D_MODEL // 2
ATT_HEAD_DIM = 128
ATT_HEADS = BRANCH_W // ATT_HEAD_DIM
ATT_KV_HEADS = ATT_HEADS // 4
ATT_GROUP = ATT_HEADS // ATT_KV_HEADS
ROPE_THETA = 10000.0
SSD_HEAD_DIM = 64
SSD_HEADS = BRANCH_W // SSD_HEAD_DIM
SSD_GROUPS = 2
SSD_STATE = 128
SSD_CONV = 4
SSD_CHUNK = 128
GDN_HEAD_DIM = 128
GDN_V_HEADS = BRANCH_W // GDN_HEAD_DIM
GDN_QK_HEADS = GDN_V_HEADS // 2
GDN_CONV = 4
GDN_CHUNK = 128
LRU_WIDTH = BRANCH_W
LRU_BLOCKS = 8
LRU_CONV = 4
LRU_C = 8.0

LANES = 128
SUBLANES = 8
VMEM_LIMIT_BYTES = 56 * 1024 * 1024

HALO = SUBLANES
NEG_BIG = -1e30
TINY = 1e-37


def _cparams(sem):
    return pltpu.CompilerParams(dimension_semantics=sem, vmem_limit_bytes=VMEM_LIMIT_BYTES)


def _sigmoid(x):
    return 0.5 * jnp.tanh(0.5 * x) + 0.5


def _silu(x):
    h = 0.5 * x
    return h * (jnp.tanh(h) + 1.0)


def _softplus(x):
    return jnp.maximum(x, 0.0) + jnp.log(1.0 + jnp.exp(-jnp.abs(x)))


def _split_bf16(a):
    hi = a.astype(BF16)
    lo = (a - hi.astype(F32)).astype(BF16)
    return hi, lo


def _dot(a, b):
    return jnp.dot(a, b, preferred_element_type=F32)


def _dot_nt(a, b):
    return lax.dot_general(a, b, (((1,), (1,)), ((), ())), preferred_element_type=F32)


def _dot_exact_rhs(a, b_bf16):
    hi, lo = _split_bf16(a)
    return _dot(hi, b_bf16) + _dot(lo, b_bf16)


def _dot_exact_lhs(a_bf16, b):
    hi, lo = _split_bf16(b)
    return _dot(a_bf16, hi) + _dot(a_bf16, lo)


def _mod_kernel(c_ref, w_ref, b_ref, o_ref):
    cond = _silu(c_ref[...])
    c_hi, c_lo = _split_bf16(cond)
    w = w_ref[0]
    w_hi, w_lo = _split_bf16(w)
    o_ref[0] = _dot(c_hi, w_hi) + _dot(c_lo, w_hi) + _dot(c_hi, w_lo) + b_ref[0]


def _modulation(c, w_mod, b_mod):
    depth, d, n = w_mod.shape
    bsz = c.shape[0]
    rows = -(-bsz // SUBLANES) * SUBLANES
    c_pad = jnp.zeros((rows, d), F32).at[:bsz].set(c)
    tn = 1024
    out = pl.pallas_call(
        _mod_kernel,
        out_shape=jax.ShapeDtypeStruct((depth, rows, n), F32),
        grid=(depth, n // tn),
        in_specs=[
            pl.BlockSpec((rows, d), lambda l, j: (0, 0)),
            pl.BlockSpec((1, d, tn), lambda l, j: (l, 0, j)),
            pl.BlockSpec((1, 1, tn), lambda l, j: (l, 0, j)),
        ],
        out_specs=pl.BlockSpec((1, rows, tn), lambda l, j: (l, 0, j)),
        compiler_params=_cparams(("arbitrary", "arbitrary")),
        name="adaln_mod",
    )(c_pad, w_mod, b_mod.reshape(depth, 1, n))
    return out[:, :bsz].reshape(depth, bsz, 3, d)


def _inproj_kernel(x_ref, mod_ref, nw_ref, w_ref, wst_ref, o_ref, ost_ref, h_even, h_odd, *, n_groups):
    g = pl.program_id(0)
    j = pl.program_id(1)
    rows = x_ref.shape[1]

    def prep(h_dst):
        x = x_ref[0]
        ms = jnp.mean(x * x, axis=-1, keepdims=True)
        y = x * lax.rsqrt(ms + EPS) * nw_ref[...]
        h = (y * (1.0 + mod_ref[0, 1:2, :]) + mod_ref[0, 0:1, :]).astype(BF16)
        h_dst[pl.ds(pl.multiple_of(j * rows, rows), rows), :] = h
        ost_ref[0] = _dot_nt(wst_ref[...], h)

    @pl.when(g == 0)
    def _():
        prep(h_even)

    @pl.when((g > 0) & (g % 2 == 0))
    def _():
        o_ref[0] = _dot_nt(h_odd[...], w_ref[...])
        prep(h_even)

    @pl.when(g % 2 == 1)
    def _():
        o_ref[0] = _dot_nt(h_even[...], w_ref[...])
        prep(h_odd)


def _inproj(x, mod_l, norm_w, w_main, w_small, *, tm=1024, nj=4):
    bsz, s, d = x.shape
    n = w_main.shape[0]
    ns = w_small.shape[0]
    tm = min(tm, s)
    ni = s // tm
    n_groups = bsz * ni
    tn = n // nj
    rows = tm // nj
    assert n % nj == 0 and tn % LANES == 0 and rows % LANES == 0

    def prep(g, j):
        gc = jnp.minimum(g, n_groups - 1)
        return gc // ni, (gc % ni) * nj + jnp.where(g < n_groups, j, nj - 1)

    def mm(g, j):
        gm = jnp.maximum(g - 1, 0)
        return gm // ni, gm % ni, jnp.where(g > 0, j, 0)

    return pl.pallas_call(
        functools.partial(_inproj_kernel, n_groups=n_groups),
        out_shape=[jax.ShapeDtypeStruct((bsz, s, n), F32), jax.ShapeDtypeStruct((bsz, ns, s), F32)],
        grid=(n_groups + 1, nj),
        in_specs=[
            pl.BlockSpec((1, rows, d), lambda g, j: (*prep(g, j), 0)),
            pl.BlockSpec((1, 3, d), lambda g, j: (prep(g, j)[0], 0, 0)),
            pl.BlockSpec((1, d), lambda g, j: (0, 0)),
            pl.BlockSpec((tn, d), lambda g, j: (mm(g, j)[2], 0)),
            pl.BlockSpec((ns, d), lambda g, j: (0, 0)),
        ],
        out_specs=[
            pl.BlockSpec((1, tm, tn), lambda g, j: mm(g, j)),
            pl.BlockSpec((1, ns, rows), lambda g, j: (prep(g, j)[0], 0, prep(g, j)[1])),
        ],
        scratch_shapes=[pltpu.VMEM((tm, d), BF16), pltpu.VMEM((tm, d), BF16)],
        compiler_params=_cparams(("arbitrary", "arbitrary")),
        name="norm_mod_inproj",
    )(x, mod_l, norm_w.reshape(1, d), w_main, w_small)


def _outproj_kernel(ya_ref, yb_ref, w_ref, x_ref, mod_ref, *rest, final_norm, half):
    if final_norm:
        fnw_ref, o_ref = rest
    else:
        (o_ref,) = rest
    acc = _dot(ya_ref[0], w_ref[0:half, :]) + _dot(yb_ref[0], w_ref[half:, :])
    gate = mod_ref[0, 2:3, :]
    xn = x_ref[0] + gate * acc
    if final_norm:
        ms = jnp.mean(xn * xn, axis=-1, keepdims=True)
        xn = xn * lax.rsqrt(ms + EPS) * fnw_ref[...]
    o_ref[0] = xn


def _outproj(ya, yb, w_out, x, mod_l, final_w=None, *, tm=512):
    bsz, s, d = x.shape
    half = ya.shape[-1]
    tm = min(tm, s)
    final_norm = final_w is not None
    in_specs = [
        pl.BlockSpec((1, tm, half), lambda b, i: (b, i, 0)),
        pl.BlockSpec((1, tm, half), lambda b, i: (b, i, 0)),
        pl.BlockSpec((2 * half, d), lambda b, i: (0, 0)),
        pl.BlockSpec((1, tm, d), lambda b, i: (b, i, 0)),
        pl.BlockSpec((1, 3, d), lambda b, i: (b, 0, 0)),
    ]
    args = [ya, yb, w_out, x, mod_l]
    if final_norm:
        in_specs.append(pl.BlockSpec((1, d), lambda b, i: (0, 0)))
        args.append(final_w.reshape(1, d))
    return pl.pallas_call(
        functools.partial(_outproj_kernel, final_norm=final_norm, half=half),
        out_shape=jax.ShapeDtypeStruct((bsz, s, d), F32),
        grid=(bsz, s // tm),
        in_specs=in_specs,
        out_specs=pl.BlockSpec((1, tm, d), lambda b, i: (b, i, 0)),
        compiler_params=_cparams(("arbitrary", "arbitrary")),
        name="outproj_residual",
    )(*args)


def _rms_rope(x, nw, cos, sin_signed):
    ms = jnp.mean(x * x, axis=-1, keepdims=True)
    y = x * lax.rsqrt(ms + EPS) * nw
    lane = lax.broadcasted_iota(jnp.int32, y.shape, 1)
    partner = jnp.where(lane % 2 == 0, pltpu.roll(y, ATT_HEAD_DIM - 1, 1), pltpu.roll(y, 1, 1))
    return y * cos + partner * sin_signed


def _attn_kernel(q_ref, k_ref, v_ref, ga_ref, cq_ref, sq_ref, ck_ref, sk_ref, qn_ref, kn_ref,
                 o_ref, k_scr, v_scr, *, tq):
    qi = pl.program_id(2)

    dh = ATT_HEAD_DIM

    @pl.when(qi == 0)
    def _():
        k_scr[...] = _rms_rope(k_ref[0], kn_ref[...], ck_ref[...], sk_ref[...]).astype(BF16)
        v_scr[:, 0:dh] = v_ref[0].astype(BF16)
        v_scr[:, dh:2 * dh] = jnp.ones((v_scr.shape[0], dh), BF16)

    scale = dh ** -0.5
    cos = cq_ref[...]
    sin = sq_ref[...]

    def scores(g):
        qg = (_rms_rope(q_ref[0, :, g * dh:(g + 1) * dh], qn_ref[...], cos, sin) * scale).astype(BF16)
        return _dot_nt(qg, k_scr[...])

    s_next = scores(0)
    for g in range(ATT_GROUP):
        s = s_next
        if g + 1 < ATT_GROUP:
            s_next = scores(g + 1)
        s = s.astype(BF16)
        p = jnp.exp(s - jnp.max(s, axis=-1, keepdims=True))
        o_ext = _dot(p, v_scr[...])
        og = o_ext[:, 0:dh] / o_ext[:, dh:2 * dh] * _silu(ga_ref[0, :, g * dh:(g + 1) * dh])
        o_ref[0, :, g * dh:(g + 1) * dh] = og.astype(BF16)


def _attention(proj, cos_t, sin_t, q_norm, k_norm, *, col_q, col_k, col_v, col_ga, tq=512):
    bsz, s, _ = proj.shape
    tq = min(tq, s)
    gw = ATT_GROUP * ATT_HEAD_DIM
    dh = ATT_HEAD_DIM
    qb, kb, vb, gb = col_q // gw, col_k // dh, col_v // dh, col_ga // gw
    return pl.pallas_call(
        functools.partial(_attn_kernel, tq=tq),
        out_shape=jax.ShapeDtypeStruct((bsz, s, BRANCH_W), BF16),
        grid=(bsz, ATT_KV_HEADS, s // tq),
        in_specs=[
            pl.BlockSpec((1, tq, gw), lambda b, h, i: (b, i, qb + h)),
            pl.BlockSpec((1, s, dh), lambda b, h, i: (b, 0, kb + h)),
            pl.BlockSpec((1, s, dh), lambda b, h, i: (b, 0, vb + h)),
            pl.BlockSpec((1, tq, gw), lambda b, h, i: (b, i, gb + h)),
            pl.BlockSpec((tq, dh), lambda b, h, i: (i, 0)),
            pl.BlockSpec((tq, dh), lambda b, h, i: (i, 0)),
            pl.BlockSpec((s, dh), lambda b, h, i: (0, 0)),
            pl.BlockSpec((s, dh), lambda b, h, i: (0, 0)),
            pl.BlockSpec((1, dh), lambda b, h, i: (0, 0)),
            pl.BlockSpec((1, dh), lambda b, h, i: (0, 0)),
        ],
        out_specs=pl.BlockSpec((1, tq, gw), lambda b, h, i: (b, i, h)),
        scratch_shapes=[pltpu.VMEM((s, dh), BF16), pltpu.VMEM((s, 2 * dh), BF16)],
        compiler_params=_cparams(("arbitrary", "arbitrary", "arbitrary")),
        name="gqa_attention",
    )(proj, proj, proj, proj, cos_t, sin_t, cos_t, sin_t, q_norm, k_norm)


def _rope_tables(s):
    t = np.arange(s)
    row = (t // GRID_W).astype(np.float64)
    col = (t % GRID_W).astype(np.float64)
    n_pairs = ATT_HEAD_DIM // 4
    freqs = ROPE_THETA ** (-np.arange(n_pairs, dtype=np.float64) / n_pairs)
    ang = np.concatenate([row[:, None] * freqs, col[:, None] * freqs], axis=-1)
    cos, sin = np.cos(ang), np.sin(ang)
    cos_t = np.repeat(cos, 2, axis=-1)
    sin_t = np.stack([-sin, sin], axis=-1).reshape(s, ATT_HEAD_DIM)
    return jnp.asarray(cos_t, F32), jnp.asarray(sin_t, F32)


def _chunk_of_step(step, nc):
    return jnp.where(step < nc, step, 2 * nc - 1 - step)


def _fill_conv_window(ext_scr, cur, prev, nxt, c, nc, rows):
    ext_scr[0:HALO, :] = jnp.where(c > 0, prev, 0.0)
    ext_scr[HALO:HALO + rows, :] = cur
    ext_scr[HALO + rows:HALO + rows + HALO, :] = jnp.where(c < nc - 1, nxt, 0.0)


CONV_SUB = 128


def _centred_conv4(ext_scr, cw_ref, cb_ref, rows):
    sub = min(rows, CONV_SUB)
    n_in = sub + 2 * HALO
    taps = (0, 1, 3)
    r = lax.broadcasted_iota(jnp.int32, (len(taps) * sub, n_in), 0)
    c = lax.broadcasted_iota(jnp.int32, (len(taps) * sub, n_in), 1)
    src = jnp.zeros_like(r)
    for i, k in enumerate(taps):
        src = jnp.where((r >= i * sub) & (r < (i + 1) * sub), r - i * sub + HALO + (k - 2), src)
    shift_mat = jnp.where(c == src, 1.0, 0.0).astype(BF16)
    outs = []
    for s0 in range(0, rows, sub):
        win = ext_scr[s0:s0 + n_in, :]
        shifted = _dot(shift_mat, win.astype(BF16))
        acc = cb_ref[...] + win[HALO:HALO + sub, :] * cw_ref[2:3, :]
        for i, k in enumerate(taps):
            acc = acc + shifted[i * sub:(i + 1) * sub, :] * cw_ref[k:k + 1, :]
        outs.append(acc)
    return outs[0] if len(outs) == 1 else jnp.concatenate(outs, axis=0)


def _scan_masks(fwd, n):
    row = lax.broadcasted_iota(jnp.int32, (n, n), 0)
    col = lax.broadcasted_iota(jnp.int32, (n, n), 1)
    sgn = jnp.where(fwd, 1, -1)
    d = (row - col) * sgn
    return d >= 0, d <= 0, d > 0


def _head_expander(heads, width):
    r = lax.broadcasted_iota(jnp.int32, (heads, heads * width), 0)
    c = lax.broadcasted_iota(jnp.int32, (heads, heads * width), 1)
    return jnp.where((c >= r * width) & (c < (r + 1) * width), 1.0, 0.0).astype(BF16)


def _ssd_kernel(xbc_ref, prev_ref, next_ref, z_ref, dtt_ref, cw_ref, cb_ref,
                bias_r_ref, bias_c_ref, alog_r_ref, alog_c_ref, dskip_ref, nw_ref,
                o_ref, ext_scr, state_scr, yf_scr, xbc_scr, *, nc):
    L, H, P, N = SSD_CHUNK, SSD_HEADS, SSD_HEAD_DIM, SSD_STATE
    HG = H // SSD_GROUPS
    GW = HG * P
    step = pl.program_id(1)
    fwd = step < nc
    c = _chunk_of_step(step, nc)

    @pl.when((step == 0) | (step == nc))
    def _():
        state_scr[...] = jnp.zeros_like(state_scr)

    row0 = pl.multiple_of(c * L, L)

    @pl.when(fwd)
    def _():
        _fill_conv_window(ext_scr, xbc_ref[0], prev_ref[0], next_ref[0], c, nc, L)
        xbc_scr[pl.ds(row0, L), :] = _silu(_centred_conv4(ext_scr, cw_ref, cb_ref, L))

    xbc = xbc_scr[pl.ds(row0, L), :]
    xs = xbc[:, :BRANCH_W]
    bs = xbc[:, BRANCH_W:BRANCH_W + SSD_GROUPS * N]
    cs = xbc[:, BRANCH_W + SSD_GROUPS * N:]

    raw_t = dtt_ref[0]
    raw = raw_t.T
    dt = _softplus(jnp.where(fwd, raw[:, 0:H], raw[:, H:2 * H]) + bias_r_ref[0][:, 0:H])
    a = dt * (-jnp.exp(alog_r_ref[0][:, 0:H]))
    dt_t = _softplus(jnp.where(fwd, raw_t[0:H, :], raw_t[H:2 * H, :]) + bias_c_ref[0][0:H, :])
    a_t = dt_t * (-jnp.exp(alog_c_ref[0][0:H, :]))
    mask, mask_t, _ = _scan_masks(fwd, L)
    cum = _dot_exact_lhs(jnp.where(mask, 1.0, 0.0).astype(BF16), a)
    cum_t = _dot_exact_rhs(a_t, jnp.where(mask_t, 1.0, 0.0).astype(BF16))
    total = jnp.sum(a, axis=0, keepdims=True)

    expand = _head_expander(H, P)
    dt_e = _dot_exact_rhs(dt, expand)
    p_e = _dot_exact_rhs(jnp.exp(cum), expand)
    q_e = _dot_exact_rhs(jnp.exp(total - cum), expand)
    tot_e = _dot_exact_rhs(jnp.broadcast_to(jnp.exp(total), (SUBLANES, H)), expand)[0:1, :]
    xd = xs * dt_e
    xdq = (xd * q_e).astype(BF16)
    xd_b = xd.astype(BF16)
    lane = lax.broadcasted_iota(jnp.int32, (L, 2 * P), 1)

    y_parts = []
    for g in range(SSD_GROUPS):
        cg = cs[:, g * N:(g + 1) * N].astype(BF16)
        bg = bs[:, g * N:(g + 1) * N]
        gmat = _dot_nt(cg, bg.astype(BF16))
        h_prev = state_scr[:, g * GW:(g + 1) * GW]
        y_off = _dot(cg, h_prev.astype(BF16)) * p_e[:, g * GW:(g + 1) * GW]
        pairs = []
        for hp in range(HG // 2):
            h0 = g * HG + 2 * hp
            xpair = xd_b[:, h0 * P:(h0 + 2) * P]
            ys = []
            for h in (h0, h0 + 1):
                dec = jnp.exp(jnp.where(mask, cum[:, h:h + 1] - cum_t[h:h + 1, :], NEG_BIG))
                ys.append(_dot((gmat * dec).astype(BF16), xpair))
            pairs.append(jnp.where(lane < P, ys[0], ys[1]))
        y_parts.append(jnp.concatenate(pairs, axis=1) + y_off)
        state_scr[:, g * GW:(g + 1) * GW] = (
            h_prev * tot_e[:, g * GW:(g + 1) * GW] + _dot(bg.T.astype(BF16), xdq[:, g * GW:(g + 1) * GW]))
    y_dir = jnp.concatenate(y_parts, axis=1)

    @pl.when(fwd)
    def _():
        yf_scr[pl.ds(row0, L), :] = y_dir

    @pl.when(jnp.logical_not(fwd))
    def _():
        y = yf_scr[pl.ds(row0, L), :] + y_dir + dskip_ref[...] * xs
        y = y * _silu(z_ref[0])
        ms = jnp.mean(y * y, axis=-1, keepdims=True)
        o_ref[0] = (y * lax.rsqrt(ms + EPS) * nw_ref[...]).astype(BF16)


def _ssd(proj, small_t, conv_w, conv_b, dt_bias, a_log, d_skip, norm_w, *, col_xbc, col_z):
    bsz, s, _ = proj.shape
    L = SSD_CHUNK
    nc = s // L
    cw = BRANCH_W + 2 * SSD_GROUPS * SSD_STATE
    hb = L // HALO
    xb, zb = col_xbc // cw, col_z // BRANCH_W
    assert col_xbc % cw == 0 and col_z % BRANCH_W == 0

    def chunk(t):
        return _chunk_of_step(t, nc)

    def direction(t):
        return jnp.where(t < nc, 0, 1)

    def conv_chunk(t):
        return jnp.minimum(t, nc - 1)

    pad = lambda v: jnp.zeros((2, 1, LANES), F32).at[:, 0, :SSD_HEADS].set(v)
    pad_c = lambda v: jnp.zeros((2, LANES, 1), F32).at[:, :SSD_HEADS, 0].set(v)
    return pl.pallas_call(
        functools.partial(_ssd_kernel, nc=nc),
        out_shape=jax.ShapeDtypeStruct((bsz, s, BRANCH_W), BF16),
        grid=(bsz, 2 * nc),
        in_specs=[
            pl.BlockSpec((1, L, cw), lambda b, t: (b, conv_chunk(t), xb)),
            pl.BlockSpec((1, HALO, cw), lambda b, t: (b, jnp.maximum(conv_chunk(t) * hb - 1, 0), xb)),
            pl.BlockSpec((1, HALO, cw), lambda b, t: (b, jnp.minimum((conv_chunk(t) + 1) * hb, s // HALO - 1), xb)),
            pl.BlockSpec((1, L, BRANCH_W), lambda b, t: (b, chunk(t), zb)),
            pl.BlockSpec((1, LANES, L), lambda b, t: (b, 0, chunk(t))),
            pl.BlockSpec((SSD_CONV, cw), lambda b, t: (0, 0)),
            pl.BlockSpec((1, cw), lambda b, t: (0, 0)),
            pl.BlockSpec((1, 1, LANES), lambda b, t: (direction(t), 0, 0)),
            pl.BlockSpec((1, LANES, 1), lambda b, t: (direction(t), 0, 0)),
            pl.BlockSpec((1, 1, LANES), lambda b, t: (direction(t), 0, 0)),
            pl.BlockSpec((1, LANES, 1), lambda b, t: (direction(t), 0, 0)),
            pl.BlockSpec((1, BRANCH_W), lambda b, t: (0, 0)),
            pl.BlockSpec((1, BRANCH_W), lambda b, t: (0, 0)),
        ],
        out_specs=pl.BlockSpec((1, L, BRANCH_W), lambda b, t: (b, jnp.where(t < nc, nc - 1, 2 * nc - 1 - t), 0)),
        scratch_shapes=[
            pltpu.VMEM((L + 2 * HALO, cw), F32),
            pltpu.VMEM((SSD_STATE, BRANCH_W), F32),
            pltpu.VMEM((s, BRANCH_W), F32),
            pltpu.VMEM((s, cw), F32),
        ],
        compiler_params=_cparams(("arbitrary", "arbitrary")),
        name="bidir_ssd",
    )(proj, proj, proj, proj, small_t, conv_w, conv_b.reshape(1, cw),
      pad(dt_bias), pad_c(dt_bias), pad(a_log), pad_c(a_log),
      jnp.repeat(d_skip, SSD_HEAD_DIM).reshape(1, BRANCH_W), norm_w.reshape(1, BRANCH_W))


AB_COL_XBC, AB_COL_K, AB_COL_V, AB_COL_Z, AB_COL_Q, AB_COL_GA, AB_N = 0, 1536, 1792, 2048, 3072, 4096, 5120


def _ab_weights(w_in):
    wt = w_in.T
    hq, hk = ATT_HEADS * ATT_HEAD_DIM, ATT_KV_HEADS * ATT_HEAD_DIM
    gn = SSD_GROUPS * SSD_STATE
    o = 0
    q = wt[o:o + hq]; o += hq
    k = wt[o:o + hk]; o += hk
    v = wt[o:o + hk]; o += hk
    ga = wt[o:o + BRANCH_W]; o += BRANCH_W
    xs = wt[o:o + BRANCH_W]; o += BRANCH_W
    bs = wt[o:o + gn]; o += gn
    cs = wt[o:o + gn]; o += gn
    dtf = wt[o:o + SSD_HEADS]; o += SSD_HEADS
    dtb = wt[o:o + SSD_HEADS]; o += SSD_HEADS
    z = wt[o:o + BRANCH_W]
    main = jnp.concatenate([xs, bs, cs, k, v, z, q, ga], axis=0).astype(BF16)
    zpad = jnp.zeros((LANES - 2 * SSD_HEADS, wt.shape[1]), wt.dtype)
    small = jnp.concatenate([dtf, dtb, zpad], axis=0).astype(BF16)
    return main, small


def _layer0(x, mod_l, norm_w, w_in, q_norm, k_norm, conv_w, conv_b, dt_bias_f, dt_bias_b,
            a_log_f, a_log_b, d_skip, ssd_norm, w_out, final_w=None):
    s = x.shape[1]
    w_main, w_small = _ab_weights(w_in)
    proj, small_t = _inproj(x, mod_l, norm_w, w_main, w_small)
    cos_t, sin_t = _rope_tables(s)
    att = _attention(proj, cos_t, sin_t, q_norm.reshape(1, ATT_HEAD_DIM), k_norm.reshape(1, ATT_HEAD_DIM),
                     col_q=AB_COL_Q, col_k=AB_COL_K, col_v=AB_COL_V, col_ga=AB_COL_GA)
    ssd = _ssd(proj, small_t, conv_w, conv_b, jnp.stack([dt_bias_f, dt_bias_b]),
               jnp.stack([a_log_f, a_log_b]), d_skip, ssd_norm, col_xbc=AB_COL_XBC, col_z=AB_COL_Z)
    return _outproj(att, ssd, w_out.astype(BF16), x, mod_l, final_w), (proj, small_t, att, ssd)


def _unit_tri_inverse(nmats, n):
    row = lax.broadcasted_iota(jnp.int32, (n, n), 0)
    col = lax.broadcasted_iota(jnp.int32, (n, n), 1)

    def same_block(size):
        return (row // size) == (col // size)

    def mm(a, b):
        return _dot(a, b).astype(BF16)

    def as_mask(cond):
        return jnp.where(cond, 1.0, 0.0).astype(BF16)

    nmats = [m.astype(BF16) for m in nmats]
    eye = as_mask(row == col)
    base = SUBLANES
    blk = same_block(base)
    blk_m = as_mask(blk)
    nd = [m * blk_m for m in nmats]
    p1 = [mm(x, x) for x in nd]
    p2 = [mm(x, x) for x in p1]
    t = [eye - x for x in nd]
    t = [x + mm(x, p) for x, p in zip(t, p1)]
    t = [x + mm(x, p) for x, p in zip(t, p2)]
    size = base
    while size < n:
        nxt = same_block(2 * size)
        off_m = as_mask(nxt & jnp.logical_not(blk))
        et = [mm(m * off_m, x) for m, x in zip(nmats, t)]
        t = [x - mm(x, y) for x, y in zip(t, et)]
        blk = nxt
        size *= 2
    return t


def _l2norm(x):
    return x * lax.rsqrt(jnp.sum(x * x, axis=-1, keepdims=True) + EPS)


def _gdn_kernel(qf_ref, pf_ref, nf_ref, qb_ref, pb_ref, nb_ref, z_ref, smtf_ref, smtb_ref,
                cw_ref, cb_ref, bias_r_ref, bias_c_ref, alog_r_ref, alog_c_ref, nw_ref,
                o_ref, ext_scr, state_scr, acc_scr, *, nc, te):
    L, HV, HQ, DK = GDN_CHUNK, GDN_V_HEADS, GDN_QK_HEADS, GDN_HEAD_DIM
    rep = HV // HQ
    nbt = acc_scr.shape[0]
    t = pl.program_id(1)

    @pl.when(t == 0)
    def _():
        state_scr[...] = jnp.zeros_like(state_scr)

    @pl.when(t < nc)
    def _():
        chunks = (t, nc - 1 - t)
        blocks = ((qf_ref, pf_ref, nf_ref, smtf_ref), (qb_ref, pb_ref, nb_ref, smtb_ref))
        kk, qk, qh, kh, vh, colv, dec, bcol, ecol, tot, nmat = ([] for _ in range(11))
        for bi, d in [(bi, d) for bi in range(nbt) for d in range(2)]:
            c = chunks[d]
            q_ref, p_ref, n_ref, smt_ref = blocks[d]
            ext = ext_scr.at[bi * 2 + d]
            _fill_conv_window(ext, q_ref[bi], p_ref[bi], n_ref[bi], c, nc, L)
            act = _silu(_centred_conv4(ext, cw_ref, cb_ref, L))
            q_n = [_l2norm(act[:, h * DK:(h + 1) * DK]) * (DK ** -0.5) for h in range(HQ)]
            k_n = [_l2norm(act[:, (HQ + h) * DK:(HQ + h + 1) * DK]) for h in range(HQ)]
            k_b = [x.astype(BF16) for x in k_n]
            kk_d = [_dot_nt(k_b[h], k_b[h]) for h in range(HQ)]
            qk_d = [_dot_nt(q_n[h].astype(BF16), k_b[h]) for h in range(HQ)]

            raw_t = smt_ref[bi]
            sm = raw_t.T[:, 2 * HV * d:2 * HV * (d + 1)]
            a_raw_t = raw_t[2 * HV * d + HV:2 * HV * (d + 1), :]
            beta = _sigmoid(sm[:, 0:HV])
            g = -jnp.exp(alog_r_ref[d][:, 0:HV]) * _softplus(sm[:, HV:2 * HV] + bias_r_ref[d][:, 0:HV])
            g_t = -jnp.exp(alog_c_ref[d][0:HV, :]) * _softplus(a_raw_t + bias_c_ref[d][0:HV, :])
            mask, mask_t, strict = _scan_masks(d == 0, L)
            cum = _dot_exact_lhs(jnp.where(mask, 1.0, 0.0).astype(BF16), g)
            cum_t = _dot_exact_rhs(g_t, jnp.where(mask_t, 1.0, 0.0).astype(BF16))
            total = jnp.sum(g, axis=0, keepdims=True)
            for h in range(HV):
                cv = cum[:, h:h + 1]
                dc = jnp.exp(jnp.where(mask, cv - cum_t[h:h + 1, :], NEG_BIG))
                bc = beta[:, h:h + 1]
                kk.append(kk_d[h // rep]); qk.append(qk_d[h // rep])
                qh.append(q_n[h // rep]); kh.append(k_n[h // rep])
                vh.append(act[:, (2 * HQ + h) * DK:(2 * HQ + h + 1) * DK])
                colv.append(cv); dec.append(dc); bcol.append(bc); ecol.append(jnp.exp(cv))
                tot.append(total[:, h:h + 1])
                nmat.append(jnp.where(strict, kk_d[h // rep] * bc * dc, 0.0))

        idx = range(nbt * 2 * HV)
        t_inv = _unit_tri_inverse(nmat, L)
        u = [_dot(t_inv[i], (vh[i] * bcol[i]).astype(BF16)) for i in idx]
        w = [_dot(t_inv[i], (kh[i] * (bcol[i] * ecol[i])).astype(BF16)) for i in idx]
        s_prev = [state_scr[i // HV, :, (i % HV) * DK:(i % HV + 1) * DK] for i in idx]
        s_b = [x.astype(BF16) for x in s_prev]
        v_new = [(u[i] - _dot(w[i].astype(BF16), s_b[i])).astype(BF16) for i in idx]
        outs = [_dot((qh[i] * ecol[i]).astype(BF16), s_b[i]) + _dot((qk[i] * dec[i]).astype(BF16), v_new[i])
                for i in idx]
        for i in idx:
            k_dec = kh[i] * jnp.exp(tot[i] - colv[i])
            state_scr[i // HV, :, (i % HV) * DK:(i % HV + 1) * DK] = (
                s_prev[i] * jnp.exp(tot[i]) + _dot(k_dec.T.astype(BF16), v_new[i]))
        o_dirs = [jnp.concatenate(outs[j * HV:(j + 1) * HV], axis=1) for j in range(nbt * 2)]
        rows = [pl.multiple_of(c * L, L) for c in chunks]

        @pl.when(t < nc // 2)
        def _():
            for bi in range(nbt):
                for d in range(2):
                    acc_scr[bi, pl.ds(rows[d], L), :] = o_dirs[bi * 2 + d]

        @pl.when(t >= nc // 2)
        def _():
            for bi in range(nbt):
                for d in range(2):
                    acc_scr[bi, pl.ds(rows[d], L), :] = acc_scr[bi, pl.ds(rows[d], L), :] + o_dirs[bi * 2 + d]

    @pl.when(t >= nc)
    def _():
        r0 = pl.multiple_of((t - nc) * te, te)
        for bi in range(nbt):
            zz = z_ref[bi]
            for hv in range(HV):
                oh = acc_scr[bi, pl.ds(r0, te), hv * DK:(hv + 1) * DK]
                ms = jnp.mean(oh * oh, axis=-1, keepdims=True)
                res = oh * lax.rsqrt(ms + EPS) * nw_ref[...] * _silu(zz[:, hv * DK:(hv + 1) * DK])
                o_ref[bi, :, hv * DK:(hv + 1) * DK] = res.astype(BF16)


def _pad_dir_rows(v, n):
    return jnp.zeros((2, 1, LANES), F32).at[:, 0, :n].set(v)


def _pad_dir_cols(v, n):
    return jnp.zeros((2, LANES, 1), F32).at[:, :n, 0].set(v)


def _gdn(proj, small_t, conv_w, conv_b, dt_bias, a_log, norm_w, *, col_qkv, col_z, te=256, nbt=1):
    bsz, s, _ = proj.shape
    L = GDN_CHUNK
    nc = s // L
    te = min(te, s)
    ne = s // te
    cw = 2 * GDN_QK_HEADS * GDN_HEAD_DIM + BRANCH_W
    hb = L // HALO
    qb, zb = col_qkv // cw, col_z // BRANCH_W
    assert col_qkv % cw == 0 and col_z % BRANCH_W == 0 and nc % 2 == 0 and bsz % nbt == 0

    def cf(t):
        return jnp.minimum(t, nc - 1)

    def cbk(t):
        return jnp.maximum(nc - 1 - t, 0)

    def ep(t):
        return jnp.maximum(t - nc, 0)

    def qkv_specs(chunk):
        return [
            pl.BlockSpec((nbt, L, cw), lambda b, t: (b, chunk(t), qb)),
            pl.BlockSpec((nbt, HALO, cw), lambda b, t: (b, jnp.maximum(chunk(t) * hb - 1, 0), qb)),
            pl.BlockSpec((nbt, HALO, cw), lambda b, t: (b, jnp.minimum((chunk(t) + 1) * hb, s // HALO - 1), qb)),
        ]

    def small_spec(chunk):
        return pl.BlockSpec((nbt, LANES, L), lambda b, t: (b, 0, chunk(t)))

    full = lambda shape: pl.BlockSpec(shape, lambda b, t: (0,) * len(shape))
    hv = GDN_V_HEADS
    return pl.pallas_call(
        functools.partial(_gdn_kernel, nc=nc, te=te),
        out_shape=jax.ShapeDtypeStruct((bsz, s, BRANCH_W), BF16),
        grid=(bsz // nbt, nc + ne),
        in_specs=qkv_specs(cf) + qkv_specs(cbk)
        + [pl.BlockSpec((nbt, te, BRANCH_W), lambda b, t: (b, ep(t), zb))]
        + [small_spec(cf), small_spec(cbk)]
        + [full((GDN_CONV, cw)), full((1, cw)), full((2, 1, LANES)), full((2, LANES, 1)),
           full((2, 1, LANES)), full((2, LANES, 1)), full((1, GDN_HEAD_DIM))],
        out_specs=pl.BlockSpec((nbt, te, BRANCH_W), lambda b, t: (b, ep(t), 0)),
        scratch_shapes=[
            pltpu.VMEM((nbt * 2, L + 2 * HALO, cw), F32),
            pltpu.VMEM((nbt * 2, GDN_HEAD_DIM, BRANCH_W), F32),
            pltpu.VMEM((nbt, s, BRANCH_W), F32),
        ],
        compiler_params=_cparams(("arbitrary", "arbitrary")),
        name="bidir_gated_deltanet",
    )(proj, proj, proj, proj, proj, proj, proj, small_t, small_t, conv_w, conv_b.reshape(1, cw),
      _pad_dir_rows(dt_bias, hv), _pad_dir_cols(dt_bias, hv), _pad_dir_rows(a_log, hv), _pad_dir_cols(a_log, hv),
      norm_w.reshape(1, GDN_HEAD_DIM))


def _local_scan(a, u, ascending):
    rows, width = a.shape
    a = a.reshape(rows // SUBLANES, SUBLANES, width)
    u = u.reshape(rows // SUBLANES, SUBLANES, width)
    sub = lax.broadcasted_iota(jnp.int32, a.shape, 1)
    d = 1
    while d < SUBLANES:
        if ascending:
            keep = sub >= d
            shift = d
        else:
            keep = sub < SUBLANES - d
            shift = SUBLANES - d
        a_sh = jnp.where(keep, pltpu.roll(a, shift, 1), 1.0)
        u_sh = jnp.where(keep, pltpu.roll(u, shift, 1), 0.0)
        u = u + a * u_sh
        a = a * a_sh
        d *= 2
    return a.reshape(rows, width), u.reshape(rows, width)


def _lru_kernel(xl_ref, prev_ref, next_ref, gl_ref, cw_ref, cb_ref, wa_ref, ba_ref, wx_ref, bx_ref, lam_ref,
                o_ref, ext_scr, a_scr, u_scr, hf_scr, carry_scr, *, nb, unroll):
    rows, width = a_scr.shape
    bw = LRU_WIDTH // LRU_BLOCKS
    step = pl.program_id(2)
    fwd = step < nb
    blk = _chunk_of_step(step, nb)

    @pl.when((step == 0) | (step == nb))
    def _():
        carry_scr[...] = jnp.zeros_like(carry_scr)

    _fill_conv_window(ext_scr, xl_ref[0], prev_ref[0], next_ref[0], blk, nb, rows)
    xc = _centred_conv4(ext_scr, cw_ref, cb_ref, rows)
    r_parts, i_parts = [], []
    for n in range(width // bw):
        xb = xc[:, n * bw:(n + 1) * bw].astype(BF16)
        r_parts.append(_dot(xb, wa_ref[0, n]))
        i_parts.append(_dot(xb, wx_ref[0, n]))
    tr = jnp.tanh(jnp.concatenate(r_parts, axis=1) + ba_ref[0])
    ti = jnp.tanh(jnp.concatenate(i_parts, axis=1) + bx_ref[0])
    log_a = ((-0.5 * LRU_C) * _softplus(-lam_ref[0])) * (tr + 1.0)
    a = jnp.exp(log_a)
    quarter = (-0.25 * jnp.tanh(log_a)) * (a * a + 1.0)
    u = (quarter * lax.rsqrt(jnp.maximum(quarter, TINY))) * ((ti + 1.0) * xc)

    n_groups = rows // SUBLANES
    base_out = pl.multiple_of(blk * rows, rows)

    def scan(ascending):
        a_loc, u_loc = _local_scan(a, u, ascending)
        a_scr[...] = a_loc
        u_scr[...] = u_loc
        last = SUBLANES - 1 if ascending else 0

        def body(it, carry):
            pos = it if ascending else n_groups - 1 - it
            r0 = pl.multiple_of(pos * SUBLANES, SUBLANES)
            h = u_scr[pl.ds(r0, SUBLANES), :] + a_scr[pl.ds(r0, SUBLANES), :] * carry
            u_scr[pl.ds(r0, SUBLANES), :] = h
            return jnp.broadcast_to(h[last:last + 1, :], h.shape)

        carry_scr[...] = lax.fori_loop(0, n_groups, body, carry_scr[...], unroll=unroll)

    @pl.when(fwd)
    def _():
        scan(True)
        hf_scr[pl.ds(base_out, rows), :] = u_scr[...]

    @pl.when(jnp.logical_not(fwd))
    def _():
        scan(False)
        o_ref[0] = ((hf_scr[pl.ds(base_out, rows), :] + u_scr[...]) * _silu(gl_ref[0])).astype(BF16)


def _lru(proj, conv_w, conv_b, wa, ba, wx, bx, lam, *, col_xl, col_gl, tt=512, width=512, unroll=8):
    bsz, s, _ = proj.shape
    tt = min(tt, s)
    nb = s // tt
    w_total = LRU_WIDTH
    bw = w_total // LRU_BLOCKS
    nbw = width // bw
    xcol, gcol = col_xl // width, col_gl // width
    assert col_xl % width == 0 and col_gl % width == 0
    hb = tt // HALO

    def blk(t):
        return _chunk_of_step(t, nb)

    def direction(t):
        return jnp.where(t < nb, 0, 1)

    return pl.pallas_call(
        functools.partial(_lru_kernel, nb=nb, unroll=unroll),
        out_shape=jax.ShapeDtypeStruct((bsz, s, w_total), BF16),
        grid=(bsz, w_total // width, 2 * nb),
        in_specs=[
            pl.BlockSpec((1, tt, width), lambda b, j, t: (b, blk(t), xcol + j)),
            pl.BlockSpec((1, HALO, width), lambda b, j, t: (b, jnp.maximum(blk(t) * hb - 1, 0), xcol + j)),
            pl.BlockSpec((1, HALO, width),
                         lambda b, j, t: (b, jnp.minimum((blk(t) + 1) * hb, s // HALO - 1), xcol + j)),
            pl.BlockSpec((1, tt, width), lambda b, j, t: (b, blk(t), gcol + j)),
            pl.BlockSpec((LRU_CONV, width), lambda b, j, t: (0, j)),
            pl.BlockSpec((1, width), lambda b, j, t: (0, j)),
            pl.BlockSpec((1, nbw, bw, bw), lambda b, j, t: (direction(t), j, 0, 0)),
            pl.BlockSpec((1, 1, width), lambda b, j, t: (direction(t), 0, j)),
            pl.BlockSpec((1, nbw, bw, bw), lambda b, j, t: (direction(t), j, 0, 0)),
            pl.BlockSpec((1, 1, width), lambda b, j, t: (direction(t), 0, j)),
            pl.BlockSpec((1, 1, width), lambda b, j, t: (direction(t), 0, j)),
        ],
        out_specs=pl.BlockSpec((1, tt, width),
                               lambda b, j, t: (b, jnp.where(t < nb, nb - 1, 2 * nb - 1 - t), j)),
        scratch_shapes=[
            pltpu.VMEM((tt + 2 * HALO, width), F32),
            pltpu.VMEM((tt, width), F32),
            pltpu.VMEM((tt, width), F32),
            pltpu.VMEM((s, width), F32),
            pltpu.VMEM((SUBLANES, width), F32),
        ],
        compiler_params=_cparams(("arbitrary", "arbitrary", "arbitrary")),
        name="bidir_rglru",
    )(proj, proj, proj, proj, conv_w, conv_b.reshape(1, w_total),
      (0.5 * wa).astype(BF16), (0.5 * ba).reshape(2, 1, w_total),
      (0.5 * wx).astype(BF16), (0.5 * bx).reshape(2, 1, w_total), lam.reshape(2, 1, w_total))


CD_COL_QKV, CD_COL_Z, CD_COL_XL, CD_COL_GL = 0, 2048, 3072, 4096


def _cd_weights(w_in):
    wt = w_in.T
    nqk = GDN_QK_HEADS * GDN_HEAD_DIM
    hv = GDN_V_HEADS
    o = 0
    q = wt[o:o + nqk]; o += nqk
    k = wt[o:o + nqk]; o += nqk
    v = wt[o:o + BRANCH_W]; o += BRANCH_W
    bf = wt[o:o + hv]; o += hv
    bb = wt[o:o + hv]; o += hv
    af = wt[o:o + hv]; o += hv
    ab = wt[o:o + hv]; o += hv
    z = wt[o:o + BRANCH_W]; o += BRANCH_W
    xl = wt[o:o + LRU_WIDTH]; o += LRU_WIDTH
    gl = wt[o:o + LRU_WIDTH]
    head = jnp.concatenate([q, k, v], axis=0).astype(BF16)
    tail = jnp.concatenate([z, xl, gl], axis=0).astype(BF16)
    main = jnp.zeros((head.shape[0] + tail.shape[0], wt.shape[1]), BF16)
    main = lax.dynamic_update_slice(main, head, (0, 0))
    main = lax.dynamic_update_slice(main, tail, (head.shape[0], 0))
    zpad = jnp.zeros((LANES - 4 * hv, wt.shape[1]), wt.dtype)
    small = jnp.concatenate([bf, af, bb, ab, zpad], axis=0).astype(BF16)
    return main, small


def _layer1(x, mod_l, norm_w, w_in, conv_w, conv_b, a_log_f, a_log_b, dt_bias_f, dt_bias_b, gdn_norm,
            lru_conv_w, lru_conv_b, wa_f, ba_f, wx_f, bx_f, lam_f, wa_b, ba_b, wx_b, bx_b, lam_b, w_out,
            final_w=None):
    w_main, w_small = _cd_weights(w_in)
    proj, small_t = _inproj(x, mod_l, norm_w, w_main, w_small)
    gdn = _gdn(proj, small_t, conv_w, conv_b, jnp.stack([dt_bias_f, dt_bias_b]),
               jnp.stack([a_log_f, a_log_b]), gdn_norm, col_qkv=CD_COL_QKV, col_z=CD_COL_Z)
    lru = _lru(proj, lru_conv_w, lru_conv_b, jnp.stack([wa_f, wa_b]), jnp.stack([ba_f, ba_b]),
               jnp.stack([wx_f, wx_b]), jnp.stack([bx_f, bx_b]), jnp.stack([lam_f, lam_b]),
               col_xl=CD_COL_XL, col_gl=CD_COL_GL)
    out = _outproj(gdn, lru, w_out.astype(BF16), x, mod_l, final_w)
    return out, (proj, small_t, gdn, lru)


def kernel(x, c, w_mod, b_mod, norm_w, ab_w_in, ab_q_norm, ab_k_norm, ab_conv_w, ab_conv_b, ab_dt_bias_f, ab_dt_bias_b, ab_a_log_f, ab_a_log_b, ab_d_skip, ab_ssd_norm, ab_w_out, cd_w_in, cd_conv_w, cd_conv_b, cd_a_log_f, cd_a_log_b, cd_dt_bias_f, cd_dt_bias_b, cd_gdn_norm, cd_lru_conv_w, cd_lru_conv_b, cd_lru_wa_f, cd_lru_ba_f, cd_lru_wx_f, cd_lru_bx_f, cd_lru_lam_f, cd_lru_wa_b, cd_lru_ba_b, cd_lru_wx_b, cd_lru_bx_b, cd_lru_lam_b, cd_w_out, final_norm_w):
    mods = _modulation(c, w_mod, b_mod)
    x1, _ = _layer0(x, mods[0], norm_w[0], ab_w_in[0], ab_q_norm[0], ab_k_norm[0], ab_conv_w[0], ab_conv_b[0],
                    ab_dt_bias_f[0], ab_dt_bias_b[0], ab_a_log_f[0], ab_a_log_b[0], ab_d_skip[0],
                    ab_ssd_norm[0], ab_w_out[0])
    out, _ = _layer1(x1, mods[1], norm_w[1], cd_w_in[0], cd_conv_w[0], cd_conv_b[0], cd_a_log_f[0], cd_a_log_b[0],
                     cd_dt_bias_f[0], cd_dt_bias_b[0], cd_gdn_norm[0], cd_lru_conv_w[0], cd_lru_conv_b[0],
                     cd_lru_wa_f[0], cd_lru_ba_f[0], cd_lru_wx_f[0], cd_lru_bx_f[0], cd_lru_lam_f[0],
                     cd_lru_wa_b[0], cd_lru_ba_b[0], cd_lru_wx_b[0], cd_lru_bx_b[0], cd_lru_lam_b[0],
                     cd_w_out[0], final_norm_w)
    return out
```

```python
import functools
import math

import jax
import jax.numpy as jnp
import numpy as np
from jax import lax
from jax.experimental import pallas as pl
from jax.experimental.pallas import tpu as pltpu

F32 = jnp.float32
BF16 = jnp.bfloat16

D_MODEL = 2048
GRID_W = 64
EPS = 1e-6
BRANCH_W = D_MODEL // 2
ATT_HEAD_DIM = 128
ATT_HEADS = BRANCH_W // ATT_HEAD_DIM
ATT_KV_HEADS = ATT_HEADS // 4
ATT_GROUP = ATT_HEADS // ATT_KV_HEADS
ROPE_THETA = 10000.0
SSD_HEAD_DIM = 64
SSD_HEADS = BRANCH_W // SSD_HEAD_DIM
SSD_GROUPS = 2
SSD_STATE = 128
SSD_CONV = 4
SSD_CHUNK = 128
GDN_HEAD_DIM = 128
GDN_V_HEADS = BRANCH_W // GDN_HEAD_DIM
GDN_QK_HEADS = GDN_V_HEADS // 2
GDN_CONV = 4
GDN_CHUNK = 128
LRU_WIDTH = BRANCH_W
LRU_BLOCKS = 8
LRU_CONV = 4
LRU_C = 8.0

LANES = 128
SUBLANES = 8
VMEM_LIMIT_BYTES = 56 * 1024 * 1024

HALO = SUBLANES
NEG_BIG = -1e30
TINY = 1e-37


def _cparams(sem):
    return pltpu.CompilerParams(dimension_semantics=sem, vmem_limit_bytes=VMEM_LIMIT_BYTES)


def _sigmoid(x):
    return 0.5 * jnp.tanh(0.5 * x) + 0.5


def _silu(x):
    h = 0.5 * x
    return h * (jnp.tanh(h) + 1.0)


def _softplus(x):
    return jnp.maximum(x, 0.0) + jnp.log(1.0 + jnp.exp(-jnp.abs(x)))


def _split_bf16(a):
    hi = a.astype(BF16)
    lo = (a - hi.astype(F32)).astype(BF16)
    return hi, lo


def _dot(a, b):
    return jnp.dot(a, b, preferred_element_type=F32)


def _dot_nt(a, b):
    return lax.dot_general(a, b, (((1,), (1,)), ((), ())), preferred_element_type=F32)


def _dot_exact_rhs(a, b_bf16):
    hi, lo = _split_bf16(a)
    return _dot(hi, b_bf16) + _dot(lo, b_bf16)


def _dot_exact_lhs(a_bf16, b):
    hi, lo = _split_bf16(b)
    return _dot(a_bf16, hi) + _dot(a_bf16, lo)


def _mod_kernel(c_ref, w_ref, b_ref, o_ref):
    cond = _silu(c_ref[...])
    c_hi, c_lo = _split_bf16(cond)
    w = w_ref[0]
    w_hi, w_lo = _split_bf16(w)
    o_ref[0] = _dot(c_hi, w_hi) + _dot(c_lo, w_hi) + _dot(c_hi, w_lo) + b_ref[0]


def _modulation(c, w_mod, b_mod):
    depth, d, n = w_mod.shape
    bsz = c.shape[0]
    rows = -(-bsz // SUBLANES) * SUBLANES
    c_pad = jnp.zeros((rows, d), F32).at[:bsz].set(c)
    tn = 1024
    out = pl.pallas_call(
        _mod_kernel,
        out_shape=jax.ShapeDtypeStruct((depth, rows, n), F32),
        grid=(depth, n // tn),
        in_specs=[
            pl.BlockSpec((rows, d), lambda l, j: (0, 0)),
            pl.BlockSpec((1, d, tn), lambda l, j: (l, 0, j)),
            pl.BlockSpec((1, 1, tn), lambda l, j: (l, 0, j)),
        ],
        out_specs=pl.BlockSpec((1, rows, tn), lambda l, j: (l, 0, j)),
        compiler_params=_cparams(("arbitrary", "arbitrary")),
        name="adaln_mod",
    )(c_pad, w_mod, b_mod.reshape(depth, 1, n))
    return out[:, :bsz].reshape(depth, bsz, 3, d)


def _inproj_kernel(x_ref, mod_ref, nw_ref, w_ref, wst_ref, o_ref, ost_ref, h_even, h_odd, *, n_groups):
    g = pl.program_id(0)
    j = pl.program_id(1)
    rows = x_ref.shape[1]

    def prep(h_dst):
        x = x_ref[0]
        ms = jnp.mean(x * x, axis=-1, keepdims=True)
        y = x * lax.rsqrt(ms + EPS) * nw_ref[...]
        h = (y * (1.0 + mod_ref[0, 1:2, :]) + mod_ref[0, 0:1, :]).astype(BF16)
        h_dst[pl.ds(pl.multiple_of(j * rows, rows), rows), :] = h
        ost_ref[0] = _dot_nt(wst_ref[...], h)

    @pl.when(g == 0)
    def _():
        prep(h_even)

    @pl.when((g > 0) & (g % 2 == 0))
    def _():
        o_ref[0] = _dot_nt(h_odd[...], w_ref[...])
        prep(h_even)

    @pl.when(g % 2 == 1)
    def _():
        o_ref[0] = _dot_nt(h_even[...], w_ref[...])
        prep(h_odd)


def _inproj(x, mod_l, norm_w, w_main, w_small, *, tm=1024, nj=4):
    bsz, s, d = x.shape
    n = w_main.shape[0]
    ns = w_small.shape[0]
    tm = min(tm, s)
    ni = s // tm
    n_groups = bsz * ni
    tn = n // nj
    rows = tm // nj
    assert n % nj == 0 and tn % LANES == 0 and rows % LANES == 0

    def prep(g, j):
        gc = jnp.minimum(g, n_groups - 1)
        return gc // ni, (gc % ni) * nj + jnp.where(g < n_groups, j, nj - 1)

    def mm(g, j):
        gm = jnp.maximum(g - 1, 0)
        return gm // ni, gm % ni, jnp.where(g > 0, j, 0)

    return pl.pallas_call(
        functools.partial(_inproj_kernel, n_groups=n_groups),
        out_shape=[jax.ShapeDtypeStruct((bsz, s, n), F32), jax.ShapeDtypeStruct((bsz, ns, s), F32)],
        grid=(n_groups + 1, nj),
        in_specs=[
            pl.BlockSpec((1, rows, d), lambda g, j: (*prep(g, j), 0)),
            pl.BlockSpec((1, 3, d), lambda g, j: (prep(g, j)[0], 0, 0)),
            pl.BlockSpec((1, d), lambda g, j: (0, 0)),
            pl.BlockSpec((tn, d), lambda g, j: (mm(g, j)[2], 0)),
            pl.BlockSpec((ns, d), lambda g, j: (0, 0)),
        ],
        out_specs=[
            pl.BlockSpec((1, tm, tn), lambda g, j: mm(g, j)),
            pl.BlockSpec((1, ns, rows), lambda g, j: (prep(g, j)[0], 0, prep(g, j)[1])),
        ],
        scratch_shapes=[pltpu.VMEM((tm, d), BF16), pltpu.VMEM((tm, d), BF16)],
        compiler_params=_cparams(("arbitrary", "arbitrary")),
        name="norm_mod_inproj",
    )(x, mod_l, norm_w.reshape(1, d), w_main, w_small)


def _outproj_kernel(ya_ref, yb_ref, w_ref, x_ref, mod_ref, *rest, final_norm, half):
    if final_norm:
        fnw_ref, o_ref = rest
    else:
        (o_ref,) = rest
    acc = _dot(ya_ref[0], w_ref[0:half, :]) + _dot(yb_ref[0], w_ref[half:, :])
    gate = mod_ref[0, 2:3, :]
    xn = x_ref[0] + gate * acc
    if final_norm:
        ms = jnp.mean(xn * xn, axis=-1, keepdims=True)
        xn = xn * lax.rsqrt(ms + EPS) * fnw_ref[...]
    o_ref[0] = xn


def _outproj(ya, yb, w_out, x, mod_l, final_w=None, *, tm=512):
    bsz, s, d = x.shape
    half = ya.shape[-1]
    tm = min(tm, s)
    final_norm = final_w is not None
    in_specs = [
        pl.BlockSpec((1, tm, half), lambda b, i: (b, i, 0)),
        pl.BlockSpec((1, tm, half), lambda b, i: (b, i, 0)),
        pl.BlockSpec((2 * half, d), lambda b, i: (0, 0)),
        pl.BlockSpec((1, tm, d), lambda b, i: (b, i, 0)),
        pl.BlockSpec((1, 3, d), lambda b, i: (b, 0, 0)),
    ]
    args = [ya, yb, w_out, x, mod_l]
    if final_norm:
        in_specs.append(pl.BlockSpec((1, d), lambda b, i: (0, 0)))
        args.append(final_w.reshape(1, d))
    return pl.pallas_call(
        functools.partial(_outproj_kernel, final_norm=final_norm, half=half),
        out_shape=jax.ShapeDtypeStruct((bsz, s, d), F32),
        grid=(bsz, s // tm),
        in_specs=in_specs,
        out_specs=pl.BlockSpec((1, tm, d), lambda b, i: (b, i, 0)),
        compiler_params=_cparams(("arbitrary", "arbitrary")),
        name="outproj_residual",
    )(*args)


def _rms_rope(x, nw, cos, sin_signed):
    ms = jnp.mean(x * x, axis=-1, keepdims=True)
    y = x * lax.rsqrt(ms + EPS) * nw
    lane = lax.broadcasted_iota(jnp.int32, y.shape, 1)
    partner = jnp.where(lane % 2 == 0, pltpu.roll(y, ATT_HEAD_DIM - 1, 1), pltpu.roll(y, 1, 1))
    return y * cos + partner * sin_signed


def _attn_kernel(q_ref, k_ref, v_ref, ga_ref, cq_ref, sq_ref, ck_ref, sk_ref, qn_ref, kn_ref,
                 o_ref, k_scr, v_scr, *, tq, sub):
    qi = pl.program_id(2)

    dh = ATT_HEAD_DIM

    @pl.when(qi == 0)
    def _():
        k_scr[...] = _rms_rope(k_ref[0], kn_ref[...], ck_ref[...], sk_ref[...]).astype(BF16)
        v_scr[:, 0:dh] = v_ref[0].astype(BF16)
        v_scr[:, dh:2 * dh] = jnp.ones((v_scr.shape[0], dh), BF16)

    scale = dh ** -0.5
    cos = cq_ref[...]
    sin = sq_ref[...]

    probs = [(g, r) for g in range(ATT_GROUP) for r in range(0, tq, sub)]

    def scores(g, r):
        qg = q_ref[0, r:r + sub, g * dh:(g + 1) * dh]
        qg = (_rms_rope(qg, qn_ref[...], cos[r:r + sub], sin[r:r + sub]) * scale).astype(BF16)
        return _dot_nt(qg, k_scr[...])

    s_next = scores(*probs[0])
    for i, (g, r) in enumerate(probs):
        s = s_next
        if i + 1 < len(probs):
            s_next = scores(*probs[i + 1])
        s = s.astype(BF16)
        p = jnp.exp(s - jnp.max(s, axis=-1, keepdims=True))
        o_ext = _dot(p, v_scr[...])
        og = o_ext[:, 0:dh] / o_ext[:, dh:2 * dh] * _silu(ga_ref[0, r:r + sub, g * dh:(g + 1) * dh])
        o_ref[0, r:r + sub, g * dh:(g + 1) * dh] = og.astype(BF16)


def _attention(proj, cos_t, sin_t, q_norm, k_norm, *, col_q, col_k, col_v, col_ga, tq=512, sub=128):
    bsz, s, _ = proj.shape
    tq = min(tq, s)
    gw = ATT_GROUP * ATT_HEAD_DIM
    dh = ATT_HEAD_DIM
    qb, kb, vb, gb = col_q // gw, col_k // dh, col_v // dh, col_ga // gw
    return pl.pallas_call(
        functools.partial(_attn_kernel, tq=tq, sub=min(sub, tq)),
        out_shape=jax.ShapeDtypeStruct((bsz, s, BRANCH_W), BF16),
        grid=(bsz, ATT_KV_HEADS, s // tq),
        in_specs=[
            pl.BlockSpec((1, tq, gw), lambda b, h, i: (b, i, qb + h)),
            pl.BlockSpec((1, s, dh), lambda b, h, i: (b, 0, kb + h)),
            pl.BlockSpec((1, s, dh), lambda b, h, i: (b, 0, vb + h)),
            pl.BlockSpec((1, tq, gw), lambda b, h, i: (b, i, gb + h)),
            pl.BlockSpec((tq, dh), lambda b, h, i: (i, 0)),
            pl.BlockSpec((tq, dh), lambda b, h, i: (i, 0)),
            pl.BlockSpec((s, dh), lambda b, h, i: (0, 0)),
            pl.BlockSpec((s, dh), lambda b, h, i: (0, 0)),
            pl.BlockSpec((1, dh), lambda b, h, i: (0, 0)),
            pl.BlockSpec((1, dh), lambda b, h, i: (0, 0)),
        ],
        out_specs=pl.BlockSpec((1, tq, gw), lambda b, h, i: (b, i, h)),
        scratch_shapes=[pltpu.VMEM((s, dh), BF16), pltpu.VMEM((s, 2 * dh), BF16)],
        compiler_params=_cparams(("arbitrary", "arbitrary", "arbitrary")),
        name="gqa_attention",
    )(proj, proj, proj, proj, cos_t, sin_t, cos_t, sin_t, q_norm, k_norm)


def _rope_tables(s):
    t = np.arange(s)
    row = (t // GRID_W).astype(np.float64)
    col = (t % GRID_W).astype(np.float64)
    n_pairs = ATT_HEAD_DIM // 4
    freqs = ROPE_THETA ** (-np.arange(n_pairs, dtype=np.float64) / n_pairs)
    ang = np.concatenate([row[:, None] * freqs, col[:, None] * freqs], axis=-1)
    cos, sin = np.cos(ang), np.sin(ang)
    cos_t = np.repeat(cos, 2, axis=-1)
    sin_t = np.stack([-sin, sin], axis=-1).reshape(s, ATT_HEAD_DIM)
    return jnp.asarray(cos_t, F32), jnp.asarray(sin_t, F32)


def _chunk_of_step(step, nc):
    return jnp.where(step < nc, step, 2 * nc - 1 - step)


def _fill_conv_window(ext_scr, cur, prev, nxt, c, nc, rows):
    ext_scr[0:HALO, :] = jnp.where(c > 0, prev, 0.0)
    ext_scr[HALO:HALO + rows, :] = cur
    ext_scr[HALO + rows:HALO + rows + HALO, :] = jnp.where(c < nc - 1, nxt, 0.0)


CONV_SUB = 128


def _centred_conv4(ext_scr, cw_ref, cb_ref, rows):
    sub = min(rows, CONV_SUB)
    n_in = sub + 2 * HALO
    taps = (0, 1, 3)
    r = lax.broadcasted_iota(jnp.int32, (len(taps) * sub, n_in), 0)
    c = lax.broadcasted_iota(jnp.int32, (len(taps) * sub, n_in), 1)
    src = jnp.zeros_like(r)
    for i, k in enumerate(taps):
        src = jnp.where((r >= i * sub) & (r < (i + 1) * sub), r - i * sub + HALO + (k - 2), src)
    shift_mat = jnp.where(c == src, 1.0, 0.0).astype(BF16)
    outs = []
    for s0 in range(0, rows, sub):
        win = ext_scr[s0:s0 + n_in, :]
        shifted = _dot(shift_mat, win.astype(BF16))
        acc = cb_ref[...] + win[HALO:HALO + sub, :] * cw_ref[2:3, :]
        for i, k in enumerate(taps):
            acc = acc + shifted[i * sub:(i + 1) * sub, :] * cw_ref[k:k + 1, :]
        outs.append(acc)
    return outs[0] if len(outs) == 1 else jnp.concatenate(outs, axis=0)


def _scan_masks(fwd, n):
    row = lax.broadcasted_iota(jnp.int32, (n, n), 0)
    col = lax.broadcasted_iota(jnp.int32, (n, n), 1)
    sgn = jnp.where(fwd, 1, -1)
    d = (row - col) * sgn
    return d >= 0, d <= 0, d > 0


def _head_expander(heads, width):
    r = lax.broadcasted_iota(jnp.int32, (heads, heads * width), 0)
    c = lax.broadcasted_iota(jnp.int32, (heads, heads * width), 1)
    return jnp.where((c >= r * width) & (c < (r + 1) * width), 1.0, 0.0).astype(BF16)


def _ssd_kernel(xf_ref, pf_ref, nf_ref, xb_ref, pb_ref, nb_ref, z_ref, dtf_ref, dtb_ref, cw_ref, cb_ref,
                bias_r_ref, bias_c_ref, alog_r_ref, alog_c_ref, dskip_ref, nw_ref,
                o_ref, ext_scr, state_scr, acc_scr, *, nc, te):
    L, H, P, N = SSD_CHUNK, SSD_HEADS, SSD_HEAD_DIM, SSD_STATE
    HG = H // SSD_GROUPS
    GW = HG * P
    t = pl.program_id(1)

    @pl.when(t == 0)
    def _():
        state_scr[...] = jnp.zeros_like(state_scr)

    @pl.when(t < nc)
    def _():
        chunks = (t, nc - 1 - t)
        blocks = ((xf_ref, pf_ref, nf_ref, dtf_ref), (xb_ref, pb_ref, nb_ref, dtb_ref))
        expand = _head_expander(H, P)
        lane = lax.broadcasted_iota(jnp.int32, (L, 2 * P), 1)
        per_dir = []
        for d in range(2):
            x_ref, p_ref, n_ref, dtt_ref = blocks[d]
            ext = ext_scr.at[d]
            _fill_conv_window(ext, x_ref[0], p_ref[0], n_ref[0], chunks[d], nc, L)
            xbc = _silu(_centred_conv4(ext, cw_ref, cb_ref, L))
            xs = xbc[:, :BRANCH_W]
            bs = xbc[:, BRANCH_W:BRANCH_W + SSD_GROUPS * N]
            cs = xbc[:, BRANCH_W + SSD_GROUPS * N:]
            raw_t = dtt_ref[0]
            raw = raw_t.T
            dt = _softplus(raw[:, d * H:(d + 1) * H] + bias_r_ref[d][:, 0:H])
            a = dt * (-jnp.exp(alog_r_ref[d][:, 0:H]))
            dt_t = _softplus(raw_t[d * H:(d + 1) * H, :] + bias_c_ref[d][0:H, :])
            a_t = dt_t * (-jnp.exp(alog_c_ref[d][0:H, :]))
            mask, mask_t, _ = _scan_masks(d == 0, L)
            cum = _dot_exact_lhs(jnp.where(mask, 1.0, 0.0).astype(BF16), a)
            cum_t = _dot_exact_rhs(a_t, jnp.where(mask_t, 1.0, 0.0).astype(BF16))
            total = jnp.sum(a, axis=0, keepdims=True)
            dt_e = _dot_exact_rhs(dt, expand)
            p_e = _dot_exact_rhs(jnp.exp(cum), expand)
            q_e = _dot_exact_rhs(jnp.exp(total - cum), expand)
            tot_e = _dot_exact_rhs(jnp.broadcast_to(jnp.exp(total), (SUBLANES, H)), expand)[0:1, :]
            xd = xs * dt_e
            per_dir.append(dict(xs=xs, bs=bs, cs=cs, mask=mask, cum=cum, cum_t=cum_t, p_e=p_e, tot_e=tot_e,
                                xdq=(xd * q_e).astype(BF16), xd_b=xd.astype(BF16)))

        combos = [(d, g) for d in range(2) for g in range(SSD_GROUPS)]
        cg = {k: per_dir[k[0]]["cs"][:, k[1] * N:(k[1] + 1) * N].astype(BF16) for k in combos}
        bg = {k: per_dir[k[0]]["bs"][:, k[1] * N:(k[1] + 1) * N] for k in combos}
        gmat = {k: _dot_nt(cg[k], bg[k].astype(BF16)) for k in combos}
        h_prev = {k: state_scr[k[0], :, k[1] * GW:(k[1] + 1) * GW] for k in combos}
        y_off = {k: _dot(cg[k], h_prev[k].astype(BF16)) * per_dir[k[0]]["p_e"][:, k[1] * GW:(k[1] + 1) * GW]
                 for k in combos}
        pairs = {k: [] for k in combos}
        for hp in range(HG // 2):
            for k in combos:
                d, g = k
                pd = per_dir[d]
                h0 = g * HG + 2 * hp
                xpair = pd["xd_b"][:, h0 * P:(h0 + 2) * P]
                ys = []
                for h in (h0, h0 + 1):
                    dec = jnp.exp(jnp.where(pd["mask"], pd["cum"][:, h:h + 1] - pd["cum_t"][h:h + 1, :], NEG_BIG))
                    ys.append(_dot((gmat[k] * dec).astype(BF16), xpair))
                pairs[k].append(jnp.where(lane < P, ys[0], ys[1]))
        for k in combos:
            d, g = k
            state_scr[d, :, g * GW:(g + 1) * GW] = (
                h_prev[k] * per_dir[d]["tot_e"][:, g * GW:(g + 1) * GW]
                + _dot(bg[k].T.astype(BF16), per_dir[d]["xdq"][:, g * GW:(g + 1) * GW]))
        y_dirs = [jnp.concatenate([jnp.concatenate(pairs[(d, g)], axis=1) + y_off[(d, g)]
                                   for g in range(SSD_GROUPS)], axis=1) for d in range(2)]
        y_dirs[0] = y_dirs[0] + dskip_ref[...] * per_dir[0]["xs"]
        rows = [pl.multiple_of(c * L, L) for c in chunks]

        @pl.when(t < nc // 2)
        def _():
            for d in range(2):
                acc_scr[pl.ds(rows[d], L), :] = y_dirs[d]

        @pl.when(t >= nc // 2)
        def _():
            for d in range(2):
                acc_scr[pl.ds(rows[d], L), :] = acc_scr[pl.ds(rows[d], L), :] + y_dirs[d]

    @pl.when(t >= nc)
    def _():
        r0 = pl.multiple_of((t - nc) * te, te)
        y = acc_scr[pl.ds(r0, te), :] * _silu(z_ref[0])
        ms = jnp.mean(y * y, axis=-1, keepdims=True)
        o_ref[0] = (y * lax.rsqrt(ms + EPS) * nw_ref[...]).astype(BF16)


def _ssd(proj, small_t, conv_w, conv_b, dt_bias, a_log, d_skip, norm_w, *, col_xbc, col_z, te=256):
    bsz, s, _ = proj.shape
    L = SSD_CHUNK
    nc = s // L
    te = min(te, s)
    ne = s // te
    cw = BRANCH_W + 2 * SSD_GROUPS * SSD_STATE
    hb = L // HALO
    xb, zb = col_xbc // cw, col_z // BRANCH_W
    assert col_xbc % cw == 0 and col_z % BRANCH_W == 0 and nc % 2 == 0

    def cf(t):
        return jnp.minimum(t, nc - 1)

    def cbk(t):
        return jnp.maximum(nc - 1 - t, 0)

    def ep(t):
        return jnp.maximum(t - nc, 0)

    def xbc_specs(chunk):
        return [
            pl.BlockSpec((1, L, cw), lambda b, t: (b, chunk(t), xb)),
            pl.BlockSpec((1, HALO, cw), lambda b, t: (b, jnp.maximum(chunk(t) * hb - 1, 0), xb)),
            pl.BlockSpec((1, HALO, cw), lambda b, t: (b, jnp.minimum((chunk(t) + 1) * hb, s // HALO - 1), xb)),
        ]

    def small_spec(chunk):
        return pl.BlockSpec((1, LANES, L), lambda b, t: (b, 0, chunk(t)))

    full = lambda shape: pl.BlockSpec(shape, lambda b, t: (0,) * len(shape))
    return pl.pallas_call(
        functools.partial(_ssd_kernel, nc=nc, te=te),
        out_shape=jax.ShapeDtypeStruct((bsz, s, BRANCH_W), BF16),
        grid=(bsz, nc + ne),
        in_specs=xbc_specs(cf) + xbc_specs(cbk)
        + [pl.BlockSpec((1, te, BRANCH_W), lambda b, t: (b, ep(t), zb))]
        + [small_spec(cf), small_spec(cbk)]
        + [full((SSD_CONV, cw)), full((1, cw)), full((2, 1, LANES)), full((2, LANES, 1)),
           full((2, 1, LANES)), full((2, LANES, 1)), full((1, BRANCH_W)), full((1, BRANCH_W))],
        out_specs=pl.BlockSpec((1, te, BRANCH_W), lambda b, t: (b, ep(t), 0)),
        scratch_shapes=[
            pltpu.VMEM((2, L + 2 * HALO, cw), F32),
            pltpu.VMEM((2, SSD_STATE, BRANCH_W), F32),
            pltpu.VMEM((s, BRANCH_W), F32),
        ],
        compiler_params=_cparams(("arbitrary", "arbitrary")),
        name="bidir_ssd",
    )(proj, proj, proj, proj, proj, proj, proj, small_t, small_t, conv_w, conv_b.reshape(1, cw),
      _pad_dir_rows(dt_bias, SSD_HEADS), _pad_dir_cols(dt_bias, SSD_HEADS),
      _pad_dir_rows(a_log, SSD_HEADS), _pad_dir_cols(a_log, SSD_HEADS),
      jnp.repeat(d_skip, SSD_HEAD_DIM).reshape(1, BRANCH_W), norm_w.reshape(1, BRANCH_W))


AB_COL_XBC, AB_COL_K, AB_COL_V, AB_COL_Z, AB_COL_Q, AB_COL_GA, AB_N = 0, 1536, 1792, 2048, 3072, 4096, 5120


def _ab_weights(w_in):
    wt = w_in.T
    hq, hk = ATT_HEADS * ATT_HEAD_DIM, ATT_KV_HEADS * ATT_HEAD_DIM
    gn = SSD_GROUPS * SSD_STATE
    o = 0
    q = wt[o:o + hq]; o += hq
    k = wt[o:o + hk]; o += hk
    v = wt[o:o + hk]; o += hk
    ga = wt[o:o + BRANCH_W]; o += BRANCH_W
    xs = wt[o:o + BRANCH_W]; o += BRANCH_W
    bs = wt[o:o + gn]; o += gn
    cs = wt[o:o + gn]; o += gn
    dtf = wt[o:o + SSD_HEADS]; o += SSD_HEADS
    dtb = wt[o:o + SSD_HEADS]; o += SSD_HEADS
    z = wt[o:o + BRANCH_W]
    main = jnp.concatenate([xs, bs, cs, k, v, z, q, ga], axis=0).astype(BF16)
    zpad = jnp.zeros((LANES - 2 * SSD_HEADS, wt.shape[1]), wt.dtype)
    small = jnp.concatenate([dtf, dtb, zpad], axis=0).astype(BF16)
    return main, small


def _layer0(x, mod_l, norm_w, w_in, q_norm, k_norm, conv_w, conv_b, dt_bias_f, dt_bias_b,
            a_log_f, a_log_b, d_skip, ssd_norm, w_out, final_w=None):
    s = x.shape[1]
    w_main, w_small = _ab_weights(w_in)
    proj, small_t = _inproj(x, mod_l, norm_w, w_main, w_small)
    cos_t, sin_t = _rope_tables(s)
    att = _attention(proj, cos_t, sin_t, q_norm.reshape(1, ATT_HEAD_DIM), k_norm.reshape(1, ATT_HEAD_DIM),
                     col_q=AB_COL_Q, col_k=AB_COL_K, col_v=AB_COL_V, col_ga=AB_COL_GA)
    ssd = _ssd(proj, small_t, conv_w, conv_b, jnp.stack([dt_bias_f, dt_bias_b]),
               jnp.stack([a_log_f, a_log_b]), d_skip, ssd_norm, col_xbc=AB_COL_XBC, col_z=AB_COL_Z)
    return _outproj(att, ssd, w_out.astype(BF16), x, mod_l, final_w), (proj, small_t, att, ssd)


def _unit_tri_inverse(nmats, n):
    row = lax.broadcasted_iota(jnp.int32, (n, n), 0)
    col = lax.broadcasted_iota(jnp.int32, (n, n), 1)

    def same_block(size):
        return (row // size) == (col // size)

    def mm(a, b):
        return _dot(a, b).astype(BF16)

    def as_mask(cond):
        return jnp.where(cond, 1.0, 0.0).astype(BF16)

    nmats = [m.astype(BF16) for m in nmats]
    eye = as_mask(row == col)
    base = SUBLANES
    blk = same_block(base)
    blk_m = as_mask(blk)
    nd = [m * blk_m for m in nmats]
    p1 = [mm(x, x) for x in nd]
    p2 = [mm(x, x) for x in p1]
    t = [eye - x for x in nd]
    t = [x + mm(x, p) for x, p in zip(t, p1)]
    t = [x + mm(x, p) for x, p in zip(t, p2)]
    size = base
    while size < n:
        nxt = same_block(2 * size)
        off_m = as_mask(nxt & jnp.logical_not(blk))
        et = [mm(m * off_m, x) for m, x in zip(nmats, t)]
        t = [x - mm(x, y) for x, y in zip(t, et)]
        blk = nxt
        size *= 2
    return t


def _l2norm(x):
    return x * lax.rsqrt(jnp.sum(x * x, axis=-1, keepdims=True) + EPS)


def _gdn_kernel(qf_ref, pf_ref, nf_ref, qb_ref, pb_ref, nb_ref, z_ref, smtf_ref, smtb_ref,
                cw_ref, cb_ref, bias_r_ref, bias_c_ref, alog_r_ref, alog_c_ref, nw_ref,
                o_ref, ext_scr, state_scr, acc_scr, *, nc, te):
    L, HV, HQ, DK = GDN_CHUNK, GDN_V_HEADS, GDN_QK_HEADS, GDN_HEAD_DIM
    rep = HV // HQ
    nbt = acc_scr.shape[0]
    t = pl.program_id(1)

    @pl.when(t == 0)
    def _():
        state_scr[...] = jnp.zeros_like(state_scr)

    @pl.when(t < nc)
    def _():
        chunks = (t, nc - 1 - t)
        blocks = ((qf_ref, pf_ref, nf_ref, smtf_ref), (qb_ref, pb_ref, nb_ref, smtb_ref))
        kk, qk, qh, kh, vh, colv, dec, bcol, ecol, tot, nmat = ([] for _ in range(11))
        for bi, d in [(bi, d) for bi in range(nbt) for d in range(2)]:
            c = chunks[d]
            q_ref, p_ref, n_ref, smt_ref = blocks[d]
            ext = ext_scr.at[bi * 2 + d]
            _fill_conv_window(ext, q_ref[bi], p_ref[bi], n_ref[bi], c, nc, L)
            act = _silu(_centred_conv4(ext, cw_ref, cb_ref, L))
            q_n = [_l2norm(act[:, h * DK:(h + 1) * DK]) * (DK ** -0.5) for h in range(HQ)]
            k_n = [_l2norm(act[:, (HQ + h) * DK:(HQ + h + 1) * DK]) for h in range(HQ)]
            k_b = [x.astype(BF16) for x in k_n]
            kk_d = [_dot_nt(k_b[h], k_b[h]) for h in range(HQ)]
            qk_d = [_dot_nt(q_n[h].astype(BF16), k_b[h]) for h in range(HQ)]

            raw_t = smt_ref[bi]
            sm = raw_t.T[:, 2 * HV * d:2 * HV * (d + 1)]
            a_raw_t = raw_t[2 * HV * d + HV:2 * HV * (d + 1), :]
            beta = _sigmoid(sm[:, 0:HV])
            g = -jnp.exp(alog_r_ref[d][:, 0:HV]) * _softplus(sm[:, HV:2 * HV] + bias_r_ref[d][:, 0:HV])
            g_t = -jnp.exp(alog_c_ref[d][0:HV, :]) * _softplus(a_raw_t + bias_c_ref[d][0:HV, :])
            mask, mask_t, strict = _scan_masks(d == 0, L)
            cum = _dot_exact_lhs(jnp.where(mask, 1.0, 0.0).astype(BF16), g)
            cum_t = _dot_exact_rhs(g_t, jnp.where(mask_t, 1.0, 0.0).astype(BF16))
            total = jnp.sum(g, axis=0, keepdims=True)
            for h in range(HV):
                cv = cum[:, h:h + 1]
                dc = jnp.exp(jnp.where(mask, cv - cum_t[h:h + 1, :], NEG_BIG))
                bc = beta[:, h:h + 1]
                kk.append(kk_d[h // rep]); qk.append(qk_d[h // rep])
                qh.append(q_n[h // rep]); kh.append(k_n[h // rep])
                vh.append(act[:, (2 * HQ + h) * DK:(2 * HQ + h + 1) * DK])
                colv.append(cv); dec.append(dc); bcol.append(bc); ecol.append(jnp.exp(cv))
                tot.append(total[:, h:h + 1])
                nmat.append(jnp.where(strict, kk_d[h // rep] * bc * dc, 0.0))

        idx = range(nbt * 2 * HV)
        t_inv = _unit_tri_inverse(nmat, L)
        u = [_dot(t_inv[i], (vh[i] * bcol[i]).astype(BF16)) for i in idx]
        w = [_dot(t_inv[i], (kh[i] * (bcol[i] * ecol[i])).astype(BF16)) for i in idx]
        s_prev = [state_scr[i // HV, :, (i % HV) * DK:(i % HV + 1) * DK] for i in idx]
        s_b = [x.astype(BF16) for x in s_prev]
        v_new = [(u[i] - _dot(w[i].astype(BF16), s_b[i])).astype(BF16) for i in idx]
        outs = [_dot((qh[i] * ecol[i]).astype(BF16), s_b[i]) + _dot((qk[i] * dec[i]).astype(BF16), v_new[i])
                for i in idx]
        for i in idx:
            k_dec = kh[i] * jnp.exp(tot[i] - colv[i])
            state_scr[i // HV, :, (i % HV) * DK:(i % HV + 1) * DK] = (
                s_prev[i] * jnp.exp(tot[i]) + _dot(k_dec.T.astype(BF16), v_new[i]))
        o_dirs = [jnp.concatenate(outs[j * HV:(j + 1) * HV], axis=1) for j in range(nbt * 2)]
        rows = [pl.multiple_of(c * L, L) for c in chunks]

        @pl.when(t < nc // 2)
        def _():
            for bi in range(nbt):
                for d in range(2):
                    acc_scr[bi, pl.ds(rows[d], L), :] = o_dirs[bi * 2 + d]

        @pl.when(t >= nc // 2)
        def _():
            for bi in range(nbt):
                for d in range(2):
                    acc_scr[bi, pl.ds(rows[d], L), :] = acc_scr[bi, pl.ds(rows[d], L), :] + o_dirs[bi * 2 + d]

    @pl.when(t >= nc)
    def _():
        r0 = pl.multiple_of((t - nc) * te, te)
        for bi in range(nbt):
            zz = z_ref[bi]
            for hv in range(HV):
                oh = acc_scr[bi, pl.ds(r0, te), hv * DK:(hv + 1) * DK]
                ms = jnp.mean(oh * oh, axis=-1, keepdims=True)
                res = oh * lax.rsqrt(ms + EPS) * nw_ref[...] * _silu(zz[:, hv * DK:(hv + 1) * DK])
                o_ref[bi, :, hv * DK:(hv + 1) * DK] = res.astype(BF16)


def _pad_dir_rows(v, n):
    return jnp.zeros((2, 1, LANES), F32).at[:, 0, :n].set(v)


def _pad_dir_cols(v, n):
    return jnp.zeros((2, LANES, 1), F32).at[:, :n, 0].set(v)


def _gdn(proj, small_t, conv_w, conv_b, dt_bias, a_log, norm_w, *, col_qkv, col_z, te=256, nbt=1):
    bsz, s, _ = proj.shape
    L = GDN_CHUNK
    nc = s // L
    te = min(te, s)
    ne = s // te
    cw = 2 * GDN_QK_HEADS * GDN_HEAD_DIM + BRANCH_W
    hb = L // HALO
    qb, zb = col_qkv // cw, col_z // BRANCH_W
    assert col_qkv % cw == 0 and col_z % BRANCH_W == 0 and nc % 2 == 0 and bsz % nbt == 0

    def cf(t):
        return jnp.minimum(t, nc - 1)

    def cbk(t):
        return jnp.maximum(nc - 1 - t, 0)

    def ep(t):
        return jnp.maximum(t - nc, 0)

    def qkv_specs(chunk):
        return [
            pl.BlockSpec((nbt, L, cw), lambda b, t: (b, chunk(t), qb)),
            pl.BlockSpec((nbt, HALO, cw), lambda b, t: (b, jnp.maximum(chunk(t) * hb - 1, 0), qb)),
            pl.BlockSpec((nbt, HALO, cw), lambda b, t: (b, jnp.minimum((chunk(t) + 1) * hb, s // HALO - 1), qb)),
        ]

    def small_spec(chunk):
        return pl.BlockSpec((nbt, LANES, L), lambda b, t: (b, 0, chunk(t)))

    full = lambda shape: pl.BlockSpec(shape, lambda b, t: (0,) * len(shape))
    hv = GDN_V_HEADS
    return pl.pallas_call(
        functools.partial(_gdn_kernel, nc=nc, te=te),
        out_shape=jax.ShapeDtypeStruct((bsz, s, BRANCH_W), BF16),
        grid=(bsz // nbt, nc + ne),
        in_specs=qkv_specs(cf) + qkv_specs(cbk)
        + [pl.BlockSpec((nbt, te, BRANCH_W), lambda b, t: (b, ep(t), zb))]
        + [small_spec(cf), small_spec(cbk)]
        + [full((GDN_CONV, cw)), full((1, cw)), full((2, 1, LANES)), full((2, LANES, 1)),
           full((2, 1, LANES)), full((2, LANES, 1)), full((1, GDN_HEAD_DIM))],
        out_specs=pl.BlockSpec((nbt, te, BRANCH_W), lambda b, t: (b, ep(t), 0)),
        scratch_shapes=[
            pltpu.VMEM((nbt * 2, L + 2 * HALO, cw), F32),
            pltpu.VMEM((nbt * 2, GDN_HEAD_DIM, BRANCH_W), F32),
            pltpu.VMEM((nbt, s, BRANCH_W), F32),
        ],
        compiler_params=_cparams(("arbitrary", "arbitrary")),
        name="bidir_gated_deltanet",
    )(proj, proj, proj, proj, proj, proj, proj, small_t, small_t, conv_w, conv_b.reshape(1, cw),
      _pad_dir_rows(dt_bias, hv), _pad_dir_cols(dt_bias, hv), _pad_dir_rows(a_log, hv), _pad_dir_cols(a_log, hv),
      norm_w.reshape(1, GDN_HEAD_DIM))


def _local_scan(a, u, ascending):
    rows, width = a.shape
    a = a.reshape(rows // SUBLANES, SUBLANES, width)
    u = u.reshape(rows // SUBLANES, SUBLANES, width)
    sub = lax.broadcasted_iota(jnp.int32, a.shape, 1)
    d = 1
    while d < SUBLANES:
        if ascending:
            keep = sub >= d
            shift = d
        else:
            keep = sub < SUBLANES - d
            shift = SUBLANES - d
        a_sh = jnp.where(keep, pltpu.roll(a, shift, 1), 1.0)
        u_sh = jnp.where(keep, pltpu.roll(u, shift, 1), 0.0)
        u = u + a * u_sh
        a = a * a_sh
        d *= 2
    return a.reshape(rows, width), u.reshape(rows, width)


def _lru_kernel(xl_ref, prev_ref, next_ref, gl_ref, cw_ref, cb_ref, wa_ref, ba_ref, wx_ref, bx_ref, lam_ref,
                o_ref, ext_scr, a_scr, u_scr, hf_scr, carry_scr, *, nb, unroll):
    rows, width = a_scr.shape
    bw = LRU_WIDTH // LRU_BLOCKS
    step = pl.program_id(2)
    fwd = step < nb
    blk = _chunk_of_step(step, nb)

    @pl.when((step == 0) | (step == nb))
    def _():
        carry_scr[...] = jnp.zeros_like(carry_scr)

    _fill_conv_window(ext_scr, xl_ref[0], prev_ref[0], next_ref[0], blk, nb, rows)
    xc = _centred_conv4(ext_scr, cw_ref, cb_ref, rows)
    r_parts, i_parts = [], []
    for n in range(width // bw):
        xb = xc[:, n * bw:(n + 1) * bw].astype(BF16)
        r_parts.append(_dot(xb, wa_ref[0, n]))
        i_parts.append(_dot(xb, wx_ref[0, n]))
    tr = jnp.tanh(jnp.concatenate(r_parts, axis=1) + ba_ref[0])
    ti = jnp.tanh(jnp.concatenate(i_parts, axis=1) + bx_ref[0])
    log_a = ((-0.5 * LRU_C) * _softplus(-lam_ref[0])) * (tr + 1.0)
    a = jnp.exp(log_a)
    quarter = (-0.25 * jnp.tanh(log_a)) * (a * a + 1.0)
    u = (quarter * lax.rsqrt(jnp.maximum(quarter, TINY))) * ((ti + 1.0) * xc)

    n_groups = rows // SUBLANES
    base_out = pl.multiple_of(blk * rows, rows)

    def scan(ascending):
        a_loc, u_loc = _local_scan(a, u, ascending)
        a_scr[...] = a_loc
        u_scr[...] = u_loc
        last = SUBLANES - 1 if ascending else 0

        def body(it, carry):
            pos = it if ascending else n_groups - 1 - it
            r0 = pl.multiple_of(pos * SUBLANES, SUBLANES)
            h = u_scr[pl.ds(r0, SUBLANES), :] + a_scr[pl.ds(r0, SUBLANES), :] * carry
            u_scr[pl.ds(r0, SUBLANES), :] = h
            return jnp.broadcast_to(h[last:last + 1, :], h.shape)

        carry_scr[...] = lax.fori_loop(0, n_groups, body, carry_scr[...], unroll=unroll)

    @pl.when(fwd)
    def _():
        scan(True)
        hf_scr[pl.ds(base_out, rows), :] = u_scr[...]

    @pl.when(jnp.logical_not(fwd))
    def _():
        scan(False)
        o_ref[0] = ((hf_scr[pl.ds(base_out, rows), :] + u_scr[...]) * _silu(gl_ref[0])).astype(BF16)


def _lru(proj, conv_w, conv_b, wa, ba, wx, bx, lam, *, col_xl, col_gl, tt=512, width=512, unroll=8):
    bsz, s, _ = proj.shape
    tt = min(tt, s)
    nb = s // tt
    w_total = LRU_WIDTH
    bw = w_total // LRU_BLOCKS
    nbw = width // bw
    xcol, gcol = col_xl // width, col_gl // width
    assert col_xl % width == 0 and col_gl % width == 0
    hb = tt // HALO

    def blk(t):
        return _chunk_of_step(t, nb)

    def direction(t):
        return jnp.where(t < nb, 0, 1)

    return pl.pallas_call(
        functools.partial(_lru_kernel, nb=nb, unroll=unroll),
        out_shape=jax.ShapeDtypeStruct((bsz, s, w_total), BF16),
        grid=(bsz, w_total // width, 2 * nb),
        in_specs=[
            pl.BlockSpec((1, tt, width), lambda b, j, t: (b, blk(t), xcol + j)),
            pl.BlockSpec((1, HALO, width), lambda b, j, t: (b, jnp.maximum(blk(t) * hb - 1, 0), xcol + j)),
            pl.BlockSpec((1, HALO, width),
                         lambda b, j, t: (b, jnp.minimum((blk(t) + 1) * hb, s // HALO - 1), xcol + j)),
            pl.BlockSpec((1, tt, width), lambda b, j, t: (b, blk(t), gcol + j)),
            pl.BlockSpec((LRU_CONV, width), lambda b, j, t: (0, j)),
            pl.BlockSpec((1, width), lambda b, j, t: (0, j)),
            pl.BlockSpec((1, nbw, bw, bw), lambda b, j, t: (direction(t), j, 0, 0)),
            pl.BlockSpec((1, 1, width), lambda b, j, t: (direction(t), 0, j)),
            pl.BlockSpec((1, nbw, bw, bw), lambda b, j, t: (direction(t), j, 0, 0)),
            pl.BlockSpec((1, 1, width), lambda b, j, t: (direction(t), 0, j)),
            pl.BlockSpec((1, 1, width), lambda b, j, t: (direction(t), 0, j)),
        ],
        out_specs=pl.BlockSpec((1, tt, width),
                               lambda b, j, t: (b, jnp.where(t < nb, nb - 1, 2 * nb - 1 - t), j)),
        scratch_shapes=[
            pltpu.VMEM((tt + 2 * HALO, width), F32),
            pltpu.VMEM((tt, width), F32),
            pltpu.VMEM((tt, width), F32),
            pltpu.VMEM((s, width), F32),
            pltpu.VMEM((SUBLANES, width), F32),
        ],
        compiler_params=_cparams(("arbitrary", "arbitrary", "arbitrary")),
        name="bidir_rglru",
    )(proj, proj, proj, proj, conv_w, conv_b.reshape(1, w_total),
      (0.5 * wa).astype(BF16), (0.5 * ba).reshape(2, 1, w_total),
      (0.5 * wx).astype(BF16), (0.5 * bx).reshape(2, 1, w_total), lam.reshape(2, 1, w_total))


CD_COL_QKV, CD_COL_Z, CD_COL_XL, CD_COL_GL = 0, 2048, 3072, 4096


def _cd_weights(w_in):
    wt = w_in.T
    nqk = GDN_QK_HEADS * GDN_HEAD_DIM
    hv = GDN_V_HEADS
    o = 0
    q = wt[o:o + nqk]; o += nqk
    k = wt[o:o + nqk]; o += nqk
    v = wt[o:o + BRANCH_W]; o += BRANCH_W
    bf = wt[o:o + hv]; o += hv
    bb = wt[o:o + hv]; o += hv
    af = wt[o:o + hv]; o += hv
    ab = wt[o:o + hv]; o += hv
    z = wt[o:o + BRANCH_W]; o += BRANCH_W
    xl = wt[o:o + LRU_WIDTH]; o += LRU_WIDTH
    gl = wt[o:o + LRU_WIDTH]
    head = jnp.concatenate([q, k, v], axis=0).astype(BF16)
    tail = jnp.concatenate([z, xl, gl], axis=0).astype(BF16)
    main = jnp.zeros((head.shape[0] + tail.shape[0], wt.shape[1]), BF16)
    main = lax.dynamic_update_slice(main, head, (0, 0))
    main = lax.dynamic_update_slice(main, tail, (head.shape[0], 0))
    zpad = jnp.zeros((LANES - 4 * hv, wt.shape[1]), wt.dtype)
    small = jnp.concatenate([bf, af, bb, ab, zpad], axis=0).astype(BF16)
    return main, small


def _layer1(x, mod_l, norm_w, w_in, conv_w, conv_b, a_log_f, a_log_b, dt_bias_f, dt_bias_b, gdn_norm,
            lru_conv_w, lru_conv_b, wa_f, ba_f, wx_f, bx_f, lam_f, wa_b, ba_b, wx_b, bx_b, lam_b, w_out,
            final_w=None):
    w_main, w_small = _cd_weights(w_in)
    proj, small_t = _inproj(x, mod_l, norm_w, w_main, w_small)
    gdn = _gdn(proj, small_t, conv_w, conv_b, jnp.stack([dt_bias_f, dt_bias_b]),
               jnp.stack([a_log_f, a_log_b]), gdn_norm, col_qkv=CD_COL_QKV, col_z=CD_COL_Z)
    lru = _lru(proj, lru_conv_w, lru_conv_b, jnp.stack([wa_f, wa_b]), jnp.stack([ba_f, ba_b]),
               jnp.stack([wx_f, wx_b]), jnp.stack([bx_f, bx_b]), jnp.stack([lam_f, lam_b]),
               col_xl=CD_COL_XL, col_gl=CD_COL_GL)
    out = _outproj(gdn, lru, w_out.astype(BF16), x, mod_l, final_w)
    return out, (proj, small_t, gdn, lru)


def kernel(x, c, w_mod, b_mod, norm_w, ab_w_in, ab_q_norm, ab_k_norm, ab_conv_w, ab_conv_b, ab_dt_bias_f, ab_dt_bias_b, ab_a_log_f, ab_a_log_b, ab_d_skip, ab_ssd_norm, ab_w_out, cd_w_in, cd_conv_w, cd_conv_b, cd_a_log_f, cd_a_log_b, cd_dt_bias_f, cd_dt_bias_b, cd_gdn_norm, cd_lru_conv_w, cd_lru_conv_b, cd_lru_wa_f, cd_lru_ba_f, cd_lru_wx_f, cd_lru_bx_f, cd_lru_lam_f, cd_lru_wa_b, cd_lru_ba_b, cd_lru_wx_b, cd_lru_bx_b, cd_lru_lam_b, cd_w_out, final_norm_w):
    mods = _modulation(c, w_mod, b_mod)
    x1, _ = _layer0(x, mods[0], norm_w[0], ab_w_in[0], ab_q_norm[0], ab_k_norm[0], ab_conv_w[0], ab_conv_b[0],
                    ab_dt_bias_f[0], ab_dt_bias_b[0], ab_a_log_f[0], ab_a_log_b[0], ab_d_skip[0],
                    ab_ssd_norm[0], ab_w_out[0])
    out, _ = _layer1(x1, mods[1], norm_w[1], cd_w_in[0], cd_conv_w[0], cd_conv_b[0], cd_a_log_f[0], cd_a_log_b[0],
                     cd_dt_bias_f[0], cd_dt_bias_b[0], cd_gdn_norm[0], cd_lru_conv_w[0], cd_lru_conv_b[0],
                     cd_lru_wa_f[0], cd_lru_ba_f[0], cd_lru_wx_f[0], cd_lru_bx_f[0], cd_lru_lam_f[0],
                     cd_lru_wa_b[0], cd_lru_ba_b[0], cd_lru_wx_b[0], cd_lru_bx_b[0], cd_lru_lam_b[0],
                     cd_w_out[0], final_norm_w)
    return out
```

```python
import functools
import math

import jax
import jax.numpy as jnp
import numpy as np
from jax import lax
from jax.experimental import pallas as pl
from jax.experimental.pallas import tpu as pltpu

F32 = jnp.float32
BF16 = jnp.bfloat16

D_MODEL = 2048
GRID_W = 64
EPS = 1e-6
BRANCH_W = D_MODEL // 2
ATT_HEAD_DIM = 128
ATT_HEADS = BRANCH_W // ATT_HEAD_DIM
ATT_KV_HEADS = ATT_HEADS // 4
ATT_GROUP = ATT_HEADS // ATT_KV_HEADS
ROPE_THETA = 10000.0
SSD_HEAD_DIM = 64
SSD_HEADS = BRANCH_W // SSD_HEAD_DIM
SSD_GROUPS = 2
SSD_STATE = 128
SSD_CONV = 4
SSD_CHUNK = 128
GDN_HEAD_DIM = 128
GDN_V_HEADS = BRANCH_W // GDN_HEAD_DIM
GDN_QK_HEADS = GDN_V_HEADS // 2
GDN_CONV = 4
GDN_CHUNK = 128
GDN_WAVE = 16
LRU_WIDTH = BRANCH_W
LRU_BLOCKS = 8
LRU_CONV = 4
LRU_C = 8.0

LANES = 128
SUBLANES = 8
VMEM_LIMIT_BYTES = 56 * 1024 * 1024

HALO = SUBLANES
NEG_BIG = -1e30
TINY = 1e-37
CAST_ROWS = 256


def _cparams(sem):
    return pltpu.CompilerParams(dimension_semantics=sem, vmem_limit_bytes=VMEM_LIMIT_BYTES)


def _sigmoid(x):
    return 0.5 * jnp.tanh(0.5 * x) + 0.5


def _silu(x):
    h = 0.5 * x
    return h * (jnp.tanh(h) + 1.0)


def _softplus(x):
    return jnp.maximum(x, 0.0) + jnp.log(1.0 + jnp.exp(-jnp.abs(x)))


def _split_bf16(a):
    hi = a.astype(BF16)
    lo = (a - hi.astype(F32)).astype(BF16)
    return hi, lo


def _dot(a, b):
    return jnp.dot(a, b, preferred_element_type=F32)


def _dot_nt(a, b):
    return lax.dot_general(a, b, (((1,), (1,)), ((), ())), preferred_element_type=F32)


def _dot_exact_rhs(a, b_bf16):
    hi, lo = _split_bf16(a)
    return _dot(hi, b_bf16) + _dot(lo, b_bf16)


def _dot_exact_lhs(a_bf16, b):
    hi, lo = _split_bf16(b)
    return _dot(a_bf16, hi) + _dot(a_bf16, lo)


def _mod_kernel(c_ref, w_ref, b_ref, o_ref):
    cond = _silu(c_ref[...])
    c_hi, c_lo = _split_bf16(cond)
    w = w_ref[0]
    w_hi, w_lo = _split_bf16(w)
    o_ref[0] = _dot(c_hi, w_hi) + _dot(c_lo, w_hi) + _dot(c_hi, w_lo) + b_ref[0]


def _modulation(c, w_mod, b_mod):
    depth, d, n = w_mod.shape
    bsz = c.shape[0]
    rows = -(-bsz // SUBLANES) * SUBLANES
    c_pad = jnp.zeros((rows, d), F32).at[:bsz].set(c)
    tn = 1536
    out = pl.pallas_call(
        _mod_kernel,
        out_shape=jax.ShapeDtypeStruct((depth, rows, n), F32),
        grid=(depth, n // tn),
        in_specs=[
            pl.BlockSpec((rows, d), lambda l, j: (0, 0)),
            pl.BlockSpec((1, d, tn), lambda l, j: (l, 0, j)),
            pl.BlockSpec((1, 1, tn), lambda l, j: (l, 0, j)),
        ],
        out_specs=pl.BlockSpec((1, rows, tn), lambda l, j: (l, 0, j)),
        compiler_params=_cparams(("arbitrary", "arbitrary")),
        name="adaln_mod",
    )(c_pad, w_mod, b_mod.reshape(depth, 1, n))
    return out[:, :bsz].reshape(depth, bsz, 3, d)


def _inproj_kernel(x_ref, mod_ref, nw_ref, w_ref, wst_ref, o_ref, ost_ref, h_even, h_odd, *, n_groups):
    g = pl.program_id(0)
    j = pl.program_id(1)
    rows = x_ref.shape[1]

    def prep(h_dst):
        x = x_ref[0]
        ms = jnp.mean(x * x, axis=-1, keepdims=True)
        y = x * lax.rsqrt(ms + EPS) * nw_ref[...]
        h = (y * (1.0 + mod_ref[0, 1:2, :]) + mod_ref[0, 0:1, :]).astype(BF16)
        h_dst[pl.ds(pl.multiple_of(j * rows, rows), rows), :] = h
        ost_ref[0] = _dot_nt(wst_ref[...], h)

    @pl.when(g == 0)
    def _():
        prep(h_even)

    @pl.when((g > 0) & (g % 2 == 0))
    def _():
        o_ref[0] = _dot_nt(h_odd[...], w_ref[...])
        prep(h_even)

    @pl.when(g % 2 == 1)
    def _():
        o_ref[0] = _dot_nt(h_even[...], w_ref[...])
        prep(h_odd)


def _inproj(x, mod_l, norm_w, w_main, w_small, *, tm=1024, nj=4):
    bsz, s, d = x.shape
    n = w_main.shape[0]
    ns = w_small.shape[0]
    tm = min(tm, s)
    ni = s // tm
    n_groups = bsz * ni
    tn = n // nj
    rows = tm // nj
    assert n % nj == 0 and tn % LANES == 0 and rows % LANES == 0

    def prep(g, j):
        gc = jnp.minimum(g, n_groups - 1)
        return gc // ni, (gc % ni) * nj + jnp.where(g < n_groups, j, nj - 1)

    def mm(g, j):
        gm = jnp.maximum(g - 1, 0)
        return gm // ni, gm % ni, jnp.where(g > 0, j, 0)

    return pl.pallas_call(
        functools.partial(_inproj_kernel, n_groups=n_groups),
        out_shape=[jax.ShapeDtypeStruct((bsz, s, n), F32), jax.ShapeDtypeStruct((bsz, ns, s), F32)],
        grid=(n_groups + 1, nj),
        in_specs=[
            pl.BlockSpec((1, rows, d), lambda g, j: (*prep(g, j), 0)),
            pl.BlockSpec((1, 3, d), lambda g, j: (prep(g, j)[0], 0, 0)),
            pl.BlockSpec((1, d), lambda g, j: (0, 0)),
            pl.BlockSpec((tn, d), lambda g, j: (mm(g, j)[2], 0)),
            pl.BlockSpec((ns, d), lambda g, j: (0, 0)),
        ],
        out_specs=[
            pl.BlockSpec((1, tm, tn), lambda g, j: mm(g, j)),
            pl.BlockSpec((1, ns, rows), lambda g, j: (prep(g, j)[0], 0, prep(g, j)[1])),
        ],
        scratch_shapes=[pltpu.VMEM((tm, d), BF16), pltpu.VMEM((tm, d), BF16)],
        compiler_params=_cparams(("arbitrary", "arbitrary")),
        name="norm_mod_inproj",
    )(x, mod_l, norm_w.reshape(1, d), w_main, w_small)


def _outproj_kernel(ya_ref, yb_ref, w_ref, x_ref, mod_ref, *rest, final_norm, half):
    if final_norm:
        fnw_ref, o_ref, w_scr = rest
    else:
        o_ref, w_scr = rest

    @pl.when((pl.program_id(0) == 0) & (pl.program_id(1) == 0))
    def _():
        rows = CAST_ROWS

        def body(r, carry):
            r0 = pl.multiple_of(r * rows, rows)
            w_scr[pl.ds(r0, rows), :] = w_ref[pl.ds(r0, rows), :].astype(BF16)
            return carry

        lax.fori_loop(0, w_ref.shape[0] // rows, body, 0)

    acc = _dot(ya_ref[0], w_scr[0:half, :]) + _dot(yb_ref[0], w_scr[half:, :])
    gate = mod_ref[0, 2:3, :]
    xn = x_ref[0] + gate * acc
    if final_norm:
        ms = jnp.mean(xn * xn, axis=-1, keepdims=True)
        xn = xn * lax.rsqrt(ms + EPS) * fnw_ref[...]
    o_ref[0] = xn


def _outproj(ya, yb, w_out, x, mod_l, final_w=None, *, tm=512):
    bsz, s, d = x.shape
    half = ya.shape[-1]
    tm = min(tm, s)
    final_norm = final_w is not None
    in_specs = [
        pl.BlockSpec((1, tm, half), lambda b, i: (b, i, 0)),
        pl.BlockSpec((1, tm, half), lambda b, i: (b, i, 0)),
        pl.BlockSpec((2 * half, d), lambda b, i: (0, 0), pipeline_mode=pl.Buffered(1)),
        pl.BlockSpec((1, tm, d), lambda b, i: (b, i, 0)),
        pl.BlockSpec((1, 3, d), lambda b, i: (b, 0, 0)),
    ]
    args = [ya, yb, w_out, x, mod_l]
    if final_norm:
        in_specs.append(pl.BlockSpec((1, d), lambda b, i: (0, 0)))
        args.append(final_w.reshape(1, d))
    return pl.pallas_call(
        functools.partial(_outproj_kernel, final_norm=final_norm, half=half),
        out_shape=jax.ShapeDtypeStruct((bsz, s, d), F32),
        grid=(bsz, s // tm),
        in_specs=in_specs,
        out_specs=pl.BlockSpec((1, tm, d), lambda b, i: (b, i, 0)),
        scratch_shapes=[pltpu.VMEM((2 * half, d), BF16)],
        compiler_params=_cparams(("arbitrary", "arbitrary")),
        name="outproj_residual",
    )(*args)


def _rms_rope(x, nw, cos, sin_signed):
    ms = jnp.mean(x * x, axis=-1, keepdims=True)
    y = x * lax.rsqrt(ms + EPS) * nw
    lane = lax.broadcasted_iota(jnp.int32, y.shape, 1)
    partner = jnp.where(lane % 2 == 0, pltpu.roll(y, ATT_HEAD_DIM - 1, 1), pltpu.roll(y, 1, 1))
    return y * cos + partner * sin_signed


def _attn_kernel(q_ref, k_ref, v_ref, ga_ref, cq_ref, sq_ref, ck_ref, sk_ref, qn_ref, kn_ref,
                 o_ref, k_scr, v_scr, *, tq, sub):
    qi = pl.program_id(2)

    dh = ATT_HEAD_DIM

    @pl.when(qi == 0)
    def _():
        k_scr[...] = _rms_rope(k_ref[0], kn_ref[...], ck_ref[...], sk_ref[...]).astype(BF16)
        v_scr[:, 0:dh] = v_ref[0].astype(BF16)
        v_scr[:, dh:2 * dh] = jnp.ones((v_scr.shape[0], dh), BF16)

    scale = dh ** -0.5
    cos = cq_ref[...]
    sin = sq_ref[...]

    probs = [(g, r) for g in range(ATT_GROUP) for r in range(0, tq, sub)]

    def scores(g, r):
        qg = q_ref[0, r:r + sub, g * dh:(g + 1) * dh]
        qg = (_rms_rope(qg, qn_ref[...], cos[r:r + sub], sin[r:r + sub]) * scale).astype(BF16)
        return _dot_nt(qg, k_scr[...])

    s_next = scores(*probs[0])
    for i, (g, r) in enumerate(probs):
        s = s_next
        if i + 1 < len(probs):
            s_next = scores(*probs[i + 1])
        s = s.astype(BF16)
        p = jnp.exp(s - jnp.max(s, axis=-1, keepdims=True))
        o_ext = _dot(p, v_scr[...])
        og = o_ext[:, 0:dh] / o_ext[:, dh:2 * dh] * _silu(ga_ref[0, r:r + sub, g * dh:(g + 1) * dh])
        o_ref[0, r:r + sub, g * dh:(g + 1) * dh] = og.astype(BF16)


def _attention(proj, cos_t, sin_t, q_norm, k_norm, *, col_q, col_k, col_v, col_ga, tq=512, sub=128):
    bsz, s, _ = proj.shape
    tq = min(tq, s)
    gw = ATT_GROUP * ATT_HEAD_DIM
    dh = ATT_HEAD_DIM
    qb, kb, vb, gb = col_q // gw, col_k // dh, col_v // dh, col_ga // gw
    return pl.pallas_call(
        functools.partial(_attn_kernel, tq=tq, sub=min(sub, tq)),
        out_shape=jax.ShapeDtypeStruct((bsz, s, BRANCH_W), BF16),
        grid=(bsz, ATT_KV_HEADS, s // tq),
        in_specs=[
            pl.BlockSpec((1, tq, gw), lambda b, h, i: (b, i, qb + h)),
            pl.BlockSpec((1, s, dh), lambda b, h, i: (b, 0, kb + h)),
            pl.BlockSpec((1, s, dh), lambda b, h, i: (b, 0, vb + h)),
            pl.BlockSpec((1, tq, gw), lambda b, h, i: (b, i, gb + h)),
            pl.BlockSpec((tq, dh), lambda b, h, i: (i, 0)),
            pl.BlockSpec((tq, dh), lambda b, h, i: (i, 0)),
            pl.BlockSpec((s, dh), lambda b, h, i: (0, 0)),
            pl.BlockSpec((s, dh), lambda b, h, i: (0, 0)),
            pl.BlockSpec((1, dh), lambda b, h, i: (0, 0)),
            pl.BlockSpec((1, dh), lambda b, h, i: (0, 0)),
        ],
        out_specs=pl.BlockSpec((1, tq, gw), lambda b, h, i: (b, i, h)),
        scratch_shapes=[pltpu.VMEM((s, dh), BF16), pltpu.VMEM((s, 2 * dh), BF16)],
        compiler_params=_cparams(("arbitrary", "arbitrary", "arbitrary")),
        name="gqa_attention",
    )(proj, proj, proj, proj, cos_t, sin_t, cos_t, sin_t, q_norm, k_norm)


def _rope_tables(s):
    t = np.arange(s)
    row = (t // GRID_W).astype(np.float64)
    col = (t % GRID_W).astype(np.float64)
    n_pairs = ATT_HEAD_DIM // 4
    freqs = ROPE_THETA ** (-np.arange(n_pairs, dtype=np.float64) / n_pairs)
    ang = np.concatenate([row[:, None] * freqs, col[:, None] * freqs], axis=-1)
    cos, sin = np.cos(ang), np.sin(ang)
    cos_t = np.repeat(cos, 2, axis=-1)
    sin_t = np.stack([-sin, sin], axis=-1).reshape(s, ATT_HEAD_DIM)
    return jnp.asarray(cos_t, F32), jnp.asarray(sin_t, F32)


def _chunk_of_step(step, nc):
    return jnp.where(step < nc, step, 2 * nc - 1 - step)


def _fill_conv_window(ext_scr, cur, prev, nxt, c, nc, rows):
    ext_scr[0:HALO, :] = jnp.where(c > 0, prev, 0.0)
    ext_scr[HALO:HALO + rows, :] = cur
    ext_scr[HALO + rows:HALO + rows + HALO, :] = jnp.where(c < nc - 1, nxt, 0.0)


CONV_SUB = 128


def _centred_conv4(ext_scr, cw_ref, cb_ref, rows):
    sub = min(rows, CONV_SUB)
    n_in = sub + 2 * HALO
    taps = (0, 1, 3)
    r = lax.broadcasted_iota(jnp.int32, (len(taps) * sub, n_in), 0)
    c = lax.broadcasted_iota(jnp.int32, (len(taps) * sub, n_in), 1)
    src = jnp.zeros_like(r)
    for i, k in enumerate(taps):
        src = jnp.where((r >= i * sub) & (r < (i + 1) * sub), r - i * sub + HALO + (k - 2), src)
    shift_mat = jnp.where(c == src, 1.0, 0.0).astype(BF16)
    outs = []
    for s0 in range(0, rows, sub):
        win = ext_scr[s0:s0 + n_in, :]
        shifted = _dot(shift_mat, win.astype(BF16))
        acc = cb_ref[...] + win[HALO:HALO + sub, :] * cw_ref[2:3, :]
        for i, k in enumerate(taps):
            acc = acc + shifted[i * sub:(i + 1) * sub, :] * cw_ref[k:k + 1, :]
        outs.append(acc)
    return outs[0] if len(outs) == 1 else jnp.concatenate(outs, axis=0)


def _scan_masks(fwd, n):
    row = lax.broadcasted_iota(jnp.int32, (n, n), 0)
    col = lax.broadcasted_iota(jnp.int32, (n, n), 1)
    sgn = jnp.where(fwd, 1, -1)
    d = (row - col) * sgn
    return d >= 0, d <= 0, d > 0


def _pad_dir_rows(v, n):
    return jnp.zeros((2, 1, LANES), F32).at[:, 0, :n].set(v)


def _pad_dir_cols(v, n):
    return jnp.zeros((2, LANES, 1), F32).at[:, :n, 0].set(v)


def _head_expander(heads, width):
    r = lax.broadcasted_iota(jnp.int32, (heads, heads * width), 0)
    c = lax.broadcasted_iota(jnp.int32, (heads, heads * width), 1)
    return jnp.where((c >= r * width) & (c < (r + 1) * width), 1.0, 0.0).astype(BF16)


def _ssd_kernel(xf_ref, pf_ref, nf_ref, xb_ref, pb_ref, nb_ref, z_ref, dtf_ref, dtb_ref, cw_ref, cb_ref,
                bias_r_ref, bias_c_ref, alog_r_ref, alog_c_ref, dskip_ref, nw_ref,
                o_ref, ext_scr, state_scr, acc_scr, *, nc, te):
    L, H, P, N = SSD_CHUNK, SSD_HEADS, SSD_HEAD_DIM, SSD_STATE
    HG = H // SSD_GROUPS
    GW = HG * P
    t = pl.program_id(1)

    @pl.when(t == 0)
    def _():
        state_scr[...] = jnp.zeros_like(state_scr)

    @pl.when(t < nc)
    def _():
        chunks = (t, nc - 1 - t)
        blocks = ((xf_ref, pf_ref, nf_ref, dtf_ref), (xb_ref, pb_ref, nb_ref, dtb_ref))
        expand = _head_expander(H, P)
        lane = lax.broadcasted_iota(jnp.int32, (L, 2 * P), 1)
        per_dir = []
        for d in range(2):
            x_ref, p_ref, n_ref, dtt_ref = blocks[d]
            ext = ext_scr.at[d]
            _fill_conv_window(ext, x_ref[0], p_ref[0], n_ref[0], chunks[d], nc, L)
            xbc = _silu(_centred_conv4(ext, cw_ref, cb_ref, L))
            xs = xbc[:, :BRANCH_W]
            bs = xbc[:, BRANCH_W:BRANCH_W + SSD_GROUPS * N]
            cs = xbc[:, BRANCH_W + SSD_GROUPS * N:]
            raw_t = dtt_ref[0]
            raw = raw_t.T
            dt = _softplus(raw[:, d * H:(d + 1) * H] + bias_r_ref[d][:, 0:H])
            a = dt * (-jnp.exp(alog_r_ref[d][:, 0:H]))
            dt_t = _softplus(raw_t[d * H:(d + 1) * H, :] + bias_c_ref[d][0:H, :])
            a_t = dt_t * (-jnp.exp(alog_c_ref[d][0:H, :]))
            mask, mask_t, _ = _scan_masks(d == 0, L)
            cum = _dot_exact_lhs(jnp.where(mask, 1.0, 0.0).astype(BF16), a)
            cum_t = _dot_exact_rhs(a_t, jnp.where(mask_t, 1.0, 0.0).astype(BF16))
            total = jnp.sum(a, axis=0, keepdims=True)
            dt_e = _dot(dt.astype(BF16), expand)
            p_e = _dot(jnp.exp(cum).astype(BF16), expand)
            q_e = _dot(jnp.exp(total - cum).astype(BF16), expand)
            tot_e = _dot_exact_rhs(jnp.broadcast_to(jnp.exp(total), (SUBLANES, H)), expand)[0:1, :]
            xd = xs * dt_e
            per_dir.append(dict(xs=xs, bs=bs, cs=cs, mask=mask, cum=cum, cum_t=cum_t, p_e=p_e, tot_e=tot_e,
                                xdq=(xd * q_e).astype(BF16), xd_b=xd.astype(BF16)))

        combos = [(d, g) for d in range(2) for g in range(SSD_GROUPS)]
        cg = {k: per_dir[k[0]]["cs"][:, k[1] * N:(k[1] + 1) * N].astype(BF16) for k in combos}
        bg = {k: per_dir[k[0]]["bs"][:, k[1] * N:(k[1] + 1) * N] for k in combos}
        gmat = {k: _dot_nt(cg[k], bg[k].astype(BF16)) for k in combos}
        h_prev = {k: state_scr[k[0], :, k[1] * GW:(k[1] + 1) * GW] for k in combos}
        y_off = {k: _dot(cg[k], h_prev[k].astype(BF16)) * per_dir[k[0]]["p_e"][:, k[1] * GW:(k[1] + 1) * GW]
                 for k in combos}
        pairs = {k: [] for k in combos}
        for hp in range(HG // 2):
            for k in combos:
                d, g = k
                pd = per_dir[d]
                h0 = g * HG + 2 * hp
                xpair = pd["xd_b"][:, h0 * P:(h0 + 2) * P]
                ys = []
                for h in (h0, h0 + 1):
                    dec = jnp.exp(jnp.where(pd["mask"], pd["cum"][:, h:h + 1] - pd["cum_t"][h:h + 1, :], NEG_BIG))
                    ys.append(_dot((gmat[k] * dec).astype(BF16), xpair))
                pairs[k].append(jnp.where(lane < P, ys[0], ys[1]))
        for k in combos:
            d, g = k
            state_scr[d, :, g * GW:(g + 1) * GW] = (
                h_prev[k] * per_dir[d]["tot_e"][:, g * GW:(g + 1) * GW]
                + _dot(bg[k].T.astype(BF16), per_dir[d]["xdq"][:, g * GW:(g + 1) * GW]))
        y_dirs = [jnp.concatenate([jnp.concatenate(pairs[(d, g)], axis=1) + y_off[(d, g)]
                                   for g in range(SSD_GROUPS)], axis=1) for d in range(2)]
        y_dirs[0] = y_dirs[0] + dskip_ref[...] * per_dir[0]["xs"]
        rows = [pl.multiple_of(c * L, L) for c in chunks]

        @pl.when(t < nc // 2)
        def _():
            for d in range(2):
                acc_scr[pl.ds(rows[d], L), :] = y_dirs[d]

        @pl.when(t >= nc // 2)
        def _():
            for d in range(2):
                acc_scr[pl.ds(rows[d], L), :] = acc_scr[pl.ds(rows[d], L), :] + y_dirs[d]

    @pl.when(t >= nc)
    def _():
        r0 = pl.multiple_of((t - nc) * te, te)
        y = acc_scr[pl.ds(r0, te), :] * _silu(z_ref[0])
        ms = jnp.mean(y * y, axis=-1, keepdims=True)
        o_ref[0] = (y * lax.rsqrt(ms + EPS) * nw_ref[...]).astype(BF16)


def _ssd(proj, small_t, conv_w, conv_b, dt_bias, a_log, d_skip, norm_w, *, col_xbc, col_z, te=256):
    bsz, s, _ = proj.shape
    L = SSD_CHUNK
    nc = s // L
    te = min(te, s)
    ne = s // te
    cw = BRANCH_W + 2 * SSD_GROUPS * SSD_STATE
    hb = L // HALO
    xb, zb = col_xbc // cw, col_z // BRANCH_W
    assert col_xbc % cw == 0 and col_z % BRANCH_W == 0 and nc % 2 == 0

    def cf(t):
        return jnp.minimum(t, nc - 1)

    def cbk(t):
        return jnp.maximum(nc - 1 - t, 0)

    def ep(t):
        return jnp.maximum(t - nc, 0)

    def xbc_specs(chunk):
        return [
            pl.BlockSpec((1, L, cw), lambda b, t: (b, chunk(t), xb)),
            pl.BlockSpec((1, HALO, cw), lambda b, t: (b, jnp.maximum(chunk(t) * hb - 1, 0), xb)),
            pl.BlockSpec((1, HALO, cw), lambda b, t: (b, jnp.minimum((chunk(t) + 1) * hb, s // HALO - 1), xb)),
        ]

    def small_spec(chunk):
        return pl.BlockSpec((1, LANES, L), lambda b, t: (b, 0, chunk(t)))

    full = lambda shape: pl.BlockSpec(shape, lambda b, t: (0,) * len(shape))
    return pl.pallas_call(
        functools.partial(_ssd_kernel, nc=nc, te=te),
        out_shape=jax.ShapeDtypeStruct((bsz, s, BRANCH_W), BF16),
        grid=(bsz, nc + ne),
        in_specs=xbc_specs(cf) + xbc_specs(cbk)
        + [pl.BlockSpec((1, te, BRANCH_W), lambda b, t: (b, ep(t), zb))]
        + [small_spec(cf), small_spec(cbk)]
        + [full((SSD_CONV, cw)), full((1, cw)), full((2, 1, LANES)), full((2, LANES, 1)),
           full((2, 1, LANES)), full((2, LANES, 1)), full((1, BRANCH_W)), full((1, BRANCH_W))],
        out_specs=pl.BlockSpec((1, te, BRANCH_W), lambda b, t: (b, ep(t), 0)),
        scratch_shapes=[
            pltpu.VMEM((2, L + 2 * HALO, cw), F32),
            pltpu.VMEM((2, SSD_STATE, BRANCH_W), F32),
            pltpu.VMEM((s, BRANCH_W), F32),
        ],
        compiler_params=_cparams(("arbitrary", "arbitrary")),
        name="bidir_ssd",
    )(proj, proj, proj, proj, proj, proj, proj, small_t, small_t, conv_w, conv_b.reshape(1, cw),
      _pad_dir_rows(dt_bias, SSD_HEADS), _pad_dir_cols(dt_bias, SSD_HEADS),
      _pad_dir_rows(a_log, SSD_HEADS), _pad_dir_cols(a_log, SSD_HEADS),
      jnp.repeat(d_skip, SSD_HEAD_DIM).reshape(1, BRANCH_W), norm_w.reshape(1, BRANCH_W))


AB_COL_XBC, AB_COL_K, AB_COL_V, AB_COL_Z, AB_COL_Q, AB_COL_GA, AB_N = 0, 1536, 1792, 2048, 3072, 4096, 5120


def _ab_weights(w_in):
    wt = w_in.T
    hq, hk = ATT_HEADS * ATT_HEAD_DIM, ATT_KV_HEADS * ATT_HEAD_DIM
    gn = SSD_GROUPS * SSD_STATE
    o = 0
    q = wt[o:o + hq]; o += hq
    k = wt[o:o + hk]; o += hk
    v = wt[o:o + hk]; o += hk
    ga = wt[o:o + BRANCH_W]; o += BRANCH_W
    xs = wt[o:o + BRANCH_W]; o += BRANCH_W
    bs = wt[o:o + gn]; o += gn
    cs = wt[o:o + gn]; o += gn
    dtf = wt[o:o + SSD_HEADS]; o += SSD_HEADS
    dtb = wt[o:o + SSD_HEADS]; o += SSD_HEADS
    z = wt[o:o + BRANCH_W]
    main = jnp.concatenate([xs, bs, cs, k, v, z, q, ga], axis=0).astype(BF16)
    zpad = jnp.zeros((LANES - 2 * SSD_HEADS, wt.shape[1]), wt.dtype)
    small = jnp.concatenate([dtf, dtb, zpad], axis=0).astype(BF16)
    return main, small


def _layer0(x, mod_l, norm_w, w_in, q_norm, k_norm, conv_w, conv_b, dt_bias_f, dt_bias_b,
            a_log_f, a_log_b, d_skip, ssd_norm, w_out, final_w=None):
    s = x.shape[1]
    w_main, w_small = _ab_weights(w_in)
    proj, small_t = _inproj(x, mod_l, norm_w, w_main, w_small)
    cos_t, sin_t = _rope_tables(s)
    att = _attention(proj, cos_t, sin_t, q_norm.reshape(1, ATT_HEAD_DIM), k_norm.reshape(1, ATT_HEAD_DIM),
                     col_q=AB_COL_Q, col_k=AB_COL_K, col_v=AB_COL_V, col_ga=AB_COL_GA)
    ssd = _ssd(proj, small_t, conv_w, conv_b, jnp.stack([dt_bias_f, dt_bias_b]),
               jnp.stack([a_log_f, a_log_b]), d_skip, ssd_norm, col_xbc=AB_COL_XBC, col_z=AB_COL_Z)
    return _outproj(att, ssd, w_out, x, mod_l, final_w), (proj, small_t, att, ssd)


def _unit_tri_inverse(nmats, n):
    row = lax.broadcasted_iota(jnp.int32, (n, n), 0)
    col = lax.broadcasted_iota(jnp.int32, (n, n), 1)

    def same_block(size):
        return (row // size) == (col // size)

    def mm(a, b):
        return _dot(a, b).astype(BF16)

    def as_mask(cond):
        return jnp.where(cond, 1.0, 0.0).astype(BF16)

    nmats = [m.astype(BF16) for m in nmats]
    eye = as_mask(row == col)
    base = SUBLANES
    blk = same_block(base)
    blk_m = as_mask(blk)
    nd = [m * blk_m for m in nmats]
    p1 = [mm(x, x) for x in nd]
    p2 = [mm(x, x) for x in p1]
    t = [eye - x for x in nd]
    t = [x + mm(x, p) for x, p in zip(t, p1)]
    t = [x + mm(x, p) for x, p in zip(t, p2)]
    size = base
    while size < n:
        nxt = same_block(2 * size)
        off_m = as_mask(nxt & jnp.logical_not(blk))
        et = [mm(m * off_m, x) for m, x in zip(nmats, t)]
        t = [x - mm(x, y) for x, y in zip(t, et)]
        blk = nxt
        size *= 2
    return t


def _l2norm(x):
    return x * lax.rsqrt(jnp.sum(x * x, axis=-1, keepdims=True) + EPS)


def _gdn_kernel(qf_ref, pf_ref, nf_ref, qb_ref, pb_ref, nb_ref, z_ref, smtf_ref, smtb_ref,
                cw_ref, cb_ref, bias_r_ref, bias_c_ref, alog_r_ref, alog_c_ref, nw_ref,
                o_ref, ext_scr, state_scr, acc_scr, *, nc, te):
    L, HV, HQ, DK = GDN_CHUNK, GDN_V_HEADS, GDN_QK_HEADS, GDN_HEAD_DIM
    rep = HV // HQ
    nbt = acc_scr.shape[0]
    t = pl.program_id(1)

    @pl.when(t == 0)
    def _():
        state_scr[...] = jnp.zeros_like(state_scr)

    @pl.when(t < nc)
    def _():
        chunks = (t, nc - 1 - t)
        blocks = ((qf_ref, pf_ref, nf_ref, smtf_ref), (qb_ref, pb_ref, nb_ref, smtb_ref))
        kk, qk, qh, kh, vh, colv, dec, bcol, ecol, tot, nmat = ([] for _ in range(11))
        for bi, d in [(bi, d) for bi in range(nbt) for d in range(2)]:
            c = chunks[d]
            q_ref, p_ref, n_ref, smt_ref = blocks[d]
            ext = ext_scr.at[bi * 2 + d]
            _fill_conv_window(ext, q_ref[bi], p_ref[bi], n_ref[bi], c, nc, L)
            act = _silu(_centred_conv4(ext, cw_ref, cb_ref, L))
            q_n = [_l2norm(act[:, h * DK:(h + 1) * DK]) * (DK ** -0.5) for h in range(HQ)]
            k_n = [_l2norm(act[:, (HQ + h) * DK:(HQ + h + 1) * DK]) for h in range(HQ)]
            k_b = [x.astype(BF16) for x in k_n]
            kk_d = [_dot_nt(k_b[h], k_b[h]) for h in range(HQ)]
            qk_d = [_dot_nt(q_n[h].astype(BF16), k_b[h]) for h in range(HQ)]

            raw_t = smt_ref[bi]
            sm = raw_t.T[:, 2 * HV * d:2 * HV * (d + 1)]
            a_raw_t = raw_t[2 * HV * d + HV:2 * HV * (d + 1), :]
            beta = _sigmoid(sm[:, 0:HV])
            g = -jnp.exp(alog_r_ref[d][:, 0:HV]) * _softplus(sm[:, HV:2 * HV] + bias_r_ref[d][:, 0:HV])
            g_t = -jnp.exp(alog_c_ref[d][0:HV, :]) * _softplus(a_raw_t + bias_c_ref[d][0:HV, :])
            mask, mask_t, strict = _scan_masks(d == 0, L)
            cum = _dot_exact_lhs(jnp.where(mask, 1.0, 0.0).astype(BF16), g)
            cum_t = _dot_exact_rhs(g_t, jnp.where(mask_t, 1.0, 0.0).astype(BF16))
            total = jnp.sum(g, axis=0, keepdims=True)
            for h in range(HV):
                cv = cum[:, h:h + 1]
                dc = jnp.exp(jnp.where(mask, cv - cum_t[h:h + 1, :], NEG_BIG))
                bc = beta[:, h:h + 1]
                kk.append(kk_d[h // rep]); qk.append(qk_d[h // rep])
                qh.append(q_n[h // rep]); kh.append(k_n[h // rep])
                vh.append(act[:, (2 * HQ + h) * DK:(2 * HQ + h + 1) * DK])
                colv.append(cv); dec.append(dc); bcol.append(bc); ecol.append(jnp.exp(cv))
                tot.append(total[:, h:h + 1])
                nmat.append(jnp.where(strict, kk_d[h // rep] * bc * dc, 0.0))

        outs = []
        for w0 in range(0, nbt * 2 * HV, GDN_WAVE):
            idx = range(w0, w0 + GDN_WAVE)
            t_inv = dict(zip(idx, _unit_tri_inverse([nmat[i] for i in idx], L)))
            u = {i: _dot(t_inv[i], (vh[i] * bcol[i]).astype(BF16)) for i in idx}
            w = {i: _dot(t_inv[i], (kh[i] * (bcol[i] * ecol[i])).astype(BF16)) for i in idx}
            s_prev = {i: state_scr[i // HV, :, (i % HV) * DK:(i % HV + 1) * DK] for i in idx}
            s_b = {i: s_prev[i].astype(BF16) for i in idx}
            v_new = {i: (u[i] - _dot(w[i].astype(BF16), s_b[i])).astype(BF16) for i in idx}
            outs += [_dot((qh[i] * ecol[i]).astype(BF16), s_b[i])
                     + _dot((qk[i] * dec[i]).astype(BF16), v_new[i]) for i in idx]
            for i in idx:
                k_dec = kh[i] * jnp.exp(tot[i] - colv[i])
                state_scr[i // HV, :, (i % HV) * DK:(i % HV + 1) * DK] = (
                    s_prev[i] * jnp.exp(tot[i]) + _dot(k_dec.T.astype(BF16), v_new[i]))
        o_dirs = [jnp.concatenate(outs[j * HV:(j + 1) * HV], axis=1) for j in range(nbt * 2)]
        rows = [pl.multiple_of(c * L, L) for c in chunks]

        @pl.when(t < nc // 2)
        def _():
            for bi in range(nbt):
                for d in range(2):
                    acc_scr[bi, pl.ds(rows[d], L), :] = o_dirs[bi * 2 + d]

        @pl.when(t >= nc // 2)
        def _():
            for bi in range(nbt):
                for d in range(2):
                    acc_scr[bi, pl.ds(rows[d], L), :] = acc_scr[bi, pl.ds(rows[d], L), :] + o_dirs[bi * 2 + d]

    @pl.when(t >= nc)
    def _():
        r0 = pl.multiple_of((t - nc) * te, te)
        for bi in range(nbt):
            zz = z_ref[bi]
            for hv in range(HV):
                oh = acc_scr[bi, pl.ds(r0, te), hv * DK:(hv + 1) * DK]
                ms = jnp.mean(oh * oh, axis=-1, keepdims=True)
                res = oh * lax.rsqrt(ms + EPS) * nw_ref[...] * _silu(zz[:, hv * DK:(hv + 1) * DK])
                o_ref[bi, :, hv * DK:(hv + 1) * DK] = res.astype(BF16)


def _gdn(proj, small_t, conv_w, conv_b, dt_bias, a_log, norm_w, *, col_qkv, col_z, te=256, nbt=1):
    bsz, s, _ = proj.shape
    L = GDN_CHUNK
    nc = s // L
    te = min(te, s)
    ne = s // te
    cw = 2 * GDN_QK_HEADS * GDN_HEAD_DIM + BRANCH_W
    hb = L // HALO
    qb, zb = col_qkv // cw, col_z // BRANCH_W
    assert col_qkv % cw == 0 and col_z % BRANCH_W == 0 and nc % 2 == 0 and bsz % nbt == 0

    def cf(t):
        return jnp.minimum(t, nc - 1)

    def cbk(t):
        return jnp.maximum(nc - 1 - t, 0)

    def ep(t):
        return jnp.maximum(t - nc, 0)

    def qkv_specs(chunk):
        return [
            pl.BlockSpec((nbt, L, cw), lambda b, t: (b, chunk(t), qb)),
            pl.BlockSpec((nbt, HALO, cw), lambda b, t: (b, jnp.maximum(chunk(t) * hb - 1, 0), qb)),
            pl.BlockSpec((nbt, HALO, cw), lambda b, t: (b, jnp.minimum((chunk(t) + 1) * hb, s // HALO - 1), qb)),
        ]

    def small_spec(chunk):
        return pl.BlockSpec((nbt, LANES, L), lambda b, t: (b, 0, chunk(t)))

    full = lambda shape: pl.BlockSpec(shape, lambda b, t: (0,) * len(shape))
    hv = GDN_V_HEADS
    return pl.pallas_call(
        functools.partial(_gdn_kernel, nc=nc, te=te),
        out_shape=jax.ShapeDtypeStruct((bsz, s, BRANCH_W), BF16),
        grid=(bsz // nbt, nc + ne),
        in_specs=qkv_specs(cf) + qkv_specs(cbk)
        + [pl.BlockSpec((nbt, te, BRANCH_W), lambda b, t: (b, ep(t), zb))]
        + [small_spec(cf), small_spec(cbk)]
        + [full((GDN_CONV, cw)), full((1, cw)), full((2, 1, LANES)), full((2, LANES, 1)),
           full((2, 1, LANES)), full((2, LANES, 1)), full((1, GDN_HEAD_DIM))],
        out_specs=pl.BlockSpec((nbt, te, BRANCH_W), lambda b, t: (b, ep(t), 0)),
        scratch_shapes=[
            pltpu.VMEM((nbt * 2, L + 2 * HALO, cw), F32),
            pltpu.VMEM((nbt * 2, GDN_HEAD_DIM, BRANCH_W), F32),
            pltpu.VMEM((nbt, s, BRANCH_W), F32),
        ],
        compiler_params=_cparams(("arbitrary", "arbitrary")),
        name="bidir_gated_deltanet",
    )(proj, proj, proj, proj, proj, proj, proj, small_t, small_t, conv_w, conv_b.reshape(1, cw),
      _pad_dir_rows(dt_bias, hv), _pad_dir_cols(dt_bias, hv), _pad_dir_rows(a_log, hv), _pad_dir_cols(a_log, hv),
      norm_w.reshape(1, GDN_HEAD_DIM))


def _local_scan(a, u, ascending):
    rows, width = a.shape
    a = a.reshape(rows // SUBLANES, SUBLANES, width)
    u = u.reshape(rows // SUBLANES, SUBLANES, width)
    sub = lax.broadcasted_iota(jnp.int32, a.shape, 1)
    d = 1
    while d < SUBLANES:
        if ascending:
            keep = sub >= d
            shift = d
        else:
            keep = sub < SUBLANES - d
            shift = SUBLANES - d
        a_sh = jnp.where(keep, pltpu.roll(a, shift, 1), 1.0)
        u_sh = jnp.where(keep, pltpu.roll(u, shift, 1), 0.0)
        u = u + a * u_sh
        a = a * a_sh
        d *= 2
    return a.reshape(rows, width), u.reshape(rows, width)


def _lru_kernel(xl_ref, prev_ref, next_ref, gl_ref, cw_ref, cb_ref, wa_ref, ba_ref, wx_ref, bx_ref, lam_ref,
                o_ref, ext_scr, a_scr, u_scr, hf_scr, carry_scr, *, nb, unroll):
    rows, width = a_scr.shape
    bw = LRU_WIDTH // LRU_BLOCKS
    step = pl.program_id(2)
    fwd = step < nb
    blk = _chunk_of_step(step, nb)

    @pl.when((step == 0) | (step == nb))
    def _():
        carry_scr[...] = jnp.zeros_like(carry_scr)

    _fill_conv_window(ext_scr, xl_ref[0], prev_ref[0], next_ref[0], blk, nb, rows)
    xc = _centred_conv4(ext_scr, cw_ref, cb_ref, rows)
    r_parts, i_parts = [], []
    for n in range(width // bw):
        xb = xc[:, n * bw:(n + 1) * bw].astype(BF16)
        r_parts.append(_dot(xb, wa_ref[0, n]))
        i_parts.append(_dot(xb, wx_ref[0, n]))
    tr = jnp.tanh(jnp.concatenate(r_parts, axis=1) + ba_ref[0])
    ti = jnp.tanh(jnp.concatenate(i_parts, axis=1) + bx_ref[0])
    log_a = ((-0.5 * LRU_C) * _softplus(-lam_ref[0])) * (tr + 1.0)
    a = jnp.exp(log_a)
    quarter = (-0.25 * jnp.tanh(log_a)) * (a * a + 1.0)
    u = (quarter * lax.rsqrt(jnp.maximum(quarter, TINY))) * ((ti + 1.0) * xc)

    n_groups = rows // SUBLANES
    base_out = pl.multiple_of(blk * rows, rows)

    def scan(ascending):
        a_loc, u_loc = _local_scan(a, u, ascending)
        a_scr[...] = a_loc
        u_scr[...] = u_loc
        last = SUBLANES - 1 if ascending else 0

        def body(it, carry):
            pos = it if ascending else n_groups - 1 - it
            r0 = pl.multiple_of(pos * SUBLANES, SUBLANES)
            h = u_scr[pl.ds(r0, SUBLANES), :] + a_scr[pl.ds(r0, SUBLANES), :] * carry
            u_scr[pl.ds(r0, SUBLANES), :] = h
            return jnp.broadcast_to(h[last:last + 1, :], h.shape)

        carry_scr[...] = lax.fori_loop(0, n_groups, body, carry_scr[...], unroll=unroll)

    @pl.when(fwd)
    def _():
        scan(True)
        hf_scr[pl.ds(base_out, rows), :] = u_scr[...]

    @pl.when(jnp.logical_not(fwd))
    def _():
        scan(False)
        o_ref[0] = ((hf_scr[pl.ds(base_out, rows), :] + u_scr[...]) * _silu(gl_ref[0])).astype(BF16)


def _lru(proj, conv_w, conv_b, wa, ba, wx, bx, lam, *, col_xl, col_gl, tt=512, width=512, unroll=8):
    bsz, s, _ = proj.shape
    tt = min(tt, s)
    nb = s // tt
    w_total = LRU_WIDTH
    bw = w_total // LRU_BLOCKS
    nbw = width // bw
    xcol, gcol = col_xl // width, col_gl // width
    assert col_xl % width == 0 and col_gl % width == 0
    hb = tt // HALO

    def blk(t):
        return _chunk_of_step(t, nb)

    def direction(t):
        return jnp.where(t < nb, 0, 1)

    return pl.pallas_call(
        functools.partial(_lru_kernel, nb=nb, unroll=unroll),
        out_shape=jax.ShapeDtypeStruct((bsz, s, w_total), BF16),
        grid=(bsz, w_total // width, 2 * nb),
        in_specs=[
            pl.BlockSpec((1, tt, width), lambda b, j, t: (b, blk(t), xcol + j)),
            pl.BlockSpec((1, HALO, width), lambda b, j, t: (b, jnp.maximum(blk(t) * hb - 1, 0), xcol + j)),
            pl.BlockSpec((1, HALO, width),
                         lambda b, j, t: (b, jnp.minimum((blk(t) + 1) * hb, s // HALO - 1), xcol + j)),
            pl.BlockSpec((1, tt, width), lambda b, j, t: (b, blk(t), gcol + j)),
            pl.BlockSpec((LRU_CONV, width), lambda b, j, t: (0, j)),
            pl.BlockSpec((1, width), lambda b, j, t: (0, j)),
            pl.BlockSpec((1, nbw, bw, bw), lambda b, j, t: (direction(t), j, 0, 0)),
            pl.BlockSpec((1, 1, width), lambda b, j, t: (direction(t), 0, j)),
            pl.BlockSpec((1, nbw, bw, bw), lambda b, j, t: (direction(t), j, 0, 0)),
            pl.BlockSpec((1, 1, width), lambda b, j, t: (direction(t), 0, j)),
            pl.BlockSpec((1, 1, width), lambda b, j, t: (direction(t), 0, j)),
        ],
        out_specs=pl.BlockSpec((1, tt, width),
                               lambda b, j, t: (b, jnp.where(t < nb, nb - 1, 2 * nb - 1 - t), j)),
        scratch_shapes=[
            pltpu.VMEM((tt + 2 * HALO, width), F32),
            pltpu.VMEM((tt, width), F32),
            pltpu.VMEM((tt, width), F32),
            pltpu.VMEM((s, width), F32),
            pltpu.VMEM((SUBLANES, width), F32),
        ],
        compiler_params=_cparams(("arbitrary", "arbitrary", "arbitrary")),
        name="bidir_rglru",
    )(proj, proj, proj, proj, conv_w, conv_b.reshape(1, w_total),
      (0.5 * wa).astype(BF16), (0.5 * ba).reshape(2, 1, w_total),
      (0.5 * wx).astype(BF16), (0.5 * bx).reshape(2, 1, w_total), lam.reshape(2, 1, w_total))


CD_COL_QKV, CD_COL_Z, CD_COL_XL, CD_COL_GL = 0, 2048, 3072, 4096


def _cd_weights(w_in):
    wt = w_in.T
    nqk = GDN_QK_HEADS * GDN_HEAD_DIM
    hv = GDN_V_HEADS
    o = 0
    q = wt[o:o + nqk]; o += nqk
    k = wt[o:o + nqk]; o += nqk
    v = wt[o:o + BRANCH_W]; o += BRANCH_W
    bf = wt[o:o + hv]; o += hv
    bb = wt[o:o + hv]; o += hv
    af = wt[o:o + hv]; o += hv
    ab = wt[o:o + hv]; o += hv
    z = wt[o:o + BRANCH_W]; o += BRANCH_W
    xl = wt[o:o + LRU_WIDTH]; o += LRU_WIDTH
    gl = wt[o:o + LRU_WIDTH]
    head = jnp.concatenate([q, k, v], axis=0).astype(BF16)
    tail = jnp.concatenate([z, xl, gl], axis=0).astype(BF16)
    main = jnp.zeros((head.shape[0] + tail.shape[0], wt.shape[1]), BF16)
    main = lax.dynamic_update_slice(main, head, (0, 0))
    main = lax.dynamic_update_slice(main, tail, (head.shape[0], 0))
    zpad = jnp.zeros((LANES - 4 * hv, wt.shape[1]), wt.dtype)
    small = jnp.concatenate([bf, af, bb, ab, zpad], axis=0).astype(BF16)
    return main, small


def _layer1(x, mod_l, norm_w, w_in, conv_w, conv_b, a_log_f, a_log_b, dt_bias_f, dt_bias_b, gdn_norm,
            lru_conv_w, lru_conv_b, wa_f, ba_f, wx_f, bx_f, lam_f, wa_b, ba_b, wx_b, bx_b, lam_b, w_out,
            final_w=None):
    w_main, w_small = _cd_weights(w_in)
    proj, small_t = _inproj(x, mod_l, norm_w, w_main, w_small)
    gdn = _gdn(proj, small_t, conv_w, conv_b, jnp.stack([dt_bias_f, dt_bias_b]),
               jnp.stack([a_log_f, a_log_b]), gdn_norm, col_qkv=CD_COL_QKV, col_z=CD_COL_Z)
    lru = _lru(proj, lru_conv_w, lru_conv_b, jnp.stack([wa_f, wa_b]), jnp.stack([ba_f, ba_b]),
               jnp.stack([wx_f, wx_b]), jnp.stack([bx_f, bx_b]), jnp.stack([lam_f, lam_b]),
               col_xl=CD_COL_XL, col_gl=CD_COL_GL)
    out = _outproj(gdn, lru, w_out, x, mod_l, final_w)
    return out, (proj, small_t, gdn, lru)


def kernel(x, c, w_mod, b_mod, norm_w, ab_w_in, ab_q_norm, ab_k_norm, ab_conv_w, ab_conv_b, ab_dt_bias_f, ab_dt_bias_b, ab_a_log_f, ab_a_log_b, ab_d_skip, ab_ssd_norm, ab_w_out, cd_w_in, cd_conv_w, cd_conv_b, cd_a_log_f, cd_a_log_b, cd_dt_bias_f, cd_dt_bias_b, cd_gdn_norm, cd_lru_conv_w, cd_lru_conv_b, cd_lru_wa_f, cd_lru_ba_f, cd_lru_wx_f, cd_lru_bx_f, cd_lru_lam_f, cd_lru_wa_b, cd_lru_ba_b, cd_lru_wx_b, cd_lru_bx_b, cd_lru_lam_b, cd_w_out, final_norm_w):
    mods = _modulation(c, w_mod, b_mod)
    x1, _ = _layer0(x, mods[0], norm_w[0], ab_w_in[0], ab_q_norm[0], ab_k_norm[0], ab_conv_w[0], ab_conv_b[0],
                    ab_dt_bias_f[0], ab_dt_bias_b[0], ab_a_log_f[0], ab_a_log_b[0], ab_d_skip[0],
                    ab_ssd_norm[0], ab_w_out[0])
    out, _ = _layer1(x1, mods[1], norm_w[1], cd_w_in[0], cd_conv_w[0], cd_conv_b[0], cd_a_log_f[0], cd_a_log_b[0],
                     cd_dt_bias_f[0], cd_dt_bias_b[0], cd_gdn_norm[0], cd_lru_conv_w[0], cd_lru_conv_b[0],
                     cd_lru_wa_f[0], cd_lru_ba_f[0], cd_lru_wx_f[0], cd_lru_bx_f[0], cd_lru_lam_f[0],
                     cd_lru_wa_b[0], cd_lru_ba_b[0], cd_lru_wx_b[0], cd_lru_bx_b[0], cd_lru_lam_b[0],
                     cd_w_out[0], final_norm_w)
    return out
```

```python
import functools
import math

import jax
import jax.numpy as jnp
import numpy as np
from jax import lax
from jax.experimental import pallas as pl
from jax.experimental.pallas import tpu as pltpu

F32 = jnp.float32
BF16 = jnp.bfloat16

D_MODEL = 2048
GRID_W = 64
EPS = 1e-6
BRANCH_W = D_MODEL // 2
ATT_HEAD_DIM = 128
ATT_HEADS = BRANCH_W // ATT_HEAD_DIM
ATT_KV_HEADS = ATT_HEADS // 4
ATT_GROUP = ATT_HEADS // ATT_KV_HEADS
ROPE_THETA = 10000.0
SSD_HEAD_DIM = 64
SSD_HEADS = BRANCH_W // SSD_HEAD_DIM
SSD_GROUPS = 2
SSD_STATE = 128
SSD_CONV = 4
SSD_CHUNK = 128
GDN_HEAD_DIM = 128
GDN_V_HEADS = BRANCH_W // GDN_HEAD_DIM
GDN_QK_HEADS = GDN_V_HEADS // 2
GDN_CONV = 4
GDN_CHUNK = 128
GDN_WAVE = 16
LRU_WIDTH = BRANCH_W
LRU_BLOCKS = 8
LRU_CONV = 4
LRU_C = 8.0

LANES = 128
SUBLANES = 8
VMEM_LIMIT_BYTES = 56 * 1024 * 1024

HALO = SUBLANES
NEG_BIG = -1e30
TINY = 1e-37
CAST_ROWS = 256


def _cparams(sem):
    return pltpu.CompilerParams(dimension_semantics=sem, vmem_limit_bytes=VMEM_LIMIT_BYTES)


def _sigmoid(x):
    return 0.5 * jnp.tanh(0.5 * x) + 0.5


def _silu(x):
    h = 0.5 * x
    return h * (jnp.tanh(h) + 1.0)


def _softplus(x):
    return jnp.maximum(x, 0.0) + jnp.log(1.0 + jnp.exp(-jnp.abs(x)))


def _split_bf16(a):
    hi = a.astype(BF16)
    lo = (a - hi.astype(F32)).astype(BF16)
    return hi, lo


def _dot(a, b):
    return jnp.dot(a, b, preferred_element_type=F32)


def _dot_nt(a, b):
    return lax.dot_general(a, b, (((1,), (1,)), ((), ())), preferred_element_type=F32)


def _dot_exact_rhs(a, b_bf16):
    hi, lo = _split_bf16(a)
    return _dot(hi, b_bf16) + _dot(lo, b_bf16)


def _dot_exact_lhs(a_bf16, b):
    hi, lo = _split_bf16(b)
    return _dot(a_bf16, hi) + _dot(a_bf16, lo)


def _mod_kernel(c_ref, w_ref, b_ref, o_ref):
    cond = _silu(c_ref[...])
    c_hi, c_lo = _split_bf16(cond)
    w = w_ref[0]
    w_hi, w_lo = _split_bf16(w)
    o_ref[0] = _dot(c_hi, w_hi) + _dot(c_lo, w_hi) + _dot(c_hi, w_lo) + b_ref[0]


def _modulation(c, w_mod, b_mod):
    depth, d, n = w_mod.shape
    bsz = c.shape[0]
    rows = -(-bsz // SUBLANES) * SUBLANES
    c_pad = jnp.zeros((rows, d), F32).at[:bsz].set(c)
    tn = 1536
    out = pl.pallas_call(
        _mod_kernel,
        out_shape=jax.ShapeDtypeStruct((depth, rows, n), F32),
        grid=(depth, n // tn),
        in_specs=[
            pl.BlockSpec((rows, d), lambda l, j: (0, 0)),
            pl.BlockSpec((1, d, tn), lambda l, j: (l, 0, j)),
            pl.BlockSpec((1, 1, tn), lambda l, j: (l, 0, j)),
        ],
        out_specs=pl.BlockSpec((1, rows, tn), lambda l, j: (l, 0, j)),
        compiler_params=_cparams(("arbitrary", "arbitrary")),
        name="adaln_mod",
    )(c_pad, w_mod, b_mod.reshape(depth, 1, n))
    return out[:, :bsz].reshape(depth, bsz, 3, d)


def _inproj_kernel(x_ref, mod_ref, nw_ref, w_ref, wst_ref, o_ref, ost_ref, h_even, h_odd, *, n_groups):
    g = pl.program_id(0)
    j = pl.program_id(1)
    rows = x_ref.shape[1]

    def prep(h_dst):
        x = x_ref[0]
        ms = jnp.mean(x * x, axis=-1, keepdims=True)
        y = x * lax.rsqrt(ms + EPS) * nw_ref[...]
        h = (y * (1.0 + mod_ref[0, 1:2, :]) + mod_ref[0, 0:1, :]).astype(BF16)
        h_dst[pl.ds(pl.multiple_of(j * rows, rows), rows), :] = h
        ost_ref[0] = _dot_nt(wst_ref[...], h)

    @pl.when(g == 0)
    def _():
        prep(h_even)

    @pl.when((g > 0) & (g % 2 == 0))
    def _():
        o_ref[0] = _dot_nt(h_odd[...], w_ref[...])
        prep(h_even)

    @pl.when(g % 2 == 1)
    def _():
        o_ref[0] = _dot_nt(h_even[...], w_ref[...])
        prep(h_odd)


def _inproj(x, mod_l, norm_w, w_main, w_small, *, tm=1024, nj=4):
    bsz, s, d = x.shape
    n = w_main.shape[0]
    ns = w_small.shape[0]
    tm = min(tm, s)
    ni = s // tm
    n_groups = bsz * ni
    tn = n // nj
    rows = tm // nj
    assert n % nj == 0 and tn % LANES == 0 and rows % LANES == 0

    def prep(g, j):
        gc = jnp.minimum(g, n_groups - 1)
        return gc // ni, (gc % ni) * nj + jnp.where(g < n_groups, j, nj - 1)

    def mm(g, j):
        gm = jnp.maximum(g - 1, 0)
        return gm // ni, gm % ni, jnp.where(g > 0, j, 0)

    return pl.pallas_call(
        functools.partial(_inproj_kernel, n_groups=n_groups),
        out_shape=[jax.ShapeDtypeStruct((bsz, s, n), F32), jax.ShapeDtypeStruct((bsz, ns, s), F32)],
        grid=(n_groups + 1, nj),
        in_specs=[
            pl.BlockSpec((1, rows, d), lambda g, j: (*prep(g, j), 0)),
            pl.BlockSpec((1, 3, d), lambda g, j: (prep(g, j)[0], 0, 0)),
            pl.BlockSpec((1, d), lambda g, j: (0, 0)),
            pl.BlockSpec((tn, d), lambda g, j: (mm(g, j)[2], 0)),
            pl.BlockSpec((ns, d), lambda g, j: (0, 0)),
        ],
        out_specs=[
            pl.BlockSpec((1, tm, tn), lambda g, j: mm(g, j)),
            pl.BlockSpec((1, ns, rows), lambda g, j: (prep(g, j)[0], 0, prep(g, j)[1])),
        ],
        scratch_shapes=[pltpu.VMEM((tm, d), BF16), pltpu.VMEM((tm, d), BF16)],
        compiler_params=_cparams(("arbitrary", "arbitrary")),
        name="norm_mod_inproj",
    )(x, mod_l, norm_w.reshape(1, d), w_main, w_small)


def _outproj_kernel(ya_ref, yb_ref, w_ref, x_ref, mod_ref, *rest, final_norm, half):
    if final_norm:
        fnw_ref, o_ref, w_scr = rest
    else:
        o_ref, w_scr = rest

    @pl.when((pl.program_id(0) == 0) & (pl.program_id(1) == 0))
    def _():
        rows = CAST_ROWS

        def body(r, carry):
            r0 = pl.multiple_of(r * rows, rows)
            w_scr[pl.ds(r0, rows), :] = w_ref[pl.ds(r0, rows), :].astype(BF16)
            return carry

        lax.fori_loop(0, w_ref.shape[0] // rows, body, 0)

    acc = _dot(ya_ref[0], w_scr[0:half, :]) + _dot(yb_ref[0], w_scr[half:, :])
    gate = mod_ref[0, 2:3, :]
    xn = x_ref[0] + gate * acc
    if final_norm:
        ms = jnp.mean(xn * xn, axis=-1, keepdims=True)
        xn = xn * lax.rsqrt(ms + EPS) * fnw_ref[...]
    o_ref[0] = xn


def _outproj(ya, yb, w_out, x, mod_l, final_w=None, *, tm=512):
    bsz, s, d = x.shape
    half = ya.shape[-1]
    tm = min(tm, s)
    final_norm = final_w is not None
    in_specs = [
        pl.BlockSpec((1, tm, half), lambda b, i: (b, i, 0)),
        pl.BlockSpec((1, tm, half), lambda b, i: (b, i, 0)),
        pl.BlockSpec((2 * half, d), lambda b, i: (0, 0), pipeline_mode=pl.Buffered(1)),
        pl.BlockSpec((1, tm, d), lambda b, i: (b, i, 0)),
        pl.BlockSpec((1, 3, d), lambda b, i: (b, 0, 0)),
    ]
    args = [ya, yb, w_out, x, mod_l]
    if final_norm:
        in_specs.append(pl.BlockSpec((1, d), lambda b, i: (0, 0)))
        args.append(final_w.reshape(1, d))
    return pl.pallas_call(
        functools.partial(_outproj_kernel, final_norm=final_norm, half=half),
        out_shape=jax.ShapeDtypeStruct((bsz, s, d), F32),
        grid=(bsz, s // tm),
        in_specs=in_specs,
        out_specs=pl.BlockSpec((1, tm, d), lambda b, i: (b, i, 0)),
        scratch_shapes=[pltpu.VMEM((2 * half, d), BF16)],
        compiler_params=_cparams(("arbitrary", "arbitrary")),
        name="outproj_residual",
    )(*args)


def _rms_rope(x, nw, cos, sin_signed):
    ms = jnp.mean(x * x, axis=-1, keepdims=True)
    y = x * lax.rsqrt(ms + EPS) * nw
    lane = lax.broadcasted_iota(jnp.int32, y.shape, 1)
    partner = jnp.where(lane % 2 == 0, pltpu.roll(y, ATT_HEAD_DIM - 1, 1), pltpu.roll(y, 1, 1))
    return y * cos + partner * sin_signed


def _attn_kernel(q_ref, k_ref, v_ref, ga_ref, cq_ref, sq_ref, ck_ref, sk_ref, qn_ref, kn_ref,
                 o_ref, k_scr, v_scr, *, tq, sub):
    qi = pl.program_id(2)

    dh = ATT_HEAD_DIM

    @pl.when(qi == 0)
    def _():
        k_scr[...] = _rms_rope(k_ref[0], kn_ref[...], ck_ref[...], sk_ref[...]).astype(BF16)
        v_scr[:, 0:dh] = v_ref[0].astype(BF16)
        v_scr[:, dh:2 * dh] = jnp.ones((v_scr.shape[0], dh), BF16)

    scale = dh ** -0.5
    cos = cq_ref[...]
    sin = sq_ref[...]

    probs = [(g, r) for g in range(ATT_GROUP) for r in range(0, tq, sub)]

    def scores(g, r):
        qg = q_ref[0, r:r + sub, g * dh:(g + 1) * dh]
        qg = (_rms_rope(qg, qn_ref[...], cos[r:r + sub], sin[r:r + sub]) * scale).astype(BF16)
        return _dot_nt(qg, k_scr[...])

    s_next = scores(*probs[0])
    for i, (g, r) in enumerate(probs):
        s = s_next
        if i + 1 < len(probs):
            s_next = scores(*probs[i + 1])
        s = s.astype(BF16)
        p = jnp.exp(s - jnp.max(s, axis=-1, keepdims=True))
        o_ext = _dot(p, v_scr[...])
        og = o_ext[:, 0:dh] / o_ext[:, dh:2 * dh] * _silu(ga_ref[0, r:r + sub, g * dh:(g + 1) * dh])
        o_ref[0, r:r + sub, g * dh:(g + 1) * dh] = og.astype(BF16)


def _attention(proj, cos_t, sin_t, q_norm, k_norm, *, col_q, col_k, col_v, col_ga, tq=512, sub=128):
    bsz, s, _ = proj.shape
    tq = min(tq, s)
    gw = ATT_GROUP * ATT_HEAD_DIM
    dh = ATT_HEAD_DIM
    qb, kb, vb, gb = col_q // gw, col_k // dh, col_v // dh, col_ga // gw
    return pl.pallas_call(
        functools.partial(_attn_kernel, tq=tq, sub=min(sub, tq)),
        out_shape=jax.ShapeDtypeStruct((bsz, s, BRANCH_W), BF16),
        grid=(bsz, ATT_KV_HEADS, s // tq),
        in_specs=[
            pl.BlockSpec((1, tq, gw), lambda b, h, i: (b, i, qb + h)),
            pl.BlockSpec((1, s, dh), lambda b, h, i: (b, 0, kb + h)),
            pl.BlockSpec((1, s, dh), lambda b, h, i: (b, 0, vb + h)),
            pl.BlockSpec((1, tq, gw), lambda b, h, i: (b, i, gb + h)),
            pl.BlockSpec((tq, dh), lambda b, h, i: (i, 0)),
            pl.BlockSpec((tq, dh), lambda b, h, i: (i, 0)),
            pl.BlockSpec((s, dh), lambda b, h, i: (0, 0)),
            pl.BlockSpec((s, dh), lambda b, h, i: (0, 0)),
            pl.BlockSpec((1, dh), lambda b, h, i: (0, 0)),
            pl.BlockSpec((1, dh), lambda b, h, i: (0, 0)),
        ],
        out_specs=pl.BlockSpec((1, tq, gw), lambda b, h, i: (b, i, h)),
        scratch_shapes=[pltpu.VMEM((s, dh), BF16), pltpu.VMEM((s, 2 * dh), BF16)],
        compiler_params=_cparams(("arbitrary", "arbitrary", "arbitrary")),
        name="gqa_attention",
    )(proj, proj, proj, proj, cos_t, sin_t, cos_t, sin_t, q_norm, k_norm)


def _rope_tables(s):
    t = np.arange(s)
    row = (t // GRID_W).astype(np.float64)
    col = (t % GRID_W).astype(np.float64)
    n_pairs = ATT_HEAD_DIM // 4
    freqs = ROPE_THETA ** (-np.arange(n_pairs, dtype=np.float64) / n_pairs)
    ang = np.concatenate([row[:, None] * freqs, col[:, None] * freqs], axis=-1)
    cos, sin = np.cos(ang), np.sin(ang)
    cos_t = np.repeat(cos, 2, axis=-1)
    sin_t = np.stack([-sin, sin], axis=-1).reshape(s, ATT_HEAD_DIM)
    return jnp.asarray(cos_t, F32), jnp.asarray(sin_t, F32)


def _chunk_of_step(step, nc):
    return jnp.where(step < nc, step, 2 * nc - 1 - step)


def _fill_conv_window(ext_scr, cur, prev, nxt, c, nc, rows):
    ext_scr[0:HALO, :] = jnp.where(c > 0, prev, 0.0)
    ext_scr[HALO:HALO + rows, :] = cur
    ext_scr[HALO + rows:HALO + rows + HALO, :] = jnp.where(c < nc - 1, nxt, 0.0)


CONV_SUB = 128


def _centred_conv4(ext_scr, cw_ref, cb_ref, rows):
    sub = min(rows, CONV_SUB)
    n_in = sub + 2 * HALO
    taps = (0, 1, 3)
    r = lax.broadcasted_iota(jnp.int32, (len(taps) * sub, n_in), 0)
    c = lax.broadcasted_iota(jnp.int32, (len(taps) * sub, n_in), 1)
    src = jnp.zeros_like(r)
    for i, k in enumerate(taps):
        src = jnp.where((r >= i * sub) & (r < (i + 1) * sub), r - i * sub + HALO + (k - 2), src)
    shift_mat = jnp.where(c == src, 1.0, 0.0).astype(BF16)
    outs = []
    for s0 in range(0, rows, sub):
        win = ext_scr[s0:s0 + n_in, :]
        shifted = _dot(shift_mat, win.astype(BF16))
        acc = cb_ref[...] + win[HALO:HALO + sub, :] * cw_ref[2:3, :]
        for i, k in enumerate(taps):
            acc = acc + shifted[i * sub:(i + 1) * sub, :] * cw_ref[k:k + 1, :]
        outs.append(acc)
    return outs[0] if len(outs) == 1 else jnp.concatenate(outs, axis=0)


def _scan_masks(fwd, n):
    row = lax.broadcasted_iota(jnp.int32, (n, n), 0)
    col = lax.broadcasted_iota(jnp.int32, (n, n), 1)
    sgn = jnp.where(fwd, 1, -1)
    d = (row - col) * sgn
    return d >= 0, d <= 0, d > 0


def _pad_dir_rows(v, n):
    return jnp.zeros((2, 1, LANES), F32).at[:, 0, :n].set(v)


def _pad_dir_cols(v, n):
    return jnp.zeros((2, LANES, 1), F32).at[:, :n, 0].set(v)


def _head_expander(heads, width):
    r = lax.broadcasted_iota(jnp.int32, (heads, heads * width), 0)
    c = lax.broadcasted_iota(jnp.int32, (heads, heads * width), 1)
    return jnp.where((c >= r * width) & (c < (r + 1) * width), 1.0, 0.0).astype(BF16)


def _ssd_kernel(xf_ref, pf_ref, nf_ref, xb_ref, pb_ref, nb_ref, z_ref, dtf_ref, dtb_ref, cw_ref, cb_ref,
                bias_r_ref, bias_c_ref, alog_r_ref, alog_c_ref, dskip_ref, nw_ref,
                o_ref, ext_scr, state_scr, acc_scr, *, nc, te):
    L, H, P, N = SSD_CHUNK, SSD_HEADS, SSD_HEAD_DIM, SSD_STATE
    HG = H // SSD_GROUPS
    GW = HG * P
    t = pl.program_id(1)

    @pl.when(t == 0)
    def _():
        state_scr[...] = jnp.zeros_like(state_scr)

    @pl.when(t < nc)
    def _():
        chunks = (t, nc - 1 - t)
        blocks = ((xf_ref, pf_ref, nf_ref, dtf_ref), (xb_ref, pb_ref, nb_ref, dtb_ref))
        expand = _head_expander(H, P)
        lane = lax.broadcasted_iota(jnp.int32, (L, 2 * P), 1)
        per_dir = []
        for d in range(2):
            x_ref, p_ref, n_ref, dtt_ref = blocks[d]
            ext = ext_scr.at[d]
            _fill_conv_window(ext, x_ref[0], p_ref[0], n_ref[0], chunks[d], nc, L)
            xbc = _silu(_centred_conv4(ext, cw_ref, cb_ref, L))
            xs = xbc[:, :BRANCH_W]
            bs = xbc[:, BRANCH_W:BRANCH_W + SSD_GROUPS * N]
            cs = xbc[:, BRANCH_W + SSD_GROUPS * N:]
            raw_t = dtt_ref[0]
            raw = raw_t.T
            dt = _softplus(raw[:, d * H:(d + 1) * H] + bias_r_ref[d][:, 0:H])
            a = dt * (-jnp.exp(alog_r_ref[d][:, 0:H]))
            dt_t = _softplus(raw_t[d * H:(d + 1) * H, :] + bias_c_ref[d][0:H, :])
            a_t = dt_t * (-jnp.exp(alog_c_ref[d][0:H, :]))
            mask, mask_t, _ = _scan_masks(d == 0, L)
            cum = _dot_exact_lhs(jnp.where(mask, 1.0, 0.0).astype(BF16), a)
            cum_t = _dot_exact_rhs(a_t, jnp.where(mask_t, 1.0, 0.0).astype(BF16))
            total = jnp.sum(a, axis=0, keepdims=True)
            dt_e = _dot(dt.astype(BF16), expand)
            p_e = _dot(jnp.exp(cum).astype(BF16), expand)
            q_e = _dot(jnp.exp(total - cum).astype(BF16), expand)
            tot_e = _dot_exact_rhs(jnp.broadcast_to(jnp.exp(total), (SUBLANES, H)), expand)[0:1, :]
            xd = xs * dt_e
            per_dir.append(dict(xs=xs, bs=bs, cs=cs, mask=mask, cum=cum, cum_t=cum_t, p_e=p_e, tot_e=tot_e,
                                xdq=(xd * q_e).astype(BF16), xd_b=xd.astype(BF16)))

        combos = [(d, g) for d in range(2) for g in range(SSD_GROUPS)]
        cg = {k: per_dir[k[0]]["cs"][:, k[1] * N:(k[1] + 1) * N].astype(BF16) for k in combos}
        bg = {k: per_dir[k[0]]["bs"][:, k[1] * N:(k[1] + 1) * N] for k in combos}
        gmat = {k: _dot_nt(cg[k], bg[k].astype(BF16)) for k in combos}
        h_prev = {k: state_scr[k[0], :, k[1] * GW:(k[1] + 1) * GW] for k in combos}
        y_off = {k: _dot(cg[k], h_prev[k].astype(BF16)) * per_dir[k[0]]["p_e"][:, k[1] * GW:(k[1] + 1) * GW]
                 for k in combos}
        pairs = {k: [] for k in combos}
        for hp in range(HG // 2):
            for k in combos:
                d, g = k
                pd = per_dir[d]
                h0 = g * HG + 2 * hp
                xpair = pd["xd_b"][:, h0 * P:(h0 + 2) * P]
                ys = []
                for h in (h0, h0 + 1):
                    dec = jnp.exp(jnp.where(pd["mask"], pd["cum"][:, h:h + 1] - pd["cum_t"][h:h + 1, :], NEG_BIG))
                    ys.append(_dot((gmat[k] * dec).astype(BF16), xpair))
                pairs[k].append(jnp.where(lane < P, ys[0], ys[1]))
        for k in combos:
            d, g = k
            state_scr[d, :, g * GW:(g + 1) * GW] = (
                h_prev[k] * per_dir[d]["tot_e"][:, g * GW:(g + 1) * GW]
                + _dot(bg[k].T.astype(BF16), per_dir[d]["xdq"][:, g * GW:(g + 1) * GW]))
        y_dirs = [jnp.concatenate([jnp.concatenate(pairs[(d, g)], axis=1) + y_off[(d, g)]
                                   for g in range(SSD_GROUPS)], axis=1) for d in range(2)]
        y_dirs[0] = y_dirs[0] + dskip_ref[...] * per_dir[0]["xs"]
        rows = [pl.multiple_of(c * L, L) for c in chunks]

        @pl.when(t < nc // 2)
        def _():
            for d in range(2):
                acc_scr[pl.ds(rows[d], L), :] = y_dirs[d]

        @pl.when(t >= nc // 2)
        def _():
            for d in range(2):
                acc_scr[pl.ds(rows[d], L), :] = acc_scr[pl.ds(rows[d], L), :] + y_dirs[d]

    @pl.when(t >= nc)
    def _():
        r0 = pl.multiple_of((t - nc) * te, te)
        y = acc_scr[pl.ds(r0, te), :] * _silu(z_ref[0])
        ms = jnp.mean(y * y, axis=-1, keepdims=True)
        o_ref[0] = (y * lax.rsqrt(ms + EPS) * nw_ref[...]).astype(BF16)


def _ssd(proj, small_t, conv_w, conv_b, dt_bias, a_log, d_skip, norm_w, *, col_xbc, col_z, te=256):
    bsz, s, _ = proj.shape
    L = SSD_CHUNK
    nc = s // L
    te = min(te, s)
    ne = s // te
    cw = BRANCH_W + 2 * SSD_GROUPS * SSD_STATE
    hb = L // HALO
    xb, zb = col_xbc // cw, col_z // BRANCH_W
    assert col_xbc % cw == 0 and col_z % BRANCH_W == 0 and nc % 2 == 0

    def cf(t):
        return jnp.minimum(t, nc - 1)

    def cbk(t):
        return jnp.maximum(nc - 1 - t, 0)

    def ep(t):
        return jnp.maximum(t - nc, 0)

    def xbc_specs(chunk):
        return [
            pl.BlockSpec((1, L, cw), lambda b, t: (b, chunk(t), xb)),
            pl.BlockSpec((1, HALO, cw), lambda b, t: (b, jnp.maximum(chunk(t) * hb - 1, 0), xb)),
            pl.BlockSpec((1, HALO, cw), lambda b, t: (b, jnp.minimum((chunk(t) + 1) * hb, s // HALO - 1), xb)),
        ]

    def small_spec(chunk):
        return pl.BlockSpec((1, LANES, L), lambda b, t: (b, 0, chunk(t)))

    full = lambda shape: pl.BlockSpec(shape, lambda b, t: (0,) * len(shape))
    return pl.pallas_call(
        functools.partial(_ssd_kernel, nc=nc, te=te),
        out_shape=jax.ShapeDtypeStruct((bsz, s, BRANCH_W), BF16),
        grid=(bsz, nc + ne),
        in_specs=xbc_specs(cf) + xbc_specs(cbk)
        + [pl.BlockSpec((1, te, BRANCH_W), lambda b, t: (b, ep(t), zb))]
        + [small_spec(cf), small_spec(cbk)]
        + [full((SSD_CONV, cw)), full((1, cw)), full((2, 1, LANES)), full((2, LANES, 1)),
           full((2, 1, LANES)), full((2, LANES, 1)), full((1, BRANCH_W)), full((1, BRANCH_W))],
        out_specs=pl.BlockSpec((1, te, BRANCH_W), lambda b, t: (b, ep(t), 0)),
        scratch_shapes=[
            pltpu.VMEM((2, L + 2 * HALO, cw), F32),
            pltpu.VMEM((2, SSD_STATE, BRANCH_W), F32),
            pltpu.VMEM((s, BRANCH_W), F32),
        ],
        compiler_params=_cparams(("arbitrary", "arbitrary")),
        name="bidir_ssd",
    )(proj, proj, proj, proj, proj, proj, proj, small_t, small_t, conv_w, conv_b.reshape(1, cw),
      _pad_dir_rows(dt_bias, SSD_HEADS), _pad_dir_cols(dt_bias, SSD_HEADS),
      _pad_dir_rows(a_log, SSD_HEADS), _pad_dir_cols(a_log, SSD_HEADS),
      jnp.repeat(d_skip, SSD_HEAD_DIM).reshape(1, BRANCH_W), norm_w.reshape(1, BRANCH_W))


AB_COL_XBC, AB_COL_K, AB_COL_V, AB_COL_Z, AB_COL_Q, AB_COL_GA, AB_N = 0, 1536, 1792, 2048, 3072, 4096, 5120


REGROUP_ROWS = 256


def _regroup_kernel(offs_ref, w_ref, o_ref):
    del offs_ref
    o_ref[...] = w_ref[...].astype(BF16)


def _regroup_cast(wt, pieces):
    rows = REGROUP_ROWS
    offsets = []
    for start, n in pieces:
        assert n % rows == 0 and start % SUBLANES == 0
        offsets += list(range(start, start + n, rows))
    d = wt.shape[1]
    return pl.pallas_call(
        _regroup_kernel,
        out_shape=jax.ShapeDtypeStruct((len(offsets) * rows, d), BF16),
        grid_spec=pltpu.PrefetchScalarGridSpec(
            num_scalar_prefetch=1,
            grid=(len(offsets),),
            in_specs=[pl.BlockSpec((pl.Element(rows), pl.Element(d)), lambda i, offs: (pl.multiple_of(offs[i], SUBLANES), 0))],
            out_specs=pl.BlockSpec((rows, d), lambda i, offs: (i, 0)),
        ),
        compiler_params=_cparams(("arbitrary",)),
        name="weight_regroup_cast",
    )(jnp.asarray(offsets, jnp.int32), wt)


def _piece_table(sizes):
    table, o = {}, 0
    for name, n in sizes:
        table[name] = (o, n)
        o += n
    return table


def _ab_weights(w_in):
    wt = w_in.T
    hq, hk = ATT_HEADS * ATT_HEAD_DIM, ATT_KV_HEADS * ATT_HEAD_DIM
    gn = SSD_GROUPS * SSD_STATE
    p = _piece_table([("q", hq), ("k", hk), ("v", hk), ("ga", BRANCH_W), ("xs", BRANCH_W), ("bs", gn),
                      ("cs", gn), ("dtf", SSD_HEADS), ("dtb", SSD_HEADS), ("z", BRANCH_W)])
    main = _regroup_cast(wt, [p[n] for n in ("xs", "bs", "cs", "k", "v", "z", "q", "ga")])
    dt0, dtn = p["dtf"][0], 2 * SSD_HEADS
    zpad = jnp.zeros((LANES - dtn, wt.shape[1]), wt.dtype)
    small = jnp.concatenate([wt[dt0:dt0 + dtn], zpad], axis=0).astype(BF16)
    return main, small


def _layer0(x, mod_l, norm_w, w_in, q_norm, k_norm, conv_w, conv_b, dt_bias_f, dt_bias_b,
            a_log_f, a_log_b, d_skip, ssd_norm, w_out, final_w=None):
    s = x.shape[1]
    w_main, w_small = _ab_weights(w_in)
    proj, small_t = _inproj(x, mod_l, norm_w, w_main, w_small)
    cos_t, sin_t = _rope_tables(s)
    att = _attention(proj, cos_t, sin_t, q_norm.reshape(1, ATT_HEAD_DIM), k_norm.reshape(1, ATT_HEAD_DIM),
                     col_q=AB_COL_Q, col_k=AB_COL_K, col_v=AB_COL_V, col_ga=AB_COL_GA)
    ssd = _ssd(proj, small_t, conv_w, conv_b, jnp.stack([dt_bias_f, dt_bias_b]),
               jnp.stack([a_log_f, a_log_b]), d_skip, ssd_norm, col_xbc=AB_COL_XBC, col_z=AB_COL_Z)
    return _outproj(att, ssd, w_out, x, mod_l, final_w), (proj, small_t, att, ssd)


def _unit_tri_inverse(nmats, n):
    row = lax.broadcasted_iota(jnp.int32, (n, n), 0)
    col = lax.broadcasted_iota(jnp.int32, (n, n), 1)

    def same_block(size):
        return (row // size) == (col // size)

    def mm(a, b):
        return _dot(a, b).astype(BF16)

    def as_mask(cond):
        return jnp.where(cond, 1.0, 0.0).astype(BF16)

    nmats = [m.astype(BF16) for m in nmats]
    eye = as_mask(row == col)
    base = SUBLANES
    blk = same_block(base)
    blk_m = as_mask(blk)
    nd = [m * blk_m for m in nmats]
    p1 = [mm(x, x) for x in nd]
    p2 = [mm(x, x) for x in p1]
    t = [eye - x for x in nd]
    t = [x + mm(x, p) for x, p in zip(t, p1)]
    t = [x + mm(x, p) for x, p in zip(t, p2)]
    size = base
    while size < n:
        nxt = same_block(2 * size)
        off_m = as_mask(nxt & jnp.logical_not(blk))
        et = [mm(m * off_m, x) for m, x in zip(nmats, t)]
        t = [x - mm(x, y) for x, y in zip(t, et)]
        blk = nxt
        size *= 2
    return t


def _l2norm(x):
    return x * lax.rsqrt(jnp.sum(x * x, axis=-1, keepdims=True) + EPS)


def _gdn_kernel(qf_ref, pf_ref, nf_ref, qb_ref, pb_ref, nb_ref, z_ref, smtf_ref, smtb_ref,
                cw_ref, cb_ref, bias_r_ref, bias_c_ref, alog_r_ref, alog_c_ref, nw_ref,
                o_ref, ext_scr, state_scr, acc_scr, *, nc, te):
    L, HV, HQ, DK = GDN_CHUNK, GDN_V_HEADS, GDN_QK_HEADS, GDN_HEAD_DIM
    rep = HV // HQ
    nbt = acc_scr.shape[0]
    t = pl.program_id(1)

    @pl.when(t == 0)
    def _():
        state_scr[...] = jnp.zeros_like(state_scr)

    @pl.when(t < nc)
    def _():
        chunks = (t, nc - 1 - t)
        blocks = ((qf_ref, pf_ref, nf_ref, smtf_ref), (qb_ref, pb_ref, nb_ref, smtb_ref))
        kk, qk, qh, kh, vh, colv, dec, bcol, ecol, tot, nmat = ([] for _ in range(11))
        for bi, d in [(bi, d) for bi in range(nbt) for d in range(2)]:
            c = chunks[d]
            q_ref, p_ref, n_ref, smt_ref = blocks[d]
            ext = ext_scr.at[bi * 2 + d]
            _fill_conv_window(ext, q_ref[bi], p_ref[bi], n_ref[bi], c, nc, L)
            act = _silu(_centred_conv4(ext, cw_ref, cb_ref, L))
            q_n = [_l2norm(act[:, h * DK:(h + 1) * DK]) * (DK ** -0.5) for h in range(HQ)]
            k_n = [_l2norm(act[:, (HQ + h) * DK:(HQ + h + 1) * DK]) for h in range(HQ)]
            k_b = [x.astype(BF16) for x in k_n]
            kk_d = [_dot_nt(k_b[h], k_b[h]) for h in range(HQ)]
            qk_d = [_dot_nt(q_n[h].astype(BF16), k_b[h]) for h in range(HQ)]

            raw_t = smt_ref[bi]
            sm = raw_t.T[:, 2 * HV * d:2 * HV * (d + 1)]
            a_raw_t = raw_t[2 * HV * d + HV:2 * HV * (d + 1), :]
            beta = _sigmoid(sm[:, 0:HV])
            g = -jnp.exp(alog_r_ref[d][:, 0:HV]) * _softplus(sm[:, HV:2 * HV] + bias_r_ref[d][:, 0:HV])
            g_t = -jnp.exp(alog_c_ref[d][0:HV, :]) * _softplus(a_raw_t + bias_c_ref[d][0:HV, :])
            mask, mask_t, strict = _scan_masks(d == 0, L)
            cum = _dot_exact_lhs(jnp.where(mask, 1.0, 0.0).astype(BF16), g)
            cum_t = _dot_exact_rhs(g_t, jnp.where(mask_t, 1.0, 0.0).astype(BF16))
            total = jnp.sum(g, axis=0, keepdims=True)
            for h in range(HV):
                cv = cum[:, h:h + 1]
                dc = jnp.exp(jnp.where(mask, cv - cum_t[h:h + 1, :], NEG_BIG))
                bc = beta[:, h:h + 1]
                kk.append(kk_d[h // rep]); qk.append(qk_d[h // rep])
                qh.append(q_n[h // rep]); kh.append(k_n[h // rep])
                vh.append(act[:, (2 * HQ + h) * DK:(2 * HQ + h + 1) * DK])
                colv.append(cv); dec.append(dc); bcol.append(bc); ecol.append(jnp.exp(cv))
                tot.append(total[:, h:h + 1])
                nmat.append(jnp.where(strict, kk_d[h // rep] * bc * dc, 0.0))

        outs = []
        for w0 in range(0, nbt * 2 * HV, GDN_WAVE):
            idx = range(w0, w0 + GDN_WAVE)
            t_inv = dict(zip(idx, _unit_tri_inverse([nmat[i] for i in idx], L)))
            u = {i: _dot(t_inv[i], (vh[i] * bcol[i]).astype(BF16)) for i in idx}
            w = {i: _dot(t_inv[i], (kh[i] * (bcol[i] * ecol[i])).astype(BF16)) for i in idx}
            s_prev = {i: state_scr[i // HV, :, (i % HV) * DK:(i % HV + 1) * DK] for i in idx}
            s_b = {i: s_prev[i].astype(BF16) for i in idx}
            v_new = {i: (u[i] - _dot(w[i].astype(BF16), s_b[i])).astype(BF16) for i in idx}
            outs += [_dot((qh[i] * ecol[i]).astype(BF16), s_b[i])
                     + _dot((qk[i] * dec[i]).astype(BF16), v_new[i]) for i in idx]
            for i in idx:
                k_dec = kh[i] * jnp.exp(tot[i] - colv[i])
                state_scr[i // HV, :, (i % HV) * DK:(i % HV + 1) * DK] = (
                    s_prev[i] * jnp.exp(tot[i]) + _dot(k_dec.T.astype(BF16), v_new[i]))
        o_dirs = [jnp.concatenate(outs[j * HV:(j + 1) * HV], axis=1) for j in range(nbt * 2)]
        rows = [pl.multiple_of(c * L, L) for c in chunks]

        @pl.when(t < nc // 2)
        def _():
            for bi in range(nbt):
                for d in range(2):
                    acc_scr[bi, pl.ds(rows[d], L), :] = o_dirs[bi * 2 + d]

        @pl.when(t >= nc // 2)
        def _():
            for bi in range(nbt):
                for d in range(2):
                    acc_scr[bi, pl.ds(rows[d], L), :] = acc_scr[bi, pl.ds(rows[d], L), :] + o_dirs[bi * 2 + d]

    @pl.when(t >= nc)
    def _():
        r0 = pl.multiple_of((t - nc) * te, te)
        for bi in range(nbt):
            zz = z_ref[bi]
            for hv in range(HV):
                oh = acc_scr[bi, pl.ds(r0, te), hv * DK:(hv + 1) * DK]
                ms = jnp.mean(oh * oh, axis=-1, keepdims=True)
                res = oh * lax.rsqrt(ms + EPS) * nw_ref[...] * _silu(zz[:, hv * DK:(hv + 1) * DK])
                o_ref[bi, :, hv * DK:(hv + 1) * DK] = res.astype(BF16)


def _gdn(proj, small_t, conv_w, conv_b, dt_bias, a_log, norm_w, *, col_qkv, col_z, te=256, nbt=1):
    bsz, s, _ = proj.shape
    L = GDN_CHUNK
    nc = s // L
    te = min(te, s)
    ne = s // te
    cw = 2 * GDN_QK_HEADS * GDN_HEAD_DIM + BRANCH_W
    hb = L // HALO
    qb, zb = col_qkv // cw, col_z // BRANCH_W
    assert col_qkv % cw == 0 and col_z % BRANCH_W == 0 and nc % 2 == 0 and bsz % nbt == 0

    def cf(t):
        return jnp.minimum(t, nc - 1)

    def cbk(t):
        return jnp.maximum(nc - 1 - t, 0)

    def ep(t):
        return jnp.maximum(t - nc, 0)

    def qkv_specs(chunk):
        return [
            pl.BlockSpec((nbt, L, cw), lambda b, t: (b, chunk(t), qb)),
            pl.BlockSpec((nbt, HALO, cw), lambda b, t: (b, jnp.maximum(chunk(t) * hb - 1, 0), qb)),
            pl.BlockSpec((nbt, HALO, cw), lambda b, t: (b, jnp.minimum((chunk(t) + 1) * hb, s // HALO - 1), qb)),
        ]

    def small_spec(chunk):
        return pl.BlockSpec((nbt, LANES, L), lambda b, t: (b, 0, chunk(t)))

    full = lambda shape: pl.BlockSpec(shape, lambda b, t: (0,) * len(shape))
    hv = GDN_V_HEADS
    return pl.pallas_call(
        functools.partial(_gdn_kernel, nc=nc, te=te),
        out_shape=jax.ShapeDtypeStruct((bsz, s, BRANCH_W), BF16),
        grid=(bsz // nbt, nc + ne),
        in_specs=qkv_specs(cf) + qkv_specs(cbk)
        + [pl.BlockSpec((nbt, te, BRANCH_W), lambda b, t: (b, ep(t), zb))]
        + [small_spec(cf), small_spec(cbk)]
        + [full((GDN_CONV, cw)), full((1, cw)), full((2, 1, LANES)), full((2, LANES, 1)),
           full((2, 1, LANES)), full((2, LANES, 1)), full((1, GDN_HEAD_DIM))],
        out_specs=pl.BlockSpec((nbt, te, BRANCH_W), lambda b, t: (b, ep(t), 0)),
        scratch_shapes=[
            pltpu.VMEM((nbt * 2, L + 2 * HALO, cw), F32),
            pltpu.VMEM((nbt * 2, GDN_HEAD_DIM, BRANCH_W), F32),
            pltpu.VMEM((nbt, s, BRANCH_W), F32),
        ],
        compiler_params=_cparams(("arbitrary", "arbitrary")),
        name="bidir_gated_deltanet",
    )(proj, proj, proj, proj, proj, proj, proj, small_t, small_t, conv_w, conv_b.reshape(1, cw),
      _pad_dir_rows(dt_bias, hv), _pad_dir_cols(dt_bias, hv), _pad_dir_rows(a_log, hv), _pad_dir_cols(a_log, hv),
      norm_w.reshape(1, GDN_HEAD_DIM))


def _local_scan(a, u, ascending):
    rows, width = a.shape
    a = a.reshape(rows // SUBLANES, SUBLANES, width)
    u = u.reshape(rows // SUBLANES, SUBLANES, width)
    sub = lax.broadcasted_iota(jnp.int32, a.shape, 1)
    d = 1
    while d < SUBLANES:
        if ascending:
            keep = sub >= d
            shift = d
        else:
            keep = sub < SUBLANES - d
            shift = SUBLANES - d
        a_sh = jnp.where(keep, pltpu.roll(a, shift, 1), 1.0)
        u_sh = jnp.where(keep, pltpu.roll(u, shift, 1), 0.0)
        u = u + a * u_sh
        a = a * a_sh
        d *= 2
    return a.reshape(rows, width), u.reshape(rows, width)


def _lru_kernel(xl_ref, prev_ref, next_ref, gl_ref, cw_ref, cb_ref, wa_ref, ba_ref, wx_ref, bx_ref, lam_ref,
                o_ref, ext_scr, a_scr, u_scr, hf_scr, carry_scr, *, nb, unroll):
    rows, width = a_scr.shape
    bw = LRU_WIDTH // LRU_BLOCKS
    step = pl.program_id(2)
    fwd = step < nb
    blk = _chunk_of_step(step, nb)

    @pl.when((step == 0) | (step == nb))
    def _():
        carry_scr[...] = jnp.zeros_like(carry_scr)

    _fill_conv_window(ext_scr, xl_ref[0], prev_ref[0], next_ref[0], blk, nb, rows)
    xc = _centred_conv4(ext_scr, cw_ref, cb_ref, rows)
    r_parts, i_parts = [], []
    for n in range(width // bw):
        xb = xc[:, n * bw:(n + 1) * bw].astype(BF16)
        r_parts.append(_dot(xb, wa_ref[0, n]))
        i_parts.append(_dot(xb, wx_ref[0, n]))
    tr = jnp.tanh(jnp.concatenate(r_parts, axis=1) + ba_ref[0])
    ti = jnp.tanh(jnp.concatenate(i_parts, axis=1) + bx_ref[0])
    log_a = ((-0.5 * LRU_C) * _softplus(-lam_ref[0])) * (tr + 1.0)
    a = jnp.exp(log_a)
    quarter = (-0.25 * jnp.tanh(log_a)) * (a * a + 1.0)
    u = (quarter * lax.rsqrt(jnp.maximum(quarter, TINY))) * ((ti + 1.0) * xc)

    n_groups = rows // SUBLANES
    base_out = pl.multiple_of(blk * rows, rows)

    def scan(ascending):
        a_loc, u_loc = _local_scan(a, u, ascending)
        a_scr[...] = a_loc
        u_scr[...] = u_loc
        last = SUBLANES - 1 if ascending else 0

        def body(it, carry):
            pos = it if ascending else n_groups - 1 - it
            r0 = pl.multiple_of(pos * SUBLANES, SUBLANES)
            h = u_scr[pl.ds(r0, SUBLANES), :] + a_scr[pl.ds(r0, SUBLANES), :] * carry
            u_scr[pl.ds(r0, SUBLANES), :] = h
            return jnp.broadcast_to(h[last:last + 1, :], h.shape)

        carry_scr[...] = lax.fori_loop(0, n_groups, body, carry_scr[...], unroll=unroll)

    @pl.when(fwd)
    def _():
        scan(True)
        hf_scr[pl.ds(base_out, rows), :] = u_scr[...]

    @pl.when(jnp.logical_not(fwd))
    def _():
        scan(False)
        o_ref[0] = ((hf_scr[pl.ds(base_out, rows), :] + u_scr[...]) * _silu(gl_ref[0])).astype(BF16)


def _lru(proj, conv_w, conv_b, wa, ba, wx, bx, lam, *, col_xl, col_gl, tt=512, width=1024, unroll=8):
    bsz, s, _ = proj.shape
    tt = min(tt, s)
    nb = s // tt
    w_total = LRU_WIDTH
    bw = w_total // LRU_BLOCKS
    nbw = width // bw
    xcol, gcol = col_xl // width, col_gl // width
    assert col_xl % width == 0 and col_gl % width == 0
    hb = tt // HALO

    def blk(t):
        return _chunk_of_step(t, nb)

    def direction(t):
        return jnp.where(t < nb, 0, 1)

    return pl.pallas_call(
        functools.partial(_lru_kernel, nb=nb, unroll=unroll),
        out_shape=jax.ShapeDtypeStruct((bsz, s, w_total), BF16),
        grid=(bsz, w_total // width, 2 * nb),
        in_specs=[
            pl.BlockSpec((1, tt, width), lambda b, j, t: (b, blk(t), xcol + j)),
            pl.BlockSpec((1, HALO, width), lambda b, j, t: (b, jnp.maximum(blk(t) * hb - 1, 0), xcol + j)),
            pl.BlockSpec((1, HALO, width),
                         lambda b, j, t: (b, jnp.minimum((blk(t) + 1) * hb, s // HALO - 1), xcol + j)),
            pl.BlockSpec((1, tt, width), lambda b, j, t: (b, blk(t), gcol + j)),
            pl.BlockSpec((LRU_CONV, width), lambda b, j, t: (0, j)),
            pl.BlockSpec((1, width), lambda b, j, t: (0, j)),
            pl.BlockSpec((1, nbw, bw, bw), lambda b, j, t: (direction(t), j, 0, 0)),
            pl.BlockSpec((1, 1, width), lambda b, j, t: (direction(t), 0, j)),
            pl.BlockSpec((1, nbw, bw, bw), lambda b, j, t: (direction(t), j, 0, 0)),
            pl.BlockSpec((1, 1, width), lambda b, j, t: (direction(t), 0, j)),
            pl.BlockSpec((1, 1, width), lambda b, j, t: (direction(t), 0, j)),
        ],
        out_specs=pl.BlockSpec((1, tt, width),
                               lambda b, j, t: (b, jnp.where(t < nb, nb - 1, 2 * nb - 1 - t), j)),
        scratch_shapes=[
            pltpu.VMEM((tt + 2 * HALO, width), F32),
            pltpu.VMEM((tt, width), F32),
            pltpu.VMEM((tt, width), F32),
            pltpu.VMEM((s, width), F32),
            pltpu.VMEM((SUBLANES, width), F32),
        ],
        compiler_params=_cparams(("arbitrary", "arbitrary", "arbitrary")),
        name="bidir_rglru",
    )(proj, proj, proj, proj, conv_w, conv_b.reshape(1, w_total),
      (0.5 * wa).astype(BF16), (0.5 * ba).reshape(2, 1, w_total),
      (0.5 * wx).astype(BF16), (0.5 * bx).reshape(2, 1, w_total), lam.reshape(2, 1, w_total))


CD_COL_QKV, CD_COL_Z, CD_COL_XL, CD_COL_GL = 0, 2048, 3072, 4096


def _cd_weights(w_in):
    wt = w_in.T
    nqk = GDN_QK_HEADS * GDN_HEAD_DIM
    hv = GDN_V_HEADS
    p = _piece_table([("q", nqk), ("k", nqk), ("v", BRANCH_W), ("bf", hv), ("bb", hv), ("af", hv), ("ab", hv),
                      ("z", BRANCH_W), ("xl", LRU_WIDTH), ("gl", LRU_WIDTH)])
    main = _regroup_cast(wt, [p[n] for n in ("q", "k", "v", "z", "xl", "gl")])
    rows = [wt[p[n][0]:p[n][0] + hv] for n in ("bf", "af", "bb", "ab")]
    zpad = jnp.zeros((LANES - 4 * hv, wt.shape[1]), wt.dtype)
    small = jnp.concatenate(rows + [zpad], axis=0).astype(BF16)
    return main, small


def _layer1(x, mod_l, norm_w, w_in, conv_w, conv_b, a_log_f, a_log_b, dt_bias_f, dt_bias_b, gdn_norm,
            lru_conv_w, lru_conv_b, wa_f, ba_f, wx_f, bx_f, lam_f, wa_b, ba_b, wx_b, bx_b, lam_b, w_out,
            final_w=None):
    w_main, w_small = _cd_weights(w_in)
    proj, small_t = _inproj(x, mod_l, norm_w, w_main, w_small)
    gdn = _gdn(proj, small_t, conv_w, conv_b, jnp.stack([dt_bias_f, dt_bias_b]),
               jnp.stack([a_log_f, a_log_b]), gdn_norm, col_qkv=CD_COL_QKV, col_z=CD_COL_Z)
    lru = _lru(proj, lru_conv_w, lru_conv_b, jnp.stack([wa_f, wa_b]), jnp.stack([ba_f, ba_b]),
               jnp.stack([wx_f, wx_b]), jnp.stack([bx_f, bx_b]), jnp.stack([lam_f, lam_b]),
               col_xl=CD_COL_XL, col_gl=CD_COL_GL)
    out = _outproj(gdn, lru, w_out, x, mod_l, final_w)
    return out, (proj, small_t, gdn, lru)


def kernel(x, c, w_mod, b_mod, norm_w, ab_w_in, ab_q_norm, ab_k_norm, ab_conv_w, ab_conv_b, ab_dt_bias_f, ab_dt_bias_b, ab_a_log_f, ab_a_log_b, ab_d_skip, ab_ssd_norm, ab_w_out, cd_w_in, cd_conv_w, cd_conv_b, cd_a_log_f, cd_a_log_b, cd_dt_bias_f, cd_dt_bias_b, cd_gdn_norm, cd_lru_conv_w, cd_lru_conv_b, cd_lru_wa_f, cd_lru_ba_f, cd_lru_wx_f, cd_lru_bx_f, cd_lru_lam_f, cd_lru_wa_b, cd_lru_ba_b, cd_lru_wx_b, cd_lru_bx_b, cd_lru_lam_b, cd_w_out, final_norm_w):
    mods = _modulation(c, w_mod, b_mod)
    x1, _ = _layer0(x, mods[0], norm_w[0], ab_w_in[0], ab_q_norm[0], ab_k_norm[0], ab_conv_w[0], ab_conv_b[0],
                    ab_dt_bias_f[0], ab_dt_bias_b[0], ab_a_log_f[0], ab_a_log_b[0], ab_d_skip[0],
                    ab_ssd_norm[0], ab_w_out[0])
    out, _ = _layer1(x1, mods[1], norm_w[1], cd_w_in[0], cd_conv_w[0], cd_conv_b[0], cd_a_log_f[0], cd_a_log_b[0],
                     cd_dt_bias_f[0], cd_dt_bias_b[0], cd_gdn_norm[0], cd_lru_conv_w[0], cd_lru_conv_b[0],
                     cd_lru_wa_f[0], cd_lru_ba_f[0], cd_lru_wx_f[0], cd_lru_bx_f[0], cd_lru_lam_f[0],
                     cd_lru_wa_b[0], cd_lru_ba_b[0], cd_lru_wx_b[0], cd_lru_bx_b[0], cd_lru_lam_b[0],
                     cd_w_out[0], final_norm_w)
    return out
```

```python
import functools

import jax
import jax.numpy as jnp
import numpy as np
from jax import lax
from jax.experimental import pallas as pl
from jax.experimental.pallas import tpu as pltpu

F32 = jnp.float32
BF16 = jnp.bfloat16

D_MODEL = 2048
GRID_W = 64
EPS = 1e-6
BRANCH_W = D_MODEL // 2
ATT_HEAD_DIM = 128
ATT_HEADS = BRANCH_W // ATT_HEAD_DIM
ATT_KV_HEADS = ATT_HEADS // 4
ATT_GROUP = ATT_HEADS // ATT_KV_HEADS
ROPE_THETA = 10000.0
SSD_HEAD_DIM = 64
SSD_HEADS = BRANCH_W // SSD_HEAD_DIM
SSD_GROUPS = 2
SSD_STATE = 128
SSD_CONV = 4
SSD_CHUNK = 128
GDN_HEAD_DIM = 128
GDN_V_HEADS = BRANCH_W // GDN_HEAD_DIM
GDN_QK_HEADS = GDN_V_HEADS // 2
GDN_CONV = 4
GDN_CHUNK = 128
GDN_WAVE = 16
LRU_WIDTH = BRANCH_W
LRU_BLOCKS = 8
LRU_CONV = 4
LRU_C = 8.0

LANES = 128
SUBLANES = 8
VMEM_LIMIT_BYTES = 56 * 1024 * 1024

HALO = SUBLANES
NEG_BIG = -1e30
TINY = 1e-37
CAST_ROWS = 256


def _cparams(sem):
    return pltpu.CompilerParams(dimension_semantics=sem, vmem_limit_bytes=VMEM_LIMIT_BYTES)


def _sigmoid(x):
    return 0.5 * jnp.tanh(0.5 * x) + 0.5


def _silu(x):
    h = 0.5 * x
    return h * (jnp.tanh(h) + 1.0)


def _softplus(x):
    return jnp.maximum(x, 0.0) + jnp.log(1.0 + jnp.exp(-jnp.abs(x)))


def _split_bf16(a):
    hi = a.astype(BF16)
    lo = (a - hi.astype(F32)).astype(BF16)
    return hi, lo


def _dot(a, b):
    return jnp.dot(a, b, preferred_element_type=F32)


def _dot_nt(a, b):
    return lax.dot_general(a, b, (((1,), (1,)), ((), ())), preferred_element_type=F32)


def _dot_exact_rhs(a, b_bf16):
    hi, lo = _split_bf16(a)
    return _dot(hi, b_bf16) + _dot(lo, b_bf16)


def _dot_exact_lhs(a_bf16, b):
    hi, lo = _split_bf16(b)
    return _dot(a_bf16, hi) + _dot(a_bf16, lo)


def _mod_kernel(c_ref, w_ref, b_ref, o_ref):
    cond = _silu(c_ref[...])
    c_hi, c_lo = _split_bf16(cond)
    w = w_ref[0]
    w_hi, w_lo = _split_bf16(w)
    o_ref[0] = _dot(c_hi, w_hi) + _dot(c_lo, w_hi) + _dot(c_hi, w_lo) + b_ref[0]


def _modulation(c, w_mod, b_mod):
    depth, d, n = w_mod.shape
    bsz = c.shape[0]
    rows = -(-bsz // SUBLANES) * SUBLANES
    c_pad = jnp.zeros((rows, d), F32).at[:bsz].set(c)
    tn = 1536
    out = pl.pallas_call(
        _mod_kernel,
        out_shape=jax.ShapeDtypeStruct((depth, rows, n), F32),
        grid=(depth, n // tn),
        in_specs=[
            pl.BlockSpec((rows, d), lambda l, j: (0, 0)),
            pl.BlockSpec((1, d, tn), lambda l, j: (l, 0, j)),
            pl.BlockSpec((1, 1, tn), lambda l, j: (l, 0, j)),
        ],
        out_specs=pl.BlockSpec((1, rows, tn), lambda l, j: (l, 0, j)),
        compiler_params=_cparams(("arbitrary", "arbitrary")),
        name="adaln_mod",
    )(c_pad, w_mod, b_mod.reshape(depth, 1, n))
    return out[:, :bsz].reshape(depth, bsz, 3, d)


def _inproj_kernel(x_ref, mod_ref, nw_ref, w_ref, wst_ref, o_ref, ost_ref, h_even, h_odd, *, n_groups):
    g = pl.program_id(0)
    j = pl.program_id(1)
    rows = x_ref.shape[1]

    def prep(h_dst):
        x = x_ref[0]
        ms = jnp.mean(x * x, axis=-1, keepdims=True)
        y = x * lax.rsqrt(ms + EPS) * nw_ref[...]
        h = (y * (1.0 + mod_ref[0, 1:2, :]) + mod_ref[0, 0:1, :]).astype(BF16)
        h_dst[pl.ds(pl.multiple_of(j * rows, rows), rows), :] = h
        ost_ref[0] = _dot_nt(wst_ref[...], h)

    @pl.when(g == 0)
    def _():
        prep(h_even)

    @pl.when((g > 0) & (g % 2 == 0))
    def _():
        o_ref[0] = _dot_nt(h_odd[...], w_ref[...])
        prep(h_even)

    @pl.when(g % 2 == 1)
    def _():
        o_ref[0] = _dot_nt(h_even[...], w_ref[...])
        prep(h_odd)


def _inproj(x, mod_l, norm_w, w_main, w_small, *, tm=1024, nj=4):
    bsz, s, d = x.shape
    n = w_main.shape[0]
    ns = w_small.shape[0]
    tm = min(tm, s)
    ni = s // tm
    n_groups = bsz * ni
    tn = n // nj
    rows = tm // nj
    assert n % nj == 0 and tn % LANES == 0 and rows % LANES == 0

    def prep(g, j):
        gc = jnp.minimum(g, n_groups - 1)
        return gc // ni, (gc % ni) * nj + jnp.where(g < n_groups, j, nj - 1)

    def mm(g, j):
        gm = jnp.maximum(g - 1, 0)
        return gm // ni, gm % ni, jnp.where(g > 0, j, 0)

    return pl.pallas_call(
        functools.partial(_inproj_kernel, n_groups=n_groups),
        out_shape=[jax.ShapeDtypeStruct((bsz, s, n), F32), jax.ShapeDtypeStruct((bsz, ns, s), F32)],
        grid=(n_groups + 1, nj),
        in_specs=[
            pl.BlockSpec((1, rows, d), lambda g, j: (*prep(g, j), 0)),
            pl.BlockSpec((1, 3, d), lambda g, j: (prep(g, j)[0], 0, 0)),
            pl.BlockSpec((1, d), lambda g, j: (0, 0)),
            pl.BlockSpec((tn, d), lambda g, j: (mm(g, j)[2], 0)),
            pl.BlockSpec((ns, d), lambda g, j: (0, 0)),
        ],
        out_specs=[
            pl.BlockSpec((1, tm, tn), lambda g, j: mm(g, j)),
            pl.BlockSpec((1, ns, rows), lambda g, j: (prep(g, j)[0], 0, prep(g, j)[1])),
        ],
        scratch_shapes=[pltpu.VMEM((tm, d), BF16), pltpu.VMEM((tm, d), BF16)],
        compiler_params=_cparams(("arbitrary", "arbitrary")),
        name="norm_mod_inproj",
    )(x, mod_l, norm_w.reshape(1, d), w_main, w_small)


def _outproj_kernel(ya_ref, yb_ref, w_ref, x_ref, mod_ref, *rest, final_norm, half):
    if final_norm:
        fnw_ref, o_ref, w_scr = rest
    else:
        o_ref, w_scr = rest

    @pl.when((pl.program_id(0) == 0) & (pl.program_id(1) == 0))
    def _():
        rows = CAST_ROWS

        def body(r, carry):
            r0 = pl.multiple_of(r * rows, rows)
            w_scr[pl.ds(r0, rows), :] = w_ref[pl.ds(r0, rows), :].astype(BF16)
            return carry

        lax.fori_loop(0, w_ref.shape[0] // rows, body, 0)

    acc = _dot(ya_ref[0], w_scr[0:half, :]) + _dot(yb_ref[0], w_scr[half:, :])
    gate = mod_ref[0, 2:3, :]
    xn = x_ref[0] + gate * acc
    if final_norm:
        ms = jnp.mean(xn * xn, axis=-1, keepdims=True)
        xn = xn * lax.rsqrt(ms + EPS) * fnw_ref[...]
    o_ref[0] = xn


def _outproj(ya, yb, w_out, x, mod_l, final_w=None, *, tm=512):
    bsz, s, d = x.shape
    half = ya.shape[-1]
    tm = min(tm, s)
    final_norm = final_w is not None
    in_specs = [
        pl.BlockSpec((1, tm, half), lambda b, i: (b, i, 0)),
        pl.BlockSpec((1, tm, half), lambda b, i: (b, i, 0)),
        pl.BlockSpec((2 * half, d), lambda b, i: (0, 0), pipeline_mode=pl.Buffered(1)),
        pl.BlockSpec((1, tm, d), lambda b, i: (b, i, 0)),
        pl.BlockSpec((1, 3, d), lambda b, i: (b, 0, 0)),
    ]
    args = [ya, yb, w_out, x, mod_l]
    if final_norm:
        in_specs.append(pl.BlockSpec((1, d), lambda b, i: (0, 0)))
        args.append(final_w.reshape(1, d))
    return pl.pallas_call(
        functools.partial(_outproj_kernel, final_norm=final_norm, half=half),
        out_shape=jax.ShapeDtypeStruct((bsz, s, d), F32),
        grid=(bsz, s // tm),
        in_specs=in_specs,
        out_specs=pl.BlockSpec((1, tm, d), lambda b, i: (b, i, 0)),
        scratch_shapes=[pltpu.VMEM((2 * half, d), BF16)],
        compiler_params=_cparams(("arbitrary", "arbitrary")),
        name="outproj_residual",
    )(*args)


def _rms_rope(x, nw, cos, sin_signed):
    ms = jnp.mean(x * x, axis=-1, keepdims=True)
    y = x * lax.rsqrt(ms + EPS) * nw
    lane = lax.broadcasted_iota(jnp.int32, y.shape, 1)
    partner = jnp.where(lane % 2 == 0, pltpu.roll(y, ATT_HEAD_DIM - 1, 1), pltpu.roll(y, 1, 1))
    return y * cos + partner * sin_signed


def _attention_kv_prep(k_ref, v_ref, ck_ref, sk_ref, kn_ref, k_scr, v_scr):
    dh = ATT_HEAD_DIM
    k_scr[...] = _rms_rope(k_ref[0], kn_ref[...], ck_ref[...], sk_ref[...]).astype(BF16)
    v_scr[:, 0:dh] = v_ref[0].astype(BF16)
    v_scr[:, dh:2 * dh] = jnp.ones((v_scr.shape[0], dh), BF16)


ATTN_AHEAD = 1


def _attention_pump(q_ref, ga_ref, cq_ref, sq_ref, qn_ref, k_scr, v_scr, o_ref, sub):
    dh = ATT_HEAD_DIM
    tq, width = q_ref.shape[1], q_ref.shape[2]
    scale = dh ** -0.5
    probs = [(g, r) for g in range(width // dh) for r in range(0, tq, sub)]
    ahead = ATTN_AHEAD
    state = {"i": 0, "s": {}}

    def scores(g, r):
        qg = q_ref[0, r:r + sub, g * dh:(g + 1) * dh]
        qg = (_rms_rope(qg, qn_ref[...], cq_ref[r:r + sub, :], sq_ref[r:r + sub, :]) * scale).astype(BF16)
        return _dot_nt(qg, k_scr[...])

    def finish(g, r, s):
        s = s.astype(BF16)
        p = jnp.exp(s - jnp.max(s, axis=-1, keepdims=True))
        o_ext = _dot(p, v_scr[...])
        og = o_ext[:, 0:dh] / o_ext[:, dh:2 * dh] * _silu(ga_ref[0, r:r + sub, g * dh:(g + 1) * dh])
        o_ref[0, r:r + sub, g * dh:(g + 1) * dh] = og.astype(BF16)

    def pump():
        i = state["i"]
        if i >= len(probs) + ahead:
            return
        if i < len(probs):
            state["s"][i] = scores(*probs[i])
        if i >= ahead:
            finish(*probs[i - ahead], state["s"].pop(i - ahead))
        state["i"] = i + 1

    return pump, len(probs) + ahead


def _attn_kernel(q_ref, k_ref, v_ref, ga_ref, cq_ref, sq_ref, ck_ref, sk_ref, qn_ref, kn_ref,
                 o_ref, k_scr, v_scr, *, sub):
    @pl.when(pl.program_id(2) == 0)
    def _():
        _attention_kv_prep(k_ref, v_ref, ck_ref, sk_ref, kn_ref, k_scr, v_scr)

    pump, n = _attention_pump(q_ref, ga_ref, cq_ref, sq_ref, qn_ref, k_scr, v_scr, o_ref, sub)
    for _ in range(n):
        pump()


def _attention(proj, cos_t, sin_t, q_norm, k_norm, *, col_q, col_k, col_v, col_ga, tq=512, sub=128):
    bsz, s, _ = proj.shape
    tq = min(tq, s)
    gw = ATT_GROUP * ATT_HEAD_DIM
    dh = ATT_HEAD_DIM
    qb, kb, vb, gb = col_q // gw, col_k // dh, col_v // dh, col_ga // gw
    return pl.pallas_call(
        functools.partial(_attn_kernel, sub=min(sub, tq)),
        out_shape=jax.ShapeDtypeStruct((bsz, s, BRANCH_W), BF16),
        grid=(bsz, ATT_KV_HEADS, s // tq),
        in_specs=[
            pl.BlockSpec((1, tq, gw), lambda b, h, i: (b, i, qb + h)),
            pl.BlockSpec((1, s, dh), lambda b, h, i: (b, 0, kb + h)),
            pl.BlockSpec((1, s, dh), lambda b, h, i: (b, 0, vb + h)),
            pl.BlockSpec((1, tq, gw), lambda b, h, i: (b, i, gb + h)),
            pl.BlockSpec((tq, dh), lambda b, h, i: (i, 0)),
            pl.BlockSpec((tq, dh), lambda b, h, i: (i, 0)),
            pl.BlockSpec((s, dh), lambda b, h, i: (0, 0)),
            pl.BlockSpec((s, dh), lambda b, h, i: (0, 0)),
            pl.BlockSpec((1, dh), lambda b, h, i: (0, 0)),
            pl.BlockSpec((1, dh), lambda b, h, i: (0, 0)),
        ],
        out_specs=pl.BlockSpec((1, tq, gw), lambda b, h, i: (b, i, h)),
        scratch_shapes=[pltpu.VMEM((s, dh), BF16), pltpu.VMEM((s, 2 * dh), BF16)],
        compiler_params=_cparams(("arbitrary", "arbitrary", "arbitrary")),
        name="gqa_attention",
    )(proj, proj, proj, proj, cos_t, sin_t, cos_t, sin_t, q_norm, k_norm)


def _rope_tables(s):
    t = np.arange(s)
    row = (t // GRID_W).astype(np.float64)
    col = (t % GRID_W).astype(np.float64)
    n_pairs = ATT_HEAD_DIM // 4
    freqs = ROPE_THETA ** (-np.arange(n_pairs, dtype=np.float64) / n_pairs)
    ang = np.concatenate([row[:, None] * freqs, col[:, None] * freqs], axis=-1)
    cos, sin = np.cos(ang), np.sin(ang)
    cos_t = np.repeat(cos, 2, axis=-1)
    sin_t = np.stack([-sin, sin], axis=-1).reshape(s, ATT_HEAD_DIM)
    return jnp.asarray(cos_t, F32), jnp.asarray(sin_t, F32)


def _chunk_of_step(step, nc):
    return jnp.where(step < nc, step, 2 * nc - 1 - step)


def _fill_conv_window(ext_scr, cur, prev, nxt, c, nc, rows):
    ext_scr[0:HALO, :] = jnp.where(c > 0, prev, 0.0)
    ext_scr[HALO:HALO + rows, :] = cur
    ext_scr[HALO + rows:HALO + rows + HALO, :] = jnp.where(c < nc - 1, nxt, 0.0)


CONV_SUB = 128


def _centred_conv4(ext_scr, cw_ref, cb_ref, rows):
    sub = min(rows, CONV_SUB)
    n_in = sub + 2 * HALO
    taps = (0, 1, 3)
    r = lax.broadcasted_iota(jnp.int32, (len(taps) * sub, n_in), 0)
    c = lax.broadcasted_iota(jnp.int32, (len(taps) * sub, n_in), 1)
    src = jnp.zeros_like(r)
    for i, k in enumerate(taps):
        src = jnp.where((r >= i * sub) & (r < (i + 1) * sub), r - i * sub + HALO + (k - 2), src)
    shift_mat = jnp.where(c == src, 1.0, 0.0).astype(BF16)
    outs = []
    for s0 in range(0, rows, sub):
        win = ext_scr[s0:s0 + n_in, :]
        shifted = _dot(shift_mat, win.astype(BF16))
        acc = cb_ref[...] + win[HALO:HALO + sub, :] * cw_ref[2:3, :]
        for i, k in enumerate(taps):
            acc = acc + shifted[i * sub:(i + 1) * sub, :] * cw_ref[k:k + 1, :]
        outs.append(acc)
    return outs[0] if len(outs) == 1 else jnp.concatenate(outs, axis=0)


def _scan_masks(fwd, n):
    row = lax.broadcasted_iota(jnp.int32, (n, n), 0)
    col = lax.broadcasted_iota(jnp.int32, (n, n), 1)
    sgn = jnp.where(fwd, 1, -1)
    d = (row - col) * sgn
    return d >= 0, d <= 0, d > 0


def _pad_dir_rows(v, n):
    return jnp.zeros((2, 1, LANES), F32).at[:, 0, :n].set(v)


def _pad_dir_cols(v, n):
    return jnp.zeros((2, LANES, 1), F32).at[:, :n, 0].set(v)


def _head_expander(heads, width):
    r = lax.broadcasted_iota(jnp.int32, (heads, heads * width), 0)
    c = lax.broadcasted_iota(jnp.int32, (heads, heads * width), 1)
    return jnp.where((c >= r * width) & (c < (r + 1) * width), 1.0, 0.0).astype(BF16)


def _ssd_kernel(xf_ref, pf_ref, nf_ref, xb_ref, pb_ref, nb_ref, z_ref, dtf_ref, dtb_ref, cw_ref, cb_ref,
                bias_r_ref, bias_c_ref, alog_r_ref, alog_c_ref, dskip_ref, nw_ref,
                o_ref, ext_scr, state_scr, acc_scr, *, nc, te):
    L, H, P, N = SSD_CHUNK, SSD_HEADS, SSD_HEAD_DIM, SSD_STATE
    HG = H // SSD_GROUPS
    GW = HG * P
    t = pl.program_id(1)

    @pl.when(t == 0)
    def _():
        state_scr[...] = jnp.zeros_like(state_scr)

    @pl.when(t < nc)
    def _():
        chunks = (t, nc - 1 - t)
        blocks = ((xf_ref, pf_ref, nf_ref, dtf_ref), (xb_ref, pb_ref, nb_ref, dtb_ref))
        expand = _head_expander(H, P)
        lane = lax.broadcasted_iota(jnp.int32, (L, 2 * P), 1)
        per_dir = []
        for d in range(2):
            x_ref, p_ref, n_ref, dtt_ref = blocks[d]
            ext = ext_scr.at[d]
            _fill_conv_window(ext, x_ref[0], p_ref[0], n_ref[0], chunks[d], nc, L)
            xbc = _silu(_centred_conv4(ext, cw_ref, cb_ref, L))
            xs = xbc[:, :BRANCH_W]
            bs = xbc[:, BRANCH_W:BRANCH_W + SSD_GROUPS * N]
            cs = xbc[:, BRANCH_W + SSD_GROUPS * N:]
            raw_t = dtt_ref[0]
            raw = raw_t.T
            dt = _softplus(raw[:, d * H:(d + 1) * H] + bias_r_ref[d][:, 0:H])
            a = dt * (-jnp.exp(alog_r_ref[d][:, 0:H]))
            dt_t = _softplus(raw_t[d * H:(d + 1) * H, :] + bias_c_ref[d][0:H, :])
            a_t = dt_t * (-jnp.exp(alog_c_ref[d][0:H, :]))
            mask, mask_t, _ = _scan_masks(d == 0, L)
            cum = _dot_exact_lhs(jnp.where(mask, 1.0, 0.0).astype(BF16), a)
            cum_t = _dot_exact_rhs(a_t, jnp.where(mask_t, 1.0, 0.0).astype(BF16))
            total = jnp.sum(a, axis=0, keepdims=True)
            dt_e = _dot(dt.astype(BF16), expand)
            p_e = _dot(jnp.exp(cum).astype(BF16), expand)
            q_e = _dot(jnp.exp(total - cum).astype(BF16), expand)
            tot_e = _dot_exact_rhs(jnp.broadcast_to(jnp.exp(total), (SUBLANES, H)), expand)[0:1, :]
            xd = xs * dt_e
            per_dir.append(dict(xs=xs, bs=bs, cs=cs, mask=mask, cum=cum, cum_t=cum_t, p_e=p_e, tot_e=tot_e,
                                xdq=(xd * q_e).astype(BF16), xd_b=xd.astype(BF16)))

        combos = [(d, g) for d in range(2) for g in range(SSD_GROUPS)]
        cg = {k: per_dir[k[0]]["cs"][:, k[1] * N:(k[1] + 1) * N].astype(BF16) for k in combos}
        bg = {k: per_dir[k[0]]["bs"][:, k[1] * N:(k[1] + 1) * N] for k in combos}
        gmat = {k: _dot_nt(cg[k], bg[k].astype(BF16)) for k in combos}
        h_prev = {k: state_scr[k[0], :, k[1] * GW:(k[1] + 1) * GW] for k in combos}
        y_off = {k: _dot(cg[k], h_prev[k].astype(BF16)) * per_dir[k[0]]["p_e"][:, k[1] * GW:(k[1] + 1) * GW]
                 for k in combos}
        pairs = {k: [] for k in combos}
        for hp in range(HG // 2):
            for k in combos:
                d, g = k
                pd = per_dir[d]
                h0 = g * HG + 2 * hp
                xpair = pd["xd_b"][:, h0 * P:(h0 + 2) * P]
                ys = []
                for h in (h0, h0 + 1):
                    dec = jnp.exp(jnp.where(pd["mask"], pd["cum"][:, h:h + 1] - pd["cum_t"][h:h + 1, :], NEG_BIG))
                    ys.append(_dot((gmat[k] * dec).astype(BF16), xpair))
                pairs[k].append(jnp.where(lane < P, ys[0], ys[1]))
        for k in combos:
            d, g = k
            state_scr[d, :, g * GW:(g + 1) * GW] = (
                h_prev[k] * per_dir[d]["tot_e"][:, g * GW:(g + 1) * GW]
                + _dot(bg[k].T.astype(BF16), per_dir[d]["xdq"][:, g * GW:(g + 1) * GW]))
        y_dirs = [jnp.concatenate([jnp.concatenate(pairs[(d, g)], axis=1) + y_off[(d, g)]
                                   for g in range(SSD_GROUPS)], axis=1) for d in range(2)]
        y_dirs[0] = y_dirs[0] + dskip_ref[...] * per_dir[0]["xs"]
        rows = [pl.multiple_of(c * L, L) for c in chunks]

        @pl.when(t < nc // 2)
        def _():
            for d in range(2):
                acc_scr[pl.ds(rows[d], L), :] = y_dirs[d]

        @pl.when(t >= nc // 2)
        def _():
            for d in range(2):
                acc_scr[pl.ds(rows[d], L), :] = acc_scr[pl.ds(rows[d], L), :] + y_dirs[d]

    @pl.when(t >= nc)
    def _():
        r0 = pl.multiple_of((t - nc) * te, te)
        y = acc_scr[pl.ds(r0, te), :] * _silu(z_ref[0])
        ms = jnp.mean(y * y, axis=-1, keepdims=True)
        o_ref[0] = (y * lax.rsqrt(ms + EPS) * nw_ref[...]).astype(BF16)


def _ssd(proj, small_t, conv_w, conv_b, dt_bias, a_log, d_skip, norm_w, *, col_xbc, col_z, te=512):
    bsz, s, _ = proj.shape
    L = SSD_CHUNK
    nc = s // L
    te = min(te, s)
    ne = s // te
    cw = BRANCH_W + 2 * SSD_GROUPS * SSD_STATE
    hb = L // HALO
    xb, zb = col_xbc // cw, col_z // BRANCH_W
    assert col_xbc % cw == 0 and col_z % BRANCH_W == 0 and nc % 2 == 0

    def cf(t):
        return jnp.minimum(t, nc - 1)

    def cbk(t):
        return jnp.maximum(nc - 1 - t, 0)

    def ep(t):
        return jnp.maximum(t - nc, 0)

    def xbc_specs(chunk):
        return [
            pl.BlockSpec((1, L, cw), lambda b, t: (b, chunk(t), xb)),
            pl.BlockSpec((1, HALO, cw), lambda b, t: (b, jnp.maximum(chunk(t) * hb - 1, 0), xb)),
            pl.BlockSpec((1, HALO, cw), lambda b, t: (b, jnp.minimum((chunk(t) + 1) * hb, s // HALO - 1), xb)),
        ]

    def small_spec(chunk):
        return pl.BlockSpec((1, LANES, L), lambda b, t: (b, 0, chunk(t)))

    full = lambda shape: pl.BlockSpec(shape, lambda b, t: (0,) * len(shape))
    in_specs = (xbc_specs(cf) + xbc_specs(cbk)
                + [pl.BlockSpec((1, te, BRANCH_W), lambda b, t: (b, ep(t), zb))]
                + [small_spec(cf), small_spec(cbk)]
                + [full((SSD_CONV, cw)), full((1, cw)), full((2, 1, LANES)), full((2, LANES, 1)),
                   full((2, 1, LANES)), full((2, LANES, 1)), full((1, BRANCH_W)), full((1, BRANCH_W))])
    args = [proj, proj, proj, proj, proj, proj, proj, small_t, small_t, conv_w, conv_b.reshape(1, cw),
            _pad_dir_rows(dt_bias, SSD_HEADS), _pad_dir_cols(dt_bias, SSD_HEADS),
            _pad_dir_rows(a_log, SSD_HEADS), _pad_dir_cols(a_log, SSD_HEADS),
            jnp.repeat(d_skip, SSD_HEAD_DIM).reshape(1, BRANCH_W), norm_w.reshape(1, BRANCH_W)]
    out_shape = [jax.ShapeDtypeStruct((bsz, s, BRANCH_W), BF16)]
    out_specs = [pl.BlockSpec((1, te, BRANCH_W), lambda b, t: (b, ep(t), 0))]
    scratch = [
        pltpu.VMEM((2, L + 2 * HALO, cw), F32),
        pltpu.VMEM((2, SSD_STATE, BRANCH_W), F32),
        pltpu.VMEM((s, BRANCH_W), F32),
    ]
    return pl.pallas_call(
        functools.partial(_ssd_kernel, nc=nc, te=te),
        out_shape=out_shape,
        grid=(bsz, nc + ne),
        in_specs=in_specs,
        out_specs=out_specs,
        scratch_shapes=scratch,
        compiler_params=_cparams(("arbitrary", "arbitrary")),
        name="bidir_ssd",
    )(*args)[0]


AB_COL_XBC, AB_COL_K, AB_COL_V, AB_COL_Z, AB_COL_Q, AB_COL_GA, AB_N = 0, 1536, 1792, 2048, 3072, 4096, 5120


REGROUP_ROWS = 256


def _regroup_kernel(offs_ref, w_ref, o_ref):
    del offs_ref
    o_ref[...] = w_ref[...].astype(BF16)


def _regroup_cast(wt, pieces):
    rows = REGROUP_ROWS
    offsets = []
    for start, n in pieces:
        assert n % rows == 0 and start % SUBLANES == 0
        offsets += list(range(start, start + n, rows))
    d = wt.shape[1]
    return pl.pallas_call(
        _regroup_kernel,
        out_shape=jax.ShapeDtypeStruct((len(offsets) * rows, d), BF16),
        grid_spec=pltpu.PrefetchScalarGridSpec(
            num_scalar_prefetch=1,
            grid=(len(offsets),),
            in_specs=[pl.BlockSpec((pl.Element(rows), pl.Element(d)), lambda i, offs: (pl.multiple_of(offs[i], SUBLANES), 0))],
            out_specs=pl.BlockSpec((rows, d), lambda i, offs: (i, 0)),
        ),
        compiler_params=_cparams(("arbitrary",)),
        name="weight_regroup_cast",
    )(jnp.asarray(offsets, jnp.int32), wt)


def _piece_table(sizes):
    table, o = {}, 0
    for name, n in sizes:
        table[name] = (o, n)
        o += n
    return table


def _ab_weights(w_in):
    wt = w_in.T
    hq, hk = ATT_HEADS * ATT_HEAD_DIM, ATT_KV_HEADS * ATT_HEAD_DIM
    gn = SSD_GROUPS * SSD_STATE
    p = _piece_table([("q", hq), ("k", hk), ("v", hk), ("ga", BRANCH_W), ("xs", BRANCH_W), ("bs", gn),
                      ("cs", gn), ("dtf", SSD_HEADS), ("dtb", SSD_HEADS), ("z", BRANCH_W)])
    main = _regroup_cast(wt, [p[n] for n in ("xs", "bs", "cs", "k", "v", "z", "q", "ga")])
    dt0, dtn = p["dtf"][0], 2 * SSD_HEADS
    zpad = jnp.zeros((LANES - dtn, wt.shape[1]), wt.dtype)
    small = jnp.concatenate([wt[dt0:dt0 + dtn], zpad], axis=0).astype(BF16)
    return main, small


def _layer0(x, mod_l, norm_w, w_in, q_norm, k_norm, conv_w, conv_b, dt_bias_f, dt_bias_b,
            a_log_f, a_log_b, d_skip, ssd_norm, w_out, final_w=None):
    s = x.shape[1]
    w_main, w_small = _ab_weights(w_in)
    proj, small_t = _inproj(x, mod_l, norm_w, w_main, w_small)
    cos_t, sin_t = _rope_tables(s)
    att = _attention(proj, cos_t, sin_t, q_norm.reshape(1, ATT_HEAD_DIM), k_norm.reshape(1, ATT_HEAD_DIM),
                     col_q=AB_COL_Q, col_k=AB_COL_K, col_v=AB_COL_V, col_ga=AB_COL_GA)
    ssd = _ssd(proj, small_t, conv_w, conv_b, jnp.stack([dt_bias_f, dt_bias_b]),
               jnp.stack([a_log_f, a_log_b]), d_skip, ssd_norm, col_xbc=AB_COL_XBC, col_z=AB_COL_Z)
    return _outproj(att, ssd, w_out, x, mod_l, final_w), (proj, small_t, att, ssd)


def _unit_tri_inverse(nmats, n):
    row = lax.broadcasted_iota(jnp.int32, (n, n), 0)
    col = lax.broadcasted_iota(jnp.int32, (n, n), 1)

    def same_block(size):
        return (row // size) == (col // size)

    def mm(a, b):
        return _dot(a, b).astype(BF16)

    def as_mask(cond):
        return jnp.where(cond, 1.0, 0.0).astype(BF16)

    nmats = [m.astype(BF16) for m in nmats]
    eye = as_mask(row == col)
    base = SUBLANES
    blk = same_block(base)
    blk_m = as_mask(blk)
    nd = [m * blk_m for m in nmats]
    p1 = [mm(x, x) for x in nd]
    p2 = [mm(x, x) for x in p1]
    t = [eye - x for x in nd]
    t = [x + mm(x, p) for x, p in zip(t, p1)]
    t = [x + mm(x, p) for x, p in zip(t, p2)]
    size = base
    while size < n:
        nxt = same_block(2 * size)
        off_m = as_mask(nxt & jnp.logical_not(blk))
        et = [mm(m * off_m, x) for m, x in zip(nmats, t)]
        t = [x - mm(x, y) for x, y in zip(t, et)]
        blk = nxt
        size *= 2
    return t


def _l2norm(x):
    return x * lax.rsqrt(jnp.sum(x * x, axis=-1, keepdims=True) + EPS)


def _gdn_kernel(qf_ref, pf_ref, nf_ref, qb_ref, pb_ref, nb_ref, z_ref, smtf_ref, smtb_ref,
                cw_ref, cb_ref, bias_r_ref, bias_c_ref, alog_r_ref, alog_c_ref, nw_ref,
                o_ref, ext_scr, state_scr, acc_scr, *, nc, te):
    L, HV, HQ, DK = GDN_CHUNK, GDN_V_HEADS, GDN_QK_HEADS, GDN_HEAD_DIM
    rep = HV // HQ
    nbt = acc_scr.shape[0]
    t = pl.program_id(1)

    @pl.when(t == 0)
    def _():
        state_scr[...] = jnp.zeros_like(state_scr)

    @pl.when(t < nc)
    def _():
        chunks = (t, nc - 1 - t)
        blocks = ((qf_ref, pf_ref, nf_ref, smtf_ref), (qb_ref, pb_ref, nb_ref, smtb_ref))
        kk, qk, qh, kh, vh, colv, dec, bcol, ecol, tot, nmat = ([] for _ in range(11))
        for bi, d in [(bi, d) for bi in range(nbt) for d in range(2)]:
            c = chunks[d]
            q_ref, p_ref, n_ref, smt_ref = blocks[d]
            ext = ext_scr.at[bi * 2 + d]
            _fill_conv_window(ext, q_ref[bi], p_ref[bi], n_ref[bi], c, nc, L)
            act = _silu(_centred_conv4(ext, cw_ref, cb_ref, L))
            q_n = [_l2norm(act[:, h * DK:(h + 1) * DK]) * (DK ** -0.5) for h in range(HQ)]
            k_n = [_l2norm(act[:, (HQ + h) * DK:(HQ + h + 1) * DK]) for h in range(HQ)]
            k_b = [x.astype(BF16) for x in k_n]
            kk_d = [_dot_nt(k_b[h], k_b[h]) for h in range(HQ)]
            qk_d = [_dot_nt(q_n[h].astype(BF16), k_b[h]) for h in range(HQ)]

            raw_t = smt_ref[bi]
            sm = raw_t.T[:, 2 * HV * d:2 * HV * (d + 1)]
            a_raw_t = raw_t[2 * HV * d + HV:2 * HV * (d + 1), :]
            beta = _sigmoid(sm[:, 0:HV])
            g = -jnp.exp(alog_r_ref[d][:, 0:HV]) * _softplus(sm[:, HV:2 * HV] + bias_r_ref[d][:, 0:HV])
            g_t = -jnp.exp(alog_c_ref[d][0:HV, :]) * _softplus(a_raw_t + bias_c_ref[d][0:HV, :])
            mask, mask_t, strict = _scan_masks(d == 0, L)
            cum = _dot_exact_lhs(jnp.where(mask, 1.0, 0.0).astype(BF16), g)
            cum_t = _dot_exact_rhs(g_t, jnp.where(mask_t, 1.0, 0.0).astype(BF16))
            total = jnp.sum(g, axis=0, keepdims=True)
            for h in range(HV):
                cv = cum[:, h:h + 1]
                dc = jnp.exp(jnp.where(mask, cv - cum_t[h:h + 1, :], NEG_BIG))
                bc = beta[:, h:h + 1]
                kk.append(kk_d[h // rep]); qk.append(qk_d[h // rep])
                qh.append(q_n[h // rep]); kh.append(k_n[h // rep])
                vh.append(act[:, (2 * HQ + h) * DK:(2 * HQ + h + 1) * DK])
                colv.append(cv); dec.append(dc); bcol.append(bc); ecol.append(jnp.exp(cv))
                tot.append(total[:, h:h + 1])
                nmat.append(jnp.where(strict, kk_d[h // rep] * bc * dc, 0.0))

        outs = []
        for w0 in range(0, nbt * 2 * HV, GDN_WAVE):
            idx = range(w0, w0 + GDN_WAVE)
            t_inv = dict(zip(idx, _unit_tri_inverse([nmat[i] for i in idx], L)))
            u = {i: _dot(t_inv[i], (vh[i] * bcol[i]).astype(BF16)) for i in idx}
            w = {i: _dot(t_inv[i], (kh[i] * (bcol[i] * ecol[i])).astype(BF16)) for i in idx}
            s_prev = {i: state_scr[i // HV, :, (i % HV) * DK:(i % HV + 1) * DK] for i in idx}
            s_b = {i: s_prev[i].astype(BF16) for i in idx}
            v_new = {i: (u[i] - _dot(w[i].astype(BF16), s_b[i])).astype(BF16) for i in idx}
            outs += [_dot((qh[i] * ecol[i]).astype(BF16), s_b[i])
                     + _dot((qk[i] * dec[i]).astype(BF16), v_new[i]) for i in idx]
            for i in idx:
                k_dec = kh[i] * jnp.exp(tot[i] - colv[i])
                state_scr[i // HV, :, (i % HV) * DK:(i % HV + 1) * DK] = (
                    s_prev[i] * jnp.exp(tot[i]) + _dot(k_dec.T.astype(BF16), v_new[i]))
        o_dirs = [jnp.concatenate(outs[j * HV:(j + 1) * HV], axis=1) for j in range(nbt * 2)]
        rows = [pl.multiple_of(c * L, L) for c in chunks]

        @pl.when(t < nc // 2)
        def _():
            for bi in range(nbt):
                for d in range(2):
                    acc_scr[bi, pl.ds(rows[d], L), :] = o_dirs[bi * 2 + d]

        @pl.when(t >= nc // 2)
        def _():
            for bi in range(nbt):
                for d in range(2):
                    acc_scr[bi, pl.ds(rows[d], L), :] = acc_scr[bi, pl.ds(rows[d], L), :] + o_dirs[bi * 2 + d]

    @pl.when(t >= nc)
    def _():
        r0 = pl.multiple_of((t - nc) * te, te)
        for bi in range(nbt):
            zz = z_ref[bi]
            for hv in range(HV):
                oh = acc_scr[bi, pl.ds(r0, te), hv * DK:(hv + 1) * DK]
                ms = jnp.mean(oh * oh, axis=-1, keepdims=True)
                res = oh * lax.rsqrt(ms + EPS) * nw_ref[...] * _silu(zz[:, hv * DK:(hv + 1) * DK])
                o_ref[bi, :, hv * DK:(hv + 1) * DK] = res.astype(BF16)


def _gdn(proj, small_t, conv_w, conv_b, dt_bias, a_log, norm_w, *, col_qkv, col_z, te=512, nbt=1):
    bsz, s, _ = proj.shape
    L = GDN_CHUNK
    nc = s // L
    te = min(te, s)
    ne = s // te
    cw = 2 * GDN_QK_HEADS * GDN_HEAD_DIM + BRANCH_W
    hb = L // HALO
    qb, zb = col_qkv // cw, col_z // BRANCH_W
    assert col_qkv % cw == 0 and col_z % BRANCH_W == 0 and nc % 2 == 0 and bsz % nbt == 0

    def cf(t):
        return jnp.minimum(t, nc - 1)

    def cbk(t):
        return jnp.maximum(nc - 1 - t, 0)

    def ep(t):
        return jnp.maximum(t - nc, 0)

    def qkv_specs(chunk):
        return [
            pl.BlockSpec((nbt, L, cw), lambda b, t: (b, chunk(t), qb)),
            pl.BlockSpec((nbt, HALO, cw), lambda b, t: (b, jnp.maximum(chunk(t) * hb - 1, 0), qb)),
            pl.BlockSpec((nbt, HALO, cw), lambda b, t: (b, jnp.minimum((chunk(t) + 1) * hb, s // HALO - 1), qb)),
        ]

    def small_spec(chunk):
        return pl.BlockSpec((nbt, LANES, L), lambda b, t: (b, 0, chunk(t)))

    full = lambda shape: pl.BlockSpec(shape, lambda b, t: (0,) * len(shape))
    hv = GDN_V_HEADS
    return pl.pallas_call(
        functools.partial(_gdn_kernel, nc=nc, te=te),
        out_shape=jax.ShapeDtypeStruct((bsz, s, BRANCH_W), BF16),
        grid=(bsz // nbt, nc + ne),
        in_specs=qkv_specs(cf) + qkv_specs(cbk)
        + [pl.BlockSpec((nbt, te, BRANCH_W), lambda b, t: (b, ep(t), zb))]
        + [small_spec(cf), small_spec(cbk)]
        + [full((GDN_CONV, cw)), full((1, cw)), full((2, 1, LANES)), full((2, LANES, 1)),
           full((2, 1, LANES)), full((2, LANES, 1)), full((1, GDN_HEAD_DIM))],
        out_specs=pl.BlockSpec((nbt, te, BRANCH_W), lambda b, t: (b, ep(t), 0)),
        scratch_shapes=[
            pltpu.VMEM((nbt * 2, L + 2 * HALO, cw), F32),
            pltpu.VMEM((nbt * 2, GDN_HEAD_DIM, BRANCH_W), F32),
            pltpu.VMEM((nbt, s, BRANCH_W), F32),
        ],
        compiler_params=_cparams(("arbitrary", "arbitrary")),
        name="bidir_gated_deltanet",
    )(proj, proj, proj, proj, proj, proj, proj, small_t, small_t, conv_w, conv_b.reshape(1, cw),
      _pad_dir_rows(dt_bias, hv), _pad_dir_cols(dt_bias, hv), _pad_dir_rows(a_log, hv), _pad_dir_cols(a_log, hv),
      norm_w.reshape(1, GDN_HEAD_DIM))


def _local_scan(a, u, ascending):
    rows, width = a.shape
    a = a.reshape(rows // SUBLANES, SUBLANES, width)
    u = u.reshape(rows // SUBLANES, SUBLANES, width)
    sub = lax.broadcasted_iota(jnp.int32, a.shape, 1)
    d = 1
    while d < SUBLANES:
        if ascending:
            keep = sub >= d
            shift = d
        else:
            keep = sub < SUBLANES - d
            shift = SUBLANES - d
        a_sh = jnp.where(keep, pltpu.roll(a, shift, 1), 1.0)
        u_sh = jnp.where(keep, pltpu.roll(u, shift, 1), 0.0)
        u = u + a * u_sh
        a = a * a_sh
        d *= 2
    return a.reshape(rows, width), u.reshape(rows, width)


def _lru_kernel(xl_ref, prev_ref, next_ref, gl_ref, cw_ref, cb_ref, wa_ref, ba_ref, wx_ref, bx_ref, lam_ref,
                o_ref, ext_scr, a_scr, u_scr, hf_scr, carry_scr, *, nb, unroll):
    rows, width = a_scr.shape
    bw = LRU_WIDTH // LRU_BLOCKS
    step = pl.program_id(2)
    fwd = step < nb
    blk = _chunk_of_step(step, nb)

    @pl.when((step == 0) | (step == nb))
    def _():
        carry_scr[...] = jnp.zeros_like(carry_scr)

    _fill_conv_window(ext_scr, xl_ref[0], prev_ref[0], next_ref[0], blk, nb, rows)
    xc = _centred_conv4(ext_scr, cw_ref, cb_ref, rows)
    r_parts, i_parts = [], []
    for n in range(width // bw):
        xb = xc[:, n * bw:(n + 1) * bw].astype(BF16)
        r_parts.append(_dot(xb, wa_ref[0, n]))
        i_parts.append(_dot(xb, wx_ref[0, n]))
    tr = jnp.tanh(jnp.concatenate(r_parts, axis=1) + ba_ref[0])
    ti = jnp.tanh(jnp.concatenate(i_parts, axis=1) + bx_ref[0])
    log_a = ((-0.5 * LRU_C) * _softplus(-lam_ref[0])) * (tr + 1.0)
    a = jnp.exp(log_a)
    quarter = (-0.25 * jnp.tanh(log_a)) * (a * a + 1.0)
    u = (quarter * lax.rsqrt(jnp.maximum(quarter, TINY))) * ((ti + 1.0) * xc)

    n_groups = rows // SUBLANES
    base_out = pl.multiple_of(blk * rows, rows)

    def scan(ascending):
        a_loc, u_loc = _local_scan(a, u, ascending)
        a_scr[...] = a_loc
        u_scr[...] = u_loc
        last = SUBLANES - 1 if ascending else 0

        def body(it, carry):
            pos = it if ascending else n_groups - 1 - it
            r0 = pl.multiple_of(pos * SUBLANES, SUBLANES)
            h = u_scr[pl.ds(r0, SUBLANES), :] + a_scr[pl.ds(r0, SUBLANES), :] * carry
            u_scr[pl.ds(r0, SUBLANES), :] = h
            return jnp.broadcast_to(h[last:last + 1, :], h.shape)

        carry_scr[...] = lax.fori_loop(0, n_groups, body, carry_scr[...], unroll=unroll)

    @pl.when(fwd)
    def _():
        scan(True)
        hf_scr[pl.ds(base_out, rows), :] = u_scr[...]

    @pl.when(jnp.logical_not(fwd))
    def _():
        scan(False)
        o_ref[0] = ((hf_scr[pl.ds(base_out, rows), :] + u_scr[...]) * _silu(gl_ref[0])).astype(BF16)


def _lru(proj, conv_w, conv_b, wa, ba, wx, bx, lam, *, col_xl, col_gl, tt=512, width=1024, unroll=8):
    bsz, s, _ = proj.shape
    tt = min(tt, s)
    nb = s // tt
    w_total = LRU_WIDTH
    bw = w_total // LRU_BLOCKS
    nbw = width // bw
    xcol, gcol = col_xl // width, col_gl // width
    assert col_xl % width == 0 and col_gl % width == 0
    hb = tt // HALO

    def blk(t):
        return _chunk_of_step(t, nb)

    def direction(t):
        return jnp.where(t < nb, 0, 1)

    return pl.pallas_call(
        functools.partial(_lru_kernel, nb=nb, unroll=unroll),
        out_shape=jax.ShapeDtypeStruct((bsz, s, w_total), BF16),
        grid=(bsz, w_total // width, 2 * nb),
        in_specs=[
            pl.BlockSpec((1, tt, width), lambda b, j, t: (b, blk(t), xcol + j)),
            pl.BlockSpec((1, HALO, width), lambda b, j, t: (b, jnp.maximum(blk(t) * hb - 1, 0), xcol + j)),
            pl.BlockSpec((1, HALO, width),
                         lambda b, j, t: (b, jnp.minimum((blk(t) + 1) * hb, s // HALO - 1), xcol + j)),
            pl.BlockSpec((1, tt, width), lambda b, j, t: (b, blk(t), gcol + j)),
            pl.BlockSpec((LRU_CONV, width), lambda b, j, t: (0, j)),
            pl.BlockSpec((1, width), lambda b, j, t: (0, j)),
            pl.BlockSpec((1, nbw, bw, bw), lambda b, j, t: (direction(t), j, 0, 0)),
            pl.BlockSpec((1, 1, width), lambda b, j, t: (direction(t), 0, j)),
            pl.BlockSpec((1, nbw, bw, bw), lambda b, j, t: (direction(t), j, 0, 0)),
            pl.BlockSpec((1, 1, width), lambda b, j, t: (direction(t), 0, j)),
            pl.BlockSpec((1, 1, width), lambda b, j, t: (direction(t), 0, j)),
        ],
        out_specs=pl.BlockSpec((1, tt, width),
                               lambda b, j, t: (b, jnp.where(t < nb, nb - 1, 2 * nb - 1 - t), j)),
        scratch_shapes=[
            pltpu.VMEM((tt + 2 * HALO, width), F32),
            pltpu.VMEM((tt, width), F32),
            pltpu.VMEM((tt, width), F32),
            pltpu.VMEM((s, width), F32),
            pltpu.VMEM((SUBLANES, width), F32),
        ],
        compiler_params=_cparams(("arbitrary", "arbitrary", "arbitrary")),
        name="bidir_rglru",
    )(proj, proj, proj, proj, conv_w, conv_b.reshape(1, w_total),
      (0.5 * wa).astype(BF16), (0.5 * ba).reshape(2, 1, w_total),
      (0.5 * wx).astype(BF16), (0.5 * bx).reshape(2, 1, w_total), lam.reshape(2, 1, w_total))


CD_COL_QKV, CD_COL_Z, CD_COL_XL, CD_COL_GL = 0, 2048, 3072, 4096


def _cd_weights(w_in):
    wt = w_in.T
    nqk = GDN_QK_HEADS * GDN_HEAD_DIM
    hv = GDN_V_HEADS
    p = _piece_table([("q", nqk), ("k", nqk), ("v", BRANCH_W), ("bf", hv), ("bb", hv), ("af", hv), ("ab", hv),
                      ("z", BRANCH_W), ("xl", LRU_WIDTH), ("gl", LRU_WIDTH)])
    main = _regroup_cast(wt, [p[n] for n in ("q", "k", "v", "z", "xl", "gl")])
    s0 = p["bf"][0]
    blk = wt[s0:s0 + 4 * hv].astype(BF16)
    zpad = jnp.zeros((LANES - 4 * hv, wt.shape[1]), BF16)
    small = jnp.concatenate([blk[0:hv], blk[2 * hv:3 * hv], blk[hv:2 * hv], blk[3 * hv:4 * hv], zpad], axis=0)
    return main, small


def _layer1(x, mod_l, norm_w, w_in, conv_w, conv_b, a_log_f, a_log_b, dt_bias_f, dt_bias_b, gdn_norm,
            lru_conv_w, lru_conv_b, wa_f, ba_f, wx_f, bx_f, lam_f, wa_b, ba_b, wx_b, bx_b, lam_b, w_out,
            final_w=None):
    w_main, w_small = _cd_weights(w_in)
    proj, small_t = _inproj(x, mod_l, norm_w, w_main, w_small)
    gdn = _gdn(proj, small_t, conv_w, conv_b, jnp.stack([dt_bias_f, dt_bias_b]),
               jnp.stack([a_log_f, a_log_b]), gdn_norm, col_qkv=CD_COL_QKV, col_z=CD_COL_Z)
    lru = _lru(proj, lru_conv_w, lru_conv_b, jnp.stack([wa_f, wa_b]), jnp.stack([ba_f, ba_b]),
               jnp.stack([wx_f, wx_b]), jnp.stack([bx_f, bx_b]), jnp.stack([lam_f, lam_b]),
               col_xl=CD_COL_XL, col_gl=CD_COL_GL)
    out = _outproj(gdn, lru, w_out, x, mod_l, final_w)
    return out, (proj, small_t, gdn, lru)


def kernel(x, c, w_mod, b_mod, norm_w, ab_w_in, ab_q_norm, ab_k_norm, ab_conv_w, ab_conv_b, ab_dt_bias_f, ab_dt_bias_b, ab_a_log_f, ab_a_log_b, ab_d_skip, ab_ssd_norm, ab_w_out, cd_w_in, cd_conv_w, cd_conv_b, cd_a_log_f, cd_a_log_b, cd_dt_bias_f, cd_dt_bias_b, cd_gdn_norm, cd_lru_conv_w, cd_lru_conv_b, cd_lru_wa_f, cd_lru_ba_f, cd_lru_wx_f, cd_lru_bx_f, cd_lru_lam_f, cd_lru_wa_b, cd_lru_ba_b, cd_lru_wx_b, cd_lru_bx_b, cd_lru_lam_b, cd_w_out, final_norm_w):
    mods = _modulation(c, w_mod, b_mod)
    x1, _ = _layer0(x, mods[0], norm_w[0], ab_w_in[0], ab_q_norm[0], ab_k_norm[0], ab_conv_w[0], ab_conv_b[0],
                    ab_dt_bias_f[0], ab_dt_bias_b[0], ab_a_log_f[0], ab_a_log_b[0], ab_d_skip[0],
                    ab_ssd_norm[0], ab_w_out[0])
    out, _ = _layer1(x1, mods[1], norm_w[1], cd_w_in[0], cd_conv_w[0], cd_conv_b[0], cd_a_log_f[0], cd_a_log_b[0],
                     cd_dt_bias_f[0], cd_dt_bias_b[0], cd_gdn_norm[0], cd_lru_conv_w[0], cd_lru_conv_b[0],
                     cd_lru_wa_f[0], cd_lru_ba_f[0], cd_lru_wx_f[0], cd_lru_bx_f[0], cd_lru_lam_f[0],
                     cd_lru_wa_b[0], cd_lru_ba_b[0], cd_lru_wx_b[0], cd_lru_bx_b[0], cd_lru_lam_b[0],
                     cd_w_out[0], final_norm_w)
    return out
```

```python
import functools

import jax
import jax.numpy as jnp
import numpy as np
from jax import lax
from jax.experimental import pallas as pl
from jax.experimental.pallas import tpu as pltpu

F32 = jnp.float32
BF16 = jnp.bfloat16

D_MODEL = 2048
GRID_W = 64
EPS = 1e-6
BRANCH_W = D_MODEL // 2
ATT_HEAD_DIM = 128
ATT_HEADS = BRANCH_W // ATT_HEAD_DIM
ATT_KV_HEADS = ATT_HEADS // 4
ATT_GROUP = ATT_HEADS // ATT_KV_HEADS
ROPE_THETA = 10000.0
SSD_HEAD_DIM = 64
SSD_HEADS = BRANCH_W // SSD_HEAD_DIM
SSD_GROUPS = 2
SSD_STATE = 128
SSD_CONV = 4
SSD_CHUNK = 128
GDN_HEAD_DIM = 128
GDN_V_HEADS = BRANCH_W // GDN_HEAD_DIM
GDN_QK_HEADS = GDN_V_HEADS // 2
GDN_CONV = 4
GDN_CHUNK = 128
GDN_WAVE = 16
LRU_WIDTH = BRANCH_W
LRU_BLOCKS = 8
LRU_CONV = 4
LRU_C = 8.0

LANES = 128
SUBLANES = 8
VMEM_LIMIT_BYTES = 56 * 1024 * 1024

HALO = SUBLANES
NEG_BIG = -1e30
TINY = 1e-37
CAST_ROWS = 256


def _cparams(sem):
    return pltpu.CompilerParams(dimension_semantics=sem, vmem_limit_bytes=VMEM_LIMIT_BYTES)


def _sigmoid(x):
    return 0.5 * jnp.tanh(0.5 * x) + 0.5


def _silu(x):
    h = 0.5 * x
    return h * (jnp.tanh(h) + 1.0)


def _softplus(x):
    return jnp.maximum(x, 0.0) + jnp.log(1.0 + jnp.exp(-jnp.abs(x)))


def _split_bf16(a):
    hi = a.astype(BF16)
    lo = (a - hi.astype(F32)).astype(BF16)
    return hi, lo


def _dot(a, b):
    return jnp.dot(a, b, preferred_element_type=F32)


def _dot_nt(a, b):
    return lax.dot_general(a, b, (((1,), (1,)), ((), ())), preferred_element_type=F32)


def _dot_exact_rhs(a, b_bf16):
    hi, lo = _split_bf16(a)
    return _dot(hi, b_bf16) + _dot(lo, b_bf16)


def _dot_exact_lhs(a_bf16, b):
    hi, lo = _split_bf16(b)
    return _dot(a_bf16, hi) + _dot(a_bf16, lo)


def _mod_kernel(c_ref, w_ref, b_ref, o_ref):
    cond = _silu(c_ref[...])
    c_hi, c_lo = _split_bf16(cond)
    w = w_ref[0]
    w_hi, w_lo = _split_bf16(w)
    o_ref[0] = _dot(c_hi, w_hi) + _dot(c_lo, w_hi) + _dot(c_hi, w_lo) + b_ref[0]


def _modulation(c, w_mod, b_mod):
    depth, d, n = w_mod.shape
    bsz = c.shape[0]
    rows = -(-bsz // SUBLANES) * SUBLANES
    c_pad = jnp.zeros((rows, d), F32).at[:bsz].set(c)
    tn = 1536
    out = pl.pallas_call(
        _mod_kernel,
        out_shape=jax.ShapeDtypeStruct((depth, rows, n), F32),
        grid=(depth, n // tn),
        in_specs=[
            pl.BlockSpec((rows, d), lambda l, j: (0, 0)),
            pl.BlockSpec((1, d, tn), lambda l, j: (l, 0, j)),
            pl.BlockSpec((1, 1, tn), lambda l, j: (l, 0, j)),
        ],
        out_specs=pl.BlockSpec((1, rows, tn), lambda l, j: (l, 0, j)),
        compiler_params=_cparams(("arbitrary", "arbitrary")),
        name="adaln_mod",
    )(c_pad, w_mod, b_mod.reshape(depth, 1, n))
    return out[:, :bsz].reshape(depth, bsz, 3, d)


def _inproj_kernel(x_ref, mod_ref, nw_ref, w_ref, wst_ref, o_ref, ost_ref, h_even, h_odd, *, n_groups):
    g = pl.program_id(0)
    j = pl.program_id(1)
    rows = x_ref.shape[1]

    def prep(h_dst):
        x = x_ref[0]
        ms = jnp.mean(x * x, axis=-1, keepdims=True)
        y = x * lax.rsqrt(ms + EPS) * nw_ref[...]
        h = (y * (1.0 + mod_ref[0, 1:2, :]) + mod_ref[0, 0:1, :]).astype(BF16)
        h_dst[pl.ds(pl.multiple_of(j * rows, rows), rows), :] = h
        ost_ref[0] = _dot_nt(wst_ref[...], h)

    @pl.when(g == 0)
    def _():
        prep(h_even)

    @pl.when((g > 0) & (g % 2 == 0))
    def _():
        o_ref[0] = _dot_nt(h_odd[...], w_ref[...])
        prep(h_even)

    @pl.when(g % 2 == 1)
    def _():
        o_ref[0] = _dot_nt(h_even[...], w_ref[...])
        prep(h_odd)


def _inproj(x, mod_l, norm_w, w_main, w_small, *, tm=1024, nj=4):
    bsz, s, d = x.shape
    n = w_main.shape[0]
    ns = w_small.shape[0]
    tm = min(tm, s)
    ni = s // tm
    n_groups = bsz * ni
    tn = n // nj
    rows = tm // nj
    assert n % nj == 0 and tn % LANES == 0 and rows % LANES == 0

    def prep(g, j):
        gc = jnp.minimum(g, n_groups - 1)
        return gc // ni, (gc % ni) * nj + jnp.where(g < n_groups, j, nj - 1)

    def mm(g, j):
        gm = jnp.maximum(g - 1, 0)
        return gm // ni, gm % ni, jnp.where(g > 0, j, 0)

    return pl.pallas_call(
        functools.partial(_inproj_kernel, n_groups=n_groups),
        out_shape=[jax.ShapeDtypeStruct((bsz, s, n), F32), jax.ShapeDtypeStruct((bsz, ns, s), F32)],
        grid=(n_groups + 1, nj),
        in_specs=[
            pl.BlockSpec((1, rows, d), lambda g, j: (*prep(g, j), 0)),
            pl.BlockSpec((1, 3, d), lambda g, j: (prep(g, j)[0], 0, 0)),
            pl.BlockSpec((1, d), lambda g, j: (0, 0)),
            pl.BlockSpec((tn, d), lambda g, j: (mm(g, j)[2], 0)),
            pl.BlockSpec((ns, d), lambda g, j: (0, 0)),
        ],
        out_specs=[
            pl.BlockSpec((1, tm, tn), lambda g, j: mm(g, j)),
            pl.BlockSpec((1, ns, rows), lambda g, j: (prep(g, j)[0], 0, prep(g, j)[1])),
        ],
        scratch_shapes=[pltpu.VMEM((tm, d), BF16), pltpu.VMEM((tm, d), BF16)],
        compiler_params=_cparams(("arbitrary", "arbitrary")),
        name="norm_mod_inproj",
    )(x, mod_l, norm_w.reshape(1, d), w_main, w_small)


def _outproj_kernel(ya_ref, yb_ref, w_ref, x_ref, mod_ref, *rest, final_norm, half):
    if final_norm:
        fnw_ref, o_ref, w_scr = rest
    else:
        o_ref, w_scr = rest

    @pl.when((pl.program_id(0) == 0) & (pl.program_id(1) == 0))
    def _():
        rows = CAST_ROWS

        def body(r, carry):
            r0 = pl.multiple_of(r * rows, rows)
            w_scr[pl.ds(r0, rows), :] = w_ref[pl.ds(r0, rows), :].astype(BF16)
            return carry

        lax.fori_loop(0, w_ref.shape[0] // rows, body, 0)

    acc = _dot(ya_ref[0], w_scr[0:half, :]) + _dot(yb_ref[0], w_scr[half:, :])
    gate = mod_ref[0, 2:3, :]
    xn = x_ref[0] + gate * acc
    if final_norm:
        ms = jnp.mean(xn * xn, axis=-1, keepdims=True)
        xn = xn * lax.rsqrt(ms + EPS) * fnw_ref[...]
    o_ref[0] = xn


def _outproj(ya, yb, w_out, x, mod_l, final_w=None, *, tm=512):
    bsz, s, d = x.shape
    half = ya.shape[-1]
    tm = min(tm, s)
    final_norm = final_w is not None
    in_specs = [
        pl.BlockSpec((1, tm, half), lambda b, i: (b, i, 0)),
        pl.BlockSpec((1, tm, half), lambda b, i: (b, i, 0)),
        pl.BlockSpec((2 * half, d), lambda b, i: (0, 0), pipeline_mode=pl.Buffered(1)),
        pl.BlockSpec((1, tm, d), lambda b, i: (b, i, 0)),
        pl.BlockSpec((1, 3, d), lambda b, i: (b, 0, 0)),
    ]
    args = [ya, yb, w_out, x, mod_l]
    if final_norm:
        in_specs.append(pl.BlockSpec((1, d), lambda b, i: (0, 0)))
        args.append(final_w.reshape(1, d))
    return pl.pallas_call(
        functools.partial(_outproj_kernel, final_norm=final_norm, half=half),
        out_shape=jax.ShapeDtypeStruct((bsz, s, d), F32),
        grid=(bsz, s // tm),
        in_specs=in_specs,
        out_specs=pl.BlockSpec((1, tm, d), lambda b, i: (b, i, 0)),
        scratch_shapes=[pltpu.VMEM((2 * half, d), BF16)],
        compiler_params=_cparams(("arbitrary", "arbitrary")),
        name="outproj_residual",
    )(*args)


def _rms_rope(x, nw, cos, sin_signed):
    ms = jnp.mean(x * x, axis=-1, keepdims=True)
    y = x * lax.rsqrt(ms + EPS) * nw
    lane = lax.broadcasted_iota(jnp.int32, y.shape, 1)
    partner = jnp.where(lane % 2 == 0, pltpu.roll(y, ATT_HEAD_DIM - 1, 1), pltpu.roll(y, 1, 1))
    return y * cos + partner * sin_signed


def _attention_kv_prep(k_ref, v_ref, ck_ref, sk_ref, kn_ref, k_scr, v_scr):
    dh = ATT_HEAD_DIM
    k_scr[...] = _rms_rope(k_ref[0], kn_ref[...], ck_ref[...], sk_ref[...]).astype(BF16)
    v_scr[:, 0:dh] = v_ref[0].astype(BF16)
    v_scr[:, dh:2 * dh] = jnp.ones((v_scr.shape[0], dh), BF16)


ATTN_AHEAD = 1


def _attention_pump(q_ref, ga_ref, cq_ref, sq_ref, qn_ref, k_scr, v_scr, o_ref, sub):
    dh = ATT_HEAD_DIM
    tq, width = q_ref.shape[1], q_ref.shape[2]
    scale = dh ** -0.5
    probs = [(g, r) for g in range(width // dh) for r in range(0, tq, sub)]
    ahead = ATTN_AHEAD
    state = {"i": 0, "s": {}}

    def scores(g, r):
        qg = q_ref[0, r:r + sub, g * dh:(g + 1) * dh]
        qg = (_rms_rope(qg, qn_ref[...], cq_ref[r:r + sub, :], sq_ref[r:r + sub, :]) * scale).astype(BF16)
        return _dot_nt(qg, k_scr[...])

    def finish(g, r, s):
        s = s.astype(BF16)
        p = jnp.exp(s - jnp.max(s, axis=-1, keepdims=True))
        o_ext = _dot(p, v_scr[...])
        og = o_ext[:, 0:dh] / o_ext[:, dh:2 * dh] * _silu(ga_ref[0, r:r + sub, g * dh:(g + 1) * dh])
        o_ref[0, r:r + sub, g * dh:(g + 1) * dh] = og.astype(BF16)

    def pump():
        i = state["i"]
        if i >= len(probs) + ahead:
            return
        if i < len(probs):
            state["s"][i] = scores(*probs[i])
        if i >= ahead:
            finish(*probs[i - ahead], state["s"].pop(i - ahead))
        state["i"] = i + 1

    return pump, len(probs) + ahead


def _attn_kernel(q_ref, k_ref, v_ref, ga_ref, cq_ref, sq_ref, ck_ref, sk_ref, qn_ref, kn_ref,
                 o_ref, k_scr, v_scr, *, sub):
    @pl.when(pl.program_id(2) == 0)
    def _():
        _attention_kv_prep(k_ref, v_ref, ck_ref, sk_ref, kn_ref, k_scr, v_scr)

    pump, n = _attention_pump(q_ref, ga_ref, cq_ref, sq_ref, qn_ref, k_scr, v_scr, o_ref, sub)
    for _ in range(n):
        pump()


def _attention(proj, cos_t, sin_t, q_norm, k_norm, *, col_q, col_k, col_v, col_ga, tq=512, sub=128):
    bsz, s, _ = proj.shape
    tq = min(tq, s)
    gw = ATT_GROUP * ATT_HEAD_DIM
    dh = ATT_HEAD_DIM
    qb, kb, vb, gb = col_q // gw, col_k // dh, col_v // dh, col_ga // gw
    return pl.pallas_call(
        functools.partial(_attn_kernel, sub=min(sub, tq)),
        out_shape=jax.ShapeDtypeStruct((bsz, s, BRANCH_W), BF16),
        grid=(bsz, ATT_KV_HEADS, s // tq),
        in_specs=[
            pl.BlockSpec((1, tq, gw), lambda b, h, i: (b, i, qb + h)),
            pl.BlockSpec((1, s, dh), lambda b, h, i: (b, 0, kb + h)),
            pl.BlockSpec((1, s, dh), lambda b, h, i: (b, 0, vb + h)),
            pl.BlockSpec((1, tq, gw), lambda b, h, i: (b, i, gb + h)),
            pl.BlockSpec((tq, dh), lambda b, h, i: (i, 0)),
            pl.BlockSpec((tq, dh), lambda b, h, i: (i, 0)),
            pl.BlockSpec((s, dh), lambda b, h, i: (0, 0)),
            pl.BlockSpec((s, dh), lambda b, h, i: (0, 0)),
            pl.BlockSpec((1, dh), lambda b, h, i: (0, 0)),
            pl.BlockSpec((1, dh), lambda b, h, i: (0, 0)),
        ],
        out_specs=pl.BlockSpec((1, tq, gw), lambda b, h, i: (b, i, h)),
        scratch_shapes=[pltpu.VMEM((s, dh), BF16), pltpu.VMEM((s, 2 * dh), BF16)],
        compiler_params=_cparams(("arbitrary", "arbitrary", "arbitrary")),
        name="gqa_attention",
    )(proj, proj, proj, proj, cos_t, sin_t, cos_t, sin_t, q_norm, k_norm)


def _rope_tables(s):
    t = np.arange(s)
    row = (t // GRID_W).astype(np.float64)
    col = (t % GRID_W).astype(np.float64)
    n_pairs = ATT_HEAD_DIM // 4
    freqs = ROPE_THETA ** (-np.arange(n_pairs, dtype=np.float64) / n_pairs)
    ang = np.concatenate([row[:, None] * freqs, col[:, None] * freqs], axis=-1)
    cos, sin = np.cos(ang), np.sin(ang)
    cos_t = np.repeat(cos, 2, axis=-1)
    sin_t = np.stack([-sin, sin], axis=-1).reshape(s, ATT_HEAD_DIM)
    return jnp.asarray(cos_t, F32), jnp.asarray(sin_t, F32)


def _chunk_of_step(step, nc):
    return jnp.where(step < nc, step, 2 * nc - 1 - step)


def _fill_conv_window(ext_scr, cur, prev, nxt, c, nc, rows):
    ext_scr[0:HALO, :] = jnp.where(c > 0, prev, 0.0)
    ext_scr[HALO:HALO + rows, :] = cur
    ext_scr[HALO + rows:HALO + rows + HALO, :] = jnp.where(c < nc - 1, nxt, 0.0)


CONV_SUB = 128


def _centred_conv4(ext_scr, cw_ref, cb_ref, rows):
    sub = min(rows, CONV_SUB)
    n_in = sub + 2 * HALO
    taps = (0, 1, 3)
    r = lax.broadcasted_iota(jnp.int32, (len(taps) * sub, n_in), 0)
    c = lax.broadcasted_iota(jnp.int32, (len(taps) * sub, n_in), 1)
    src = jnp.zeros_like(r)
    for i, k in enumerate(taps):
        src = jnp.where((r >= i * sub) & (r < (i + 1) * sub), r - i * sub + HALO + (k - 2), src)
    shift_mat = jnp.where(c == src, 1.0, 0.0).astype(BF16)
    outs = []
    for s0 in range(0, rows, sub):
        win = ext_scr[s0:s0 + n_in, :]
        shifted = _dot(shift_mat, win.astype(BF16))
        acc = cb_ref[...] + win[HALO:HALO + sub, :] * cw_ref[2:3, :]
        for i, k in enumerate(taps):
            acc = acc + shifted[i * sub:(i + 1) * sub, :] * cw_ref[k:k + 1, :]
        outs.append(acc)
    return outs[0] if len(outs) == 1 else jnp.concatenate(outs, axis=0)


def _scan_masks(fwd, n):
    row = lax.broadcasted_iota(jnp.int32, (n, n), 0)
    col = lax.broadcasted_iota(jnp.int32, (n, n), 1)
    sgn = jnp.where(fwd, 1, -1)
    d = (row - col) * sgn
    return d >= 0, d <= 0, d > 0


def _pad_dir_rows(v, n):
    return jnp.zeros((2, 1, LANES), F32).at[:, 0, :n].set(v)


def _pad_dir_cols(v, n):
    return jnp.zeros((2, LANES, 1), F32).at[:, :n, 0].set(v)


def _head_expander(heads, width):
    r = lax.broadcasted_iota(jnp.int32, (heads, heads * width), 0)
    c = lax.broadcasted_iota(jnp.int32, (heads, heads * width), 1)
    return jnp.where((c >= r * width) & (c < (r + 1) * width), 1.0, 0.0).astype(BF16)


def _ssd_kernel(xf_ref, pf_ref, nf_ref, xb_ref, pb_ref, nb_ref, z_ref, dtf_ref, dtb_ref, cw_ref, cb_ref,
                bias_r_ref, bias_c_ref, alog_r_ref, alog_c_ref, dskip_ref, nw_ref,
                o_ref, ext_scr, state_scr, acc_scr, *, nc, te):
    L, H, P, N = SSD_CHUNK, SSD_HEADS, SSD_HEAD_DIM, SSD_STATE
    HG = H // SSD_GROUPS
    GW = HG * P
    t = pl.program_id(1)

    @pl.when(t == 0)
    def _():
        state_scr[...] = jnp.zeros_like(state_scr)

    @pl.when(t < nc)
    def _():
        chunks = (t, nc - 1 - t)
        blocks = ((xf_ref, pf_ref, nf_ref, dtf_ref), (xb_ref, pb_ref, nb_ref, dtb_ref))
        expand = _head_expander(H, P)
        lane = lax.broadcasted_iota(jnp.int32, (L, 2 * P), 1)
        per_dir = []
        for d in range(2):
            x_ref, p_ref, n_ref, dtt_ref = blocks[d]
            ext = ext_scr.at[d]
            _fill_conv_window(ext, x_ref[0], p_ref[0], n_ref[0], chunks[d], nc, L)
            xbc = _silu(_centred_conv4(ext, cw_ref, cb_ref, L))
            xs = xbc[:, :BRANCH_W]
            bs = xbc[:, BRANCH_W:BRANCH_W + SSD_GROUPS * N]
            cs = xbc[:, BRANCH_W + SSD_GROUPS * N:]
            raw_t = dtt_ref[0]
            raw = raw_t.T
            dt = _softplus(raw[:, d * H:(d + 1) * H] + bias_r_ref[d][:, 0:H])
            a = dt * (-jnp.exp(alog_r_ref[d][:, 0:H]))
            dt_t = _softplus(raw_t[d * H:(d + 1) * H, :] + bias_c_ref[d][0:H, :])
            a_t = dt_t * (-jnp.exp(alog_c_ref[d][0:H, :]))
            mask, mask_t, _ = _scan_masks(d == 0, L)
            cum = _dot_exact_lhs(jnp.where(mask, 1.0, 0.0).astype(BF16), a)
            cum_t = _dot_exact_rhs(a_t, jnp.where(mask_t, 1.0, 0.0).astype(BF16))
            total = jnp.sum(a, axis=0, keepdims=True)
            dt_e = _dot(dt.astype(BF16), expand)
            p_e = _dot(jnp.exp(cum).astype(BF16), expand)
            q_e = _dot(jnp.exp(total - cum).astype(BF16), expand)
            tot_e = _dot_exact_rhs(jnp.broadcast_to(jnp.exp(total), (SUBLANES, H)), expand)[0:1, :]
            xd = xs * dt_e
            per_dir.append(dict(xs=xs, bs=bs, cs=cs, mask=mask, cum=cum, cum_t=cum_t, p_e=p_e, tot_e=tot_e,
                                xdq=(xd * q_e).astype(BF16), xd_b=xd.astype(BF16)))

        combos = [(d, g) for d in range(2) for g in range(SSD_GROUPS)]
        cg = {k: per_dir[k[0]]["cs"][:, k[1] * N:(k[1] + 1) * N].astype(BF16) for k in combos}
        bg = {k: per_dir[k[0]]["bs"][:, k[1] * N:(k[1] + 1) * N] for k in combos}
        gmat = {k: _dot_nt(cg[k], bg[k].astype(BF16)) for k in combos}
        h_prev = {k: state_scr[k[0], :, k[1] * GW:(k[1] + 1) * GW] for k in combos}
        y_off = {k: _dot(cg[k], h_prev[k].astype(BF16)) * per_dir[k[0]]["p_e"][:, k[1] * GW:(k[1] + 1) * GW]
                 for k in combos}
        pairs = {k: [] for k in combos}
        for hp in range(HG // 2):
            for k in combos:
                d, g = k
                pd = per_dir[d]
                h0 = g * HG + 2 * hp
                xpair = pd["xd_b"][:, h0 * P:(h0 + 2) * P]
                ys = []
                for h in (h0, h0 + 1):
                    dec = jnp.exp(jnp.where(pd["mask"], pd["cum"][:, h:h + 1] - pd["cum_t"][h:h + 1, :], NEG_BIG))
                    ys.append(_dot((gmat[k] * dec).astype(BF16), xpair))
                pairs[k].append(jnp.where(lane < P, ys[0], ys[1]))
        for k in combos:
            d, g = k
            state_scr[d, :, g * GW:(g + 1) * GW] = (
                h_prev[k] * per_dir[d]["tot_e"][:, g * GW:(g + 1) * GW]
                + _dot(bg[k].T.astype(BF16), per_dir[d]["xdq"][:, g * GW:(g + 1) * GW]))
        y_dirs = [jnp.concatenate([jnp.concatenate(pairs[(d, g)], axis=1) + y_off[(d, g)]
                                   for g in range(SSD_GROUPS)], axis=1) for d in range(2)]
        y_dirs[0] = y_dirs[0] + dskip_ref[...] * per_dir[0]["xs"]
        rows = [pl.multiple_of(c * L, L) for c in chunks]

        @pl.when(t < nc // 2)
        def _():
            for d in range(2):
                acc_scr[pl.ds(rows[d], L), :] = y_dirs[d]

        @pl.when(t >= nc // 2)
        def _():
            for d in range(2):
                acc_scr[pl.ds(rows[d], L), :] = acc_scr[pl.ds(rows[d], L), :] + y_dirs[d]

    @pl.when(t >= nc)
    def _():
        r0 = pl.multiple_of((t - nc) * te, te)
        y = acc_scr[pl.ds(r0, te), :] * _silu(z_ref[0])
        ms = jnp.mean(y * y, axis=-1, keepdims=True)
        o_ref[0] = (y * lax.rsqrt(ms + EPS) * nw_ref[...]).astype(BF16)


def _ssd(proj, small_t, conv_w, conv_b, dt_bias, a_log, d_skip, norm_w, *, col_xbc, col_z, te=512):
    bsz, s, _ = proj.shape
    L = SSD_CHUNK
    nc = s // L
    te = min(te, s)
    ne = s // te
    cw = BRANCH_W + 2 * SSD_GROUPS * SSD_STATE
    hb = L // HALO
    xb, zb = col_xbc // cw, col_z // BRANCH_W
    assert col_xbc % cw == 0 and col_z % BRANCH_W == 0 and nc % 2 == 0

    def cf(t):
        return jnp.minimum(t, nc - 1)

    def cbk(t):
        return jnp.maximum(nc - 1 - t, 0)

    def ep(t):
        return jnp.maximum(t - nc, 0)

    def xbc_specs(chunk):
        return [
            pl.BlockSpec((1, L, cw), lambda b, t: (b, chunk(t), xb)),
            pl.BlockSpec((1, HALO, cw), lambda b, t: (b, jnp.maximum(chunk(t) * hb - 1, 0), xb)),
            pl.BlockSpec((1, HALO, cw), lambda b, t: (b, jnp.minimum((chunk(t) + 1) * hb, s // HALO - 1), xb)),
        ]

    def small_spec(chunk):
        return pl.BlockSpec((1, LANES, L), lambda b, t: (b, 0, chunk(t)))

    full = lambda shape: pl.BlockSpec(shape, lambda b, t: (0,) * len(shape))
    in_specs = (xbc_specs(cf) + xbc_specs(cbk)
                + [pl.BlockSpec((1, te, BRANCH_W), lambda b, t: (b, ep(t), zb))]
                + [small_spec(cf), small_spec(cbk)]
                + [full((SSD_CONV, cw)), full((1, cw)), full((2, 1, LANES)), full((2, LANES, 1)),
                   full((2, 1, LANES)), full((2, LANES, 1)), full((1, BRANCH_W)), full((1, BRANCH_W))])
    args = [proj, proj, proj, proj, proj, proj, proj, small_t, small_t, conv_w, conv_b.reshape(1, cw),
            _pad_dir_rows(dt_bias, SSD_HEADS), _pad_dir_cols(dt_bias, SSD_HEADS),
            _pad_dir_rows(a_log, SSD_HEADS), _pad_dir_cols(a_log, SSD_HEADS),
            jnp.repeat(d_skip, SSD_HEAD_DIM).reshape(1, BRANCH_W), norm_w.reshape(1, BRANCH_W)]
    out_shape = [jax.ShapeDtypeStruct((bsz, s, BRANCH_W), BF16)]
    out_specs = [pl.BlockSpec((1, te, BRANCH_W), lambda b, t: (b, ep(t), 0))]
    scratch = [
        pltpu.VMEM((2, L + 2 * HALO, cw), F32),
        pltpu.VMEM((2, SSD_STATE, BRANCH_W), F32),
        pltpu.VMEM((s, BRANCH_W), F32),
    ]
    return pl.pallas_call(
        functools.partial(_ssd_kernel, nc=nc, te=te),
        out_shape=out_shape,
        grid=(bsz, nc + ne),
        in_specs=in_specs,
        out_specs=out_specs,
        scratch_shapes=scratch,
        compiler_params=_cparams(("arbitrary", "arbitrary")),
        name="bidir_ssd",
    )(*args)[0]


AB_COL_XBC, AB_COL_K, AB_COL_V, AB_COL_Z, AB_COL_Q, AB_COL_GA, AB_N = 0, 1536, 1792, 2048, 3072, 4096, 5120


REGROUP_ROWS = 256


def _regroup_kernel(offs_ref, w_ref, o_ref):
    del offs_ref
    o_ref[...] = w_ref[...].astype(BF16)


def _regroup_cast(wt, pieces):
    rows = REGROUP_ROWS
    offsets = []
    for start, n in pieces:
        assert n % rows == 0 and start % SUBLANES == 0
        offsets += list(range(start, start + n, rows))
    d = wt.shape[1]
    return pl.pallas_call(
        _regroup_kernel,
        out_shape=jax.ShapeDtypeStruct((len(offsets) * rows, d), BF16),
        grid_spec=pltpu.PrefetchScalarGridSpec(
            num_scalar_prefetch=1,
            grid=(len(offsets),),
            in_specs=[pl.BlockSpec((pl.Element(rows), pl.Element(d)), lambda i, offs: (pl.multiple_of(offs[i], SUBLANES), 0))],
            out_specs=pl.BlockSpec((rows, d), lambda i, offs: (i, 0)),
        ),
        compiler_params=_cparams(("arbitrary",)),
        name="weight_regroup_cast",
    )(jnp.asarray(offsets, jnp.int32), wt)


def _piece_table(sizes):
    table, o = {}, 0
    for name, n in sizes:
        table[name] = (o, n)
        o += n
    return table


def _ab_weights(w_in):
    wt = w_in.T
    hq, hk = ATT_HEADS * ATT_HEAD_DIM, ATT_KV_HEADS * ATT_HEAD_DIM
    gn = SSD_GROUPS * SSD_STATE
    p = _piece_table([("q", hq), ("k", hk), ("v", hk), ("ga", BRANCH_W), ("xs", BRANCH_W), ("bs", gn),
                      ("cs", gn), ("dtf", SSD_HEADS), ("dtb", SSD_HEADS), ("z", BRANCH_W)])
    main = _regroup_cast(wt, [p[n] for n in ("xs", "bs", "cs", "k", "v", "z", "q", "ga")])
    dt0, dtn = p["dtf"][0], 2 * SSD_HEADS
    zpad = jnp.zeros((LANES - dtn, wt.shape[1]), wt.dtype)
    small = jnp.concatenate([wt[dt0:dt0 + dtn], zpad], axis=0).astype(BF16)
    return main, small


def _layer0(x, mod_l, norm_w, w_in, q_norm, k_norm, conv_w, conv_b, dt_bias_f, dt_bias_b,
            a_log_f, a_log_b, d_skip, ssd_norm, w_out, final_w=None):
    s = x.shape[1]
    w_main, w_small = _ab_weights(w_in)
    proj, small_t = _inproj(x, mod_l, norm_w, w_main, w_small)
    cos_t, sin_t = _rope_tables(s)
    att = _attention(proj, cos_t, sin_t, q_norm.reshape(1, ATT_HEAD_DIM), k_norm.reshape(1, ATT_HEAD_DIM),
                     col_q=AB_COL_Q, col_k=AB_COL_K, col_v=AB_COL_V, col_ga=AB_COL_GA)
    ssd = _ssd(proj, small_t, conv_w, conv_b, jnp.stack([dt_bias_f, dt_bias_b]),
               jnp.stack([a_log_f, a_log_b]), d_skip, ssd_norm, col_xbc=AB_COL_XBC, col_z=AB_COL_Z)
    return _outproj(att, ssd, w_out, x, mod_l, final_w), (proj, small_t, att, ssd)


def _unit_tri_inverse(nmats, n):
    row = lax.broadcasted_iota(jnp.int32, (n, n), 0)
    col = lax.broadcasted_iota(jnp.int32, (n, n), 1)

    def same_block(size):
        return (row // size) == (col // size)

    def mm(a, b):
        return _dot(a, b).astype(BF16)

    def as_mask(cond):
        return jnp.where(cond, 1.0, 0.0).astype(BF16)

    nmats = [m.astype(BF16) for m in nmats]
    eye = as_mask(row == col)
    base = SUBLANES
    blk = same_block(base)
    blk_m = as_mask(blk)
    nd = [m * blk_m for m in nmats]
    p1 = [mm(x, x) for x in nd]
    p2 = [mm(x, x) for x in p1]
    t = [eye - x for x in nd]
    t = [x + mm(x, p) for x, p in zip(t, p1)]
    t = [x + mm(x, p) for x, p in zip(t, p2)]
    size = base
    while size < n:
        nxt = same_block(2 * size)
        off_m = as_mask(nxt & jnp.logical_not(blk))
        et = [mm(m * off_m, x) for m, x in zip(nmats, t)]
        t = [x - mm(x, y) for x, y in zip(t, et)]
        blk = nxt
        size *= 2
    return t


def _l2norm(x):
    return x * lax.rsqrt(jnp.sum(x * x, axis=-1, keepdims=True) + EPS)


def _gdn_kernel(qf_ref, pf_ref, nf_ref, qb_ref, pb_ref, nb_ref, z_ref, smtf_ref, smtb_ref,
                cw_ref, cb_ref, bias_r_ref, bias_c_ref, alog_r_ref, alog_c_ref, nw_ref,
                o_ref, ext_scr, state_scr, acc_scr, *, nc, te):
    L, HV, HQ, DK = GDN_CHUNK, GDN_V_HEADS, GDN_QK_HEADS, GDN_HEAD_DIM
    rep = HV // HQ
    nbt = acc_scr.shape[0]
    t = pl.program_id(1)

    @pl.when(t == 0)
    def _():
        state_scr[...] = jnp.zeros_like(state_scr)

    @pl.when(t < nc)
    def _():
        chunks = (t, nc - 1 - t)
        blocks = ((qf_ref, pf_ref, nf_ref, smtf_ref), (qb_ref, pb_ref, nb_ref, smtb_ref))
        kk, qk, qh, kh, vh, colv, dec, bcol, ecol, tot, nmat = ([] for _ in range(11))
        for bi, d in [(bi, d) for bi in range(nbt) for d in range(2)]:
            c = chunks[d]
            q_ref, p_ref, n_ref, smt_ref = blocks[d]
            ext = ext_scr.at[bi * 2 + d]
            _fill_conv_window(ext, q_ref[bi], p_ref[bi], n_ref[bi], c, nc, L)
            act = _silu(_centred_conv4(ext, cw_ref, cb_ref, L))
            q_n = [_l2norm(act[:, h * DK:(h + 1) * DK]) * (DK ** -0.5) for h in range(HQ)]
            k_n = [_l2norm(act[:, (HQ + h) * DK:(HQ + h + 1) * DK]) for h in range(HQ)]
            k_b = [x.astype(BF16) for x in k_n]
            kk_d = [_dot_nt(k_b[h], k_b[h]) for h in range(HQ)]
            qk_d = [_dot_nt(q_n[h].astype(BF16), k_b[h]) for h in range(HQ)]

            raw_t = smt_ref[bi]
            raw = raw_t.T
            a_raw = raw[:, (2 + d) * HV:(3 + d) * HV]
            a_raw_t = raw_t[(2 + d) * HV:(3 + d) * HV, :]
            beta = _sigmoid(raw[:, d * HV:(d + 1) * HV])
            g = -jnp.exp(alog_r_ref[d][:, 0:HV]) * _softplus(a_raw + bias_r_ref[d][:, 0:HV])
            g_t = -jnp.exp(alog_c_ref[d][0:HV, :]) * _softplus(a_raw_t + bias_c_ref[d][0:HV, :])
            mask, mask_t, strict = _scan_masks(d == 0, L)
            cum = _dot_exact_lhs(jnp.where(mask, 1.0, 0.0).astype(BF16), g)
            cum_t = _dot_exact_rhs(g_t, jnp.where(mask_t, 1.0, 0.0).astype(BF16))
            total = jnp.sum(g, axis=0, keepdims=True)
            for h in range(HV):
                cv = cum[:, h:h + 1]
                dc = jnp.exp(jnp.where(mask, cv - cum_t[h:h + 1, :], NEG_BIG))
                bc = beta[:, h:h + 1]
                kk.append(kk_d[h // rep]); qk.append(qk_d[h // rep])
                qh.append(q_n[h // rep]); kh.append(k_n[h // rep])
                vh.append(act[:, (2 * HQ + h) * DK:(2 * HQ + h + 1) * DK])
                colv.append(cv); dec.append(dc); bcol.append(bc); ecol.append(jnp.exp(cv))
                tot.append(total[:, h:h + 1])
                nmat.append(jnp.where(strict, kk_d[h // rep] * bc * dc, 0.0))

        outs = []
        for w0 in range(0, nbt * 2 * HV, GDN_WAVE):
            idx = range(w0, w0 + GDN_WAVE)
            t_inv = dict(zip(idx, _unit_tri_inverse([nmat[i] for i in idx], L)))
            u = {i: _dot(t_inv[i], (vh[i] * bcol[i]).astype(BF16)) for i in idx}
            w = {i: _dot(t_inv[i], (kh[i] * (bcol[i] * ecol[i])).astype(BF16)) for i in idx}
            s_prev = {i: state_scr[i // HV, :, (i % HV) * DK:(i % HV + 1) * DK] for i in idx}
            s_b = {i: s_prev[i].astype(BF16) for i in idx}
            v_new = {i: (u[i] - _dot(w[i].astype(BF16), s_b[i])).astype(BF16) for i in idx}
            outs += [_dot((qh[i] * ecol[i]).astype(BF16), s_b[i])
                     + _dot((qk[i] * dec[i]).astype(BF16), v_new[i]) for i in idx]
            for i in idx:
                k_dec = kh[i] * jnp.exp(tot[i] - colv[i])
                state_scr[i // HV, :, (i % HV) * DK:(i % HV + 1) * DK] = (
                    s_prev[i] * jnp.exp(tot[i]) + _dot(k_dec.T.astype(BF16), v_new[i]))
        o_dirs = [jnp.concatenate(outs[j * HV:(j + 1) * HV], axis=1) for j in range(nbt * 2)]
        rows = [pl.multiple_of(c * L, L) for c in chunks]

        @pl.when(t < nc // 2)
        def _():
            for bi in range(nbt):
                for d in range(2):
                    acc_scr[bi, pl.ds(rows[d], L), :] = o_dirs[bi * 2 + d]

        @pl.when(t >= nc // 2)
        def _():
            for bi in range(nbt):
                for d in range(2):
                    acc_scr[bi, pl.ds(rows[d], L), :] = acc_scr[bi, pl.ds(rows[d], L), :] + o_dirs[bi * 2 + d]

    @pl.when(t >= nc)
    def _():
        r0 = pl.multiple_of((t - nc) * te, te)
        for bi in range(nbt):
            zz = z_ref[bi]
            for hv in range(HV):
                oh = acc_scr[bi, pl.ds(r0, te), hv * DK:(hv + 1) * DK]
                ms = jnp.mean(oh * oh, axis=-1, keepdims=True)
                res = oh * lax.rsqrt(ms + EPS) * nw_ref[...] * _silu(zz[:, hv * DK:(hv + 1) * DK])
                o_ref[bi, :, hv * DK:(hv + 1) * DK] = res.astype(BF16)


def _gdn(proj, small_t, conv_w, conv_b, dt_bias, a_log, norm_w, *, col_qkv, col_z, te=512, nbt=1):
    bsz, s, _ = proj.shape
    L = GDN_CHUNK
    nc = s // L
    te = min(te, s)
    ne = s // te
    cw = 2 * GDN_QK_HEADS * GDN_HEAD_DIM + BRANCH_W
    hb = L // HALO
    qb, zb = col_qkv // cw, col_z // BRANCH_W
    assert col_qkv % cw == 0 and col_z % BRANCH_W == 0 and nc % 2 == 0 and bsz % nbt == 0

    def cf(t):
        return jnp.minimum(t, nc - 1)

    def cbk(t):
        return jnp.maximum(nc - 1 - t, 0)

    def ep(t):
        return jnp.maximum(t - nc, 0)

    def qkv_specs(chunk):
        return [
            pl.BlockSpec((nbt, L, cw), lambda b, t: (b, chunk(t), qb)),
            pl.BlockSpec((nbt, HALO, cw), lambda b, t: (b, jnp.maximum(chunk(t) * hb - 1, 0), qb)),
            pl.BlockSpec((nbt, HALO, cw), lambda b, t: (b, jnp.minimum((chunk(t) + 1) * hb, s // HALO - 1), qb)),
        ]

    def small_spec(chunk):
        return pl.BlockSpec((nbt, LANES, L), lambda b, t: (b, 0, chunk(t)))

    full = lambda shape: pl.BlockSpec(shape, lambda b, t: (0,) * len(shape))
    hv = GDN_V_HEADS
    return pl.pallas_call(
        functools.partial(_gdn_kernel, nc=nc, te=te),
        out_shape=jax.ShapeDtypeStruct((bsz, s, BRANCH_W), BF16),
        grid=(bsz // nbt, nc + ne),
        in_specs=qkv_specs(cf) + qkv_specs(cbk)
        + [pl.BlockSpec((nbt, te, BRANCH_W), lambda b, t: (b, ep(t), zb))]
        + [small_spec(cf), small_spec(cbk)]
        + [full((GDN_CONV, cw)), full((1, cw)), full((2, 1, LANES)), full((2, LANES, 1)),
           full((2, 1, LANES)), full((2, LANES, 1)), full((1, GDN_HEAD_DIM))],
        out_specs=pl.BlockSpec((nbt, te, BRANCH_W), lambda b, t: (b, ep(t), 0)),
        scratch_shapes=[
            pltpu.VMEM((nbt * 2, L + 2 * HALO, cw), F32),
            pltpu.VMEM((nbt * 2, GDN_HEAD_DIM, BRANCH_W), F32),
            pltpu.VMEM((nbt, s, BRANCH_W), F32),
        ],
        compiler_params=_cparams(("arbitrary", "arbitrary")),
        name="bidir_gated_deltanet",
    )(proj, proj, proj, proj, proj, proj, proj, small_t, small_t, conv_w, conv_b.reshape(1, cw),
      _pad_dir_rows(dt_bias, hv), _pad_dir_cols(dt_bias, hv), _pad_dir_rows(a_log, hv), _pad_dir_cols(a_log, hv),
      norm_w.reshape(1, GDN_HEAD_DIM))


def _local_scan(a, u, ascending):
    rows, width = a.shape
    a = a.reshape(rows // SUBLANES, SUBLANES, width)
    u = u.reshape(rows // SUBLANES, SUBLANES, width)
    sub = lax.broadcasted_iota(jnp.int32, a.shape, 1)
    d = 1
    while d < SUBLANES:
        if ascending:
            keep = sub >= d
            shift = d
        else:
            keep = sub < SUBLANES - d
            shift = SUBLANES - d
        a_sh = jnp.where(keep, pltpu.roll(a, shift, 1), 1.0)
        u_sh = jnp.where(keep, pltpu.roll(u, shift, 1), 0.0)
        u = u + a * u_sh
        a = a * a_sh
        d *= 2
    return a.reshape(rows, width), u.reshape(rows, width)


def _lru_kernel(xl_ref, prev_ref, next_ref, gl_ref, cw_ref, cb_ref, wa_ref, ba_ref, wx_ref, bx_ref, lam_ref,
                o_ref, ext_scr, a_scr, u_scr, hf_scr, carry_scr, *, nb, unroll):
    rows, width = a_scr.shape
    bw = LRU_WIDTH // LRU_BLOCKS
    step = pl.program_id(2)
    fwd = step < nb
    blk = _chunk_of_step(step, nb)

    @pl.when((step == 0) | (step == nb))
    def _():
        carry_scr[...] = jnp.zeros_like(carry_scr)

    _fill_conv_window(ext_scr, xl_ref[0], prev_ref[0], next_ref[0], blk, nb, rows)
    xc = _centred_conv4(ext_scr, cw_ref, cb_ref, rows)
    r_parts, i_parts = [], []
    for n in range(width // bw):
        xb = xc[:, n * bw:(n + 1) * bw].astype(BF16)
        r_parts.append(_dot(xb, wa_ref[0, n]))
        i_parts.append(_dot(xb, wx_ref[0, n]))
    tr = jnp.tanh(jnp.concatenate(r_parts, axis=1) + ba_ref[0])
    ti = jnp.tanh(jnp.concatenate(i_parts, axis=1) + bx_ref[0])
    log_a = ((-0.5 * LRU_C) * _softplus(-lam_ref[0])) * (tr + 1.0)
    a = jnp.exp(log_a)
    quarter = (-0.25 * jnp.tanh(log_a)) * (a * a + 1.0)
    u = (quarter * lax.rsqrt(jnp.maximum(quarter, TINY))) * ((ti + 1.0) * xc)

    n_groups = rows // SUBLANES
    base_out = pl.multiple_of(blk * rows, rows)

    def scan(ascending):
        a_loc, u_loc = _local_scan(a, u, ascending)
        a_scr[...] = a_loc
        u_scr[...] = u_loc
        last = SUBLANES - 1 if ascending else 0

        def body(it, carry):
            pos = it if ascending else n_groups - 1 - it
            r0 = pl.multiple_of(pos * SUBLANES, SUBLANES)
            h = u_scr[pl.ds(r0, SUBLANES), :] + a_scr[pl.ds(r0, SUBLANES), :] * carry
            u_scr[pl.ds(r0, SUBLANES), :] = h
            return jnp.broadcast_to(h[last:last + 1, :], h.shape)

        carry_scr[...] = lax.fori_loop(0, n_groups, body, carry_scr[...], unroll=unroll)

    @pl.when(fwd)
    def _():
        scan(True)
        hf_scr[pl.ds(base_out, rows), :] = u_scr[...]

    @pl.when(jnp.logical_not(fwd))
    def _():
        scan(False)
        o_ref[0] = ((hf_scr[pl.ds(base_out, rows), :] + u_scr[...]) * _silu(gl_ref[0])).astype(BF16)


def _lru(proj, conv_w, conv_b, wa, ba, wx, bx, lam, *, col_xl, col_gl, tt=1024, width=1024, unroll=8):
    bsz, s, _ = proj.shape
    tt = min(tt, s)
    nb = s // tt
    w_total = LRU_WIDTH
    bw = w_total // LRU_BLOCKS
    nbw = width // bw
    xcol, gcol = col_xl // width, col_gl // width
    assert col_xl % width == 0 and col_gl % width == 0
    hb = tt // HALO

    def blk(t):
        return _chunk_of_step(t, nb)

    def direction(t):
        return jnp.where(t < nb, 0, 1)

    return pl.pallas_call(
        functools.partial(_lru_kernel, nb=nb, unroll=unroll),
        out_shape=jax.ShapeDtypeStruct((bsz, s, w_total), BF16),
        grid=(bsz, w_total // width, 2 * nb),
        in_specs=[
            pl.BlockSpec((1, tt, width), lambda b, j, t: (b, blk(t), xcol + j)),
            pl.BlockSpec((1, HALO, width), lambda b, j, t: (b, jnp.maximum(blk(t) * hb - 1, 0), xcol + j)),
            pl.BlockSpec((1, HALO, width),
                         lambda b, j, t: (b, jnp.minimum((blk(t) + 1) * hb, s // HALO - 1), xcol + j)),
            pl.BlockSpec((1, tt, width), lambda b, j, t: (b, blk(t), gcol + j)),
            pl.BlockSpec((LRU_CONV, width), lambda b, j, t: (0, j)),
            pl.BlockSpec((1, width), lambda b, j, t: (0, j)),
            pl.BlockSpec((1, nbw, bw, bw), lambda b, j, t: (direction(t), j, 0, 0)),
            pl.BlockSpec((1, 1, width), lambda b, j, t: (direction(t), 0, j)),
            pl.BlockSpec((1, nbw, bw, bw), lambda b, j, t: (direction(t), j, 0, 0)),
            pl.BlockSpec((1, 1, width), lambda b, j, t: (direction(t), 0, j)),
            pl.BlockSpec((1, 1, width), lambda b, j, t: (direction(t), 0, j)),
        ],
        out_specs=pl.BlockSpec((1, tt, width),
                               lambda b, j, t: (b, jnp.where(t < nb, nb - 1, 2 * nb - 1 - t), j)),
        scratch_shapes=[
            pltpu.VMEM((tt + 2 * HALO, width), F32),
            pltpu.VMEM((tt, width), F32),
            pltpu.VMEM((tt, width), F32),
            pltpu.VMEM((s, width), F32),
            pltpu.VMEM((SUBLANES, width), F32),
        ],
        compiler_params=_cparams(("arbitrary", "arbitrary", "arbitrary")),
        name="bidir_rglru",
    )(proj, proj, proj, proj, conv_w, conv_b.reshape(1, w_total),
      (0.5 * wa).astype(BF16), (0.5 * ba).reshape(2, 1, w_total),
      (0.5 * wx).astype(BF16), (0.5 * bx).reshape(2, 1, w_total), lam.reshape(2, 1, w_total))


CD_COL_QKV, CD_COL_Z, CD_COL_XL, CD_COL_GL = 0, 2048, 3072, 4096


def _cd_weights(w_in):
    wt = w_in.T
    nqk = GDN_QK_HEADS * GDN_HEAD_DIM
    hv = GDN_V_HEADS
    p = _piece_table([("q", nqk), ("k", nqk), ("v", BRANCH_W), ("bf", hv), ("bb", hv), ("af", hv), ("ab", hv),
                      ("z", BRANCH_W), ("xl", LRU_WIDTH), ("gl", LRU_WIDTH)])
    main = _regroup_cast(wt, [p[n] for n in ("q", "k", "v", "z", "xl", "gl")])
    s0 = p["bf"][0]
    zpad = jnp.zeros((LANES - 4 * hv, wt.shape[1]), wt.dtype)
    small = jnp.concatenate([wt[s0:s0 + 4 * hv], zpad], axis=0).astype(BF16)
    return main, small


def _layer1(x, mod_l, norm_w, w_in, conv_w, conv_b, a_log_f, a_log_b, dt_bias_f, dt_bias_b, gdn_norm,
            lru_conv_w, lru_conv_b, wa_f, ba_f, wx_f, bx_f, lam_f, wa_b, ba_b, wx_b, bx_b, lam_b, w_out,
            final_w=None):
    w_main, w_small = _cd_weights(w_in)
    proj, small_t = _inproj(x, mod_l, norm_w, w_main, w_small)
    gdn = _gdn(proj, small_t, conv_w, conv_b, jnp.stack([dt_bias_f, dt_bias_b]),
               jnp.stack([a_log_f, a_log_b]), gdn_norm, col_qkv=CD_COL_QKV, col_z=CD_COL_Z)
    lru = _lru(proj, lru_conv_w, lru_conv_b, jnp.stack([wa_f, wa_b]), jnp.stack([ba_f, ba_b]),
               jnp.stack([wx_f, wx_b]), jnp.stack([bx_f, bx_b]), jnp.stack([lam_f, lam_b]),
               col_xl=CD_COL_XL, col_gl=CD_COL_GL)
    out = _outproj(gdn, lru, w_out, x, mod_l, final_w)
    return out, (proj, small_t, gdn, lru)


def kernel(x, c, w_mod, b_mod, norm_w, ab_w_in, ab_q_norm, ab_k_norm, ab_conv_w, ab_conv_b, ab_dt_bias_f, ab_dt_bias_b, ab_a_log_f, ab_a_log_b, ab_d_skip, ab_ssd_norm, ab_w_out, cd_w_in, cd_conv_w, cd_conv_b, cd_a_log_f, cd_a_log_b, cd_dt_bias_f, cd_dt_bias_b, cd_gdn_norm, cd_lru_conv_w, cd_lru_conv_b, cd_lru_wa_f, cd_lru_ba_f, cd_lru_wx_f, cd_lru_bx_f, cd_lru_lam_f, cd_lru_wa_b, cd_lru_ba_b, cd_lru_wx_b, cd_lru_bx_b, cd_lru_lam_b, cd_w_out, final_norm_w):
    mods = _modulation(c, w_mod, b_mod)
    x1, _ = _layer0(x, mods[0], norm_w[0], ab_w_in[0], ab_q_norm[0], ab_k_norm[0], ab_conv_w[0], ab_conv_b[0],
                    ab_dt_bias_f[0], ab_dt_bias_b[0], ab_a_log_f[0], ab_a_log_b[0], ab_d_skip[0],
                    ab_ssd_norm[0], ab_w_out[0])
    out, _ = _layer1(x1, mods[1], norm_w[1], cd_w_in[0], cd_conv_w[0], cd_conv_b[0], cd_a_log_f[0], cd_a_log_b[0],
                     cd_dt_bias_f[0], cd_dt_bias_b[0], cd_gdn_norm[0], cd_lru_conv_w[0], cd_lru_conv_b[0],
                     cd_lru_wa_f[0], cd_lru_ba_f[0], cd_lru_wx_f[0], cd_lru_bx_f[0], cd_lru_lam_f[0],
                     cd_lru_wa_b[0], cd_lru_ba_b[0], cd_lru_wx_b[0], cd_lru_bx_b[0], cd_lru_lam_b[0],
                     cd_w_out[0], final_norm_w)
    return out
```

```python
import functools

import jax
import jax.numpy as jnp
import numpy as np
from jax import lax
from jax.experimental import pallas as pl
from jax.experimental.pallas import tpu as pltpu

F32 = jnp.float32
BF16 = jnp.bfloat16

D_MODEL = 2048
GRID_W = 64
EPS = 1e-6
BRANCH_W = D_MODEL // 2
ATT_HEAD_DIM = 128
ATT_HEADS = BRANCH_W // ATT_HEAD_DIM
ATT_KV_HEADS = ATT_HEADS // 4
ATT_GROUP = ATT_HEADS // ATT_KV_HEADS
ROPE_THETA = 10000.0
SSD_HEAD_DIM = 64
SSD_HEADS = BRANCH_W // SSD_HEAD_DIM
SSD_GROUPS = 2
SSD_STATE = 128
SSD_CONV = 4
SSD_CHUNK = 128
GDN_HEAD_DIM = 128
GDN_V_HEADS = BRANCH_W // GDN_HEAD_DIM
GDN_QK_HEADS = GDN_V_HEADS // 2
GDN_CONV = 4
GDN_CHUNK = 128
GDN_WAVE = 16
LRU_WIDTH = BRANCH_W
LRU_BLOCKS = 8
LRU_CONV = 4
LRU_C = 8.0

LANES = 128
SUBLANES = 8
VMEM_LIMIT_BYTES = 56 * 1024 * 1024

HALO = SUBLANES
NEG_BIG = -1e30
TINY = 1e-37
CAST_ROWS = 256


def _cparams(sem):
    return pltpu.CompilerParams(dimension_semantics=sem, vmem_limit_bytes=VMEM_LIMIT_BYTES)


def _sigmoid(x):
    return 0.5 * jnp.tanh(0.5 * x) + 0.5


def _silu(x):
    h = 0.5 * x
    return h * (jnp.tanh(h) + 1.0)


def _softplus(x):
    return jnp.maximum(x, 0.0) + jnp.log(1.0 + jnp.exp(-jnp.abs(x)))


def _split_bf16(a):
    hi = a.astype(BF16)
    lo = (a - hi.astype(F32)).astype(BF16)
    return hi, lo


def _dot(a, b):
    return jnp.dot(a, b, preferred_element_type=F32)


def _dot_nt(a, b):
    return lax.dot_general(a, b, (((1,), (1,)), ((), ())), preferred_element_type=F32)


def _dot_exact_rhs(a, b_bf16):
    hi, lo = _split_bf16(a)
    return _dot(hi, b_bf16) + _dot(lo, b_bf16)


def _dot_exact_lhs(a_bf16, b):
    hi, lo = _split_bf16(b)
    return _dot(a_bf16, hi) + _dot(a_bf16, lo)


def _mod_kernel(c_ref, w_ref, b_ref, o_ref):
    cond = _silu(c_ref[...])
    c_hi, c_lo = _split_bf16(cond)
    w = w_ref[0]
    w_hi, w_lo = _split_bf16(w)
    o_ref[0] = _dot(c_hi, w_hi) + _dot(c_lo, w_hi) + _dot(c_hi, w_lo) + b_ref[0]


def _modulation(c, w_mod, b_mod):
    depth, d, n = w_mod.shape
    bsz = c.shape[0]
    rows = -(-bsz // SUBLANES) * SUBLANES
    c_pad = jnp.zeros((rows, d), F32).at[:bsz].set(c)
    tn = 1536
    out = pl.pallas_call(
        _mod_kernel,
        out_shape=jax.ShapeDtypeStruct((depth, rows, n), F32),
        grid=(depth, n // tn),
        in_specs=[
            pl.BlockSpec((rows, d), lambda l, j: (0, 0)),
            pl.BlockSpec((1, d, tn), lambda l, j: (l, 0, j)),
            pl.BlockSpec((1, 1, tn), lambda l, j: (l, 0, j)),
        ],
        out_specs=pl.BlockSpec((1, rows, tn), lambda l, j: (l, 0, j)),
        compiler_params=_cparams(("arbitrary", "arbitrary")),
        name="adaln_mod",
    )(c_pad, w_mod, b_mod.reshape(depth, 1, n))
    return out[:, :bsz].reshape(depth, bsz, 3, d)


def _inproj_kernel(x_ref, mod_ref, nw_ref, w_ref, wst_ref, o_ref, ost_ref, h_even, h_odd, *, n_groups):
    g = pl.program_id(0)
    j = pl.program_id(1)
    rows = x_ref.shape[1]

    def prep(h_dst):
        x = x_ref[0]
        ms = jnp.mean(x * x, axis=-1, keepdims=True)
        y = x * lax.rsqrt(ms + EPS) * nw_ref[...]
        h = (y * (1.0 + mod_ref[0, 1:2, :]) + mod_ref[0, 0:1, :]).astype(BF16)
        h_dst[pl.ds(pl.multiple_of(j * rows, rows), rows), :] = h
        ost_ref[0] = _dot_nt(wst_ref[...], h)

    @pl.when(g == 0)
    def _():
        prep(h_even)

    @pl.when((g > 0) & (g % 2 == 0))
    def _():
        o_ref[0] = _dot_nt(h_odd[...], w_ref[...])
        prep(h_even)

    @pl.when(g % 2 == 1)
    def _():
        o_ref[0] = _dot_nt(h_even[...], w_ref[...])
        prep(h_odd)


def _inproj(x, mod_l, norm_w, w_main, w_small, *, tm=1024, nj=4):
    bsz, s, d = x.shape
    n = w_main.shape[0]
    ns = w_small.shape[0]
    tm = min(tm, s)
    ni = s // tm
    n_groups = bsz * ni
    tn = n // nj
    rows = tm // nj
    assert n % nj == 0 and tn % LANES == 0 and rows % LANES == 0

    def prep(g, j):
        gc = jnp.minimum(g, n_groups - 1)
        return gc // ni, (gc % ni) * nj + jnp.where(g < n_groups, j, nj - 1)

    def mm(g, j):
        gm = jnp.maximum(g - 1, 0)
        return gm // ni, gm % ni, jnp.where(g > 0, j, 0)

    return pl.pallas_call(
        functools.partial(_inproj_kernel, n_groups=n_groups),
        out_shape=[jax.ShapeDtypeStruct((bsz, s, n), F32), jax.ShapeDtypeStruct((bsz, ns, s), F32)],
        grid=(n_groups + 1, nj),
        in_specs=[
            pl.BlockSpec((1, rows, d), lambda g, j: (*prep(g, j), 0)),
            pl.BlockSpec((1, 3, d), lambda g, j: (prep(g, j)[0], 0, 0)),
            pl.BlockSpec((1, d), lambda g, j: (0, 0)),
            pl.BlockSpec((tn, d), lambda g, j: (mm(g, j)[2], 0)),
            pl.BlockSpec((ns, d), lambda g, j: (0, 0)),
        ],
        out_specs=[
            pl.BlockSpec((1, tm, tn), lambda g, j: mm(g, j)),
            pl.BlockSpec((1, ns, rows), lambda g, j: (prep(g, j)[0], 0, prep(g, j)[1])),
        ],
        scratch_shapes=[pltpu.VMEM((tm, d), BF16), pltpu.VMEM((tm, d), BF16)],
        compiler_params=_cparams(("arbitrary", "arbitrary")),
        name="norm_mod_inproj",
    )(x, mod_l, norm_w.reshape(1, d), w_main, w_small)


def _outproj_kernel(ya_ref, yb_ref, w_ref, x_ref, mod_ref, *rest, final_norm, half):
    if final_norm:
        fnw_ref, o_ref, w_scr = rest
    else:
        o_ref, w_scr = rest

    @pl.when((pl.program_id(0) == 0) & (pl.program_id(1) == 0))
    def _():
        rows = CAST_ROWS

        def body(r, carry):
            r0 = pl.multiple_of(r * rows, rows)
            w_scr[pl.ds(r0, rows), :] = w_ref[pl.ds(r0, rows), :].astype(BF16)
            return carry

        lax.fori_loop(0, w_ref.shape[0] // rows, body, 0)

    acc = _dot(ya_ref[0], w_scr[0:half, :]) + _dot(yb_ref[0], w_scr[half:, :])
    gate = mod_ref[0, 2:3, :]
    xn = x_ref[0] + gate * acc
    if final_norm:
        ms = jnp.mean(xn * xn, axis=-1, keepdims=True)
        xn = xn * lax.rsqrt(ms + EPS) * fnw_ref[...]
    o_ref[0] = xn


def _outproj(ya, yb, w_out, x, mod_l, final_w=None, *, tm=512):
    bsz, s, d = x.shape
    half = ya.shape[-1]
    tm = min(tm, s)
    final_norm = final_w is not None
    in_specs = [
        pl.BlockSpec((1, tm, half), lambda b, i: (b, i, 0)),
        pl.BlockSpec((1, tm, half), lambda b, i: (b, i, 0)),
        pl.BlockSpec((2 * half, d), lambda b, i: (0, 0), pipeline_mode=pl.Buffered(1)),
        pl.BlockSpec((1, tm, d), lambda b, i: (b, i, 0)),
        pl.BlockSpec((1, 3, d), lambda b, i: (b, 0, 0)),
    ]
    args = [ya, yb, w_out, x, mod_l]
    if final_norm:
        in_specs.append(pl.BlockSpec((1, d), lambda b, i: (0, 0)))
        args.append(final_w.reshape(1, d))
    return pl.pallas_call(
        functools.partial(_outproj_kernel, final_norm=final_norm, half=half),
        out_shape=jax.ShapeDtypeStruct((bsz, s, d), F32),
        grid=(bsz, s // tm),
        in_specs=in_specs,
        out_specs=pl.BlockSpec((1, tm, d), lambda b, i: (b, i, 0)),
        scratch_shapes=[pltpu.VMEM((2 * half, d), BF16)],
        compiler_params=_cparams(("arbitrary", "arbitrary")),
        name="outproj_residual",
    )(*args)


def _rms_rope(x, nw, cos, sin_signed):
    ms = jnp.mean(x * x, axis=-1, keepdims=True)
    y = x * lax.rsqrt(ms + EPS) * nw
    lane = lax.broadcasted_iota(jnp.int32, y.shape, 1)
    partner = jnp.where(lane % 2 == 0, pltpu.roll(y, ATT_HEAD_DIM - 1, 1), pltpu.roll(y, 1, 1))
    return y * cos + partner * sin_signed


def _attention_kv_prep(k_ref, v_ref, ck_ref, sk_ref, kn_ref, k_scr, v_scr):
    dh = ATT_HEAD_DIM
    k_scr[...] = _rms_rope(k_ref[0], kn_ref[...], ck_ref[...], sk_ref[...]).astype(BF16)
    v_scr[:, 0:dh] = v_ref[0].astype(BF16)
    v_scr[:, dh:2 * dh] = jnp.ones((v_scr.shape[0], dh), BF16)


ATTN_AHEAD = 1


def _attention_pump(q_ref, ga_ref, cq_ref, sq_ref, qn_ref, k_scr, v_scr, o_ref, sub):
    dh = ATT_HEAD_DIM
    tq, width = q_ref.shape[1], q_ref.shape[2]
    scale = dh ** -0.5
    probs = [(g, r) for g in range(width // dh) for r in range(0, tq, sub)]
    ahead = ATTN_AHEAD
    state = {"i": 0, "s": {}}

    def scores(g, r):
        qg = q_ref[0, r:r + sub, g * dh:(g + 1) * dh]
        qg = (_rms_rope(qg, qn_ref[...], cq_ref[r:r + sub, :], sq_ref[r:r + sub, :]) * scale).astype(BF16)
        return _dot_nt(qg, k_scr[...])

    def finish(g, r, s):
        s = s.astype(BF16)
        p = jnp.exp(s - jnp.max(s, axis=-1, keepdims=True))
        o_ext = _dot(p, v_scr[...])
        og = o_ext[:, 0:dh] / o_ext[:, dh:2 * dh] * _silu(ga_ref[0, r:r + sub, g * dh:(g + 1) * dh])
        o_ref[0, r:r + sub, g * dh:(g + 1) * dh] = og.astype(BF16)

    def pump():
        i = state["i"]
        if i >= len(probs) + ahead:
            return
        if i < len(probs):
            state["s"][i] = scores(*probs[i])
        if i >= ahead:
            finish(*probs[i - ahead], state["s"].pop(i - ahead))
        state["i"] = i + 1

    return pump, len(probs) + ahead


def _attn_kernel(q_ref, k_ref, v_ref, ga_ref, cq_ref, sq_ref, ck_ref, sk_ref, qn_ref, kn_ref,
                 o_ref, k_scr, v_scr, *, sub):
    @pl.when(pl.program_id(2) == 0)
    def _():
        _attention_kv_prep(k_ref, v_ref, ck_ref, sk_ref, kn_ref, k_scr, v_scr)

    pump, n = _attention_pump(q_ref, ga_ref, cq_ref, sq_ref, qn_ref, k_scr, v_scr, o_ref, sub)
    for _ in range(n):
        pump()


def _attention(proj, cos_t, sin_t, q_norm, k_norm, *, col_q, col_k, col_v, col_ga, tq=512, sub=128):
    bsz, s, _ = proj.shape
    tq = min(tq, s)
    gw = ATT_GROUP * ATT_HEAD_DIM
    dh = ATT_HEAD_DIM
    qb, kb, vb, gb = col_q // gw, col_k // dh, col_v // dh, col_ga // gw
    return pl.pallas_call(
        functools.partial(_attn_kernel, sub=min(sub, tq)),
        out_shape=jax.ShapeDtypeStruct((bsz, s, BRANCH_W), BF16),
        grid=(bsz, ATT_KV_HEADS, s // tq),
        in_specs=[
            pl.BlockSpec((1, tq, gw), lambda b, h, i: (b, i, qb + h)),
            pl.BlockSpec((1, s, dh), lambda b, h, i: (b, 0, kb + h)),
            pl.BlockSpec((1, s, dh), lambda b, h, i: (b, 0, vb + h)),
            pl.BlockSpec((1, tq, gw), lambda b, h, i: (b, i, gb + h)),
            pl.BlockSpec((tq, dh), lambda b, h, i: (i, 0)),
            pl.BlockSpec((tq, dh), lambda b, h, i: (i, 0)),
            pl.BlockSpec((s, dh), lambda b, h, i: (0, 0)),
            pl.BlockSpec((s, dh), lambda b, h, i: (0, 0)),
            pl.BlockSpec((1, dh), lambda b, h, i: (0, 0)),
            pl.BlockSpec((1, dh), lambda b, h, i: (0, 0)),
        ],
        out_specs=pl.BlockSpec((1, tq, gw), lambda b, h, i: (b, i, h)),
        scratch_shapes=[pltpu.VMEM((s, dh), BF16), pltpu.VMEM((s, 2 * dh), BF16)],
        compiler_params=_cparams(("arbitrary", "arbitrary", "arbitrary")),
        name="gqa_attention",
    )(proj, proj, proj, proj, cos_t, sin_t, cos_t, sin_t, q_norm, k_norm)


def _rope_tables(s):
    t = np.arange(s)
    row = (t // GRID_W).astype(np.float64)
    col = (t % GRID_W).astype(np.float64)
    n_pairs = ATT_HEAD_DIM // 4
    freqs = ROPE_THETA ** (-np.arange(n_pairs, dtype=np.float64) / n_pairs)
    ang = np.concatenate([row[:, None] * freqs, col[:, None] * freqs], axis=-1)
    cos, sin = np.cos(ang), np.sin(ang)
    cos_t = np.repeat(cos, 2, axis=-1)
    sin_t = np.stack([-sin, sin], axis=-1).reshape(s, ATT_HEAD_DIM)
    return jnp.asarray(cos_t, F32), jnp.asarray(sin_t, F32)


def _chunk_of_step(step, nc):
    return jnp.where(step < nc, step, 2 * nc - 1 - step)


def _fill_conv_window(ext_scr, cur, prev, nxt, c, nc, rows):
    ext_scr[0:HALO, :] = jnp.where(c > 0, prev, 0.0)
    ext_scr[HALO:HALO + rows, :] = cur
    ext_scr[HALO + rows:HALO + rows + HALO, :] = jnp.where(c < nc - 1, nxt, 0.0)


CONV_SUB = 128


def _centred_conv4(ext_scr, cw_ref, cb_ref, rows):
    sub = min(rows, CONV_SUB)
    n_in = sub + 2 * HALO
    taps = (0, 1, 3)
    r = lax.broadcasted_iota(jnp.int32, (len(taps) * sub, n_in), 0)
    c = lax.broadcasted_iota(jnp.int32, (len(taps) * sub, n_in), 1)
    src = jnp.zeros_like(r)
    for i, k in enumerate(taps):
        src = jnp.where((r >= i * sub) & (r < (i + 1) * sub), r - i * sub + HALO + (k - 2), src)
    shift_mat = jnp.where(c == src, 1.0, 0.0).astype(BF16)
    outs = []
    for s0 in range(0, rows, sub):
        win = ext_scr[s0:s0 + n_in, :]
        shifted = _dot(shift_mat, win.astype(BF16))
        acc = cb_ref[...] + win[HALO:HALO + sub, :] * cw_ref[2:3, :]
        for i, k in enumerate(taps):
            acc = acc + shifted[i * sub:(i + 1) * sub, :] * cw_ref[k:k + 1, :]
        outs.append(acc)
    return outs[0] if len(outs) == 1 else jnp.concatenate(outs, axis=0)


def _scan_masks(fwd, n):
    row = lax.broadcasted_iota(jnp.int32, (n, n), 0)
    col = lax.broadcasted_iota(jnp.int32, (n, n), 1)
    sgn = jnp.where(fwd, 1, -1)
    d = (row - col) * sgn
    return d >= 0, d <= 0, d > 0


def _pad_dir_rows(v, n):
    return jnp.zeros((2, 1, LANES), F32).at[:, 0, :n].set(v)


def _pad_dir_cols(v, n):
    return jnp.zeros((2, LANES, 1), F32).at[:, :n, 0].set(v)


def _head_expander(heads, width):
    r = lax.broadcasted_iota(jnp.int32, (heads, heads * width), 0)
    c = lax.broadcasted_iota(jnp.int32, (heads, heads * width), 1)
    return jnp.where((c >= r * width) & (c < (r + 1) * width), 1.0, 0.0).astype(BF16)


def _ssd_kernel(xf_ref, pf_ref, nf_ref, xb_ref, pb_ref, nb_ref, z_ref, dtf_ref, dtb_ref, cw_ref, cb_ref,
                bias_r_ref, bias_c_ref, alog_r_ref, alog_c_ref, dskip_ref, nw_ref,
                o_ref, ext_scr, state_scr, acc_scr, *, nc, te):
    L, H, P, N = SSD_CHUNK, SSD_HEADS, SSD_HEAD_DIM, SSD_STATE
    HG = H // SSD_GROUPS
    GW = HG * P
    t = pl.program_id(1)

    @pl.when(t == 0)
    def _():
        state_scr[...] = jnp.zeros_like(state_scr)

    @pl.when(t < nc)
    def _():
        chunks = (t, nc - 1 - t)
        blocks = ((xf_ref, pf_ref, nf_ref, dtf_ref), (xb_ref, pb_ref, nb_ref, dtb_ref))
        expand = _head_expander(H, P)
        lane = lax.broadcasted_iota(jnp.int32, (L, 2 * P), 1)
        per_dir = []
        for d in range(2):
            x_ref, p_ref, n_ref, dtt_ref = blocks[d]
            ext = ext_scr.at[d]
            _fill_conv_window(ext, x_ref[0], p_ref[0], n_ref[0], chunks[d], nc, L)
            xbc = _silu(_centred_conv4(ext, cw_ref, cb_ref, L))
            xs = xbc[:, :BRANCH_W]
            bs = xbc[:, BRANCH_W:BRANCH_W + SSD_GROUPS * N]
            cs = xbc[:, BRANCH_W + SSD_GROUPS * N:]
            raw_t = dtt_ref[0]
            raw = raw_t.T
            dt = _softplus(raw[:, d * H:(d + 1) * H] + bias_r_ref[d][:, 0:H])
            a = dt * (-jnp.exp(alog_r_ref[d][:, 0:H]))
            dt_t = _softplus(raw_t[d * H:(d + 1) * H, :] + bias_c_ref[d][0:H, :])
            a_t = dt_t * (-jnp.exp(alog_c_ref[d][0:H, :]))
            mask, mask_t, _ = _scan_masks(d == 0, L)
            cum = _dot_exact_lhs(jnp.where(mask, 1.0, 0.0).astype(BF16), a)
            cum_t = _dot_exact_rhs(a_t, jnp.where(mask_t, 1.0, 0.0).astype(BF16))
            total = jnp.sum(a, axis=0, keepdims=True)
            dt_e = _dot(dt.astype(BF16), expand)
            p_e = _dot(jnp.exp(cum).astype(BF16), expand)
            q_e = _dot(jnp.exp(total - cum).astype(BF16), expand)
            tot_e = _dot_exact_rhs(jnp.broadcast_to(jnp.exp(total), (SUBLANES, H)), expand)[0:1, :]
            xd = xs * dt_e
            per_dir.append(dict(xs=xs, bs=bs, cs=cs, mask=mask, cum=cum, cum_t=cum_t, p_e=p_e, tot_e=tot_e,
                                xdq=(xd * q_e).astype(BF16), xd_b=xd.astype(BF16)))

        combos = [(d, g) for d in range(2) for g in range(SSD_GROUPS)]
        cg = {k: per_dir[k[0]]["cs"][:, k[1] * N:(k[1] + 1) * N].astype(BF16) for k in combos}
        bg = {k: per_dir[k[0]]["bs"][:, k[1] * N:(k[1] + 1) * N] for k in combos}
        gmat = {k: _dot_nt(cg[k], bg[k].astype(BF16)) for k in combos}
        h_prev = {k: state_scr[k[0], :, k[1] * GW:(k[1] + 1) * GW] for k in combos}
        y_off = {k: _dot(cg[k], h_prev[k].astype(BF16)) * per_dir[k[0]]["p_e"][:, k[1] * GW:(k[1] + 1) * GW]
                 for k in combos}
        pairs = {k: [] for k in combos}
        for hp in range(HG // 2):
            for k in combos:
                d, g = k
                pd = per_dir[d]
                h0 = g * HG + 2 * hp
                xpair = pd["xd_b"][:, h0 * P:(h0 + 2) * P]
                ys = []
                for h in (h0, h0 + 1):
                    dec = jnp.exp(jnp.where(pd["mask"], pd["cum"][:, h:h + 1] - pd["cum_t"][h:h + 1, :], NEG_BIG))
                    ys.append(_dot((gmat[k] * dec).astype(BF16), xpair))
                pairs[k].append(jnp.where(lane < P, ys[0], ys[1]))
        for k in combos:
            d, g = k
            state_scr[d, :, g * GW:(g + 1) * GW] = (
                h_prev[k] * per_dir[d]["tot_e"][:, g * GW:(g + 1) * GW]
                + _dot(bg[k].T.astype(BF16), per_dir[d]["xdq"][:, g * GW:(g + 1) * GW]))
        y_dirs = [jnp.concatenate([jnp.concatenate(pairs[(d, g)], axis=1) + y_off[(d, g)]
                                   for g in range(SSD_GROUPS)], axis=1) for d in range(2)]
        y_dirs[0] = y_dirs[0] + dskip_ref[...] * per_dir[0]["xs"]
        rows = [pl.multiple_of(c * L, L) for c in chunks]

        @pl.when(t < nc // 2)
        def _():
            for d in range(2):
                acc_scr[pl.ds(rows[d], L), :] = y_dirs[d]

        @pl.when(t >= nc // 2)
        def _():
            for d in range(2):
                acc_scr[pl.ds(rows[d], L), :] = acc_scr[pl.ds(rows[d], L), :] + y_dirs[d]

    @pl.when(t >= nc)
    def _():
        r0 = pl.multiple_of((t - nc) * te, te)
        y = acc_scr[pl.ds(r0, te), :] * _silu(z_ref[0])
        ms = jnp.mean(y * y, axis=-1, keepdims=True)
        o_ref[0] = (y * lax.rsqrt(ms + EPS) * nw_ref[...]).astype(BF16)


def _ssd(proj, small_t, conv_w, conv_b, dt_bias, a_log, d_skip, norm_w, *, col_xbc, col_z, te=512):
    bsz, s, _ = proj.shape
    L = SSD_CHUNK
    nc = s // L
    te = min(te, s)
    ne = s // te
    cw = BRANCH_W + 2 * SSD_GROUPS * SSD_STATE
    hb = L // HALO
    xb, zb = col_xbc // cw, col_z // BRANCH_W
    assert col_xbc % cw == 0 and col_z % BRANCH_W == 0 and nc % 2 == 0

    def cf(t):
        return jnp.minimum(t, nc - 1)

    def cbk(t):
        return jnp.maximum(nc - 1 - t, 0)

    def ep(t):
        return jnp.maximum(t - nc, 0)

    def xbc_specs(chunk):
        return [
            pl.BlockSpec((1, L, cw), lambda b, t: (b, chunk(t), xb)),
            pl.BlockSpec((1, HALO, cw), lambda b, t: (b, jnp.maximum(chunk(t) * hb - 1, 0), xb)),
            pl.BlockSpec((1, HALO, cw), lambda b, t: (b, jnp.minimum((chunk(t) + 1) * hb, s // HALO - 1), xb)),
        ]

    def small_spec(chunk):
        return pl.BlockSpec((1, LANES, L), lambda b, t: (b, 0, chunk(t)))

    full = lambda shape: pl.BlockSpec(shape, lambda b, t: (0,) * len(shape))
    in_specs = (xbc_specs(cf) + xbc_specs(cbk)
                + [pl.BlockSpec((1, te, BRANCH_W), lambda b, t: (b, ep(t), zb))]
                + [small_spec(cf), small_spec(cbk)]
                + [full((SSD_CONV, cw)), full((1, cw)), full((2, 1, LANES)), full((2, LANES, 1)),
                   full((2, 1, LANES)), full((2, LANES, 1)), full((1, BRANCH_W)), full((1, BRANCH_W))])
    args = [proj, proj, proj, proj, proj, proj, proj, small_t, small_t, conv_w, conv_b.reshape(1, cw),
            _pad_dir_rows(dt_bias, SSD_HEADS), _pad_dir_cols(dt_bias, SSD_HEADS),
            _pad_dir_rows(a_log, SSD_HEADS), _pad_dir_cols(a_log, SSD_HEADS),
            jnp.repeat(d_skip, SSD_HEAD_DIM).reshape(1, BRANCH_W), norm_w.reshape(1, BRANCH_W)]
    out_shape = [jax.ShapeDtypeStruct((bsz, s, BRANCH_W), BF16)]
    out_specs = [pl.BlockSpec((1, te, BRANCH_W), lambda b, t: (b, ep(t), 0))]
    scratch = [
        pltpu.VMEM((2, L + 2 * HALO, cw), F32),
        pltpu.VMEM((2, SSD_STATE, BRANCH_W), F32),
        pltpu.VMEM((s, BRANCH_W), F32),
    ]
    return pl.pallas_call(
        functools.partial(_ssd_kernel, nc=nc, te=te),
        out_shape=out_shape,
        grid=(bsz, nc + ne),
        in_specs=in_specs,
        out_specs=out_specs,
        scratch_shapes=scratch,
        compiler_params=_cparams(("arbitrary", "arbitrary")),
        name="bidir_ssd",
    )(*args)[0]


AB_COL_XBC, AB_COL_K, AB_COL_V, AB_COL_Z, AB_COL_Q, AB_COL_GA, AB_N = 0, 1536, 1792, 2048, 3072, 4096, 5120


REGROUP_ROWS = 256


def _regroup_kernel(offs_ref, w_ref, o_ref):
    del offs_ref
    o_ref[...] = w_ref[...].astype(BF16)


def _regroup_cast(wt, pieces):
    rows = REGROUP_ROWS
    offsets = []
    for start, n in pieces:
        assert n % rows == 0 and start % SUBLANES == 0
        offsets += list(range(start, start + n, rows))
    d = wt.shape[1]
    return pl.pallas_call(
        _regroup_kernel,
        out_shape=jax.ShapeDtypeStruct((len(offsets) * rows, d), BF16),
        grid_spec=pltpu.PrefetchScalarGridSpec(
            num_scalar_prefetch=1,
            grid=(len(offsets),),
            in_specs=[pl.BlockSpec((pl.Element(rows), pl.Element(d)), lambda i, offs: (pl.multiple_of(offs[i], SUBLANES), 0))],
            out_specs=pl.BlockSpec((rows, d), lambda i, offs: (i, 0)),
        ),
        compiler_params=_cparams(("arbitrary",)),
        name="weight_regroup_cast",
    )(jnp.asarray(offsets, jnp.int32), wt)


def _piece_table(sizes):
    table, o = {}, 0
    for name, n in sizes:
        table[name] = (o, n)
        o += n
    return table


def _ab_weights(w_in):
    wt = w_in.T
    hq, hk = ATT_HEADS * ATT_HEAD_DIM, ATT_KV_HEADS * ATT_HEAD_DIM
    gn = SSD_GROUPS * SSD_STATE
    p = _piece_table([("q", hq), ("k", hk), ("v", hk), ("ga", BRANCH_W), ("xs", BRANCH_W), ("bs", gn),
                      ("cs", gn), ("dtf", SSD_HEADS), ("dtb", SSD_HEADS), ("z", BRANCH_W)])
    main = _regroup_cast(wt, [p[n] for n in ("xs", "bs", "cs", "k", "v", "z", "q", "ga")])
    dt0, dtn = p["dtf"][0], 2 * SSD_HEADS
    zpad = jnp.zeros((LANES - dtn, wt.shape[1]), wt.dtype)
    small = jnp.concatenate([wt[dt0:dt0 + dtn], zpad], axis=0).astype(BF16)
    return main, small


def _layer0(x, mod_l, norm_w, w_in, q_norm, k_norm, conv_w, conv_b, dt_bias_f, dt_bias_b,
            a_log_f, a_log_b, d_skip, ssd_norm, w_out, final_w=None):
    s = x.shape[1]
    w_main, w_small = _ab_weights(w_in)
    proj, small_t = _inproj(x, mod_l, norm_w, w_main, w_small)
    cos_t, sin_t = _rope_tables(s)
    att = _attention(proj, cos_t, sin_t, q_norm.reshape(1, ATT_HEAD_DIM), k_norm.reshape(1, ATT_HEAD_DIM),
                     col_q=AB_COL_Q, col_k=AB_COL_K, col_v=AB_COL_V, col_ga=AB_COL_GA)
    ssd = _ssd(proj, small_t, conv_w, conv_b, jnp.stack([dt_bias_f, dt_bias_b]),
               jnp.stack([a_log_f, a_log_b]), d_skip, ssd_norm, col_xbc=AB_COL_XBC, col_z=AB_COL_Z)
    return _outproj(att, ssd, w_out, x, mod_l, final_w), (proj, small_t, att, ssd)


def _unit_tri_inverse(nmats, n):
    row = lax.broadcasted_iota(jnp.int32, (n, n), 0)
    col = lax.broadcasted_iota(jnp.int32, (n, n), 1)

    def same_block(size):
        return (row // size) == (col // size)

    def mm(a, b):
        return _dot(a, b).astype(BF16)

    def as_mask(cond):
        return jnp.where(cond, 1.0, 0.0).astype(BF16)

    nmats = [m.astype(BF16) for m in nmats]
    eye = as_mask(row == col)
    base = SUBLANES
    blk = same_block(base)
    blk_m = as_mask(blk)
    nd = [m * blk_m for m in nmats]
    p1 = [mm(x, x) for x in nd]
    p2 = [mm(x, x) for x in p1]
    t = [eye - x for x in nd]
    t = [x + mm(x, p) for x, p in zip(t, p1)]
    t = [x + mm(x, p) for x, p in zip(t, p2)]
    size = base
    while size < n:
        nxt = same_block(2 * size)
        off_m = as_mask(nxt & jnp.logical_not(blk))
        et = [mm(m * off_m, x) for m, x in zip(nmats, t)]
        t = [x - mm(x, y) for x, y in zip(t, et)]
        blk = nxt
        size *= 2
    return t


def _l2norm(x):
    return x * lax.rsqrt(jnp.sum(x * x, axis=-1, keepdims=True) + EPS)


def _gdn_kernel(qf_ref, pf_ref, nf_ref, qb_ref, pb_ref, nb_ref, z_ref, smtf_ref, smtb_ref,
                cw_ref, cb_ref, bias_r_ref, bias_c_ref, alog_r_ref, alog_c_ref, nw_ref,
                o_ref, ext_scr, state_scr, acc_scr, *, nc, te):
    L, HV, HQ, DK = GDN_CHUNK, GDN_V_HEADS, GDN_QK_HEADS, GDN_HEAD_DIM
    rep = HV // HQ
    nbt = acc_scr.shape[0]
    t = pl.program_id(1)

    @pl.when(t == 0)
    def _():
        state_scr[...] = jnp.zeros_like(state_scr)

    @pl.when(t < nc)
    def _():
        chunks = (t, nc - 1 - t)
        blocks = ((qf_ref, pf_ref, nf_ref, smtf_ref), (qb_ref, pb_ref, nb_ref, smtb_ref))
        kk, qk, qh, kh, vh, colv, dec, bcol, ecol, tot, nmat = ([] for _ in range(11))
        for bi, d in [(bi, d) for bi in range(nbt) for d in range(2)]:
            c = chunks[d]
            q_ref, p_ref, n_ref, smt_ref = blocks[d]
            ext = ext_scr.at[bi * 2 + d]
            _fill_conv_window(ext, q_ref[bi], p_ref[bi], n_ref[bi], c, nc, L)
            act = _silu(_centred_conv4(ext, cw_ref, cb_ref, L))
            q_n = [_l2norm(act[:, h * DK:(h + 1) * DK]) * (DK ** -0.5) for h in range(HQ)]
            k_n = [_l2norm(act[:, (HQ + h) * DK:(HQ + h + 1) * DK]) for h in range(HQ)]
            k_b = [x.astype(BF16) for x in k_n]
            kk_d = [_dot_nt(k_b[h], k_b[h]) for h in range(HQ)]
            qk_d = [_dot_nt(q_n[h].astype(BF16), k_b[h]) for h in range(HQ)]

            raw_t = smt_ref[bi]
            raw = raw_t.T
            a_raw = raw[:, (2 + d) * HV:(3 + d) * HV]
            a_raw_t = raw_t[(2 + d) * HV:(3 + d) * HV, :]
            beta = _sigmoid(raw[:, d * HV:(d + 1) * HV])
            g = -jnp.exp(alog_r_ref[d][:, 0:HV]) * _softplus(a_raw + bias_r_ref[d][:, 0:HV])
            g_t = -jnp.exp(alog_c_ref[d][0:HV, :]) * _softplus(a_raw_t + bias_c_ref[d][0:HV, :])
            mask, mask_t, strict = _scan_masks(d == 0, L)
            cum = _dot_exact_lhs(jnp.where(mask, 1.0, 0.0).astype(BF16), g)
            cum_t = _dot_exact_rhs(g_t, jnp.where(mask_t, 1.0, 0.0).astype(BF16))
            total = jnp.sum(g, axis=0, keepdims=True)
            for h in range(HV):
                cv = cum[:, h:h + 1]
                dc = jnp.exp(jnp.where(mask, cv - cum_t[h:h + 1, :], NEG_BIG))
                bc = beta[:, h:h + 1]
                kk.append(kk_d[h // rep]); qk.append(qk_d[h // rep])
                qh.append(q_n[h // rep]); kh.append(k_n[h // rep])
                vh.append(act[:, (2 * HQ + h) * DK:(2 * HQ + h + 1) * DK])
                colv.append(cv); dec.append(dc); bcol.append(bc); ecol.append(jnp.exp(cv))
                tot.append(total[:, h:h + 1])
                nmat.append(jnp.where(strict, kk_d[h // rep] * bc * dc, 0.0))

        outs = []
        for w0 in range(0, nbt * 2 * HV, GDN_WAVE):
            idx = range(w0, w0 + GDN_WAVE)
            t_inv = dict(zip(idx, _unit_tri_inverse([nmat[i] for i in idx], L)))
            u = {i: _dot(t_inv[i], (vh[i] * bcol[i]).astype(BF16)) for i in idx}
            w = {i: _dot(t_inv[i], (kh[i] * (bcol[i] * ecol[i])).astype(BF16)) for i in idx}
            s_prev = {i: state_scr[i // HV, :, (i % HV) * DK:(i % HV + 1) * DK] for i in idx}
            s_b = {i: s_prev[i].astype(BF16) for i in idx}
            v_new = {i: (u[i] - _dot(w[i].astype(BF16), s_b[i])).astype(BF16) for i in idx}
            outs += [_dot((qh[i] * ecol[i]).astype(BF16), s_b[i])
                     + _dot((qk[i] * dec[i]).astype(BF16), v_new[i]) for i in idx]
            for i in idx:
                k_dec = kh[i] * jnp.exp(tot[i] - colv[i])
                state_scr[i // HV, :, (i % HV) * DK:(i % HV + 1) * DK] = (
                    s_prev[i] * jnp.exp(tot[i]) + _dot(k_dec.T.astype(BF16), v_new[i]))
        o_dirs = [jnp.concatenate(outs[j * HV:(j + 1) * HV], axis=1) for j in range(nbt * 2)]
        rows = [pl.multiple_of(c * L, L) for c in chunks]

        @pl.when(t < nc // 2)
        def _():
            for bi in range(nbt):
                for d in range(2):
                    acc_scr[bi, pl.ds(rows[d], L), :] = o_dirs[bi * 2 + d]

        @pl.when(t >= nc // 2)
        def _():
            for bi in range(nbt):
                for d in range(2):
                    acc_scr[bi, pl.ds(rows[d], L), :] = acc_scr[bi, pl.ds(rows[d], L), :] + o_dirs[bi * 2 + d]

    @pl.when(t >= nc)
    def _():
        r0 = pl.multiple_of((t - nc) * te, te)
        for bi in range(nbt):
            zz = z_ref[bi]
            for hv in range(HV):
                oh = acc_scr[bi, pl.ds(r0, te), hv * DK:(hv + 1) * DK]
                ms = jnp.mean(oh * oh, axis=-1, keepdims=True)
                res = oh * lax.rsqrt(ms + EPS) * nw_ref[...] * _silu(zz[:, hv * DK:(hv + 1) * DK])
                o_ref[bi, :, hv * DK:(hv + 1) * DK] = res.astype(BF16)


def _gdn(proj, small_t, conv_w, conv_b, dt_bias, a_log, norm_w, *, col_qkv, col_z, te=512, nbt=2):
    bsz, s, _ = proj.shape
    L = GDN_CHUNK
    nc = s // L
    te = min(te, s)
    ne = s // te
    cw = 2 * GDN_QK_HEADS * GDN_HEAD_DIM + BRANCH_W
    hb = L // HALO
    qb, zb = col_qkv // cw, col_z // BRANCH_W
    assert col_qkv % cw == 0 and col_z % BRANCH_W == 0 and nc % 2 == 0 and bsz % nbt == 0

    def cf(t):
        return jnp.minimum(t, nc - 1)

    def cbk(t):
        return jnp.maximum(nc - 1 - t, 0)

    def ep(t):
        return jnp.maximum(t - nc, 0)

    def qkv_specs(chunk):
        return [
            pl.BlockSpec((nbt, L, cw), lambda b, t: (b, chunk(t), qb)),
            pl.BlockSpec((nbt, HALO, cw), lambda b, t: (b, jnp.maximum(chunk(t) * hb - 1, 0), qb)),
            pl.BlockSpec((nbt, HALO, cw), lambda b, t: (b, jnp.minimum((chunk(t) + 1) * hb, s // HALO - 1), qb)),
        ]

    def small_spec(chunk):
        return pl.BlockSpec((nbt, LANES, L), lambda b, t: (b, 0, chunk(t)))

    full = lambda shape: pl.BlockSpec(shape, lambda b, t: (0,) * len(shape))
    hv = GDN_V_HEADS
    return pl.pallas_call(
        functools.partial(_gdn_kernel, nc=nc, te=te),
        out_shape=jax.ShapeDtypeStruct((bsz, s, BRANCH_W), BF16),
        grid=(bsz // nbt, nc + ne),
        in_specs=qkv_specs(cf) + qkv_specs(cbk)
        + [pl.BlockSpec((nbt, te, BRANCH_W), lambda b, t: (b, ep(t), zb))]
        + [small_spec(cf), small_spec(cbk)]
        + [full((GDN_CONV, cw)), full((1, cw)), full((2, 1, LANES)), full((2, LANES, 1)),
           full((2, 1, LANES)), full((2, LANES, 1)), full((1, GDN_HEAD_DIM))],
        out_specs=pl.BlockSpec((nbt, te, BRANCH_W), lambda b, t: (b, ep(t), 0)),
        scratch_shapes=[
            pltpu.VMEM((nbt * 2, L + 2 * HALO, cw), F32),
            pltpu.VMEM((nbt * 2, GDN_HEAD_DIM, BRANCH_W), F32),
            pltpu.VMEM((nbt, s, BRANCH_W), F32),
        ],
        compiler_params=_cparams(("arbitrary", "arbitrary")),
        name="bidir_gated_deltanet",
    )(proj, proj, proj, proj, proj, proj, proj, small_t, small_t, conv_w, conv_b.reshape(1, cw),
      _pad_dir_rows(dt_bias, hv), _pad_dir_cols(dt_bias, hv), _pad_dir_rows(a_log, hv), _pad_dir_cols(a_log, hv),
      norm_w.reshape(1, GDN_HEAD_DIM))


def _local_scan(a, u, ascending):
    rows, width = a.shape
    a = a.reshape(rows // SUBLANES, SUBLANES, width)
    u = u.reshape(rows // SUBLANES, SUBLANES, width)
    sub = lax.broadcasted_iota(jnp.int32, a.shape, 1)
    d = 1
    while d < SUBLANES:
        if ascending:
            keep = sub >= d
            shift = d
        else:
            keep = sub < SUBLANES - d
            shift = SUBLANES - d
        a_sh = jnp.where(keep, pltpu.roll(a, shift, 1), 1.0)
        u_sh = jnp.where(keep, pltpu.roll(u, shift, 1), 0.0)
        u = u + a * u_sh
        a = a * a_sh
        d *= 2
    return a.reshape(rows, width), u.reshape(rows, width)


def _lru_kernel(xl_ref, prev_ref, next_ref, gl_ref, cw_ref, cb_ref, wa_ref, ba_ref, wx_ref, bx_ref, lam_ref,
                o_ref, ext_scr, a_scr, u_scr, hf_scr, carry_scr, *, nb, unroll):
    rows, width = a_scr.shape
    bw = LRU_WIDTH // LRU_BLOCKS
    step = pl.program_id(2)
    fwd = step < nb
    blk = _chunk_of_step(step, nb)

    @pl.when((step == 0) | (step == nb))
    def _():
        carry_scr[...] = jnp.zeros_like(carry_scr)

    _fill_conv_window(ext_scr, xl_ref[0], prev_ref[0], next_ref[0], blk, nb, rows)
    xc = _centred_conv4(ext_scr, cw_ref, cb_ref, rows)
    r_parts, i_parts = [], []
    for n in range(width // bw):
        xb = xc[:, n * bw:(n + 1) * bw].astype(BF16)
        r_parts.append(_dot(xb, wa_ref[0, n]))
        i_parts.append(_dot(xb, wx_ref[0, n]))
    tr = jnp.tanh(jnp.concatenate(r_parts, axis=1) + ba_ref[0])
    ti = jnp.tanh(jnp.concatenate(i_parts, axis=1) + bx_ref[0])
    log_a = ((-0.5 * LRU_C) * _softplus(-lam_ref[0])) * (tr + 1.0)
    a = jnp.exp(log_a)
    quarter = (-0.25 * jnp.tanh(log_a)) * (a * a + 1.0)
    u = (quarter * lax.rsqrt(jnp.maximum(quarter, TINY))) * ((ti + 1.0) * xc)

    n_groups = rows // SUBLANES
    base_out = pl.multiple_of(blk * rows, rows)

    def scan(ascending):
        a_loc, u_loc = _local_scan(a, u, ascending)
        a_scr[...] = a_loc
        u_scr[...] = u_loc
        last = SUBLANES - 1 if ascending else 0

        def body(it, carry):
            pos = it if ascending else n_groups - 1 - it
            r0 = pl.multiple_of(pos * SUBLANES, SUBLANES)
            h = u_scr[pl.ds(r0, SUBLANES), :] + a_scr[pl.ds(r0, SUBLANES), :] * carry
            u_scr[pl.ds(r0, SUBLANES), :] = h
            return jnp.broadcast_to(h[last:last + 1, :], h.shape)

        carry_scr[...] = lax.fori_loop(0, n_groups, body, carry_scr[...], unroll=unroll)

    @pl.when(fwd)
    def _():
        scan(True)
        hf_scr[pl.ds(base_out, rows), :] = u_scr[...]

    @pl.when(jnp.logical_not(fwd))
    def _():
        scan(False)
        o_ref[0] = ((hf_scr[pl.ds(base_out, rows), :] + u_scr[...]) * _silu(gl_ref[0])).astype(BF16)


def _lru(proj, conv_w, conv_b, wa, ba, wx, bx, lam, *, col_xl, col_gl, tt=1024, width=1024, unroll=8):
    bsz, s, _ = proj.shape
    tt = min(tt, s)
    nb = s // tt
    w_total = LRU_WIDTH
    bw = w_total // LRU_BLOCKS
    nbw = width // bw
    xcol, gcol = col_xl // width, col_gl // width
    assert col_xl % width == 0 and col_gl % width == 0
    hb = tt // HALO

    def blk(t):
        return _chunk_of_step(t, nb)

    def direction(t):
        return jnp.where(t < nb, 0, 1)

    return pl.pallas_call(
        functools.partial(_lru_kernel, nb=nb, unroll=unroll),
        out_shape=jax.ShapeDtypeStruct((bsz, s, w_total), BF16),
        grid=(bsz, w_total // width, 2 * nb),
        in_specs=[
            pl.BlockSpec((1, tt, width), lambda b, j, t: (b, blk(t), xcol + j)),
            pl.BlockSpec((1, HALO, width), lambda b, j, t: (b, jnp.maximum(blk(t) * hb - 1, 0), xcol + j)),
            pl.BlockSpec((1, HALO, width),
                         lambda b, j, t: (b, jnp.minimum((blk(t) + 1) * hb, s // HALO - 1), xcol + j)),
            pl.BlockSpec((1, tt, width), lambda b, j, t: (b, blk(t), gcol + j)),
            pl.BlockSpec((LRU_CONV, width), lambda b, j, t: (0, j)),
            pl.BlockSpec((1, width), lambda b, j, t: (0, j)),
            pl.BlockSpec((1, nbw, bw, bw), lambda b, j, t: (direction(t), j, 0, 0)),
            pl.BlockSpec((1, 1, width), lambda b, j, t: (direction(t), 0, j)),
            pl.BlockSpec((1, nbw, bw, bw), lambda b, j, t: (direction(t), j, 0, 0)),
            pl.BlockSpec((1, 1, width), lambda b, j, t: (direction(t), 0, j)),
            pl.BlockSpec((1, 1, width), lambda b, j, t: (direction(t), 0, j)),
        ],
        out_specs=pl.BlockSpec((1, tt, width),
                               lambda b, j, t: (b, jnp.where(t < nb, nb - 1, 2 * nb - 1 - t), j)),
        scratch_shapes=[
            pltpu.VMEM((tt + 2 * HALO, width), F32),
            pltpu.VMEM((tt, width), F32),
            pltpu.VMEM((tt, width), F32),
            pltpu.VMEM((s, width), F32),
            pltpu.VMEM((SUBLANES, width), F32),
        ],
        compiler_params=_cparams(("arbitrary", "arbitrary", "arbitrary")),
        name="bidir_rglru",
    )(proj, proj, proj, proj, conv_w, conv_b.reshape(1, w_total),
      (0.5 * wa).astype(BF16), (0.5 * ba).reshape(2, 1, w_total),
      (0.5 * wx).astype(BF16), (0.5 * bx).reshape(2, 1, w_total), lam.reshape(2, 1, w_total))


CD_COL_QKV, CD_COL_Z, CD_COL_XL, CD_COL_GL = 0, 2048, 3072, 4096


def _cd_weights(w_in):
    wt = w_in.T
    nqk = GDN_QK_HEADS * GDN_HEAD_DIM
    hv = GDN_V_HEADS
    p = _piece_table([("q", nqk), ("k", nqk), ("v", BRANCH_W), ("bf", hv), ("bb", hv), ("af", hv), ("ab", hv),
                      ("z", BRANCH_W), ("xl", LRU_WIDTH), ("gl", LRU_WIDTH)])
    main = _regroup_cast(wt, [p[n] for n in ("q", "k", "v", "z", "xl", "gl")])
    s0 = p["bf"][0]
    zpad = jnp.zeros((LANES - 4 * hv, wt.shape[1]), wt.dtype)
    small = jnp.concatenate([wt[s0:s0 + 4 * hv], zpad], axis=0).astype(BF16)
    return main, small


def _layer1(x, mod_l, norm_w, w_in, conv_w, conv_b, a_log_f, a_log_b, dt_bias_f, dt_bias_b, gdn_norm,
            lru_conv_w, lru_conv_b, wa_f, ba_f, wx_f, bx_f, lam_f, wa_b, ba_b, wx_b, bx_b, lam_b, w_out,
            final_w=None):
    w_main, w_small = _cd_weights(w_in)
    proj, small_t = _inproj(x, mod_l, norm_w, w_main, w_small)
    gdn = _gdn(proj, small_t, conv_w, conv_b, jnp.stack([dt_bias_f, dt_bias_b]),
               jnp.stack([a_log_f, a_log_b]), gdn_norm, col_qkv=CD_COL_QKV, col_z=CD_COL_Z)
    lru = _lru(proj, lru_conv_w, lru_conv_b, jnp.stack([wa_f, wa_b]), jnp.stack([ba_f, ba_b]),
               jnp.stack([wx_f, wx_b]), jnp.stack([bx_f, bx_b]), jnp.stack([lam_f, lam_b]),
               col_xl=CD_COL_XL, col_gl=CD_COL_GL)
    out = _outproj(gdn, lru, w_out, x, mod_l, final_w)
    return out, (proj, small_t, gdn, lru)


def kernel(x, c, w_mod, b_mod, norm_w, ab_w_in, ab_q_norm, ab_k_norm, ab_conv_w, ab_conv_b, ab_dt_bias_f, ab_dt_bias_b, ab_a_log_f, ab_a_log_b, ab_d_skip, ab_ssd_norm, ab_w_out, cd_w_in, cd_conv_w, cd_conv_b, cd_a_log_f, cd_a_log_b, cd_dt_bias_f, cd_dt_bias_b, cd_gdn_norm, cd_lru_conv_w, cd_lru_conv_b, cd_lru_wa_f, cd_lru_ba_f, cd_lru_wx_f, cd_lru_bx_f, cd_lru_lam_f, cd_lru_wa_b, cd_lru_ba_b, cd_lru_wx_b, cd_lru_bx_b, cd_lru_lam_b, cd_w_out, final_norm_w):
    mods = _modulation(c, w_mod, b_mod)
    x1, _ = _layer0(x, mods[0], norm_w[0], ab_w_in[0], ab_q_norm[0], ab_k_norm[0], ab_conv_w[0], ab_conv_b[0],
                    ab_dt_bias_f[0], ab_dt_bias_b[0], ab_a_log_f[0], ab_a_log_b[0], ab_d_skip[0],
                    ab_ssd_norm[0], ab_w_out[0])
    out, _ = _layer1(x1, mods[1], norm_w[1], cd_w_in[0], cd_conv_w[0], cd_conv_b[0], cd_a_log_f[0], cd_a_log_b[0],
                     cd_dt_bias_f[0], cd_dt_bias_b[0], cd_gdn_norm[0], cd_lru_conv_w[0], cd_lru_conv_b[0],
                     cd_lru_wa_f[0], cd_lru_ba_f[0], cd_lru_wx_f[0], cd_lru_bx_f[0], cd_lru_lam_f[0],
                     cd_lru_wa_b[0], cd_lru_ba_b[0], cd_lru_wx_b[0], cd_lru_bx_b[0], cd_lru_lam_b[0],
                     cd_w_out[0], final_norm_w)
    return out
```

```python
import functools

import jax
import jax.numpy as jnp
import numpy as np
from jax import lax
from jax.experimental import pallas as pl
from jax.experimental.pallas import tpu as pltpu

F32 = jnp.float32
BF16 = jnp.bfloat16

D_MODEL = 2048
GRID_W = 64
EPS = 1e-6
BRANCH_W = D_MODEL // 2
ATT_HEAD_DIM = 128
ATT_HEADS = BRANCH_W // ATT_HEAD_DIM
ATT_KV_HEADS = ATT_HEADS // 4
ATT_GROUP = ATT_HEADS // ATT_KV_HEADS
ROPE_THETA = 10000.0
SSD_HEAD_DIM = 64
SSD_HEADS = BRANCH_W // SSD_HEAD_DIM
SSD_GROUPS = 2
SSD_STATE = 128
SSD_CONV = 4
SSD_CHUNK = 128
GDN_HEAD_DIM = 128
GDN_V_HEADS = BRANCH_W // GDN_HEAD_DIM
GDN_QK_HEADS = GDN_V_HEADS // 2
GDN_CONV = 4
GDN_CHUNK = 128
GDN_WAVE = 16
LRU_WIDTH = BRANCH_W
LRU_BLOCKS = 8
LRU_CONV = 4
LRU_C = 8.0

LANES = 128
SUBLANES = 8
VMEM_LIMIT_BYTES = 56 * 1024 * 1024

HALO = SUBLANES
NEG_BIG = -1e30
TINY = 1e-37
CAST_ROWS = 256


def _cparams(sem):
    return pltpu.CompilerParams(dimension_semantics=sem, vmem_limit_bytes=VMEM_LIMIT_BYTES)


def _sigmoid(x):
    return 0.5 * jnp.tanh(0.5 * x) + 0.5


def _silu(x):
    h = 0.5 * x
    return h * (jnp.tanh(h) + 1.0)


def _softplus(x):
    return jnp.maximum(x, 0.0) + jnp.log(1.0 + jnp.exp(-jnp.abs(x)))


def _split_bf16(a):
    hi = a.astype(BF16)
    lo = (a - hi.astype(F32)).astype(BF16)
    return hi, lo


def _dot(a, b):
    return jnp.dot(a, b, preferred_element_type=F32)


def _dot_nt(a, b):
    return lax.dot_general(a, b, (((1,), (1,)), ((), ())), preferred_element_type=F32)


def _dot_exact_rhs(a, b_bf16):
    hi, lo = _split_bf16(a)
    return _dot(hi, b_bf16) + _dot(lo, b_bf16)


def _dot_exact_lhs(a_bf16, b):
    hi, lo = _split_bf16(b)
    return _dot(a_bf16, hi) + _dot(a_bf16, lo)


def _mod_kernel(c_ref, w_ref, b_ref, o_ref):
    cond = _silu(c_ref[...])
    c_hi, c_lo = _split_bf16(cond)
    w = w_ref[0]
    w_hi, w_lo = _split_bf16(w)
    o_ref[0] = _dot(c_hi, w_hi) + _dot(c_lo, w_hi) + _dot(c_hi, w_lo) + b_ref[0]


def _modulation(c, w_mod, b_mod):
    depth, d, n = w_mod.shape
    bsz = c.shape[0]
    rows = -(-bsz // SUBLANES) * SUBLANES
    c_pad = jnp.zeros((rows, d), F32).at[:bsz].set(c)
    tn = 1536
    out = pl.pallas_call(
        _mod_kernel,
        out_shape=jax.ShapeDtypeStruct((depth, rows, n), F32),
        grid=(depth, n // tn),
        in_specs=[
            pl.BlockSpec((rows, d), lambda l, j: (0, 0)),
            pl.BlockSpec((1, d, tn), lambda l, j: (l, 0, j)),
            pl.BlockSpec((1, 1, tn), lambda l, j: (l, 0, j)),
        ],
        out_specs=pl.BlockSpec((1, rows, tn), lambda l, j: (l, 0, j)),
        compiler_params=_cparams(("arbitrary", "arbitrary")),
        name="adaln_mod",
    )(c_pad, w_mod, b_mod.reshape(depth, 1, n))
    return out[:, :bsz].reshape(depth, bsz, 3, d)


def _inproj_kernel(x_ref, mod_ref, nw_ref, w_ref, wst_ref, o_ref, ost_ref, h_even, h_odd, *, n_groups):
    g = pl.program_id(0)
    j = pl.program_id(1)
    rows = x_ref.shape[1]

    def prep(h_dst):
        x = x_ref[0]
        ms = jnp.mean(x * x, axis=-1, keepdims=True)
        y = x * lax.rsqrt(ms + EPS) * nw_ref[...]
        h = (y * (1.0 + mod_ref[0, 1:2, :]) + mod_ref[0, 0:1, :]).astype(BF16)
        h_dst[pl.ds(pl.multiple_of(j * rows, rows), rows), :] = h
        ost_ref[0] = _dot_nt(wst_ref[...], h)

    @pl.when(g == 0)
    def _():
        prep(h_even)

    @pl.when((g > 0) & (g % 2 == 0))
    def _():
        o_ref[0] = _dot_nt(h_odd[...], w_ref[...])
        prep(h_even)

    @pl.when(g % 2 == 1)
    def _():
        o_ref[0] = _dot_nt(h_even[...], w_ref[...])
        prep(h_odd)


def _inproj(x, mod_l, norm_w, w_main, w_small, *, tm=1024, nj=4):
    bsz, s, d = x.shape
    n = w_main.shape[0]
    ns = w_small.shape[0]
    tm = min(tm, s)
    ni = s // tm
    n_groups = bsz * ni
    tn = n // nj
    rows = tm // nj
    assert n % nj == 0 and tn % LANES == 0 and rows % LANES == 0

    def prep(g, j):
        gc = jnp.minimum(g, n_groups - 1)
        return gc // ni, (gc % ni) * nj + jnp.where(g < n_groups, j, nj - 1)

    def mm(g, j):
        gm = jnp.maximum(g - 1, 0)
        return gm // ni, gm % ni, jnp.where(g > 0, j, 0)

    return pl.pallas_call(
        functools.partial(_inproj_kernel, n_groups=n_groups),
        out_shape=[jax.ShapeDtypeStruct((bsz, s, n), F32), jax.ShapeDtypeStruct((bsz, ns, s), F32)],
        grid=(n_groups + 1, nj),
        in_specs=[
            pl.BlockSpec((1, rows, d), lambda g, j: (*prep(g, j), 0)),
            pl.BlockSpec((1, 3, d), lambda g, j: (prep(g, j)[0], 0, 0)),
            pl.BlockSpec((1, d), lambda g, j: (0, 0)),
            pl.BlockSpec((tn, d), lambda g, j: (mm(g, j)[2], 0)),
            pl.BlockSpec((ns, d), lambda g, j: (0, 0)),
        ],
        out_specs=[
            pl.BlockSpec((1, tm, tn), lambda g, j: mm(g, j)),
            pl.BlockSpec((1, ns, rows), lambda g, j: (prep(g, j)[0], 0, prep(g, j)[1])),
        ],
        scratch_shapes=[pltpu.VMEM((tm, d), BF16), pltpu.VMEM((tm, d), BF16)],
        compiler_params=_cparams(("arbitrary", "arbitrary")),
        name="norm_mod_inproj",
    )(x, mod_l, norm_w.reshape(1, d), w_main, w_small)


def _outproj_kernel(ya_ref, yb_ref, w_ref, x_ref, mod_ref, *rest, final_norm, half):
    if final_norm:
        fnw_ref, o_ref, w_scr = rest
    else:
        o_ref, w_scr = rest

    @pl.when((pl.program_id(0) == 0) & (pl.program_id(1) == 0))
    def _():
        rows = CAST_ROWS

        def body(r, carry):
            r0 = pl.multiple_of(r * rows, rows)
            w_scr[pl.ds(r0, rows), :] = w_ref[pl.ds(r0, rows), :].astype(BF16)
            return carry

        lax.fori_loop(0, w_ref.shape[0] // rows, body, 0)

    acc = _dot(ya_ref[0], w_scr[0:half, :]) + _dot(yb_ref[0], w_scr[half:, :])
    gate = mod_ref[0, 2:3, :]
    xn = x_ref[0] + gate * acc
    if final_norm:
        ms = jnp.mean(xn * xn, axis=-1, keepdims=True)
        xn = xn * lax.rsqrt(ms + EPS) * fnw_ref[...]
    o_ref[0] = xn


def _outproj(ya, yb, w_out, x, mod_l, final_w=None, *, tm=512):
    bsz, s, d = x.shape
    half = ya.shape[-1]
    tm = min(tm, s)
    final_norm = final_w is not None
    in_specs = [
        pl.BlockSpec((1, tm, half), lambda b, i: (b, i, 0)),
        pl.BlockSpec((1, tm, half), lambda b, i: (b, i, 0)),
        pl.BlockSpec((2 * half, d), lambda b, i: (0, 0), pipeline_mode=pl.Buffered(1)),
        pl.BlockSpec((1, tm, d), lambda b, i: (b, i, 0)),
        pl.BlockSpec((1, 3, d), lambda b, i: (b, 0, 0)),
    ]
    args = [ya, yb, w_out, x, mod_l]
    if final_norm:
        in_specs.append(pl.BlockSpec((1, d), lambda b, i: (0, 0)))
        args.append(final_w.reshape(1, d))
    return pl.pallas_call(
        functools.partial(_outproj_kernel, final_norm=final_norm, half=half),
        out_shape=jax.ShapeDtypeStruct((bsz, s, d), F32),
        grid=(bsz, s // tm),
        in_specs=in_specs,
        out_specs=pl.BlockSpec((1, tm, d), lambda b, i: (b, i, 0)),
        scratch_shapes=[pltpu.VMEM((2 * half, d), BF16)],
        compiler_params=_cparams(("arbitrary", "arbitrary")),
        name="outproj_residual",
    )(*args)


def _rms_rope(x, nw, cos, sin_signed):
    ms = jnp.mean(x * x, axis=-1, keepdims=True)
    y = x * lax.rsqrt(ms + EPS) * nw
    lane = lax.broadcasted_iota(jnp.int32, y.shape, 1)
    partner = jnp.where(lane % 2 == 0, pltpu.roll(y, ATT_HEAD_DIM - 1, 1), pltpu.roll(y, 1, 1))
    return y * cos + partner * sin_signed


def _attention_kv_prep(k_ref, v_ref, ck_ref, sk_ref, kn_ref, k_scr, v_scr):
    dh = ATT_HEAD_DIM
    k_scr[...] = _rms_rope(k_ref[0], kn_ref[...], ck_ref[...], sk_ref[...]).astype(BF16)
    v_scr[:, 0:dh] = v_ref[0].astype(BF16)
    v_scr[:, dh:2 * dh] = jnp.ones((v_scr.shape[0], dh), BF16)


ATTN_AHEAD = 1


def _attention_pump(q_ref, ga_ref, cq_ref, sq_ref, qn_ref, k_scr, v_scr, o_ref, sub):
    dh = ATT_HEAD_DIM
    tq, width = q_ref.shape[1], q_ref.shape[2]
    scale = dh ** -0.5
    probs = [(g, r) for g in range(width // dh) for r in range(0, tq, sub)]
    ahead = ATTN_AHEAD
    state = {"i": 0, "s": {}}

    def scores(g, r):
        qg = q_ref[0, r:r + sub, g * dh:(g + 1) * dh]
        qg = (_rms_rope(qg, qn_ref[...], cq_ref[r:r + sub, :], sq_ref[r:r + sub, :]) * scale).astype(BF16)
        return _dot_nt(qg, k_scr[...])

    def finish(g, r, s):
        s = s.astype(BF16)
        p = jnp.exp(s - jnp.max(s, axis=-1, keepdims=True))
        o_ext = _dot(p, v_scr[...])
        og = o_ext[:, 0:dh] / o_ext[:, dh:2 * dh] * _silu(ga_ref[0, r:r + sub, g * dh:(g + 1) * dh])
        o_ref[0, r:r + sub, g * dh:(g + 1) * dh] = og.astype(BF16)

    def pump():
        i = state["i"]
        if i >= len(probs) + ahead:
            return
        if i < len(probs):
            state["s"][i] = scores(*probs[i])
        if i >= ahead:
            finish(*probs[i - ahead], state["s"].pop(i - ahead))
        state["i"] = i + 1

    return pump, len(probs) + ahead


def _attn_kernel(q_ref, k_ref, v_ref, ga_ref, cq_ref, sq_ref, ck_ref, sk_ref, qn_ref, kn_ref,
                 o_ref, k_scr, v_scr, *, sub):
    @pl.when(pl.program_id(2) == 0)
    def _():
        _attention_kv_prep(k_ref, v_ref, ck_ref, sk_ref, kn_ref, k_scr, v_scr)

    pump, n = _attention_pump(q_ref, ga_ref, cq_ref, sq_ref, qn_ref, k_scr, v_scr, o_ref, sub)
    for _ in range(n):
        pump()


def _attention(proj, cos_t, sin_t, q_norm, k_norm, *, col_q, col_k, col_v, col_ga, tq=512, sub=128):
    bsz, s, _ = proj.shape
    tq = min(tq, s)
    gw = ATT_GROUP * ATT_HEAD_DIM
    dh = ATT_HEAD_DIM
    qb, kb, vb, gb = col_q // gw, col_k // dh, col_v // dh, col_ga // gw
    return pl.pallas_call(
        functools.partial(_attn_kernel, sub=min(sub, tq)),
        out_shape=jax.ShapeDtypeStruct((bsz, s, BRANCH_W), BF16),
        grid=(bsz, ATT_KV_HEADS, s // tq),
        in_specs=[
            pl.BlockSpec((1, tq, gw), lambda b, h, i: (b, i, qb + h)),
            pl.BlockSpec((1, s, dh), lambda b, h, i: (b, 0, kb + h)),
            pl.BlockSpec((1, s, dh), lambda b, h, i: (b, 0, vb + h)),
            pl.BlockSpec((1, tq, gw), lambda b, h, i: (b, i, gb + h)),
            pl.BlockSpec((tq, dh), lambda b, h, i: (i, 0)),
            pl.BlockSpec((tq, dh), lambda b, h, i: (i, 0)),
            pl.BlockSpec((s, dh), lambda b, h, i: (0, 0)),
            pl.BlockSpec((s, dh), lambda b, h, i: (0, 0)),
            pl.BlockSpec((1, dh), lambda b, h, i: (0, 0)),
            pl.BlockSpec((1, dh), lambda b, h, i: (0, 0)),
        ],
        out_specs=pl.BlockSpec((1, tq, gw), lambda b, h, i: (b, i, h)),
        scratch_shapes=[pltpu.VMEM((s, dh), BF16), pltpu.VMEM((s, 2 * dh), BF16)],
        compiler_params=_cparams(("arbitrary", "arbitrary", "arbitrary")),
        name="gqa_attention",
    )(proj, proj, proj, proj, cos_t, sin_t, cos_t, sin_t, q_norm, k_norm)


def _rope_tables(s):
    t = np.arange(s)
    row = (t // GRID_W).astype(np.float64)
    col = (t % GRID_W).astype(np.float64)
    n_pairs = ATT_HEAD_DIM // 4
    freqs = ROPE_THETA ** (-np.arange(n_pairs, dtype=np.float64) / n_pairs)
    ang = np.concatenate([row[:, None] * freqs, col[:, None] * freqs], axis=-1)
    cos, sin = np.cos(ang), np.sin(ang)
    cos_t = np.repeat(cos, 2, axis=-1)
    sin_t = np.stack([-sin, sin], axis=-1).reshape(s, ATT_HEAD_DIM)
    return jnp.asarray(cos_t, F32), jnp.asarray(sin_t, F32)


def _chunk_of_step(step, nc):
    return jnp.where(step < nc, step, 2 * nc - 1 - step)


def _fill_conv_window(ext_scr, cur, prev, nxt, c, nc, rows):
    ext_scr[0:HALO, :] = jnp.where(c > 0, prev, 0.0)
    ext_scr[HALO:HALO + rows, :] = cur
    ext_scr[HALO + rows:HALO + rows + HALO, :] = jnp.where(c < nc - 1, nxt, 0.0)


CONV_SUB = 128


def _centred_conv4(ext_scr, cw_ref, cb_ref, rows):
    sub = min(rows, CONV_SUB)
    n_in = sub + 2 * HALO
    taps = (0, 1, 3)
    r = lax.broadcasted_iota(jnp.int32, (len(taps) * sub, n_in), 0)
    c = lax.broadcasted_iota(jnp.int32, (len(taps) * sub, n_in), 1)
    src = jnp.zeros_like(r)
    for i, k in enumerate(taps):
        src = jnp.where((r >= i * sub) & (r < (i + 1) * sub), r - i * sub + HALO + (k - 2), src)
    shift_mat = jnp.where(c == src, 1.0, 0.0).astype(BF16)
    outs = []
    for s0 in range(0, rows, sub):
        win = ext_scr[s0:s0 + n_in, :]
        shifted = _dot(shift_mat, win.astype(BF16))
        acc = cb_ref[...] + win[HALO:HALO + sub, :] * cw_ref[2:3, :]
        for i, k in enumerate(taps):
            acc = acc + shifted[i * sub:(i + 1) * sub, :] * cw_ref[k:k + 1, :]
        outs.append(acc)
    return outs[0] if len(outs) == 1 else jnp.concatenate(outs, axis=0)


def _scan_masks(fwd, n):
    row = lax.broadcasted_iota(jnp.int32, (n, n), 0)
    col = lax.broadcasted_iota(jnp.int32, (n, n), 1)
    sgn = jnp.where(fwd, 1, -1)
    d = (row - col) * sgn
    return d >= 0, d <= 0, d > 0


def _pad_dir_rows(v, n):
    return jnp.zeros((2, 1, LANES), F32).at[:, 0, :n].set(v)


def _pad_dir_cols(v, n):
    return jnp.zeros((2, LANES, 1), F32).at[:, :n, 0].set(v)


def _head_expander(heads, width):
    r = lax.broadcasted_iota(jnp.int32, (heads, heads * width), 0)
    c = lax.broadcasted_iota(jnp.int32, (heads, heads * width), 1)
    return jnp.where((c >= r * width) & (c < (r + 1) * width), 1.0, 0.0).astype(BF16)


def _ssd_kernel(xf_ref, pf_ref, nf_ref, xb_ref, pb_ref, nb_ref, z_ref, dtf_ref, dtb_ref, cw_ref, cb_ref,
                bias_r_ref, bias_c_ref, alog_r_ref, alog_c_ref, dskip_ref, nw_ref,
                o_ref, ext_scr, state_scr, acc_scr, *, nc, te):
    L, H, P, N = SSD_CHUNK, SSD_HEADS, SSD_HEAD_DIM, SSD_STATE
    HG = H // SSD_GROUPS
    GW = HG * P
    nbt = acc_scr.shape[0]
    t = pl.program_id(1)

    @pl.when(t == 0)
    def _():
        state_scr[...] = jnp.zeros_like(state_scr)

    @pl.when(t < nc)
    def _():
        chunks = (t, nc - 1 - t)
        blocks = ((xf_ref, pf_ref, nf_ref, dtf_ref), (xb_ref, pb_ref, nb_ref, dtb_ref))
        expand = _head_expander(H, P)
        lane = lax.broadcasted_iota(jnp.int32, (L, 2 * P), 1)
        per_dir = []
        for bi, d in [(bi, d) for bi in range(nbt) for d in range(2)]:
            x_ref, p_ref, n_ref, dtt_ref = blocks[d]
            ext = ext_scr.at[bi * 2 + d]
            _fill_conv_window(ext, x_ref[bi], p_ref[bi], n_ref[bi], chunks[d], nc, L)
            xbc = _silu(_centred_conv4(ext, cw_ref, cb_ref, L))
            xs = xbc[:, :BRANCH_W]
            bs = xbc[:, BRANCH_W:BRANCH_W + SSD_GROUPS * N]
            cs = xbc[:, BRANCH_W + SSD_GROUPS * N:]
            raw_t = dtt_ref[bi]
            raw = raw_t.T
            dt = _softplus(raw[:, d * H:(d + 1) * H] + bias_r_ref[d][:, 0:H])
            a = dt * (-jnp.exp(alog_r_ref[d][:, 0:H]))
            dt_t = _softplus(raw_t[d * H:(d + 1) * H, :] + bias_c_ref[d][0:H, :])
            a_t = dt_t * (-jnp.exp(alog_c_ref[d][0:H, :]))
            mask, mask_t, _ = _scan_masks(d == 0, L)
            cum = _dot_exact_lhs(jnp.where(mask, 1.0, 0.0).astype(BF16), a)
            cum_t = _dot_exact_rhs(a_t, jnp.where(mask_t, 1.0, 0.0).astype(BF16))
            total = jnp.sum(a, axis=0, keepdims=True)
            dt_e = _dot(dt.astype(BF16), expand)
            p_e = _dot(jnp.exp(cum).astype(BF16), expand)
            q_e = _dot(jnp.exp(total - cum).astype(BF16), expand)
            tot_e = _dot_exact_rhs(jnp.broadcast_to(jnp.exp(total), (SUBLANES, H)), expand)[0:1, :]
            xd = xs * dt_e
            per_dir.append(dict(xs=xs, bs=bs, cs=cs, mask=mask, cum=cum, cum_t=cum_t, p_e=p_e, tot_e=tot_e,
                                xdq=(xd * q_e).astype(BF16), xd_b=xd.astype(BF16)))

        y_dirs = {}
        for bi in range(nbt):
            combos = [(bi * 2 + d, g) for d in range(2) for g in range(SSD_GROUPS)]
            cg = {k: per_dir[k[0]]["cs"][:, k[1] * N:(k[1] + 1) * N].astype(BF16) for k in combos}
            bg = {k: per_dir[k[0]]["bs"][:, k[1] * N:(k[1] + 1) * N] for k in combos}
            gmat = {k: _dot_nt(cg[k], bg[k].astype(BF16)) for k in combos}
            h_prev = {k: state_scr[k[0], :, k[1] * GW:(k[1] + 1) * GW] for k in combos}
            y_off = {k: _dot(cg[k], h_prev[k].astype(BF16)) * per_dir[k[0]]["p_e"][:, k[1] * GW:(k[1] + 1) * GW]
                     for k in combos}
            pairs = {k: [] for k in combos}
            for hp in range(HG // 2):
                for k in combos:
                    j, g = k
                    pd = per_dir[j]
                    h0 = g * HG + 2 * hp
                    xpair = pd["xd_b"][:, h0 * P:(h0 + 2) * P]
                    ys = []
                    for h in (h0, h0 + 1):
                        dec = jnp.exp(jnp.where(pd["mask"], pd["cum"][:, h:h + 1] - pd["cum_t"][h:h + 1, :],
                                                NEG_BIG))
                        ys.append(_dot((gmat[k] * dec).astype(BF16), xpair))
                    pairs[k].append(jnp.where(lane < P, ys[0], ys[1]))
            for k in combos:
                j, g = k
                state_scr[j, :, g * GW:(g + 1) * GW] = (
                    h_prev[k] * per_dir[j]["tot_e"][:, g * GW:(g + 1) * GW]
                    + _dot(bg[k].T.astype(BF16), per_dir[j]["xdq"][:, g * GW:(g + 1) * GW]))
            for d in range(2):
                j = bi * 2 + d
                y_dirs[j] = jnp.concatenate([jnp.concatenate(pairs[(j, g)], axis=1) + y_off[(j, g)]
                                             for g in range(SSD_GROUPS)], axis=1)
            y_dirs[bi * 2] = y_dirs[bi * 2] + dskip_ref[...] * per_dir[bi * 2]["xs"]
        rows = [pl.multiple_of(c * L, L) for c in chunks]

        @pl.when(t < nc // 2)
        def _():
            for bi in range(nbt):
                for d in range(2):
                    acc_scr[bi, pl.ds(rows[d], L), :] = y_dirs[bi * 2 + d]

        @pl.when(t >= nc // 2)
        def _():
            for bi in range(nbt):
                for d in range(2):
                    acc_scr[bi, pl.ds(rows[d], L), :] = acc_scr[bi, pl.ds(rows[d], L), :] + y_dirs[bi * 2 + d]

    @pl.when(t >= nc)
    def _():
        r0 = pl.multiple_of((t - nc) * te, te)
        for bi in range(nbt):
            y = acc_scr[bi, pl.ds(r0, te), :] * _silu(z_ref[bi])
            ms = jnp.mean(y * y, axis=-1, keepdims=True)
            o_ref[bi] = (y * lax.rsqrt(ms + EPS) * nw_ref[...]).astype(BF16)


def _ssd(proj, small_t, conv_w, conv_b, dt_bias, a_log, d_skip, norm_w, *, col_xbc, col_z, te=512, nbt=2):
    bsz, s, _ = proj.shape
    L = SSD_CHUNK
    nc = s // L
    te = min(te, s)
    ne = s // te
    cw = BRANCH_W + 2 * SSD_GROUPS * SSD_STATE
    hb = L // HALO
    xb, zb = col_xbc // cw, col_z // BRANCH_W
    assert col_xbc % cw == 0 and col_z % BRANCH_W == 0 and nc % 2 == 0 and bsz % nbt == 0

    def cf(t):
        return jnp.minimum(t, nc - 1)

    def cbk(t):
        return jnp.maximum(nc - 1 - t, 0)

    def ep(t):
        return jnp.maximum(t - nc, 0)

    def xbc_specs(chunk):
        return [
            pl.BlockSpec((nbt, L, cw), lambda b, t: (b, chunk(t), xb)),
            pl.BlockSpec((nbt, HALO, cw), lambda b, t: (b, jnp.maximum(chunk(t) * hb - 1, 0), xb)),
            pl.BlockSpec((nbt, HALO, cw), lambda b, t: (b, jnp.minimum((chunk(t) + 1) * hb, s // HALO - 1), xb)),
        ]

    def small_spec(chunk):
        return pl.BlockSpec((nbt, LANES, L), lambda b, t: (b, 0, chunk(t)))

    full = lambda shape: pl.BlockSpec(shape, lambda b, t: (0,) * len(shape))
    in_specs = (xbc_specs(cf) + xbc_specs(cbk)
                + [pl.BlockSpec((nbt, te, BRANCH_W), lambda b, t: (b, ep(t), zb))]
                + [small_spec(cf), small_spec(cbk)]
                + [full((SSD_CONV, cw)), full((1, cw)), full((2, 1, LANES)), full((2, LANES, 1)),
                   full((2, 1, LANES)), full((2, LANES, 1)), full((1, BRANCH_W)), full((1, BRANCH_W))])
    args = [proj, proj, proj, proj, proj, proj, proj, small_t, small_t, conv_w, conv_b.reshape(1, cw),
            _pad_dir_rows(dt_bias, SSD_HEADS), _pad_dir_cols(dt_bias, SSD_HEADS),
            _pad_dir_rows(a_log, SSD_HEADS), _pad_dir_cols(a_log, SSD_HEADS),
            jnp.repeat(d_skip, SSD_HEAD_DIM).reshape(1, BRANCH_W), norm_w.reshape(1, BRANCH_W)]
    out_shape = [jax.ShapeDtypeStruct((bsz, s, BRANCH_W), BF16)]
    out_specs = [pl.BlockSpec((nbt, te, BRANCH_W), lambda b, t: (b, ep(t), 0))]
    scratch = [
        pltpu.VMEM((nbt * 2, L + 2 * HALO, cw), F32),
        pltpu.VMEM((nbt * 2, SSD_STATE, BRANCH_W), F32),
        pltpu.VMEM((nbt, s, BRANCH_W), F32),
    ]
    return pl.pallas_call(
        functools.partial(_ssd_kernel, nc=nc, te=te),
        out_shape=out_shape,
        grid=(bsz // nbt, nc + ne),
        in_specs=in_specs,
        out_specs=out_specs,
        scratch_shapes=scratch,
        compiler_params=_cparams(("arbitrary", "arbitrary")),
        name="bidir_ssd",
    )(*args)[0]


AB_COL_XBC, AB_COL_K, AB_COL_V, AB_COL_Z, AB_COL_Q, AB_COL_GA, AB_N = 0, 1536, 1792, 2048, 3072, 4096, 5120


REGROUP_ROWS = 256


def _regroup_kernel(offs_ref, w_ref, o_ref):
    del offs_ref
    o_ref[...] = w_ref[...].astype(BF16)


def _regroup_cast(wt, pieces):
    rows = REGROUP_ROWS
    offsets = []
    for start, n in pieces:
        assert n % rows == 0 and start % SUBLANES == 0
        offsets += list(range(start, start + n, rows))
    d = wt.shape[1]
    return pl.pallas_call(
        _regroup_kernel,
        out_shape=jax.ShapeDtypeStruct((len(offsets) * rows, d), BF16),
        grid_spec=pltpu.PrefetchScalarGridSpec(
            num_scalar_prefetch=1,
            grid=(len(offsets),),
            in_specs=[pl.BlockSpec((pl.Element(rows), pl.Element(d)), lambda i, offs: (pl.multiple_of(offs[i], SUBLANES), 0))],
            out_specs=pl.BlockSpec((rows, d), lambda i, offs: (i, 0)),
        ),
        compiler_params=_cparams(("arbitrary",)),
        name="weight_regroup_cast",
    )(jnp.asarray(offsets, jnp.int32), wt)


def _piece_table(sizes):
    table, o = {}, 0
    for name, n in sizes:
        table[name] = (o, n)
        o += n
    return table


def _ab_weights(w_in):
    wt = w_in.T
    hq, hk = ATT_HEADS * ATT_HEAD_DIM, ATT_KV_HEADS * ATT_HEAD_DIM
    gn = SSD_GROUPS * SSD_STATE
    p = _piece_table([("q", hq), ("k", hk), ("v", hk), ("ga", BRANCH_W), ("xs", BRANCH_W), ("bs", gn),
                      ("cs", gn), ("dtf", SSD_HEADS), ("dtb", SSD_HEADS), ("z", BRANCH_W)])
    main = _regroup_cast(wt, [p[n] for n in ("xs", "bs", "cs", "k", "v", "z", "q", "ga")])
    dt0, dtn = p["dtf"][0], 2 * SSD_HEADS
    zpad = jnp.zeros((LANES - dtn, wt.shape[1]), wt.dtype)
    small = jnp.concatenate([wt[dt0:dt0 + dtn], zpad], axis=0).astype(BF16)
    return main, small


def _layer0(x, mod_l, norm_w, w_in, q_norm, k_norm, conv_w, conv_b, dt_bias_f, dt_bias_b,
            a_log_f, a_log_b, d_skip, ssd_norm, w_out, final_w=None):
    s = x.shape[1]
    w_main, w_small = _ab_weights(w_in)
    proj, small_t = _inproj(x, mod_l, norm_w, w_main, w_small)
    cos_t, sin_t = _rope_tables(s)
    att = _attention(proj, cos_t, sin_t, q_norm.reshape(1, ATT_HEAD_DIM), k_norm.reshape(1, ATT_HEAD_DIM),
                     col_q=AB_COL_Q, col_k=AB_COL_K, col_v=AB_COL_V, col_ga=AB_COL_GA)
    ssd = _ssd(proj, small_t, conv_w, conv_b, jnp.stack([dt_bias_f, dt_bias_b]),
               jnp.stack([a_log_f, a_log_b]), d_skip, ssd_norm, col_xbc=AB_COL_XBC, col_z=AB_COL_Z)
    return _outproj(att, ssd, w_out, x, mod_l, final_w), (proj, small_t, att, ssd)


def _unit_tri_inverse(nmats, n):
    row = lax.broadcasted_iota(jnp.int32, (n, n), 0)
    col = lax.broadcasted_iota(jnp.int32, (n, n), 1)

    def same_block(size):
        return (row // size) == (col // size)

    def mm(a, b):
        return _dot(a, b).astype(BF16)

    def as_mask(cond):
        return jnp.where(cond, 1.0, 0.0).astype(BF16)

    nmats = [m.astype(BF16) for m in nmats]
    eye = as_mask(row == col)
    base = SUBLANES
    blk = same_block(base)
    blk_m = as_mask(blk)
    nd = [m * blk_m for m in nmats]
    p1 = [mm(x, x) for x in nd]
    p2 = [mm(x, x) for x in p1]
    t = [eye - x for x in nd]
    t = [x + mm(x, p) for x, p in zip(t, p1)]
    t = [x + mm(x, p) for x, p in zip(t, p2)]
    size = base
    while size < n:
        nxt = same_block(2 * size)
        off_m = as_mask(nxt & jnp.logical_not(blk))
        et = [mm(m * off_m, x) for m, x in zip(nmats, t)]
        t = [x - mm(x, y) for x, y in zip(t, et)]
        blk = nxt
        size *= 2
    return t


def _l2norm(x):
    return x * lax.rsqrt(jnp.sum(x * x, axis=-1, keepdims=True) + EPS)


def _gdn_kernel(qf_ref, pf_ref, nf_ref, qb_ref, pb_ref, nb_ref, z_ref, smtf_ref, smtb_ref,
                cw_ref, cb_ref, bias_r_ref, bias_c_ref, alog_r_ref, alog_c_ref, nw_ref,
                o_ref, ext_scr, state_scr, acc_scr, *, nc, te):
    L, HV, HQ, DK = GDN_CHUNK, GDN_V_HEADS, GDN_QK_HEADS, GDN_HEAD_DIM
    rep = HV // HQ
    nbt = acc_scr.shape[0]
    t = pl.program_id(1)

    @pl.when(t == 0)
    def _():
        state_scr[...] = jnp.zeros_like(state_scr)

    @pl.when(t < nc)
    def _():
        chunks = (t, nc - 1 - t)
        blocks = ((qf_ref, pf_ref, nf_ref, smtf_ref), (qb_ref, pb_ref, nb_ref, smtb_ref))
        kk, qk, qh, kh, vh, colv, dec, bcol, ecol, tot, nmat = ([] for _ in range(11))
        for bi, d in [(bi, d) for bi in range(nbt) for d in range(2)]:
            c = chunks[d]
            q_ref, p_ref, n_ref, smt_ref = blocks[d]
            ext = ext_scr.at[bi * 2 + d]
            _fill_conv_window(ext, q_ref[bi], p_ref[bi], n_ref[bi], c, nc, L)
            act = _silu(_centred_conv4(ext, cw_ref, cb_ref, L))
            q_n = [_l2norm(act[:, h * DK:(h + 1) * DK]) * (DK ** -0.5) for h in range(HQ)]
            k_n = [_l2norm(act[:, (HQ + h) * DK:(HQ + h + 1) * DK]) for h in range(HQ)]
            k_b = [x.astype(BF16) for x in k_n]
            kk_d = [_dot_nt(k_b[h], k_b[h]) for h in range(HQ)]
            qk_d = [_dot_nt(q_n[h].astype(BF16), k_b[h]) for h in range(HQ)]

            raw_t = smt_ref[bi]
            raw = raw_t.T
            a_raw = raw[:, (2 + d) * HV:(3 + d) * HV]
            a_raw_t = raw_t[(2 + d) * HV:(3 + d) * HV, :]
            beta = _sigmoid(raw[:, d * HV:(d + 1) * HV])
            g = -jnp.exp(alog_r_ref[d][:, 0:HV]) * _softplus(a_raw + bias_r_ref[d][:, 0:HV])
            g_t = -jnp.exp(alog_c_ref[d][0:HV, :]) * _softplus(a_raw_t + bias_c_ref[d][0:HV, :])
            mask, mask_t, strict = _scan_masks(d == 0, L)
            cum = _dot_exact_lhs(jnp.where(mask, 1.0, 0.0).astype(BF16), g)
            cum_t = _dot_exact_rhs(g_t, jnp.where(mask_t, 1.0, 0.0).astype(BF16))
            total = jnp.sum(g, axis=0, keepdims=True)
            for h in range(HV):
                cv = cum[:, h:h + 1]
                dc = jnp.exp(jnp.where(mask, cv - cum_t[h:h + 1, :], NEG_BIG))
                bc = beta[:, h:h + 1]
                kk.append(kk_d[h // rep]); qk.append(qk_d[h // rep])
                qh.append(q_n[h // rep]); kh.append(k_n[h // rep])
                vh.append(act[:, (2 * HQ + h) * DK:(2 * HQ + h + 1) * DK])
                colv.append(cv); dec.append(dc); bcol.append(bc); ecol.append(jnp.exp(cv))
                tot.append(total[:, h:h + 1])
                nmat.append(jnp.where(strict, kk_d[h // rep] * bc * dc, 0.0))

        outs = []
        for w0 in range(0, nbt * 2 * HV, GDN_WAVE):
            idx = range(w0, w0 + GDN_WAVE)
            t_inv = dict(zip(idx, _unit_tri_inverse([nmat[i] for i in idx], L)))
            u = {i: _dot(t_inv[i], (vh[i] * bcol[i]).astype(BF16)) for i in idx}
            w = {i: _dot(t_inv[i], (kh[i] * (bcol[i] * ecol[i])).astype(BF16)) for i in idx}
            s_prev = {i: state_scr[i // HV, :, (i % HV) * DK:(i % HV + 1) * DK] for i in idx}
            s_b = {i: s_prev[i].astype(BF16) for i in idx}
            v_new = {i: (u[i] - _dot(w[i].astype(BF16), s_b[i])).astype(BF16) for i in idx}
            outs += [_dot((qh[i] * ecol[i]).astype(BF16), s_b[i])
                     + _dot((qk[i] * dec[i]).astype(BF16), v_new[i]) for i in idx]
            for i in idx:
                k_dec = kh[i] * jnp.exp(tot[i] - colv[i])
                state_scr[i // HV, :, (i % HV) * DK:(i % HV + 1) * DK] = (
                    s_prev[i] * jnp.exp(tot[i]) + _dot(k_dec.T.astype(BF16), v_new[i]))
        o_dirs = [jnp.concatenate(outs[j * HV:(j + 1) * HV], axis=1) for j in range(nbt * 2)]
        rows = [pl.multiple_of(c * L, L) for c in chunks]

        @pl.when(t < nc // 2)
        def _():
            for bi in range(nbt):
                for d in range(2):
                    acc_scr[bi, pl.ds(rows[d], L), :] = o_dirs[bi * 2 + d]

        @pl.when(t >= nc // 2)
        def _():
            for bi in range(nbt):
                for d in range(2):
                    acc_scr[bi, pl.ds(rows[d], L), :] = acc_scr[bi, pl.ds(rows[d], L), :] + o_dirs[bi * 2 + d]

    @pl.when(t >= nc)
    def _():
        r0 = pl.multiple_of((t - nc) * te, te)
        for bi in range(nbt):
            zz = z_ref[bi]
            for hv in range(HV):
                oh = acc_scr[bi, pl.ds(r0, te), hv * DK:(hv + 1) * DK]
                ms = jnp.mean(oh * oh, axis=-1, keepdims=True)
                res = oh * lax.rsqrt(ms + EPS) * nw_ref[...] * _silu(zz[:, hv * DK:(hv + 1) * DK])
                o_ref[bi, :, hv * DK:(hv + 1) * DK] = res.astype(BF16)


def _gdn(proj, small_t, conv_w, conv_b, dt_bias, a_log, norm_w, *, col_qkv, col_z, te=512, nbt=2):
    bsz, s, _ = proj.shape
    L = GDN_CHUNK
    nc = s // L
    te = min(te, s)
    ne = s // te
    cw = 2 * GDN_QK_HEADS * GDN_HEAD_DIM + BRANCH_W
    hb = L // HALO
    qb, zb = col_qkv // cw, col_z // BRANCH_W
    assert col_qkv % cw == 0 and col_z % BRANCH_W == 0 and nc % 2 == 0 and bsz % nbt == 0

    def cf(t):
        return jnp.minimum(t, nc - 1)

    def cbk(t):
        return jnp.maximum(nc - 1 - t, 0)

    def ep(t):
        return jnp.maximum(t - nc, 0)

    def qkv_specs(chunk):
        return [
            pl.BlockSpec((nbt, L, cw), lambda b, t: (b, chunk(t), qb)),
            pl.BlockSpec((nbt, HALO, cw), lambda b, t: (b, jnp.maximum(chunk(t) * hb - 1, 0), qb)),
            pl.BlockSpec((nbt, HALO, cw), lambda b, t: (b, jnp.minimum((chunk(t) + 1) * hb, s // HALO - 1), qb)),
        ]

    def small_spec(chunk):
        return pl.BlockSpec((nbt, LANES, L), lambda b, t: (b, 0, chunk(t)))

    full = lambda shape: pl.BlockSpec(shape, lambda b, t: (0,) * len(shape))
    hv = GDN_V_HEADS
    return pl.pallas_call(
        functools.partial(_gdn_kernel, nc=nc, te=te),
        out_shape=jax.ShapeDtypeStruct((bsz, s, BRANCH_W), BF16),
        grid=(bsz // nbt, nc + ne),
        in_specs=qkv_specs(cf) + qkv_specs(cbk)
        + [pl.BlockSpec((nbt, te, BRANCH_W), lambda b, t: (b, ep(t), zb))]
        + [small_spec(cf), small_spec(cbk)]
        + [full((GDN_CONV, cw)), full((1, cw)), full((2, 1, LANES)), full((2, LANES, 1)),
           full((2, 1, LANES)), full((2, LANES, 1)), full((1, GDN_HEAD_DIM))],
        out_specs=pl.BlockSpec((nbt, te, BRANCH_W), lambda b, t: (b, ep(t), 0)),
        scratch_shapes=[
            pltpu.VMEM((nbt * 2, L + 2 * HALO, cw), F32),
            pltpu.VMEM((nbt * 2, GDN_HEAD_DIM, BRANCH_W), F32),
            pltpu.VMEM((nbt, s, BRANCH_W), F32),
        ],
        compiler_params=_cparams(("arbitrary", "arbitrary")),
        name="bidir_gated_deltanet",
    )(proj, proj, proj, proj, proj, proj, proj, small_t, small_t, conv_w, conv_b.reshape(1, cw),
      _pad_dir_rows(dt_bias, hv), _pad_dir_cols(dt_bias, hv), _pad_dir_rows(a_log, hv), _pad_dir_cols(a_log, hv),
      norm_w.reshape(1, GDN_HEAD_DIM))


def _local_scan(a, u, ascending):
    rows, width = a.shape
    a = a.reshape(rows // SUBLANES, SUBLANES, width)
    u = u.reshape(rows // SUBLANES, SUBLANES, width)
    sub = lax.broadcasted_iota(jnp.int32, a.shape, 1)
    d = 1
    while d < SUBLANES:
        if ascending:
            keep = sub >= d
            shift = d
        else:
            keep = sub < SUBLANES - d
            shift = SUBLANES - d
        a_sh = jnp.where(keep, pltpu.roll(a, shift, 1), 1.0)
        u_sh = jnp.where(keep, pltpu.roll(u, shift, 1), 0.0)
        u = u + a * u_sh
        a = a * a_sh
        d *= 2
    return a.reshape(rows, width), u.reshape(rows, width)


def _lru_kernel(xl_ref, prev_ref, next_ref, gl_ref, cw_ref, cb_ref, wa_ref, ba_ref, wx_ref, bx_ref, lam_ref,
                o_ref, ext_scr, a_scr, u_scr, hf_scr, carry_scr, *, nb, unroll):
    rows, width = a_scr.shape
    bw = LRU_WIDTH // LRU_BLOCKS
    step = pl.program_id(2)
    fwd = step < nb
    blk = _chunk_of_step(step, nb)

    @pl.when((step == 0) | (step == nb))
    def _():
        carry_scr[...] = jnp.zeros_like(carry_scr)

    _fill_conv_window(ext_scr, xl_ref[0], prev_ref[0], next_ref[0], blk, nb, rows)
    xc = _centred_conv4(ext_scr, cw_ref, cb_ref, rows)
    r_parts, i_parts = [], []
    for n in range(width // bw):
        xb = xc[:, n * bw:(n + 1) * bw].astype(BF16)
        r_parts.append(_dot(xb, wa_ref[0, n]))
        i_parts.append(_dot(xb, wx_ref[0, n]))
    tr = jnp.tanh(jnp.concatenate(r_parts, axis=1) + ba_ref[0])
    ti = jnp.tanh(jnp.concatenate(i_parts, axis=1) + bx_ref[0])
    log_a = ((-0.5 * LRU_C) * _softplus(-lam_ref[0])) * (tr + 1.0)
    a = jnp.exp(log_a)
    quarter = (-0.25 * jnp.tanh(log_a)) * (a * a + 1.0)
    u = (quarter * lax.rsqrt(jnp.maximum(quarter, TINY))) * ((ti + 1.0) * xc)

    n_groups = rows // SUBLANES
    base_out = pl.multiple_of(blk * rows, rows)

    def scan(ascending):
        a_loc, u_loc = _local_scan(a, u, ascending)
        a_scr[...] = a_loc
        u_scr[...] = u_loc
        last = SUBLANES - 1 if ascending else 0

        def body(it, carry):
            pos = it if ascending else n_groups - 1 - it
            r0 = pl.multiple_of(pos * SUBLANES, SUBLANES)
            h = u_scr[pl.ds(r0, SUBLANES), :] + a_scr[pl.ds(r0, SUBLANES), :] * carry
            u_scr[pl.ds(r0, SUBLANES), :] = h
            return jnp.broadcast_to(h[last:last + 1, :], h.shape)

        carry_scr[...] = lax.fori_loop(0, n_groups, body, carry_scr[...], unroll=unroll)

    @pl.when(fwd)
    def _():
        scan(True)
        hf_scr[pl.ds(base_out, rows), :] = u_scr[...]

    @pl.when(jnp.logical_not(fwd))
    def _():
        scan(False)
        o_ref[0] = ((hf_scr[pl.ds(base_out, rows), :] + u_scr[...]) * _silu(gl_ref[0])).astype(BF16)


def _lru(proj, conv_w, conv_b, wa, ba, wx, bx, lam, *, col_xl, col_gl, tt=1024, width=1024, unroll=8):
    bsz, s, _ = proj.shape
    tt = min(tt, s)
    nb = s // tt
    w_total = LRU_WIDTH
    bw = w_total // LRU_BLOCKS
    nbw = width // bw
    xcol, gcol = col_xl // width, col_gl // width
    assert col_xl % width == 0 and col_gl % width == 0
    hb = tt // HALO

    def blk(t):
        return _chunk_of_step(t, nb)

    def direction(t):
        return jnp.where(t < nb, 0, 1)

    return pl.pallas_call(
        functools.partial(_lru_kernel, nb=nb, unroll=unroll),
        out_shape=jax.ShapeDtypeStruct((bsz, s, w_total), BF16),
        grid=(bsz, w_total // width, 2 * nb),
        in_specs=[
            pl.BlockSpec((1, tt, width), lambda b, j, t: (b, blk(t), xcol + j)),
            pl.BlockSpec((1, HALO, width), lambda b, j, t: (b, jnp.maximum(blk(t) * hb - 1, 0), xcol + j)),
            pl.BlockSpec((1, HALO, width),
                         lambda b, j, t: (b, jnp.minimum((blk(t) + 1) * hb, s // HALO - 1), xcol + j)),
            pl.BlockSpec((1, tt, width), lambda b, j, t: (b, blk(t), gcol + j)),
            pl.BlockSpec((LRU_CONV, width), lambda b, j, t: (0, j)),
            pl.BlockSpec((1, width), lambda b, j, t: (0, j)),
            pl.BlockSpec((1, nbw, bw, bw), lambda b, j, t: (direction(t), j, 0, 0)),
            pl.BlockSpec((1, 1, width), lambda b, j, t: (direction(t), 0, j)),
            pl.BlockSpec((1, nbw, bw, bw), lambda b, j, t: (direction(t), j, 0, 0)),
            pl.BlockSpec((1, 1, width), lambda b, j, t: (direction(t), 0, j)),
            pl.BlockSpec((1, 1, width), lambda b, j, t: (direction(t), 0, j)),
        ],
        out_specs=pl.BlockSpec((1, tt, width),
                               lambda b, j, t: (b, jnp.where(t < nb, nb - 1, 2 * nb - 1 - t), j)),
        scratch_shapes=[
            pltpu.VMEM((tt + 2 * HALO, width), F32),
            pltpu.VMEM((tt, width), F32),
            pltpu.VMEM((tt, width), F32),
            pltpu.VMEM((s, width), F32),
            pltpu.VMEM((SUBLANES, width), F32),
        ],
        compiler_params=_cparams(("arbitrary", "arbitrary", "arbitrary")),
        name="bidir_rglru",
    )(proj, proj, proj, proj, conv_w, conv_b.reshape(1, w_total),
      (0.5 * wa).astype(BF16), (0.5 * ba).reshape(2, 1, w_total),
      (0.5 * wx).astype(BF16), (0.5 * bx).reshape(2, 1, w_total), lam.reshape(2, 1, w_total))


CD_COL_QKV, CD_COL_Z, CD_COL_XL, CD_COL_GL = 0, 2048, 3072, 4096


def _cd_weights(w_in):
    wt = w_in.T
    nqk = GDN_QK_HEADS * GDN_HEAD_DIM
    hv = GDN_V_HEADS
    p = _piece_table([("q", nqk), ("k", nqk), ("v", BRANCH_W), ("bf", hv), ("bb", hv), ("af", hv), ("ab", hv),
                      ("z", BRANCH_W), ("xl", LRU_WIDTH), ("gl", LRU_WIDTH)])
    main = _regroup_cast(wt, [p[n] for n in ("q", "k", "v", "z", "xl", "gl")])
    s0 = p["bf"][0]
    zpad = jnp.zeros((LANES - 4 * hv, wt.shape[1]), wt.dtype)
    small = jnp.concatenate([wt[s0:s0 + 4 * hv], zpad], axis=0).astype(BF16)
    return main, small


def _layer1(x, mod_l, norm_w, w_in, conv_w, conv_b, a_log_f, a_log_b, dt_bias_f, dt_bias_b, gdn_norm,
            lru_conv_w, lru_conv_b, wa_f, ba_f, wx_f, bx_f, lam_f, wa_b, ba_b, wx_b, bx_b, lam_b, w_out,
            final_w=None):
    w_main, w_small = _cd_weights(w_in)
    proj, small_t = _inproj(x, mod_l, norm_w, w_main, w_small)
    gdn = _gdn(proj, small_t, conv_w, conv_b, jnp.stack([dt_bias_f, dt_bias_b]),
               jnp.stack([a_log_f, a_log_b]), gdn_norm, col_qkv=CD_COL_QKV, col_z=CD_COL_Z)
    lru = _lru(proj, lru_conv_w, lru_conv_b, jnp.stack([wa_f, wa_b]), jnp.stack([ba_f, ba_b]),
               jnp.stack([wx_f, wx_b]), jnp.stack([bx_f, bx_b]), jnp.stack([lam_f, lam_b]),
               col_xl=CD_COL_XL, col_gl=CD_COL_GL)
    out = _outproj(gdn, lru, w_out, x, mod_l, final_w)
    return out, (proj, small_t, gdn, lru)


def kernel(x, c, w_mod, b_mod, norm_w, ab_w_in, ab_q_norm, ab_k_norm, ab_conv_w, ab_conv_b, ab_dt_bias_f, ab_dt_bias_b, ab_a_log_f, ab_a_log_b, ab_d_skip, ab_ssd_norm, ab_w_out, cd_w_in, cd_conv_w, cd_conv_b, cd_a_log_f, cd_a_log_b, cd_dt_bias_f, cd_dt_bias_b, cd_gdn_norm, cd_lru_conv_w, cd_lru_conv_b, cd_lru_wa_f, cd_lru_ba_f, cd_lru_wx_f, cd_lru_bx_f, cd_lru_lam_f, cd_lru_wa_b, cd_lru_ba_b, cd_lru_wx_b, cd_lru_bx_b, cd_lru_lam_b, cd_w_out, final_norm_w):
    mods = _modulation(c, w_mod, b_mod)
    x1, _ = _layer0(x, mods[0], norm_w[0], ab_w_in[0], ab_q_norm[0], ab_k_norm[0], ab_conv_w[0], ab_conv_b[0],
                    ab_dt_bias_f[0], ab_dt_bias_b[0], ab_a_log_f[0], ab_a_log_b[0], ab_d_skip[0],
                    ab_ssd_norm[0], ab_w_out[0])
    out, _ = _layer1(x1, mods[1], norm_w[1], cd_w_in[0], cd_conv_w[0], cd_conv_b[0], cd_a_log_f[0], cd_a_log_b[0],
                     cd_dt_bias_f[0], cd_dt_bias_b[0], cd_gdn_norm[0], cd_lru_conv_w[0], cd_lru_conv_b[0],
                     cd_lru_wa_f[0], cd_lru_ba_f[0], cd_lru_wx_f[0], cd_lru_bx_f[0], cd_lru_lam_f[0],
                     cd_lru_wa_b[0], cd_lru_ba_b[0], cd_lru_wx_b[0], cd_lru_bx_b[0], cd_lru_lam_b[0],
                     cd_w_out[0], final_norm_w)
    return out
```

```python
import functools

import jax
import jax.numpy as jnp
import numpy as np
from jax import lax
from jax.experimental import pallas as pl
from jax.experimental.pallas import tpu as pltpu

F32 = jnp.float32
BF16 = jnp.bfloat16

D_MODEL = 2048
GRID_W = 64
EPS = 1e-6
BRANCH_W = D_MODEL // 2
ATT_HEAD_DIM = 128
ATT_HEADS = BRANCH_W // ATT_HEAD_DIM
ATT_KV_HEADS = ATT_HEADS // 4
ATT_GROUP = ATT_HEADS // ATT_KV_HEADS
ROPE_THETA = 10000.0
SSD_HEAD_DIM = 64
SSD_HEADS = BRANCH_W // SSD_HEAD_DIM
SSD_GROUPS = 2
SSD_STATE = 128
SSD_CONV = 4
SSD_CHUNK = 128
GDN_HEAD_DIM = 128
GDN_V_HEADS = BRANCH_W // GDN_HEAD_DIM
GDN_QK_HEADS = GDN_V_HEADS // 2
GDN_CONV = 4
GDN_CHUNK = 128
GDN_WAVE = 16
LRU_WIDTH = BRANCH_W
LRU_BLOCKS = 8
LRU_CONV = 4
LRU_C = 8.0

LANES = 128
SUBLANES = 8
VMEM_LIMIT_BYTES = 56 * 1024 * 1024

HALO = SUBLANES
NEG_BIG = -1e30
TINY = 1e-37
CAST_ROWS = 256


def _cparams(sem):
    return pltpu.CompilerParams(dimension_semantics=sem, vmem_limit_bytes=VMEM_LIMIT_BYTES)


def _sigmoid(x):
    return 0.5 * jnp.tanh(0.5 * x) + 0.5


def _silu(x):
    h = 0.5 * x
    return h * (jnp.tanh(h) + 1.0)


def _softplus(x):
    return jnp.maximum(x, 0.0) + jnp.log(1.0 + jnp.exp(-jnp.abs(x)))


def _split_bf16(a):
    hi = a.astype(BF16)
    lo = (a - hi.astype(F32)).astype(BF16)
    return hi, lo


def _dot(a, b):
    return jnp.dot(a, b, preferred_element_type=F32)


def _dot_nt(a, b):
    return lax.dot_general(a, b, (((1,), (1,)), ((), ())), preferred_element_type=F32)


def _dot_exact_rhs(a, b_bf16):
    hi, lo = _split_bf16(a)
    return _dot(hi, b_bf16) + _dot(lo, b_bf16)


def _dot_exact_lhs(a_bf16, b):
    hi, lo = _split_bf16(b)
    return _dot(a_bf16, hi) + _dot(a_bf16, lo)


def _mod_kernel(c_ref, w_ref, b_ref, o_ref):
    cond = _silu(c_ref[...])
    c_hi, c_lo = _split_bf16(cond)
    w = w_ref[0]
    w_hi, w_lo = _split_bf16(w)
    o_ref[0] = _dot(c_hi, w_hi) + _dot(c_lo, w_hi) + _dot(c_hi, w_lo) + b_ref[0]


def _modulation(c, w_mod, b_mod):
    depth, d, n = w_mod.shape
    bsz = c.shape[0]
    rows = -(-bsz // SUBLANES) * SUBLANES
    c_pad = jnp.zeros((rows, d), F32).at[:bsz].set(c)
    tn = 1536
    out = pl.pallas_call(
        _mod_kernel,
        out_shape=jax.ShapeDtypeStruct((depth, rows, n), F32),
        grid=(depth, n // tn),
        in_specs=[
            pl.BlockSpec((rows, d), lambda l, j: (0, 0)),
            pl.BlockSpec((1, d, tn), lambda l, j: (l, 0, j)),
            pl.BlockSpec((1, 1, tn), lambda l, j: (l, 0, j)),
        ],
        out_specs=pl.BlockSpec((1, rows, tn), lambda l, j: (l, 0, j)),
        compiler_params=_cparams(("arbitrary", "arbitrary")),
        name="adaln_mod",
    )(c_pad, w_mod, b_mod.reshape(depth, 1, n))
    return out[:, :bsz].reshape(depth, bsz, 3, d)


def _inproj_kernel(x_ref, mod_ref, nw_ref, w_ref, wst_ref, o_ref, ost_ref, h_even, h_odd, *, n_groups):
    g = pl.program_id(0)
    j = pl.program_id(1)
    rows = x_ref.shape[1]

    def prep(h_dst):
        x = x_ref[0]
        ms = jnp.mean(x * x, axis=-1, keepdims=True)
        y = x * lax.rsqrt(ms + EPS) * nw_ref[...]
        h = (y * (1.0 + mod_ref[0, 1:2, :]) + mod_ref[0, 0:1, :]).astype(BF16)
        h_dst[pl.ds(pl.multiple_of(j * rows, rows), rows), :] = h
        ost_ref[0] = _dot_nt(wst_ref[...], h)

    @pl.when(g == 0)
    def _():
        prep(h_even)

    @pl.when((g > 0) & (g % 2 == 0))
    def _():
        o_ref[0] = _dot_nt(h_odd[...], w_ref[...])
        prep(h_even)

    @pl.when(g % 2 == 1)
    def _():
        o_ref[0] = _dot_nt(h_even[...], w_ref[...])
        prep(h_odd)


def _inproj(x, mod_l, norm_w, w_main, w_small, *, tm=1024, nj=4):
    bsz, s, d = x.shape
    n = w_main.shape[0]
    ns = w_small.shape[0]
    tm = min(tm, s)
    ni = s // tm
    n_groups = bsz * ni
    tn = n // nj
    rows = tm // nj
    assert n % nj == 0 and tn % LANES == 0 and rows % LANES == 0

    def prep(g, j):
        gc = jnp.minimum(g, n_groups - 1)
        return gc // ni, (gc % ni) * nj + jnp.where(g < n_groups, j, nj - 1)

    def mm(g, j):
        gm = jnp.maximum(g - 1, 0)
        return gm // ni, gm % ni, jnp.where(g > 0, j, 0)

    return pl.pallas_call(
        functools.partial(_inproj_kernel, n_groups=n_groups),
        out_shape=[jax.ShapeDtypeStruct((bsz, s, n), F32), jax.ShapeDtypeStruct((bsz, ns, s), F32)],
        grid=(n_groups + 1, nj),
        in_specs=[
            pl.BlockSpec((1, rows, d), lambda g, j: (*prep(g, j), 0)),
            pl.BlockSpec((1, 3, d), lambda g, j: (prep(g, j)[0], 0, 0)),
            pl.BlockSpec((1, d), lambda g, j: (0, 0)),
            pl.BlockSpec((tn, d), lambda g, j: (mm(g, j)[2], 0)),
            pl.BlockSpec((ns, d), lambda g, j: (0, 0)),
        ],
        out_specs=[
            pl.BlockSpec((1, tm, tn), lambda g, j: mm(g, j)),
            pl.BlockSpec((1, ns, rows), lambda g, j: (prep(g, j)[0], 0, prep(g, j)[1])),
        ],
        scratch_shapes=[pltpu.VMEM((tm, d), BF16), pltpu.VMEM((tm, d), BF16)],
        compiler_params=_cparams(("arbitrary", "arbitrary")),
        name="norm_mod_inproj",
    )(x, mod_l, norm_w.reshape(1, d), w_main, w_small)


def _outproj_kernel(ya_ref, yb_ref, w_ref, x_ref, mod_ref, *rest, final_norm, half):
    if final_norm:
        fnw_ref, o_ref, w_scr = rest
    else:
        o_ref, w_scr = rest

    @pl.when((pl.program_id(0) == 0) & (pl.program_id(1) == 0))
    def _():
        rows = CAST_ROWS

        def body(r, carry):
            r0 = pl.multiple_of(r * rows, rows)
            w_scr[pl.ds(r0, rows), :] = w_ref[pl.ds(r0, rows), :].astype(BF16)
            return carry

        lax.fori_loop(0, w_ref.shape[0] // rows, body, 0)

    acc = _dot(ya_ref[0], w_scr[0:half, :]) + _dot(yb_ref[0], w_scr[half:, :])
    gate = mod_ref[0, 2:3, :]
    xn = x_ref[0] + gate * acc
    if final_norm:
        ms = jnp.mean(xn * xn, axis=-1, keepdims=True)
        xn = xn * lax.rsqrt(ms + EPS) * fnw_ref[...]
    o_ref[0] = xn


def _outproj(ya, yb, w_out, x, mod_l, final_w=None, *, tm=512):
    bsz, s, d = x.shape
    half = ya.shape[-1]
    tm = min(tm, s)
    final_norm = final_w is not None
    in_specs = [
        pl.BlockSpec((1, tm, half), lambda b, i: (b, i, 0)),
        pl.BlockSpec((1, tm, half), lambda b, i: (b, i, 0)),
        pl.BlockSpec((2 * half, d), lambda b, i: (0, 0), pipeline_mode=pl.Buffered(1)),
        pl.BlockSpec((1, tm, d), lambda b, i: (b, i, 0)),
        pl.BlockSpec((1, 3, d), lambda b, i: (b, 0, 0)),
    ]
    args = [ya, yb, w_out, x, mod_l]
    if final_norm:
        in_specs.append(pl.BlockSpec((1, d), lambda b, i: (0, 0)))
        args.append(final_w.reshape(1, d))
    return pl.pallas_call(
        functools.partial(_outproj_kernel, final_norm=final_norm, half=half),
        out_shape=jax.ShapeDtypeStruct((bsz, s, d), F32),
        grid=(bsz, s // tm),
        in_specs=in_specs,
        out_specs=pl.BlockSpec((1, tm, d), lambda b, i: (b, i, 0)),
        scratch_shapes=[pltpu.VMEM((2 * half, d), BF16)],
        compiler_params=_cparams(("arbitrary", "arbitrary")),
        name="outproj_residual",
    )(*args)


def _rms_rope(x, nw, cos, sin_signed):
    ms = jnp.mean(x * x, axis=-1, keepdims=True)
    y = x * lax.rsqrt(ms + EPS) * nw
    lane = lax.broadcasted_iota(jnp.int32, y.shape, 1)
    partner = jnp.where(lane % 2 == 0, pltpu.roll(y, ATT_HEAD_DIM - 1, 1), pltpu.roll(y, 1, 1))
    return y * cos + partner * sin_signed


def _attention_kv_prep(k_ref, v_ref, ck_ref, sk_ref, kn_ref, k_scr, v_scr):
    dh = ATT_HEAD_DIM
    k_scr[...] = _rms_rope(k_ref[0], kn_ref[...], ck_ref[...], sk_ref[...]).astype(BF16)
    v_scr[:, 0:dh] = v_ref[0].astype(BF16)
    v_scr[:, dh:2 * dh] = jnp.ones((v_scr.shape[0], dh), BF16)


ATTN_AHEAD = 1


def _attention_pump(q_ref, ga_ref, cq_ref, sq_ref, qn_ref, k_scr, v_scr, o_ref, sub):
    dh = ATT_HEAD_DIM
    tq, width = q_ref.shape[1], q_ref.shape[2]
    scale = dh ** -0.5
    probs = [(g, r) for g in range(width // dh) for r in range(0, tq, sub)]
    ahead = ATTN_AHEAD
    state = {"i": 0, "s": {}}

    def scores(g, r):
        qg = q_ref[0, r:r + sub, g * dh:(g + 1) * dh]
        qg = (_rms_rope(qg, qn_ref[...], cq_ref[r:r + sub, :], sq_ref[r:r + sub, :]) * scale).astype(BF16)
        return _dot_nt(qg, k_scr[...])

    def finish(g, r, s):
        s = s.astype(BF16)
        p = jnp.exp(s - jnp.max(s, axis=-1, keepdims=True))
        o_ext = _dot(p, v_scr[...])
        og = o_ext[:, 0:dh] / o_ext[:, dh:2 * dh] * _silu(ga_ref[0, r:r + sub, g * dh:(g + 1) * dh])
        o_ref[0, r:r + sub, g * dh:(g + 1) * dh] = og.astype(BF16)

    def pump():
        i = state["i"]
        if i >= len(probs) + ahead:
            return
        if i < len(probs):
            state["s"][i] = scores(*probs[i])
        if i >= ahead:
            finish(*probs[i - ahead], state["s"].pop(i - ahead))
        state["i"] = i + 1

    return pump, len(probs) + ahead


def _attn_kernel(q_ref, k_ref, v_ref, ga_ref, cq_ref, sq_ref, ck_ref, sk_ref, qn_ref, kn_ref,
                 o_ref, k_scr, v_scr, *, sub):
    @pl.when(pl.program_id(2) == 0)
    def _():
        _attention_kv_prep(k_ref, v_ref, ck_ref, sk_ref, kn_ref, k_scr, v_scr)

    pump, n = _attention_pump(q_ref, ga_ref, cq_ref, sq_ref, qn_ref, k_scr, v_scr, o_ref, sub)
    for _ in range(n):
        pump()


def _attention(proj, cos_t, sin_t, q_norm, k_norm, *, col_q, col_k, col_v, col_ga, tq=512, sub=128):
    bsz, s, _ = proj.shape
    tq = min(tq, s)
    gw = ATT_GROUP * ATT_HEAD_DIM
    dh = ATT_HEAD_DIM
    qb, kb, vb, gb = col_q // gw, col_k // dh, col_v // dh, col_ga // gw
    return pl.pallas_call(
        functools.partial(_attn_kernel, sub=min(sub, tq)),
        out_shape=jax.ShapeDtypeStruct((bsz, s, BRANCH_W), BF16),
        grid=(bsz, ATT_KV_HEADS, s // tq),
        in_specs=[
            pl.BlockSpec((1, tq, gw), lambda b, h, i: (b, i, qb + h)),
            pl.BlockSpec((1, s, dh), lambda b, h, i: (b, 0, kb + h)),
            pl.BlockSpec((1, s, dh), lambda b, h, i: (b, 0, vb + h)),
            pl.BlockSpec((1, tq, gw), lambda b, h, i: (b, i, gb + h)),
            pl.BlockSpec((tq, dh), lambda b, h, i: (i, 0)),
            pl.BlockSpec((tq, dh), lambda b, h, i: (i, 0)),
            pl.BlockSpec((s, dh), lambda b, h, i: (0, 0)),
            pl.BlockSpec((s, dh), lambda b, h, i: (0, 0)),
            pl.BlockSpec((1, dh), lambda b, h, i: (0, 0)),
            pl.BlockSpec((1, dh), lambda b, h, i: (0, 0)),
        ],
        out_specs=pl.BlockSpec((1, tq, gw), lambda b, h, i: (b, i, h)),
        scratch_shapes=[pltpu.VMEM((s, dh), BF16), pltpu.VMEM((s, 2 * dh), BF16)],
        compiler_params=_cparams(("arbitrary", "arbitrary", "arbitrary")),
        name="gqa_attention",
    )(proj, proj, proj, proj, cos_t, sin_t, cos_t, sin_t, q_norm, k_norm)


def _rope_tables(s):
    t = np.arange(s)
    row = (t // GRID_W).astype(np.float64)
    col = (t % GRID_W).astype(np.float64)
    n_pairs = ATT_HEAD_DIM // 4
    freqs = ROPE_THETA ** (-np.arange(n_pairs, dtype=np.float64) / n_pairs)
    ang = np.concatenate([row[:, None] * freqs, col[:, None] * freqs], axis=-1)
    cos, sin = np.cos(ang), np.sin(ang)
    cos_t = np.repeat(cos, 2, axis=-1)
    sin_t = np.stack([-sin, sin], axis=-1).reshape(s, ATT_HEAD_DIM)
    return jnp.asarray(cos_t, F32), jnp.asarray(sin_t, F32)


def _chunk_of_step(step, nc):
    return jnp.where(step < nc, step, 2 * nc - 1 - step)


def _fill_conv_window(ext_scr, cur, prev, nxt, c, nc, rows):
    ext_scr[0:HALO, :] = jnp.where(c > 0, prev, 0.0)
    ext_scr[HALO:HALO + rows, :] = cur
    ext_scr[HALO + rows:HALO + rows + HALO, :] = jnp.where(c < nc - 1, nxt, 0.0)


CONV_SUB = 128


def _centred_conv4(ext_scr, cw_ref, cb_ref, rows):
    sub = min(rows, CONV_SUB)
    n_in = sub + 2 * HALO
    taps = (0, 1, 3)
    r = lax.broadcasted_iota(jnp.int32, (len(taps) * sub, n_in), 0)
    c = lax.broadcasted_iota(jnp.int32, (len(taps) * sub, n_in), 1)
    src = jnp.zeros_like(r)
    for i, k in enumerate(taps):
        src = jnp.where((r >= i * sub) & (r < (i + 1) * sub), r - i * sub + HALO + (k - 2), src)
    shift_mat = jnp.where(c == src, 1.0, 0.0).astype(BF16)
    outs = []
    for s0 in range(0, rows, sub):
        win = ext_scr[s0:s0 + n_in, :]
        shifted = _dot(shift_mat, win.astype(BF16))
        acc = cb_ref[...] + win[HALO:HALO + sub, :] * cw_ref[2:3, :]
        for i, k in enumerate(taps):
            acc = acc + shifted[i * sub:(i + 1) * sub, :] * cw_ref[k:k + 1, :]
        outs.append(acc)
    return outs[0] if len(outs) == 1 else jnp.concatenate(outs, axis=0)


def _scan_masks(fwd, n):
    row = lax.broadcasted_iota(jnp.int32, (n, n), 0)
    col = lax.broadcasted_iota(jnp.int32, (n, n), 1)
    sgn = jnp.where(fwd, 1, -1)
    d = (row - col) * sgn
    return d >= 0, d <= 0, d > 0


def _pad_dir_rows(v, n):
    return jnp.zeros((2, 1, LANES), F32).at[:, 0, :n].set(v)


def _pad_dir_cols(v, n):
    return jnp.zeros((2, LANES, 1), F32).at[:, :n, 0].set(v)


def _head_expander(heads, width):
    r = lax.broadcasted_iota(jnp.int32, (heads, heads * width), 0)
    c = lax.broadcasted_iota(jnp.int32, (heads, heads * width), 1)
    return jnp.where((c >= r * width) & (c < (r + 1) * width), 1.0, 0.0).astype(BF16)


def _ssd_kernel(xf_ref, pf_ref, nf_ref, xb_ref, pb_ref, nb_ref, z_ref, dtf_ref, dtb_ref, cw_ref, cb_ref,
                bias_r_ref, bias_c_ref, alog_r_ref, alog_c_ref, dskip_ref, nw_ref,
                o_ref, ext_scr, state_scr, acc_scr, *, nc, te):
    L, H, P, N = SSD_CHUNK, SSD_HEADS, SSD_HEAD_DIM, SSD_STATE
    HG = H // SSD_GROUPS
    GW = HG * P
    nbt = acc_scr.shape[0]
    t = pl.program_id(1)

    @pl.when(t == 0)
    def _():
        state_scr[...] = jnp.zeros_like(state_scr)

    @pl.when(t < nc)
    def _():
        chunks = (t, nc - 1 - t)
        blocks = ((xf_ref, pf_ref, nf_ref, dtf_ref), (xb_ref, pb_ref, nb_ref, dtb_ref))
        expand = _head_expander(H, P)
        lane = lax.broadcasted_iota(jnp.int32, (L, 2 * P), 1)
        lane_lo = jnp.where(lane < P, 1.0, 0.0).astype(BF16)
        lane_hi = jnp.where(lane < P, 0.0, 1.0).astype(BF16)
        per_dir = []
        for bi, d in [(bi, d) for bi in range(nbt) for d in range(2)]:
            x_ref, p_ref, n_ref, dtt_ref = blocks[d]
            ext = ext_scr.at[bi * 2 + d]
            _fill_conv_window(ext, x_ref[bi], p_ref[bi], n_ref[bi], chunks[d], nc, L)
            xbc = _silu(_centred_conv4(ext, cw_ref, cb_ref, L))
            xs = xbc[:, :BRANCH_W]
            bs = xbc[:, BRANCH_W:BRANCH_W + SSD_GROUPS * N]
            cs = xbc[:, BRANCH_W + SSD_GROUPS * N:]
            raw_t = dtt_ref[bi]
            raw = raw_t.T
            dt = _softplus(raw[:, d * H:(d + 1) * H] + bias_r_ref[d][:, 0:H])
            a = dt * (-jnp.exp(alog_r_ref[d][:, 0:H]))
            dt_t = _softplus(raw_t[d * H:(d + 1) * H, :] + bias_c_ref[d][0:H, :])
            a_t = dt_t * (-jnp.exp(alog_c_ref[d][0:H, :]))
            mask, mask_t, _ = _scan_masks(d == 0, L)
            cum = _dot_exact_lhs(jnp.where(mask, 1.0, 0.0).astype(BF16), a)
            cum_t = _dot_exact_rhs(a_t, jnp.where(mask_t, 1.0, 0.0).astype(BF16))
            total = jnp.sum(a, axis=0, keepdims=True)
            dt_e = _dot(dt.astype(BF16), expand)
            p_e = _dot(jnp.exp(cum).astype(BF16), expand)
            q_e = _dot(jnp.exp(total - cum).astype(BF16), expand)
            tot_e = _dot_exact_rhs(jnp.broadcast_to(jnp.exp(total), (SUBLANES, H)), expand)[0:1, :]
            xd = xs * dt_e
            per_dir.append(dict(xs=xs, bs=bs, cs=cs, mask=mask, cum=cum, cum_t=cum_t, p_e=p_e, tot_e=tot_e,
                                xdq=(xd * q_e).astype(BF16), xd_b=xd.astype(BF16)))

        y_dirs = {}
        for bi in range(nbt):
            combos = [(bi * 2 + d, g) for d in range(2) for g in range(SSD_GROUPS)]
            cg = {k: per_dir[k[0]]["cs"][:, k[1] * N:(k[1] + 1) * N].astype(BF16) for k in combos}
            bg = {k: per_dir[k[0]]["bs"][:, k[1] * N:(k[1] + 1) * N] for k in combos}
            gmat = {k: _dot_nt(cg[k], bg[k].astype(BF16)) for k in combos}
            h_prev = {k: state_scr[k[0], :, k[1] * GW:(k[1] + 1) * GW] for k in combos}
            y_off = {k: _dot(cg[k], h_prev[k].astype(BF16)) * per_dir[k[0]]["p_e"][:, k[1] * GW:(k[1] + 1) * GW]
                     for k in combos}
            pairs = {k: [] for k in combos}
            for hp in range(HG // 2):
                for k in combos:
                    j, g = k
                    pd = per_dir[j]
                    h0 = g * HG + 2 * hp
                    xpair = pd["xd_b"][:, h0 * P:(h0 + 2) * P]
                    ms = []
                    for h in (h0, h0 + 1):
                        dec = jnp.exp(jnp.where(pd["mask"], pd["cum"][:, h:h + 1] - pd["cum_t"][h:h + 1, :],
                                                NEG_BIG))
                        ms.append((gmat[k] * dec).astype(BF16))
                    rhs = jnp.concatenate([xpair * lane_lo, xpair * lane_hi], axis=0)
                    pairs[k].append(_dot(jnp.concatenate(ms, axis=1), rhs))
            for k in combos:
                j, g = k
                state_scr[j, :, g * GW:(g + 1) * GW] = (
                    h_prev[k] * per_dir[j]["tot_e"][:, g * GW:(g + 1) * GW]
                    + _dot(bg[k].T.astype(BF16), per_dir[j]["xdq"][:, g * GW:(g + 1) * GW]))
            for d in range(2):
                j = bi * 2 + d
                y_dirs[j] = jnp.concatenate([jnp.concatenate(pairs[(j, g)], axis=1) + y_off[(j, g)]
                                             for g in range(SSD_GROUPS)], axis=1)
            y_dirs[bi * 2] = y_dirs[bi * 2] + dskip_ref[...] * per_dir[bi * 2]["xs"]
        rows = [pl.multiple_of(c * L, L) for c in chunks]

        @pl.when(t < nc // 2)
        def _():
            for bi in range(nbt):
                for d in range(2):
                    acc_scr[bi, pl.ds(rows[d], L), :] = y_dirs[bi * 2 + d]

        @pl.when(t >= nc // 2)
        def _():
            for bi in range(nbt):
                for d in range(2):
                    acc_scr[bi, pl.ds(rows[d], L), :] = acc_scr[bi, pl.ds(rows[d], L), :] + y_dirs[bi * 2 + d]

    @pl.when(t >= nc)
    def _():
        r0 = pl.multiple_of((t - nc) * te, te)
        for bi in range(nbt):
            y = acc_scr[bi, pl.ds(r0, te), :] * _silu(z_ref[bi])
            ms = jnp.mean(y * y, axis=-1, keepdims=True)
            o_ref[bi] = (y * lax.rsqrt(ms + EPS) * nw_ref[...]).astype(BF16)


def _ssd(proj, small_t, conv_w, conv_b, dt_bias, a_log, d_skip, norm_w, *, col_xbc, col_z, te=512, nbt=2):
    bsz, s, _ = proj.shape
    L = SSD_CHUNK
    nc = s // L
    te = min(te, s)
    ne = s // te
    cw = BRANCH_W + 2 * SSD_GROUPS * SSD_STATE
    hb = L // HALO
    xb, zb = col_xbc // cw, col_z // BRANCH_W
    assert col_xbc % cw == 0 and col_z % BRANCH_W == 0 and nc % 2 == 0 and bsz % nbt == 0

    def cf(t):
        return jnp.minimum(t, nc - 1)

    def cbk(t):
        return jnp.maximum(nc - 1 - t, 0)

    def ep(t):
        return jnp.maximum(t - nc, 0)

    def xbc_specs(chunk):
        return [
            pl.BlockSpec((nbt, L, cw), lambda b, t: (b, chunk(t), xb)),
            pl.BlockSpec((nbt, HALO, cw), lambda b, t: (b, jnp.maximum(chunk(t) * hb - 1, 0), xb)),
            pl.BlockSpec((nbt, HALO, cw), lambda b, t: (b, jnp.minimum((chunk(t) + 1) * hb, s // HALO - 1), xb)),
        ]

    def small_spec(chunk):
        return pl.BlockSpec((nbt, LANES, L), lambda b, t: (b, 0, chunk(t)))

    full = lambda shape: pl.BlockSpec(shape, lambda b, t: (0,) * len(shape))
    in_specs = (xbc_specs(cf) + xbc_specs(cbk)
                + [pl.BlockSpec((nbt, te, BRANCH_W), lambda b, t: (b, ep(t), zb))]
                + [small_spec(cf), small_spec(cbk)]
                + [full((SSD_CONV, cw)), full((1, cw)), full((2, 1, LANES)), full((2, LANES, 1)),
                   full((2, 1, LANES)), full((2, LANES, 1)), full((1, BRANCH_W)), full((1, BRANCH_W))])
    args = [proj, proj, proj, proj, proj, proj, proj, small_t, small_t, conv_w, conv_b.reshape(1, cw),
            _pad_dir_rows(dt_bias, SSD_HEADS), _pad_dir_cols(dt_bias, SSD_HEADS),
            _pad_dir_rows(a_log, SSD_HEADS), _pad_dir_cols(a_log, SSD_HEADS),
            jnp.repeat(d_skip, SSD_HEAD_DIM).reshape(1, BRANCH_W), norm_w.reshape(1, BRANCH_W)]
    out_shape = [jax.ShapeDtypeStruct((bsz, s, BRANCH_W), BF16)]
    out_specs = [pl.BlockSpec((nbt, te, BRANCH_W), lambda b, t: (b, ep(t), 0))]
    scratch = [
        pltpu.VMEM((nbt * 2, L + 2 * HALO, cw), F32),
        pltpu.VMEM((nbt * 2, SSD_STATE, BRANCH_W), F32),
        pltpu.VMEM((nbt, s, BRANCH_W), F32),
    ]
    return pl.pallas_call(
        functools.partial(_ssd_kernel, nc=nc, te=te),
        out_shape=out_shape,
        grid=(bsz // nbt, nc + ne),
        in_specs=in_specs,
        out_specs=out_specs,
        scratch_shapes=scratch,
        compiler_params=_cparams(("arbitrary", "arbitrary")),
        name="bidir_ssd",
    )(*args)[0]


AB_COL_XBC, AB_COL_K, AB_COL_V, AB_COL_Z, AB_COL_Q, AB_COL_GA, AB_N = 0, 1536, 1792, 2048, 3072, 4096, 5120


REGROUP_ROWS = 256


def _regroup_kernel(offs_ref, w_ref, o_ref):
    del offs_ref
    o_ref[...] = w_ref[...].astype(BF16)


def _regroup_cast(wt, pieces):
    rows = REGROUP_ROWS
    offsets = []
    for start, n in pieces:
        assert n % rows == 0 and start % SUBLANES == 0
        offsets += list(range(start, start + n, rows))
    d = wt.shape[1]
    return pl.pallas_call(
        _regroup_kernel,
        out_shape=jax.ShapeDtypeStruct((len(offsets) * rows, d), BF16),
        grid_spec=pltpu.PrefetchScalarGridSpec(
            num_scalar_prefetch=1,
            grid=(len(offsets),),
            in_specs=[pl.BlockSpec((pl.Element(rows), pl.Element(d)), lambda i, offs: (pl.multiple_of(offs[i], SUBLANES), 0))],
            out_specs=pl.BlockSpec((rows, d), lambda i, offs: (i, 0)),
        ),
        compiler_params=_cparams(("arbitrary",)),
        name="weight_regroup_cast",
    )(jnp.asarray(offsets, jnp.int32), wt)


def _piece_table(sizes):
    table, o = {}, 0
    for name, n in sizes:
        table[name] = (o, n)
        o += n
    return table


def _ab_weights(w_in):
    wt = w_in.T
    hq, hk = ATT_HEADS * ATT_HEAD_DIM, ATT_KV_HEADS * ATT_HEAD_DIM
    gn = SSD_GROUPS * SSD_STATE
    p = _piece_table([("q", hq), ("k", hk), ("v", hk), ("ga", BRANCH_W), ("xs", BRANCH_W), ("bs", gn),
                      ("cs", gn), ("dtf", SSD_HEADS), ("dtb", SSD_HEADS), ("z", BRANCH_W)])
    main = _regroup_cast(wt, [p[n] for n in ("xs", "bs", "cs", "k", "v", "z", "q", "ga")])
    dt0, dtn = p["dtf"][0], 2 * SSD_HEADS
    zpad = jnp.zeros((LANES - dtn, wt.shape[1]), wt.dtype)
    small = jnp.concatenate([wt[dt0:dt0 + dtn], zpad], axis=0).astype(BF16)
    return main, small


def _layer0(x, mod_l, norm_w, w_in, q_norm, k_norm, conv_w, conv_b, dt_bias_f, dt_bias_b,
            a_log_f, a_log_b, d_skip, ssd_norm, w_out, final_w=None):
    s = x.shape[1]
    w_main, w_small = _ab_weights(w_in)
    proj, small_t = _inproj(x, mod_l, norm_w, w_main, w_small)
    cos_t, sin_t = _rope_tables(s)
    att = _attention(proj, cos_t, sin_t, q_norm.reshape(1, ATT_HEAD_DIM), k_norm.reshape(1, ATT_HEAD_DIM),
                     col_q=AB_COL_Q, col_k=AB_COL_K, col_v=AB_COL_V, col_ga=AB_COL_GA)
    ssd = _ssd(proj, small_t, conv_w, conv_b, jnp.stack([dt_bias_f, dt_bias_b]),
               jnp.stack([a_log_f, a_log_b]), d_skip, ssd_norm, col_xbc=AB_COL_XBC, col_z=AB_COL_Z)
    return _outproj(att, ssd, w_out, x, mod_l, final_w), (proj, small_t, att, ssd)


def _unit_tri_inverse(nmats, n):
    row = lax.broadcasted_iota(jnp.int32, (n, n), 0)
    col = lax.broadcasted_iota(jnp.int32, (n, n), 1)

    def same_block(size):
        return (row // size) == (col // size)

    def mm(a, b):
        return _dot(a, b).astype(BF16)

    def as_mask(cond):
        return jnp.where(cond, 1.0, 0.0).astype(BF16)

    nmats = [m.astype(BF16) for m in nmats]
    eye = as_mask(row == col)
    base = SUBLANES
    blk = same_block(base)
    blk_m = as_mask(blk)
    nd = [m * blk_m for m in nmats]
    p1 = [mm(x, x) for x in nd]
    p2 = [mm(x, x) for x in p1]
    t = [eye - x for x in nd]
    t = [x + mm(x, p) for x, p in zip(t, p1)]
    t = [x + mm(x, p) for x, p in zip(t, p2)]
    size = base
    while size < n:
        nxt = same_block(2 * size)
        off_m = as_mask(nxt & jnp.logical_not(blk))
        et = [mm(m * off_m, x) for m, x in zip(nmats, t)]
        t = [x - mm(x, y) for x, y in zip(t, et)]
        blk = nxt
        size *= 2
    return t


def _l2norm(x):
    return x * lax.rsqrt(jnp.sum(x * x, axis=-1, keepdims=True) + EPS)


def _gdn_kernel(qf_ref, pf_ref, nf_ref, qb_ref, pb_ref, nb_ref, z_ref, smtf_ref, smtb_ref,
                cw_ref, cb_ref, bias_r_ref, bias_c_ref, alog_r_ref, alog_c_ref, nw_ref,
                o_ref, ext_scr, state_scr, acc_scr, *, nc, te):
    L, HV, HQ, DK = GDN_CHUNK, GDN_V_HEADS, GDN_QK_HEADS, GDN_HEAD_DIM
    rep = HV // HQ
    nbt = acc_scr.shape[0]
    t = pl.program_id(1)

    @pl.when(t == 0)
    def _():
        state_scr[...] = jnp.zeros_like(state_scr)

    @pl.when(t < nc)
    def _():
        chunks = (t, nc - 1 - t)
        blocks = ((qf_ref, pf_ref, nf_ref, smtf_ref), (qb_ref, pb_ref, nb_ref, smtb_ref))
        kk, qk, qh, kh, vh, colv, dec, bcol, ecol, tot, nmat = ([] for _ in range(11))
        for bi, d in [(bi, d) for bi in range(nbt) for d in range(2)]:
            c = chunks[d]
            q_ref, p_ref, n_ref, smt_ref = blocks[d]
            ext = ext_scr.at[bi * 2 + d]
            _fill_conv_window(ext, q_ref[bi], p_ref[bi], n_ref[bi], c, nc, L)
            act = _silu(_centred_conv4(ext, cw_ref, cb_ref, L))
            q_n = [_l2norm(act[:, h * DK:(h + 1) * DK]) * (DK ** -0.5) for h in range(HQ)]
            k_n = [_l2norm(act[:, (HQ + h) * DK:(HQ + h + 1) * DK]) for h in range(HQ)]
            k_b = [x.astype(BF16) for x in k_n]
            kk_d = [_dot_nt(k_b[h], k_b[h]) for h in range(HQ)]
            qk_d = [_dot_nt(q_n[h].astype(BF16), k_b[h]) for h in range(HQ)]

            raw_t = smt_ref[bi]
            raw = raw_t.T
            a_raw = raw[:, (2 + d) * HV:(3 + d) * HV]
            a_raw_t = raw_t[(2 + d) * HV:(3 + d) * HV, :]
            beta = _sigmoid(raw[:, d * HV:(d + 1) * HV])
            g = -jnp.exp(alog_r_ref[d][:, 0:HV]) * _softplus(a_raw + bias_r_ref[d][:, 0:HV])
            g_t = -jnp.exp(alog_c_ref[d][0:HV, :]) * _softplus(a_raw_t + bias_c_ref[d][0:HV, :])
            mask, mask_t, strict = _scan_masks(d == 0, L)
            cum = _dot_exact_lhs(jnp.where(mask, 1.0, 0.0).astype(BF16), g)
            cum_t = _dot_exact_rhs(g_t, jnp.where(mask_t, 1.0, 0.0).astype(BF16))
            total = jnp.sum(g, axis=0, keepdims=True)
            for h in range(HV):
                cv = cum[:, h:h + 1]
                dc = jnp.exp(jnp.where(mask, cv - cum_t[h:h + 1, :], NEG_BIG))
                bc = beta[:, h:h + 1]
                kk.append(kk_d[h // rep]); qk.append(qk_d[h // rep])
                qh.append(q_n[h // rep]); kh.append(k_n[h // rep])
                vh.append(act[:, (2 * HQ + h) * DK:(2 * HQ + h + 1) * DK])
                colv.append(cv); dec.append(dc); bcol.append(bc); ecol.append(jnp.exp(cv))
                tot.append(total[:, h:h + 1])
                nmat.append(jnp.where(strict, kk_d[h // rep] * bc * dc, 0.0))

        outs = []
        for w0 in range(0, nbt * 2 * HV, GDN_WAVE):
            idx = range(w0, w0 + GDN_WAVE)
            t_inv = dict(zip(idx, _unit_tri_inverse([nmat[i] for i in idx], L)))
            u = {i: _dot(t_inv[i], (vh[i] * bcol[i]).astype(BF16)) for i in idx}
            w = {i: _dot(t_inv[i], (kh[i] * (bcol[i] * ecol[i])).astype(BF16)) for i in idx}
            s_prev = {i: state_scr[i // HV, :, (i % HV) * DK:(i % HV + 1) * DK] for i in idx}
            s_b = {i: s_prev[i].astype(BF16) for i in idx}
            v_new = {i: (u[i] - _dot(w[i].astype(BF16), s_b[i])).astype(BF16) for i in idx}
            outs += [_dot((qh[i] * ecol[i]).astype(BF16), s_b[i])
                     + _dot((qk[i] * dec[i]).astype(BF16), v_new[i]) for i in idx]
            for i in idx:
                k_dec = kh[i] * jnp.exp(tot[i] - colv[i])
                state_scr[i // HV, :, (i % HV) * DK:(i % HV + 1) * DK] = (
                    s_prev[i] * jnp.exp(tot[i]) + _dot(k_dec.T.astype(BF16), v_new[i]))
        o_dirs = [jnp.concatenate(outs[j * HV:(j + 1) * HV], axis=1) for j in range(nbt * 2)]
        rows = [pl.multiple_of(c * L, L) for c in chunks]

        @pl.when(t < nc // 2)
        def _():
            for bi in range(nbt):
                for d in range(2):
                    acc_scr[bi, pl.ds(rows[d], L), :] = o_dirs[bi * 2 + d]

        @pl.when(t >= nc // 2)
        def _():
            for bi in range(nbt):
                for d in range(2):
                    acc_scr[bi, pl.ds(rows[d], L), :] = acc_scr[bi, pl.ds(rows[d], L), :] + o_dirs[bi * 2 + d]

    @pl.when(t >= nc)
    def _():
        r0 = pl.multiple_of((t - nc) * te, te)
        for bi in range(nbt):
            zz = z_ref[bi]
            for hv in range(HV):
                oh = acc_scr[bi, pl.ds(r0, te), hv * DK:(hv + 1) * DK]
                ms = jnp.mean(oh * oh, axis=-1, keepdims=True)
                res = oh * lax.rsqrt(ms + EPS) * nw_ref[...] * _silu(zz[:, hv * DK:(hv + 1) * DK])
                o_ref[bi, :, hv * DK:(hv + 1) * DK] = res.astype(BF16)


def _gdn(proj, small_t, conv_w, conv_b, dt_bias, a_log, norm_w, *, col_qkv, col_z, te=512, nbt=2):
    bsz, s, _ = proj.shape
    L = GDN_CHUNK
    nc = s // L
    te = min(te, s)
    ne = s // te
    cw = 2 * GDN_QK_HEADS * GDN_HEAD_DIM + BRANCH_W
    hb = L // HALO
    qb, zb = col_qkv // cw, col_z // BRANCH_W
    assert col_qkv % cw == 0 and col_z % BRANCH_W == 0 and nc % 2 == 0 and bsz % nbt == 0

    def cf(t):
        return jnp.minimum(t, nc - 1)

    def cbk(t):
        return jnp.maximum(nc - 1 - t, 0)

    def ep(t):
        return jnp.maximum(t - nc, 0)

    def qkv_specs(chunk):
        return [
            pl.BlockSpec((nbt, L, cw), lambda b, t: (b, chunk(t), qb)),
            pl.BlockSpec((nbt, HALO, cw), lambda b, t: (b, jnp.maximum(chunk(t) * hb - 1, 0), qb)),
            pl.BlockSpec((nbt, HALO, cw), lambda b, t: (b, jnp.minimum((chunk(t) + 1) * hb, s // HALO - 1), qb)),
        ]

    def small_spec(chunk):
        return pl.BlockSpec((nbt, LANES, L), lambda b, t: (b, 0, chunk(t)))

    full = lambda shape: pl.BlockSpec(shape, lambda b, t: (0,) * len(shape))
    hv = GDN_V_HEADS
    return pl.pallas_call(
        functools.partial(_gdn_kernel, nc=nc, te=te),
        out_shape=jax.ShapeDtypeStruct((bsz, s, BRANCH_W), BF16),
        grid=(bsz // nbt, nc + ne),
        in_specs=qkv_specs(cf) + qkv_specs(cbk)
        + [pl.BlockSpec((nbt, te, BRANCH_W), lambda b, t: (b, ep(t), zb))]
        + [small_spec(cf), small_spec(cbk)]
        + [full((GDN_CONV, cw)), full((1, cw)), full((2, 1, LANES)), full((2, LANES, 1)),
           full((2, 1, LANES)), full((2, LANES, 1)), full((1, GDN_HEAD_DIM))],
        out_specs=pl.BlockSpec((nbt, te, BRANCH_W), lambda b, t: (b, ep(t), 0)),
        scratch_shapes=[
            pltpu.VMEM((nbt * 2, L + 2 * HALO, cw), F32),
            pltpu.VMEM((nbt * 2, GDN_HEAD_DIM, BRANCH_W), F32),
            pltpu.VMEM((nbt, s, BRANCH_W), F32),
        ],
        compiler_params=_cparams(("arbitrary", "arbitrary")),
        name="bidir_gated_deltanet",
    )(proj, proj, proj, proj, proj, proj, proj, small_t, small_t, conv_w, conv_b.reshape(1, cw),
      _pad_dir_rows(dt_bias, hv), _pad_dir_cols(dt_bias, hv), _pad_dir_rows(a_log, hv), _pad_dir_cols(a_log, hv),
      norm_w.reshape(1, GDN_HEAD_DIM))


def _local_scan(a, u, ascending):
    rows, width = a.shape
    a = a.reshape(rows // SUBLANES, SUBLANES, width)
    u = u.reshape(rows // SUBLANES, SUBLANES, width)
    sub = lax.broadcasted_iota(jnp.int32, a.shape, 1)
    d = 1
    while d < SUBLANES:
        if ascending:
            keep = sub >= d
            shift = d
        else:
            keep = sub < SUBLANES - d
            shift = SUBLANES - d
        a_sh = jnp.where(keep, pltpu.roll(a, shift, 1), 1.0)
        u_sh = jnp.where(keep, pltpu.roll(u, shift, 1), 0.0)
        u = u + a * u_sh
        a = a * a_sh
        d *= 2
    return a.reshape(rows, width), u.reshape(rows, width)


def _lru_kernel(xl_ref, prev_ref, next_ref, gl_ref, cw_ref, cb_ref, wa_ref, ba_ref, wx_ref, bx_ref, lam_ref,
                o_ref, ext_scr, a_scr, u_scr, hf_scr, carry_scr, *, nb, unroll):
    rows, width = a_scr.shape
    bw = LRU_WIDTH // LRU_BLOCKS
    step = pl.program_id(2)
    fwd = step < nb
    blk = _chunk_of_step(step, nb)

    @pl.when((step == 0) | (step == nb))
    def _():
        carry_scr[...] = jnp.zeros_like(carry_scr)

    _fill_conv_window(ext_scr, xl_ref[0], prev_ref[0], next_ref[0], blk, nb, rows)
    xc = _centred_conv4(ext_scr, cw_ref, cb_ref, rows)
    r_parts, i_parts = [], []
    for n in range(width // bw):
        xb = xc[:, n * bw:(n + 1) * bw].astype(BF16)
        r_parts.append(_dot(xb, wa_ref[0, n]))
        i_parts.append(_dot(xb, wx_ref[0, n]))
    tr = jnp.tanh(jnp.concatenate(r_parts, axis=1) + ba_ref[0])
    ti = jnp.tanh(jnp.concatenate(i_parts, axis=1) + bx_ref[0])
    log_a = ((-0.5 * LRU_C) * _softplus(-lam_ref[0])) * (tr + 1.0)
    a = jnp.exp(log_a)
    quarter = (-0.25 * jnp.tanh(log_a)) * (a * a + 1.0)
    u = (quarter * lax.rsqrt(jnp.maximum(quarter, TINY))) * ((ti + 1.0) * xc)

    n_groups = rows // SUBLANES
    base_out = pl.multiple_of(blk * rows, rows)

    def scan(ascending):
        a_loc, u_loc = _local_scan(a, u, ascending)
        a_scr[...] = a_loc
        u_scr[...] = u_loc
        last = SUBLANES - 1 if ascending else 0

        def body(it, carry):
            pos = it if ascending else n_groups - 1 - it
            r0 = pl.multiple_of(pos * SUBLANES, SUBLANES)
            h = u_scr[pl.ds(r0, SUBLANES), :] + a_scr[pl.ds(r0, SUBLANES), :] * carry
            u_scr[pl.ds(r0, SUBLANES), :] = h
            return jnp.broadcast_to(h[last:last + 1, :], h.shape)

        carry_scr[...] = lax.fori_loop(0, n_groups, body, carry_scr[...], unroll=unroll)

    @pl.when(fwd)
    def _():
        scan(True)
        hf_scr[pl.ds(base_out, rows), :] = u_scr[...]

    @pl.when(jnp.logical_not(fwd))
    def _():
        scan(False)
        o_ref[0] = ((hf_scr[pl.ds(base_out, rows), :] + u_scr[...]) * _silu(gl_ref[0])).astype(BF16)


def _lru(proj, conv_w, conv_b, wa, ba, wx, bx, lam, *, col_xl, col_gl, tt=1024, width=1024, unroll=8):
    bsz, s, _ = proj.shape
    tt = min(tt, s)
    nb = s // tt
    w_total = LRU_WIDTH
    bw = w_total // LRU_BLOCKS
    nbw = width // bw
    xcol, gcol = col_xl // width, col_gl // width
    assert col_xl % width == 0 and col_gl % width == 0
    hb = tt // HALO

    def blk(t):
        return _chunk_of_step(t, nb)

    def direction(t):
        return jnp.where(t < nb, 0, 1)

    return pl.pallas_call(
        functools.partial(_lru_kernel, nb=nb, unroll=unroll),
        out_shape=jax.ShapeDtypeStruct((bsz, s, w_total), BF16),
        grid=(bsz, w_total // width, 2 * nb),
        in_specs=[
            pl.BlockSpec((1, tt, width), lambda b, j, t: (b, blk(t), xcol + j)),
            pl.BlockSpec((1, HALO, width), lambda b, j, t: (b, jnp.maximum(blk(t) * hb - 1, 0), xcol + j)),
            pl.BlockSpec((1, HALO, width),
                         lambda b, j, t: (b, jnp.minimum((blk(t) + 1) * hb, s // HALO - 1), xcol + j)),
            pl.BlockSpec((1, tt, width), lambda b, j, t: (b, blk(t), gcol + j)),
            pl.BlockSpec((LRU_CONV, width), lambda b, j, t: (0, j)),
            pl.BlockSpec((1, width), lambda b, j, t: (0, j)),
            pl.BlockSpec((1, nbw, bw, bw), lambda b, j, t: (direction(t), j, 0, 0)),
            pl.BlockSpec((1, 1, width), lambda b, j, t: (direction(t), 0, j)),
            pl.BlockSpec((1, nbw, bw, bw), lambda b, j, t: (direction(t), j, 0, 0)),
            pl.BlockSpec((1, 1, width), lambda b, j, t: (direction(t), 0, j)),
            pl.BlockSpec((1, 1, width), lambda b, j, t: (direction(t), 0, j)),
        ],
        out_specs=pl.BlockSpec((1, tt, width),
                               lambda b, j, t: (b, jnp.where(t < nb, nb - 1, 2 * nb - 1 - t), j)),
        scratch_shapes=[
            pltpu.VMEM((tt + 2 * HALO, width), F32),
            pltpu.VMEM((tt, width), F32),
            pltpu.VMEM((tt, width), F32),
            pltpu.VMEM((s, width), F32),
            pltpu.VMEM((SUBLANES, width), F32),
        ],
        compiler_params=_cparams(("arbitrary", "arbitrary", "arbitrary")),
        name="bidir_rglru",
    )(proj, proj, proj, proj, conv_w, conv_b.reshape(1, w_total),
      (0.5 * wa).astype(BF16), (0.5 * ba).reshape(2, 1, w_total),
      (0.5 * wx).astype(BF16), (0.5 * bx).reshape(2, 1, w_total), lam.reshape(2, 1, w_total))


CD_COL_QKV, CD_COL_Z, CD_COL_XL, CD_COL_GL = 0, 2048, 3072, 4096


def _cd_weights(w_in):
    wt = w_in.T
    nqk = GDN_QK_HEADS * GDN_HEAD_DIM
    hv = GDN_V_HEADS
    p = _piece_table([("q", nqk), ("k", nqk), ("v", BRANCH_W), ("bf", hv), ("bb", hv), ("af", hv), ("ab", hv),
                      ("z", BRANCH_W), ("xl", LRU_WIDTH), ("gl", LRU_WIDTH)])
    main = _regroup_cast(wt, [p[n] for n in ("q", "k", "v", "z", "xl", "gl")])
    s0 = p["bf"][0]
    zpad = jnp.zeros((LANES - 4 * hv, wt.shape[1]), wt.dtype)
    small = jnp.concatenate([wt[s0:s0 + 4 * hv], zpad], axis=0).astype(BF16)
    return main, small


def _layer1(x, mod_l, norm_w, w_in, conv_w, conv_b, a_log_f, a_log_b, dt_bias_f, dt_bias_b, gdn_norm,
            lru_conv_w, lru_conv_b, wa_f, ba_f, wx_f, bx_f, lam_f, wa_b, ba_b, wx_b, bx_b, lam_b, w_out,
            final_w=None):
    w_main, w_small = _cd_weights(w_in)
    proj, small_t = _inproj(x, mod_l, norm_w, w_main, w_small)
    gdn = _gdn(proj, small_t, conv_w, conv_b, jnp.stack([dt_bias_f, dt_bias_b]),
               jnp.stack([a_log_f, a_log_b]), gdn_norm, col_qkv=CD_COL_QKV, col_z=CD_COL_Z)
    lru = _lru(proj, lru_conv_w, lru_conv_b, jnp.stack([wa_f, wa_b]), jnp.stack([ba_f, ba_b]),
               jnp.stack([wx_f, wx_b]), jnp.stack([bx_f, bx_b]), jnp.stack([lam_f, lam_b]),
               col_xl=CD_COL_XL, col_gl=CD_COL_GL)
    out = _outproj(gdn, lru, w_out, x, mod_l, final_w)
    return out, (proj, small_t, gdn, lru)


def kernel(x, c, w_mod, b_mod, norm_w, ab_w_in, ab_q_norm, ab_k_norm, ab_conv_w, ab_conv_b, ab_dt_bias_f, ab_dt_bias_b, ab_a_log_f, ab_a_log_b, ab_d_skip, ab_ssd_norm, ab_w_out, cd_w_in, cd_conv_w, cd_conv_b, cd_a_log_f, cd_a_log_b, cd_dt_bias_f, cd_dt_bias_b, cd_gdn_norm, cd_lru_conv_w, cd_lru_conv_b, cd_lru_wa_f, cd_lru_ba_f, cd_lru_wx_f, cd_lru_bx_f, cd_lru_lam_f, cd_lru_wa_b, cd_lru_ba_b, cd_lru_wx_b, cd_lru_bx_b, cd_lru_lam_b, cd_w_out, final_norm_w):
    mods = _modulation(c, w_mod, b_mod)
    x1, _ = _layer0(x, mods[0], norm_w[0], ab_w_in[0], ab_q_norm[0], ab_k_norm[0], ab_conv_w[0], ab_conv_b[0],
                    ab_dt_bias_f[0], ab_dt_bias_b[0], ab_a_log_f[0], ab_a_log_b[0], ab_d_skip[0],
                    ab_ssd_norm[0], ab_w_out[0])
    out, _ = _layer1(x1, mods[1], norm_w[1], cd_w_in[0], cd_conv_w[0], cd_conv_b[0], cd_a_log_f[0], cd_a_log_b[0],
                     cd_dt_bias_f[0], cd_dt_bias_b[0], cd_gdn_norm[0], cd_lru_conv_w[0], cd_lru_conv_b[0],
                     cd_lru_wa_f[0], cd_lru_ba_f[0], cd_lru_wx_f[0], cd_lru_bx_f[0], cd_lru_lam_f[0],
                     cd_lru_wa_b[0], cd_lru_ba_b[0], cd_lru_wx_b[0], cd_lru_bx_b[0], cd_lru_lam_b[0],
                     cd_w_out[0], final_norm_w)
    return out
```
